```python
import jax, jax.numpy as jnp
from jax import lax
import numpy as np

D_MODEL = 1024
BATCH = 32
SEQ = 2048
DEPTH = 2
DEC_BATCH = 16
DEC_SEQ = 32
PAST_LEN = 2048

CHUNK = 64
N_META = 16
Q_BLOCK = 128
GROUP_W = D_MODEL // 4
D_MIX = 4 * GROUP_W
H_A = 4
DK_A = GROUP_W // H_A
DV_A = GROUP_W // H_A
H_B = 4
HD_B = GROUP_W // H_B
FOX_BIAS_MEAN = 2.0
H_C = 4
HD_C = GROUP_W // H_C
H_IDX = 8
D_IDX = 32
TOP_K_MAX = 256
IDX_SCALE = (H_IDX * D_IDX) ** -0.5
H_D = 4
HD_D = GROUP_W // H_D
ROPE_BASE = 10000.0
EPS = 1e-6

SPLIT_SIZES = (
    GROUP_W, GROUP_W, GROUP_W, GROUP_W,
    GROUP_W, GROUP_W, GROUP_W, H_B, GROUP_W,
    GROUP_W, HD_C, HD_C, GROUP_W, H_IDX * D_IDX, D_IDX, H_IDX,
    GROUP_W, GROUP_W, GROUP_W, GROUP_W,
)
N_IN = sum(SPLIT_SIZES)

kernel_name = 'hymba_hgrn2_fox_dsa_retnet_stream'


def _rms(x, g):
    xf = x.astype(jnp.float32)
    y = xf * lax.rsqrt(jnp.mean(xf * xf, axis=-1, keepdims=True) + EPS)
    return (y * g.astype(jnp.float32)).astype(x.dtype)


def _head_rms(o):
    of = o.astype(jnp.float32)
    return (of * lax.rsqrt(jnp.mean(of * of, axis=-1, keepdims=True) + EPS)).astype(o.dtype)


def _rope(x, pos):
    half = x.shape[-1] // 2
    inv = ROPE_BASE ** (-jnp.arange(half, dtype=jnp.float32) / half)
    ang = pos.astype(jnp.float32)[:, None] * inv[None, :]
    cos = jnp.cos(ang)[None, :, None, :]
    sin = jnp.sin(ang)[None, :, None, :]
    x1 = x[..., :half].astype(jnp.float32)
    x2 = x[..., half:].astype(jnp.float32)
    return jnp.concatenate([x1 * cos - x2 * sin, x1 * sin + x2 * cos], axis=-1).astype(x.dtype)


def _ret_log_gamma():
    return jnp.log(1.0 - 2.0 ** (-5.0 - jnp.arange(H_D, dtype=jnp.float32)))


def _project(xn, w_in, fox_b, lb, log_gamma, pos):
    B, T, _ = xn.shape
    dt = xn.dtype
    split_idx = [int(i) for i in np.cumsum(SPLIT_SIZES)[:-1]]
    (aq, af, ai, ag, bq, bk, bv, bf, bg, cq, ck, cv, cg, ciq, cik, ciw,
     dq, dk, dv, dg) = jnp.split(xn @ w_in, split_idx, axis=-1)
    heads = lambda t, h: t.reshape(B, T, h, -1)
    f_a = lb + (1.0 - lb) * jax.nn.sigmoid(af.astype(jnp.float32))
    mix_a = (heads(aq, H_A), heads((1.0 - f_a).astype(dt), H_A), heads(ai, H_A),
             heads(jnp.log(f_a), H_A))
    logf_b = jax.nn.log_sigmoid(bf.astype(jnp.float32) + fox_b.astype(jnp.float32))
    mix_b = (heads(bq, H_B), heads(bk, H_B), heads(bv, H_B), logf_b)
    mix_c = (heads(cq, H_C), ck, cv, heads(ciq, H_IDX), cik, ciw)
    q_d = _rope(heads(dq, H_D), pos)
    k_d = _rope(heads(dk, H_D), pos) * (HD_D ** -0.5)
    logf_d = jnp.broadcast_to(log_gamma[:, None], (B, T, H_D, HD_D))
    mix_d = (q_d, k_d, heads(dv, H_D), logf_d)
    return mix_a, mix_b, mix_c, mix_d, (ag, bg, cg, dg)


def _gla(q, k, v, logf, s0, block):
    B, T, H, DK = q.shape
    DV = v.shape[-1]
    n = T // block
    f32 = jnp.float32

    def blocks(a):
        return jnp.moveaxis(a.reshape((B, n, block) + a.shape[2:]), 1, 0)

    causal = jnp.tril(jnp.ones((block, block), bool))[None, :, :, None, None]

    def step(S, inp):
        qb, kb, vb, gb = (t.astype(f32) for t in inp)
        b = jnp.cumsum(gb, axis=1)
        inter = jnp.einsum('bthk,bhkv->bthv', qb * jnp.exp(b), S)
        decay = jnp.exp(jnp.where(causal, b[:, :, None] - b[:, None, :], -jnp.inf))
        att = jnp.einsum('bthk,bshk,btshk->bhts', qb, kb, decay)
        intra = jnp.einsum('bhts,bshv->bthv', att, vb)
        b_end = b[:, -1]
        S_new = jnp.exp(b_end)[..., None] * S + jnp.einsum(
            'bshk,bshv->bhkv', kb * jnp.exp(b_end[:, None] - b), vb)
        return S_new, inter + intra

    S, o = lax.scan(step, s0.astype(f32), (blocks(q), blocks(k), blocks(v), blocks(logf)))
    o = jnp.moveaxis(o, 0, 1).reshape(B, T, H, DV)
    return o.astype(v.dtype), S


def _gla_prompt(q, k, v, logf):
    B, _, H, DK = q.shape
    s0 = jnp.zeros((B, H, DK, v.shape[-1]), jnp.float32)
    m = N_META
    o_m, s_m = _gla(q[:, :m], k[:, :m], v[:, :m], logf[:, :m], s0, m)
    o_r, s_r = _gla(q[:, m:], k[:, m:], v[:, m:], logf[:, m:], s_m, CHUNK)
    return jnp.concatenate([o_m, o_r], axis=1), s_r


def _spans(length):
    return [(0, N_META)] + [(s, s + Q_BLOCK) for s in range(N_META, length, Q_BLOCK)]


def _fox_attend(q, k, v, c_q, c_k, pos_q, pos_k):
    s = jnp.einsum('bqhd,bkhd->bhqk', q, k).astype(jnp.float32) * (HD_B ** -0.5)
    s = s + (jnp.swapaxes(c_q, 1, 2)[..., :, None] - jnp.swapaxes(c_k, 1, 2)[..., None, :])
    s = jnp.where((pos_k[None, :] <= pos_q[:, None])[None, None], s, -jnp.inf)
    p = jax.nn.softmax(s, axis=-1).astype(v.dtype)
    return jnp.einsum('bhqk,bkhd->bqhd', p, v)


def _fox_prompt(q, k, v, logf):
    L = q.shape[1]
    c = jnp.cumsum(logf.astype(jnp.float32), axis=1)
    pos = jnp.arange(L)
    outs = [_fox_attend(q[:, s:e], k[:, :e], v[:, :e], c[:, s:e], c[:, :e], pos[s:e], pos[:e])
            for s, e in _spans(L)]
    return jnp.concatenate(outs, axis=1)


def _dsa_attend(q, k, v, iq, ik, iw, cid_q, cid_k, k_top):
    sc = jnp.einsum('bqhd,bkd->bqhk', iq, ik).astype(jnp.float32)
    score = jnp.einsum('bqhk,bqh->bqk', jax.nn.relu(sc), iw.astype(jnp.float32)) * IDX_SCALE
    adm = cid_k[None, :] <= cid_q[:, None]
    score = jnp.where(adm[None], score, -jnp.inf)
    kk = min(k_top, k.shape[1])
    _, sel = lax.top_k(score, kk)
    gather = jax.vmap(lambda a, i: a[i])
    k_sel = gather(k, sel)
    v_sel = gather(v, sel)
    valid = cid_k[sel] <= cid_q[None, :, None]
    s = jnp.einsum('bqhd,bqsd->bqhs', q, k_sel).astype(jnp.float32) * (HD_C ** -0.5)
    s = jnp.where(valid[:, :, None, :], s, -jnp.inf)
    p = jax.nn.softmax(s, axis=-1).astype(v.dtype)
    return jnp.einsum('bqhs,bqsd->bqhd', p, v_sel)


def _dsa_prompt(q, k, v, iq, ik, iw):
    L = q.shape[1]
    cid = jnp.concatenate([jnp.full((N_META,), -1, jnp.int32),
                           jnp.arange(L - N_META, dtype=jnp.int32) // CHUNK])
    k_top = min(TOP_K_MAX, (L - N_META) // 4)
    outs = [_dsa_attend(q[:, s:e], k[:, :e], v[:, :e], iq[:, s:e], ik[:, :e], iw[:, s:e],
                        cid[s:e], cid[:e], k_top)
            for s, e in _spans(L)]
    return jnp.concatenate(outs, axis=1)


def _merge(o_a, o_b, o_c, o_d, gates, w_out, g_post):
    B, T = o_a.shape[:2]
    flat = lambda o: o.reshape(B, T, GROUP_W)
    ag, bg, cg, dg = gates
    y = jnp.concatenate([flat(_head_rms(o_a)) * jax.nn.silu(ag),
                         flat(o_b) * jax.nn.silu(bg),
                         flat(o_c) * jax.nn.silu(cg),
                         flat(_head_rms(o_d)) * jax.nn.silu(dg)], axis=-1)
    return _rms(y @ w_out, g_post)


def setup_inputs(seed: int = 0) -> dict:
    key = jax.random.key(seed)
    ks = jax.random.split(key, 17)
    nrm = lambda k, s, scale=1.0: jax.random.normal(k, s, jnp.float32) * scale
    return {
        'x_prompt': nrm(ks[0], (BATCH, SEQ, D_MODEL)),
        'x_sample': nrm(ks[1], (DEC_BATCH, DEC_SEQ, D_MODEL)),
        'state_hgrn': nrm(ks[2], (DEPTH, DEC_BATCH, H_A, DK_A, DV_A), 0.5),
        'cache_fox_k': nrm(ks[3], (DEPTH, DEC_BATCH, PAST_LEN, H_B, HD_B)),
        'cache_fox_v': nrm(ks[4], (DEPTH, DEC_BATCH, PAST_LEN, H_B, HD_B)),
        'cache_fox_logf': jax.nn.log_sigmoid(FOX_BIAS_MEAN + nrm(ks[5], (DEPTH, DEC_BATCH, PAST_LEN, H_B))),
        'cache_dsa_k': nrm(ks[6], (DEPTH, DEC_BATCH, PAST_LEN, HD_C)),
        'cache_dsa_v': nrm(ks[7], (DEPTH, DEC_BATCH, PAST_LEN, HD_C)),
        'cache_dsa_idx_k': nrm(ks[8], (DEPTH, DEC_BATCH, PAST_LEN, D_IDX)),
        'state_ret': nrm(ks[9], (DEPTH, DEC_BATCH, H_D, HD_D, HD_D), 0.5),
        'meta_tokens': nrm(ks[10], (N_META, D_MODEL)),
        'w_in': nrm(ks[11], (DEPTH, D_MODEL, N_IN), D_MODEL ** -0.5),
        'w_out': nrm(ks[12], (DEPTH, D_MIX, D_MODEL), D_MIX ** -0.5),
        'fox_bias': FOX_BIAS_MEAN + nrm(ks[13], (DEPTH, H_B), 0.5),
        'hgrn_lb': nrm(ks[14], (DEPTH, H_A * DK_A), 0.5),
        'norm_pre': 1.0 + nrm(ks[15], (DEPTH, D_MODEL), 0.05),
        'norm_post': 1.0 + nrm(ks[16], (DEPTH, D_MODEL), 0.05),
    }


def reference(x_prompt, x_sample, state_hgrn, cache_fox_k, cache_fox_v, cache_fox_logf,
              cache_dsa_k, cache_dsa_v, cache_dsa_idx_k, state_ret, meta_tokens, w_in, w_out,
              fox_bias, hgrn_lb, norm_pre, norm_post):
    dt = x_prompt.dtype
    sm = jax.nn.softmax(hgrn_lb.astype(jnp.float32), axis=0)
    lbs = jnp.cumsum(sm, axis=0) - sm[0:1]
    log_gamma = _ret_log_gamma()

    B = x_prompt.shape[0]
    xp = jnp.concatenate([jnp.broadcast_to(meta_tokens.astype(dt)[None], (B, N_META, D_MODEL)),
                          x_prompt], axis=1)
    xs = x_sample
    Lp = xp.shape[1]
    T = xs.shape[1]
    past = cache_fox_k.shape[2]
    pos_p = jnp.arange(Lp) - N_META
    pos_s = past + jnp.arange(T)
    pos_all = jnp.arange(past + T)
    cid_all = pos_all // CHUNK
    k_top_s = min(TOP_K_MAX, (past + T) // 4)

    ph, pfk, pfv, pfl, pck, pcv, pci, pr = [], [], [], [], [], [], [], []
    sh, sfk, sfv, sfl, sck, scv, sci, sr = [], [], [], [], [], [], [], []
    for l in range(DEPTH):
        xn = _rms(xp, norm_pre[l])
        mix_a, mix_b, mix_c, mix_d, gates = _project(xn, w_in[l], fox_bias[l], lbs[l], log_gamma, pos_p)
        o_a, st_a = _gla_prompt(*mix_a)
        q_b, k_b, v_b, lf_b = mix_b
        o_b = _fox_prompt(q_b, k_b, v_b, lf_b)
        q_c, k_c, v_c, iq_c, ik_c, iw_c = mix_c
        o_c = _dsa_prompt(q_c, k_c, v_c, iq_c, ik_c, iw_c)
        o_d, st_d = _gla_prompt(*mix_d)
        xp = xp + _merge(o_a, o_b, o_c, o_d, gates, w_out[l], norm_post[l])
        ph.append(st_a); pfk.append(k_b); pfv.append(v_b); pfl.append(lf_b)
        pck.append(k_c); pcv.append(v_c); pci.append(ik_c); pr.append(st_d)

        xn = _rms(xs, norm_pre[l])
        mix_a, mix_b, mix_c, mix_d, gates = _project(xn, w_in[l], fox_bias[l], lbs[l], log_gamma, pos_s)
        o_a, st_a = _gla(*mix_a, state_hgrn[l], T)
        q_b, k_b, v_b, lf_b = mix_b
        kb_all = jnp.concatenate([cache_fox_k[l], k_b], axis=1)
        vb_all = jnp.concatenate([cache_fox_v[l], v_b], axis=1)
        c_all = jnp.cumsum(jnp.concatenate([cache_fox_logf[l].astype(jnp.float32), lf_b], axis=1), axis=1)
        o_b = _fox_attend(q_b, kb_all, vb_all, c_all[:, past:], c_all, pos_s, pos_all)
        q_c, k_c, v_c, iq_c, ik_c, iw_c = mix_c
        kc_all = jnp.concatenate([cache_dsa_k[l], k_c], axis=1)
        vc_all = jnp.concatenate([cache_dsa_v[l], v_c], axis=1)
        ic_all = jnp.concatenate([cache_dsa_idx_k[l], ik_c], axis=1)
        o_c = _dsa_attend(q_c, kc_all, vc_all, iq_c, ic_all, iw_c, cid_all[past:], cid_all, k_top_s)
        o_d, st_d = _gla(*mix_d, state_ret[l], T)
        xs = xs + _merge(o_a, o_b, o_c, o_d, gates, w_out[l], norm_post[l])
        sh.append(st_a); sfk.append(k_b); sfv.append(v_b); sfl.append(lf_b)
        sck.append(k_c); scv.append(v_c); sci.append(ik_c); sr.append(st_d)

    y_prompt = xp[:, N_META:]
    y_sample = xs
    st = lambda xs_list: jnp.stack(xs_list, axis=0).astype(dt)
    hgrn_p, fox_k_p, fox_v_p, fox_logf_p = st(ph), st(pfk), st(pfv), st(pfl)
    dsa_k_p, dsa_v_p, dsa_idx_k_p, ret_p = st(pck), st(pcv), st(pci), st(pr)
    hgrn_s, fox_k_s, fox_v_s, fox_logf_s = st(sh), st(sfk), st(sfv), st(sfl)
    dsa_k_s, dsa_v_s, dsa_idx_k_s, ret_s = st(sck), st(scv), st(sci), st(sr)
    return (y_prompt, y_sample,
            hgrn_p, fox_k_p, fox_v_p, fox_logf_p, dsa_k_p, dsa_v_p, dsa_idx_k_p, ret_p,
            hgrn_s, fox_k_s, fox_v_s, fox_logf_s, dsa_k_s, dsa_v_s, dsa_idx_k_s, ret_s)
```

```python
import functools

import numpy as np
import jax
import jax.numpy as jnp
from jax import lax
from jax.experimental import pallas as pl
from jax.experimental.pallas import tpu as pltpu

F32 = jnp.float32
BF16 = jnp.bfloat16
I32 = jnp.int32

N_META = 16
CHUNK = 64
N_GROUPS = 4
N_HEADS = 4
H_IDX = 8
D_IDX = 32
TOP_K_MAX = 256
ROPE_BASE = 10000.0
EPS = 1e-6
FOX_HEADS = 4

LANES = 128
SUBLANES = 8
ROW_BLOCK = 128
GROUP_W = 2 * LANES
VMEM_LIMIT_BYTES = 56 * 1024 * 1024

IK0 = 0
IW0 = 32
BF0 = 64

NEG = -1e30
INT_MIN = np.int32(-2 ** 31)


def _cparams(*sem):
    return pltpu.CompilerParams(dimension_semantics=sem, vmem_limit_bytes=VMEM_LIMIT_BYTES)


def _split3(x):
    h = x.astype(BF16)
    r = x - h.astype(F32)
    m = r.astype(BF16)
    lo = (r - m.astype(F32)).astype(BF16)
    return h, m, lo


def _dot_exact_lhs(a_bf16, x):
    d = lambda y: jnp.dot(a_bf16, y, preferred_element_type=F32)
    h, m, lo = _split3(x)
    return d(h) + d(m) + d(lo)


def _dot_exact_rhs(x, a_bf16):
    d = lambda y: jnp.dot(y, a_bf16, preferred_element_type=F32)
    h, m, lo = _split3(x)
    return d(h) + d(m) + d(lo)


def _dot_nt(a, b):
    return lax.dot_general(a, b, (((1,), (1,)), ((), ())), preferred_element_type=F32)


def _dot_tn(a, b):
    return lax.dot_general(a, b, (((0,), (0,)), ((), ())), preferred_element_type=F32)


def _log2(n):
    assert n > 0 and n & (n - 1) == 0, n
    return n.bit_length() - 1


def _head_masks(width, n_heads):
    lane = lax.broadcasted_iota(I32, (1, width), 1)
    sh = _log2(width // n_heads)
    return [(lax.shift_right_logical(lane, sh) == h).astype(F32) for h in range(n_heads)]


def _block_diag(width, n_heads, value):
    r = lax.broadcasted_iota(I32, (width, width), 0)
    c = lax.broadcasted_iota(I32, (width, width), 1)
    sh = _log2(width // n_heads)
    same = lax.shift_right_logical(r, sh) == lax.shift_right_logical(c, sh)
    return jnp.where(same, value, 0.0).astype(F32)


def _silu(x):
    return x * jax.nn.sigmoid(x)


def _proj_kernel(x_ref, g_ref, w_ref, o_ref, *, col_chunk):
    x = x_ref[...]
    ms = jnp.mean(x * x, axis=-1, keepdims=True)
    xn = ((x * lax.rsqrt(ms + EPS)) * g_ref[...]).astype(BF16)
    for c in range(o_ref.shape[1] // col_chunk):
        cols = slice(c * col_chunk, (c + 1) * col_chunk)
        o_ref[:, cols] = jnp.dot(xn, w_ref[:, cols], preferred_element_type=F32)


def _project(x2d, g, w_bf16, tm):
    rows, d = x2d.shape
    n = w_bf16.shape[1]
    return pl.pallas_call(
        functools.partial(_proj_kernel, col_chunk=1024),
        grid=(rows // tm,),
        in_specs=[pl.BlockSpec((tm, d), lambda i: (i, 0)),
                  pl.BlockSpec((1, d), lambda i: (0, 0)),
                  pl.BlockSpec((d, n), lambda i: (0, 0))],
        out_specs=pl.BlockSpec((tm, n), lambda i: (i, 0)),
        out_shape=jax.ShapeDtypeStruct((rows, n), F32),
        compiler_params=_cparams("parallel"),
        name="proj",
    )(x2d, g, w_bf16)


def _gla_chunk(qa, qe, ka, ke2, v, decay_end, a_mask, st, hm, bd):
    c = qa.shape[0]
    q_stack = jnp.concatenate([qa * hm[h] for h in range(N_HEADS)], axis=0).astype(BF16)
    att = _dot_nt(q_stack, ka.astype(BF16)) * a_mask
    o_stack = jnp.dot(att.astype(BF16), v.astype(BF16), preferred_element_type=F32)
    o_intra = o_stack[0:c] * hm[0]
    for h in range(1, N_HEADS):
        o_intra = o_intra + o_stack[h * c:(h + 1) * c] * hm[h]
    o_inter = _dot_nt(qe.astype(BF16), st.astype(BF16))
    st_new = st * decay_end + _dot_tn(v.astype(BF16), ke2.astype(BF16)) * bd
    return o_inter + o_intra, st_new


def _head_rms_gate(o, gate, bd_mean_bf16):
    h, m, _ = _split3(o * o)
    ms = (jnp.dot(h, bd_mean_bf16, preferred_element_type=F32)
          + jnp.dot(m, bd_mean_bf16, preferred_element_type=F32))
    return (o * lax.rsqrt(ms + EPS)) * _silu(gate)


def _hgrn_kernel(a_ref, lb_ref, st0_ref, y_ref, st_ref, *, chunk, n_chunks, row0):
    w = lb_ref.shape[1]
    hm = _head_masks(w, N_HEADS)
    bd = _block_diag(w, N_HEADS, 1.0)
    bd_mean = _block_diag(w, N_HEADS, 1.0 / (w // N_HEADS)).astype(BF16)
    r = lax.broadcasted_iota(I32, (chunk, chunk), 0)
    s = lax.broadcasted_iota(I32, (chunk, chunk), 1)
    tri = (r >= s).astype(BF16)
    assert chunk & (chunk - 1) == 0
    rs = lax.broadcasted_iota(I32, (N_HEADS * chunk, chunk), 0) & (chunk - 1)
    ss = lax.broadcasted_iota(I32, (N_HEADS * chunk, chunk), 1)
    causal = (rs >= ss).astype(F32)
    lb = lb_ref[...]

    if row0 > 0:
        y_ref[0:row0, :] = jnp.zeros((row0, w), F32)
    tail = row0 + n_chunks * chunk
    if tail < y_ref.shape[0]:
        y_ref[tail:, :] = jnp.zeros((y_ref.shape[0] - tail, w), F32)
    st_ref[...] = st0_ref[...]

    def body(c, carry):
        rows = pl.ds(pl.multiple_of(row0 + c * chunk, chunk), chunk)
        q = a_ref[rows, 0:w]
        f = lb + (1.0 - lb) * jax.nn.sigmoid(a_ref[rows, w:2 * w])
        k = 1.0 - f
        v = a_ref[rows, 2 * w:3 * w]
        gate = a_ref[rows, 3 * w:4 * w]
        b = _dot_exact_lhs(tri, jnp.log(f))
        b_end = b[chunk - 1:chunk, :]
        qe = q * jnp.exp(b)
        o, st_new = _gla_chunk(qe, qe, k * jnp.exp(-b), k * jnp.exp(b_end - b), v,
                               jnp.exp(b_end), causal, st_ref[...], hm, bd)
        st_ref[...] = st_new
        y_ref[rows, :] = _head_rms_gate(o, gate, bd_mean)
        return carry

    lax.fori_loop(0, n_chunks, body, 0)


def _ret_kernel(d_ref, cos_ref, sin_ref, eb_ref, ke2s_ref, dend_ref, gam_ref, st0_ref, y_ref, st_ref,
                *, chunk, n_chunks, row0):
    w = st0_ref.shape[0]
    hd = w // N_HEADS
    hm = _head_masks(w, N_HEADS)
    bd = _block_diag(w, N_HEADS, 1.0)
    bd_mean = _block_diag(w, N_HEADS, 1.0 / hd).astype(BF16)
    lane = lax.broadcasted_iota(I32, (1, w), 1)
    first_half = (lane & (hd - 1)) < (hd // 2)
    eb = eb_ref[...]
    ke2s = ke2s_ref[...]
    dend = dend_ref[...]
    gam = gam_ref[...]

    if row0 > 0:
        y_ref[0:row0, :] = jnp.zeros((row0, w), F32)
    tail = row0 + n_chunks * chunk
    if tail < y_ref.shape[0]:
        y_ref[tail:, :] = jnp.zeros((y_ref.shape[0] - tail, w), F32)
    st_ref[...] = st0_ref[...]

    def rope(x, cos, sin_signed):
        swapped = jnp.where(first_half, pltpu.roll(x, w - hd // 2, 1), pltpu.roll(x, hd // 2, 1))
        return x * cos + swapped * sin_signed

    def body(c, carry):
        rows = pl.ds(pl.multiple_of(row0 + c * chunk, chunk), chunk)
        cos = cos_ref[rows, :]
        sin = sin_ref[rows, :]
        q = rope(d_ref[rows, 0:w], cos, sin)
        k = rope(d_ref[rows, w:2 * w], cos, sin) * (hd ** -0.5)
        v = d_ref[rows, 2 * w:3 * w]
        gate = d_ref[rows, 3 * w:4 * w]
        o, st_new = _gla_chunk(q, q * eb, k, k * ke2s, v, dend, gam, st_ref[...], hm, bd)
        st_ref[...] = st_new
        y_ref[rows, :] = _head_rms_gate(o, gate, bd_mean)
        return carry

    lax.fori_loop(0, n_chunks, body, 0)


def _hgrn(p3, col_block, lb, st0, chunk, n_chunks, row0):
    bsz, lp, _ = p3.shape
    w = lb.shape[1]
    return pl.pallas_call(
        functools.partial(_hgrn_kernel, chunk=chunk, n_chunks=n_chunks, row0=row0),
        grid=(bsz,),
        in_specs=[pl.BlockSpec((None, lp, 4 * w), lambda b: (b, 0, col_block)),
                  pl.BlockSpec((1, w), lambda b: (0, 0)),
                  pl.BlockSpec((None, w, w), lambda b: (b, 0, 0))],
        out_specs=[pl.BlockSpec((None, lp, w), lambda b: (b, 0, 0)),
                   pl.BlockSpec((None, w, w), lambda b: (b, 0, 0))],
        out_shape=[jax.ShapeDtypeStruct((bsz, lp, w), F32),
                   jax.ShapeDtypeStruct((bsz, w, w), F32)],
        compiler_params=_cparams("parallel"),
        name="hgrn",
    )(p3, lb, st0)


def _ret(p3, col_block, tables, st0, chunk, n_chunks, row0):
    bsz, lp, _ = p3.shape
    w = st0.shape[1]
    cos, sin, eb, ke2s, dend, gam = tables
    full = lambda a: pl.BlockSpec(a.shape, lambda b: (0,) * a.ndim)
    return pl.pallas_call(
        functools.partial(_ret_kernel, chunk=chunk, n_chunks=n_chunks, row0=row0),
        grid=(bsz,),
        in_specs=[pl.BlockSpec((None, lp, 4 * w), lambda b: (b, 0, col_block)),
                  full(cos), full(sin), full(eb), full(ke2s), full(dend), full(gam),
                  pl.BlockSpec((None, w, w), lambda b: (b, 0, 0))],
        out_specs=[pl.BlockSpec((None, lp, w), lambda b: (b, 0, 0)),
                   pl.BlockSpec((None, w, w), lambda b: (b, 0, 0))],
        out_shape=[jax.ShapeDtypeStruct((bsz, lp, w), F32),
                   jax.ShapeDtypeStruct((bsz, w, w), F32)],
        compiler_params=_cparams("parallel"),
        name="ret",
    )(p3, cos, sin, eb, ke2s, dend, gam, st0)


def _log_sigmoid(x):
    return -(jnp.maximum(-x, 0.0) + jnp.log(1.0 + jnp.exp(-jnp.abs(x))))


def _foxprep_kernel(z_ref, bias_ref, lf_ref, cc_ref, cr_ref, *, n_pass):
    nblk = z_ref.shape[0] // ROW_BLOCK
    r = lax.broadcasted_iota(I32, (ROW_BLOCK, ROW_BLOCK), 0)
    s = lax.broadcasted_iota(I32, (ROW_BLOCK, ROW_BLOCK), 1)
    tri = (r >= s).astype(BF16)
    bias = bias_ref[...]

    def body(i, carry):
        rows = pl.ds(pl.multiple_of(i * ROW_BLOCK, ROW_BLOCK), ROW_BLOCK)
        z = z_ref[rows, :]
        rowi = i * ROW_BLOCK + lax.broadcasted_iota(I32, (ROW_BLOCK, 1), 0)
        lf = jnp.where(rowi < n_pass, z, _log_sigmoid(z + bias))
        lf_ref[rows, :] = lf
        cs = _dot_exact_lhs(tri, lf) + carry
        cc_ref[rows, :] = cs
        cr_ref[:, rows] = cs.T[BF0:BF0 + SUBLANES, :]
        return cs[ROW_BLOCK - 1:ROW_BLOCK, :]

    lax.fori_loop(0, nblk, body, jnp.zeros((1, LANES), F32))


def _foxprep(z3, z_col, bias, n_pass):
    bsz, lk, _ = z3.shape
    return pl.pallas_call(
        functools.partial(_foxprep_kernel, n_pass=n_pass),
        grid=(bsz,),
        in_specs=[pl.BlockSpec((None, lk, LANES), lambda b: (b, 0, z_col)),
                  pl.BlockSpec((1, LANES), lambda b: (0, 0))],
        out_specs=[pl.BlockSpec((None, lk, LANES), lambda b: (b, 0, 0)),
                   pl.BlockSpec((None, lk, LANES), lambda b: (b, 0, 0)),
                   pl.BlockSpec((None, SUBLANES, lk), lambda b: (b, 0, 0))],
        out_shape=[jax.ShapeDtypeStruct((bsz, lk, LANES), F32),
                   jax.ShapeDtypeStruct((bsz, lk, LANES), F32),
                   jax.ShapeDtypeStruct((bsz, SUBLANES, lk), F32)],
        compiler_params=_cparams("parallel"),
        name="foxprep",
    )(z3, bias)


def _fox_kernel(q_ref, k_ref, v_ref, g_ref, cc_ref, cr_ref, y_ref, *, key_lo, q_off, causal_blocks):
    qb, w = q_ref.shape
    nkb_total = k_ref.shape[0] // ROW_BLOCK
    j = pl.program_id(1)
    nkb = jnp.minimum(j + 1, nkb_total) if causal_blocks else nkb_total
    hm = _head_masks(w, N_HEADS)
    scale = (w // N_HEADS) ** -0.5
    q = q_ref[...]
    qh = [(q * hm[h]).astype(BF16) for h in range(N_HEADS)]
    ccol = cc_ref[...]
    cq = [ccol[:, BF0 + h:BF0 + h + 1] for h in range(N_HEADS)]
    qrow = j * qb + lax.broadcasted_iota(I32, (qb, 1), 0) + q_off

    def body(kb, carry):
        ms, ls, acc = carry
        k0 = pl.multiple_of(kb * ROW_BLOCK, ROW_BLOCK)
        kblk = k_ref[pl.ds(k0, ROW_BLOCK), :].astype(BF16)
        vblk = v_ref[pl.ds(k0, ROW_BLOCK), :].astype(BF16)
        kidx = k0 + lax.broadcasted_iota(I32, (1, ROW_BLOCK), 1)
        ok = (kidx >= key_lo) & (kidx <= qrow)
        new_ms, new_ls = [], []
        alpha_full = jnp.zeros((qb, w), F32)
        pv = jnp.zeros((qb, w), F32)
        for h in range(N_HEADS):
            sc = _dot_nt(qh[h], kblk) * scale
            sc = sc + (cq[h] - cr_ref[h:h + 1, pl.ds(k0, ROW_BLOCK)])
            sc = jnp.where(ok, sc, NEG)
            m_new = jnp.maximum(ms[h], jnp.max(sc, axis=-1, keepdims=True))
            alpha = jnp.exp(ms[h] - m_new)
            p = jnp.exp(sc - m_new)
            new_ms.append(m_new)
            new_ls.append(alpha * ls[h] + jnp.sum(p, axis=-1, keepdims=True))
            alpha_full = alpha_full + alpha * hm[h]
            pv = pv + jnp.dot(p.astype(BF16), vblk, preferred_element_type=F32) * hm[h]
        return tuple(new_ms), tuple(new_ls), acc * alpha_full + pv

    init = (tuple(jnp.full((qb, 1), NEG, F32) for _ in range(N_HEADS)),
            tuple(jnp.zeros((qb, 1), F32) for _ in range(N_HEADS)),
            jnp.zeros((qb, w), F32))
    ms, ls, acc = lax.fori_loop(0, nkb, body, init)
    l_full = jnp.zeros((qb, w), F32)
    for h in range(N_HEADS):
        l_full = l_full + ls[h] * hm[h]
    y_ref[...] = (acc / l_full) * _silu(g_ref[...])


def _fox(q_arr, q_col, k_arr, k_col, v_arr, v_col, g_arr, g_col, cc, cr, key_lo, q_off, causal_blocks):
    bsz, lq, _ = q_arr.shape
    lk = k_arr.shape[1]
    w = GROUP_W
    nqb = lq // ROW_BLOCK
    return pl.pallas_call(
        functools.partial(_fox_kernel, key_lo=key_lo, q_off=q_off, causal_blocks=causal_blocks),
        grid=(bsz, nqb),
        in_specs=[pl.BlockSpec((None, ROW_BLOCK, w), lambda b, j: (b, j, q_col)),
                  pl.BlockSpec((None, lk, w), lambda b, j: (b, 0, k_col)),
                  pl.BlockSpec((None, lk, w), lambda b, j: (b, 0, v_col)),
                  pl.BlockSpec((None, ROW_BLOCK, w), lambda b, j: (b, j, g_col)),
                  pl.BlockSpec((None, ROW_BLOCK, LANES), lambda b, j: (b, j + q_off // ROW_BLOCK, 0)),
                  pl.BlockSpec((None, SUBLANES, lk), lambda b, j: (b, 0, 0))],
        out_specs=pl.BlockSpec((None, ROW_BLOCK, w), lambda b, j: (b, j, 0)),
        out_shape=jax.ShapeDtypeStruct((bsz, lq, w), F32),
        compiler_params=_cparams("parallel", "arbitrary"),
        name="fox",
    )(q_arr, k_arr, v_arr, g_arr, cc, cr)


def _dsa_kernel(cq_ref, cg_ref, ciq_ref, mq_ref, ckv_ref, mk_ref, y_ref, key_scr,
                *, k_top, key_lo, key_hi, chunk_causal):
    qb, w = cq_ref.shape
    hd = w // N_HEADS
    nkb_total = ckv_ref.shape[0] // ROW_BLOCK
    j = pl.program_id(1)
    nkb = jnp.minimum(j + 1, nkb_total) if chunk_causal else nkb_total
    idx_scale = (H_IDX * D_IDX) ** -0.5
    scale = hd ** -0.5

    iq_t = ciq_ref[...].T
    iw_t = mq_ref[...].T[IW0:IW0 + H_IDX, :]
    q_t = cq_ref[...].T
    zpad_i = jnp.zeros((LANES - D_IDX, qb), F32)
    iq_rhs = [jnp.concatenate([iq_t[h * D_IDX:(h + 1) * D_IDX, :], zpad_i], axis=0).astype(BF16)
              for h in range(H_IDX)]
    zpad_q = jnp.zeros((LANES - hd, qb), F32)
    q_rhs = [jnp.concatenate([q_t[h * hd:(h + 1) * hd, :], zpad_q], axis=0).astype(BF16)
             for h in range(N_HEADS)]

    qrow = j * qb + lax.broadcasted_iota(I32, (1, qb), 1)
    if chunk_causal:
        hi = (lax.shift_right_logical(qrow, 6) + 1) * CHUNK
    else:
        hi = jnp.full((1, qb), key_hi, I32)
    sub = lax.broadcasted_iota(I32, (ROW_BLOCK, 1), 0)

    def score_block(kb, carry):
        k0 = pl.multiple_of(kb * ROW_BLOCK, ROW_BLOCK)
        mk = mk_ref[pl.ds(k0, ROW_BLOCK), :].astype(BF16)
        acc = jnp.zeros((ROW_BLOCK, qb), F32)
        for h in range(H_IDX):
            sc = jnp.dot(mk, iq_rhs[h], preferred_element_type=F32)
            acc = acc + jnp.maximum(sc, 0.0) * iw_t[h:h + 1, :]
        score = acc * idx_scale + 0.0
        kidx = k0 + sub
        adm = (kidx >= key_lo) & (kidx < hi)
        u = pltpu.bitcast(score, I32)
        key = u ^ (lax.shift_right_arithmetic(u, 31) & np.int32(0x7FFFFFFF))
        key_scr[pl.ds(k0, ROW_BLOCK), :] = jnp.where(adm, key, INT_MIN)
        return carry

    lax.fori_loop(0, nkb, score_block, 0)

    def count(pred):
        def cb(kb, c8):
            k0 = pl.multiple_of(kb * ROW_BLOCK, ROW_BLOCK)
            ind = jnp.where(pred(key_scr[pl.ds(k0, ROW_BLOCK), :], k0), 1, 0).astype(I32)
            return c8 + jnp.sum(ind.reshape(ROW_BLOCK // SUBLANES, SUBLANES, qb), axis=0)
        c8 = lax.fori_loop(0, nkb, cb, jnp.zeros((SUBLANES, qb), I32))
        return jnp.sum(c8, axis=0, keepdims=True)

    def bisect(n_bits, enough):
        def bit_body(i, ans):
            cand = ans | lax.shift_left(np.int32(1), (n_bits - 1 - i).astype(I32))
            return jnp.where(enough(cand), cand, ans)
        return lax.fori_loop(0, n_bits, bit_body, jnp.zeros((1, qb), I32))

    ans = bisect(32, lambda cand: count(lambda kk, k0: kk >= (cand ^ INT_MIN)) >= k_top)
    thr = ans ^ INT_MIN
    need = k_top - count(lambda kk, k0: kk > thr)
    rev_base = np.int32(4095)
    ans2 = bisect(12, lambda cand: count(
        lambda kk, k0: (kk == thr) & ((rev_base - (k0 + sub)) >= cand)) >= need)

    def attend(kb, carry):
        ms, ls, acc = carry
        k0 = pl.multiple_of(kb * ROW_BLOCK, ROW_BLOCK)
        kk = key_scr[pl.ds(k0, ROW_BLOCK), :]
        sel = ((kk > thr) | ((kk == thr) & ((rev_base - (k0 + sub)) >= ans2))) & (kk != INT_MIN)
        kv = ckv_ref[pl.ds(k0, ROW_BLOCK), :].astype(BF16)
        new_ms, new_ls, new_acc = [], [], []
        for h in range(N_HEADS):
            sc = jnp.dot(kv, q_rhs[h], preferred_element_type=F32) * scale
            sc = jnp.where(sel, sc, NEG)
            m_new = jnp.maximum(ms[h], jnp.max(sc, axis=0, keepdims=True))
            alpha = jnp.exp(ms[h] - m_new)
            p = jnp.where(sel, jnp.exp(sc - m_new), 0.0)
            new_ms.append(m_new)
            new_ls.append(alpha * ls[h] + jnp.sum(p, axis=0, keepdims=True))
            pv = _dot_tn(kv, p.astype(BF16))[hd:2 * hd, :]
            new_acc.append(acc[h] * alpha + pv)
        return tuple(new_ms), tuple(new_ls), tuple(new_acc)

    init = (tuple(jnp.full((1, qb), NEG, F32) for _ in range(N_HEADS)),
            tuple(jnp.zeros((1, qb), F32) for _ in range(N_HEADS)),
            tuple(jnp.zeros((hd, qb), F32) for _ in range(N_HEADS)))
    ms, ls, acc = lax.fori_loop(0, nkb, attend, init)
    o_t = jnp.concatenate(
        [acc[h] / jnp.where(ls[h] > 0.0, ls[h], 1.0) for h in range(N_HEADS)], axis=0)
    y_ref[...] = o_t.T * _silu(cg_ref[...])


def _dsa(p3, cols, ckv_arr, ckv_col, mk_arr, mk_col, k_top, key_lo, key_hi, chunk_causal):
    bsz, lq, _ = p3.shape
    lk = ckv_arr.shape[1]
    w = GROUP_W
    cq_col, cg_col, ciq_col, mq_col = cols
    return pl.pallas_call(
        functools.partial(_dsa_kernel, k_top=k_top, key_lo=key_lo, key_hi=key_hi,
                          chunk_causal=chunk_causal),
        grid=(bsz, lq // ROW_BLOCK),
        in_specs=[pl.BlockSpec((None, ROW_BLOCK, w), lambda b, j: (b, j, cq_col)),
                  pl.BlockSpec((None, ROW_BLOCK, w), lambda b, j: (b, j, cg_col)),
                  pl.BlockSpec((None, ROW_BLOCK, w), lambda b, j: (b, j, ciq_col)),
                  pl.BlockSpec((None, ROW_BLOCK, LANES), lambda b, j: (b, j, mq_col)),
                  pl.BlockSpec((None, lk, LANES), lambda b, j: (b, 0, ckv_col)),
                  pl.BlockSpec((None, lk, LANES), lambda b, j: (b, 0, mk_col))],
        out_specs=pl.BlockSpec((None, ROW_BLOCK, w), lambda b, j: (b, j, 0)),
        out_shape=jax.ShapeDtypeStruct((bsz, lq, w), F32),
        scratch_shapes=[pltpu.VMEM((lk, ROW_BLOCK), I32)],
        compiler_params=_cparams("parallel", "arbitrary"),
        name="dsa",
    )(p3, p3, p3, p3, ckv_arr, mk_arr)


def _merge_kernel(ya_ref, yb_ref, yc_ref, yd_ref, x_ref, w_ref, g_ref, o_ref, *, period, valid_lo, valid_hi):
    tm = x_ref.shape[0]
    gw = ya_ref.shape[1]
    acc = jnp.zeros(o_ref.shape, F32)
    for i, y_ref in enumerate((ya_ref, yb_ref, yc_ref, yd_ref)):
        acc = acc + jnp.dot(y_ref[...].astype(BF16), w_ref[i * gw:(i + 1) * gw, :],
                            preferred_element_type=F32)
    ms = jnp.mean(acc * acc, axis=-1, keepdims=True)
    out = x_ref[...] + (acc * lax.rsqrt(ms + EPS)) * g_ref[...]
    r0 = pl.program_id(0) * tm
    local = (r0 - (r0 // period) * period) + lax.broadcasted_iota(I32, (tm, 1), 0)
    local = jnp.where(local >= period, local - period, local)
    valid = (local >= valid_lo) & (local < valid_hi)
    o_ref[...] = jnp.where(valid, out, 0.0)


def _merge(ys, x2d, w_bf16, g, tm, period, valid_lo, valid_hi):
    rows, d = x2d.shape
    assert tm <= period
    gw = ys[0].shape[1]
    yspec = pl.BlockSpec((tm, gw), lambda i: (i, 0))
    return pl.pallas_call(
        functools.partial(_merge_kernel, period=period, valid_lo=valid_lo, valid_hi=valid_hi),
        grid=(rows // tm,),
        in_specs=[yspec, yspec, yspec, yspec,
                  pl.BlockSpec((tm, d), lambda i: (i, 0)),
                  pl.BlockSpec(w_bf16.shape, lambda i: (0, 0)),
                  pl.BlockSpec((1, d), lambda i: (0, 0))],
        out_specs=pl.BlockSpec((tm, d), lambda i: (i, 0)),
        out_shape=jax.ShapeDtypeStruct((rows, d), F32),
        compiler_params=_cparams("parallel"),
        name="merge",
    )(*ys, x2d, w_bf16, g)


def _row_tile(rows):
    return next(t for t in (512, 256, ROW_BLOCK) if rows % t == 0)


def _column_layout(gw):
    sizes = [gw] * 4 + [gw, gw, gw, FOX_HEADS, gw] + [gw, gw // 4, gw // 4, gw, H_IDX * D_IDX, D_IDX, H_IDX] + [gw] * 4
    names = ["aq", "af", "ai", "ag", "bq", "bk", "bv", "bf", "bg",
             "cq", "ck", "cv", "cg", "ciq", "cik", "ciw", "dq", "dk", "dv", "dg"]
    start = dict(zip(names, np.cumsum([0] + sizes[:-1])))
    size = dict(zip(names, sizes))
    src = -np.ones((16 * gw,), np.int64)
    def put(dst, name, off=0):
        src[dst + off:dst + off + size[name]] = np.arange(start[name], start[name] + size[name])
    for i, n in enumerate(["aq", "af", "ai", "ag", "bq", "bk", "bv", "bg", "cq", "cg", "ciq"]):
        put(i * gw, n)
    ckv0 = 11 * gw
    put(ckv0, "ck")
    put(ckv0 + gw // 4, "cv")
    misc0 = ckv0 + LANES
    put(misc0, "cik", IK0)
    put(misc0, "ciw", IW0)
    put(misc0, "bf", BF0)
    for i, n in enumerate(["dq", "dk", "dv", "dg"]):
        put(12 * gw + i * gw, n)
    return src


def _relayout_w_in(w_in_l, src):
    cols = jnp.take(w_in_l, jnp.asarray(np.maximum(src, 0)), axis=1)
    return jnp.where(jnp.asarray(src >= 0)[None, :], cols, 0.0).astype(BF16)


def _ret_tables(pos, chunk, gw):
    hd = gw // N_HEADS
    half = hd // 2
    inv = ROPE_BASE ** (-jnp.arange(half, dtype=F32) / half)
    ang = pos.astype(F32)[:, None] * inv[None, :]
    cos_h = jnp.concatenate([jnp.cos(ang), jnp.cos(ang)], axis=-1)
    sin_h = jnp.concatenate([-jnp.sin(ang), jnp.sin(ang)], axis=-1)
    cos = jnp.tile(cos_h, (1, N_HEADS))
    sin = jnp.tile(sin_h, (1, N_HEADS))
    lg = jnp.log(1.0 - 2.0 ** (-5.0 - jnp.arange(N_HEADS, dtype=F32)))
    lg_l = jnp.repeat(lg, hd)[None, :]
    t = jnp.arange(chunk, dtype=F32)[:, None]
    eb = jnp.exp((t + 1.0) * lg_l)
    ke2s = jnp.exp((chunk - 1.0 - t) * lg_l)
    dend = jnp.exp(chunk * lg_l)
    dt = jnp.arange(chunk, dtype=F32)[:, None] - jnp.arange(chunk, dtype=F32)[None, :]
    gam = jnp.concatenate([jnp.where(dt >= 0, jnp.exp(dt * lg[h]), 0.0) for h in range(N_HEADS)], axis=0)
    return cos, sin, eb, ke2s, dend, gam


def _state_to_bd(state):
    bsz, h, k, v = state.shape
    eye = jnp.eye(h, dtype=state.dtype)
    st = jnp.einsum('bhkv,hg->bhvgk', state, eye)
    return st.reshape(bsz, h * v, h * k)


def _bd_to_state(st, h):
    bsz, hv, hk = st.shape
    st5 = st.reshape(bsz, h, hv // h, h, hk // h)
    diag = jnp.stack([st5[:, i, :, i, :] for i in range(h)], axis=1)
    return jnp.swapaxes(diag, 2, 3)


def kernel(x_prompt, x_sample, state_hgrn, cache_fox_k, cache_fox_v, cache_fox_logf, cache_dsa_k,
           cache_dsa_v, cache_dsa_idx_k, state_ret, meta_tokens, w_in, w_out, fox_bias, hgrn_lb,
           norm_pre, norm_post):
    bsz, seq, d = x_prompt.shape
    dbsz, t_new, _ = x_sample.shape
    depth = w_in.shape[0]
    past = cache_fox_k.shape[2]
    gw = d // N_GROUPS
    hd = gw // N_HEADS
    assert gw == 2 * LANES and seq % ROW_BLOCK == 0 and past % ROW_BLOCK == 0 and t_new <= ROW_BLOCK
    assert t_new % SUBLANES == 0

    pad_front = ROW_BLOCK - N_META
    lp = ROW_BLOCK + seq
    ls = ROW_BLOCK
    lks = past + ROW_BLOCK
    n_chunks_p = seq // CHUNK + 1
    k_top_p = min(TOP_K_MAX, seq // 4)
    k_top_s = min(TOP_K_MAX, (past + t_new) // 4)

    src = _column_layout(gw)
    n_cols = src.shape[0]
    col = {"a": 0, "bq": 4, "bk": 5, "bv": 6, "bg": 7, "cq": 8, "cg": 9, "ciq": 10, "d": 3}
    ckv_col = (11 * gw) // LANES
    misc_col = ckv_col + 1

    sm = jax.nn.softmax(hgrn_lb.astype(F32), axis=0)
    lbs = jnp.cumsum(sm, axis=0) - sm[0:1]

    xp = jnp.concatenate([jnp.zeros((bsz, pad_front, d), F32),
                          jnp.broadcast_to(meta_tokens.astype(F32)[None], (bsz, N_META, d)),
                          x_prompt], axis=1)
    xs = jnp.concatenate([x_sample, jnp.zeros((dbsz, ls - t_new, d), F32)], axis=1)

    tab_p = _ret_tables(jnp.arange(lp) - ROW_BLOCK, CHUNK, gw)
    tab_s = _ret_tables(past + jnp.arange(ls), t_new, gw)
    zero_state_p = jnp.zeros((bsz, gw, gw), F32)

    outs_p = {k: [] for k in ("hgrn", "fk", "fv", "fl", "ck", "cv", "ci", "ret")}
    outs_s = {k: [] for k in ("hgrn", "fk", "fv", "fl", "ck", "cv", "ci", "ret")}

    for l in range(depth):
        w_l = _relayout_w_in(w_in[l], src)
        w_o = w_out[l].astype(BF16)
        g_pre = norm_pre[l][None, :]
        g_post = norm_post[l][None, :]
        lb = lbs[l][None, :]
        bias = jnp.zeros((1, LANES), F32).at[0, BF0:BF0 + FOX_HEADS].set(fox_bias[l].astype(F32))

        p = _project(xp.reshape(bsz * lp, d), g_pre, w_l, _row_tile(bsz * lp)).reshape(bsz, lp, n_cols)
        ya, st_a = _hgrn(p, 0, lb, zero_state_p, CHUNK, n_chunks_p, CHUNK)
        yd, st_d = _ret(p, 3, tab_p, zero_state_p, CHUNK, n_chunks_p, CHUNK)
        lf, cc, cr = _foxprep(p, misc_col, bias, 0)
        yb = _fox(p, col["bq"], p, col["bk"], p, col["bv"], p, col["bg"], cc, cr, pad_front, 0, True)
        yc = _dsa(p, (col["cq"], col["cg"], col["ciq"], misc_col), p, ckv_col, p, misc_col,
                  k_top_p, pad_front, 0, True)
        flat = lambda a: a.reshape(bsz * lp, gw)
        xp = _merge([flat(ya), flat(yb), flat(yc), flat(yd)], xp.reshape(bsz * lp, d), w_o, g_post,
                    _row_tile(bsz * lp), lp, pad_front, lp).reshape(bsz, lp, d)
        pv = p[:, pad_front:, :]
        outs_p["hgrn"].append(_bd_to_state(st_a, N_HEADS))
        outs_p["ret"].append(_bd_to_state(st_d, N_HEADS))
        outs_p["fk"].append(pv[:, :, 5 * gw:6 * gw].reshape(bsz, -1, N_HEADS, hd))
        outs_p["fv"].append(pv[:, :, 6 * gw:7 * gw].reshape(bsz, -1, N_HEADS, hd))
        outs_p["fl"].append(lf[:, pad_front:, BF0:BF0 + FOX_HEADS])
        outs_p["ck"].append(pv[:, :, 11 * gw:11 * gw + hd])
        outs_p["cv"].append(pv[:, :, 11 * gw + hd:11 * gw + 2 * hd])
        outs_p["ci"].append(pv[:, :, misc_col * LANES + IK0:misc_col * LANES + IK0 + D_IDX])

        ps = _project(xs.reshape(dbsz * ls, d), g_pre, w_l, _row_tile(dbsz * ls)).reshape(dbsz, ls, n_cols)
        ya, st_a = _hgrn(ps, 0, lb, _state_to_bd(state_hgrn[l].astype(F32)), t_new, 1, 0)
        yd, st_d = _ret(ps, 3, tab_s, _state_to_bd(state_ret[l].astype(F32)), t_new, 1, 0)
        z = jnp.concatenate(
            [jnp.pad(cache_fox_logf[l].astype(F32), ((0, 0), (0, 0), (BF0, LANES - BF0 - FOX_HEADS))),
             ps[:, :, misc_col * LANES:(misc_col + 1) * LANES]], axis=1)
        lf, cc, cr = _foxprep(z, 0, bias, past)
        k_all = jnp.concatenate([cache_fox_k[l].reshape(dbsz, past, gw), ps[:, :, 5 * gw:6 * gw]], axis=1)
        v_all = jnp.concatenate([cache_fox_v[l].reshape(dbsz, past, gw), ps[:, :, 6 * gw:7 * gw]], axis=1)
        yb = _fox(ps, col["bq"], k_all, 0, v_all, 0, ps, col["bg"], cc, cr, 0, past, False)
        ckv_all = jnp.concatenate(
            [jnp.concatenate([cache_dsa_k[l], cache_dsa_v[l]], axis=-1).astype(F32),
             ps[:, :, 11 * gw:11 * gw + LANES]], axis=1)
        mk_all = jnp.concatenate(
            [jnp.pad(cache_dsa_idx_k[l].astype(F32), ((0, 0), (0, 0), (IK0, LANES - IK0 - D_IDX))),
             ps[:, :, misc_col * LANES:(misc_col + 1) * LANES]], axis=1)
        yc = _dsa(ps, (col["cq"], col["cg"], col["ciq"], misc_col), ckv_all, 0, mk_all, 0,
                  k_top_s, 0, past + t_new, False)
        flat = lambda a: a.reshape(dbsz * ls, gw)
        xs = _merge([flat(ya), flat(yb), flat(yc), flat(yd)], xs.reshape(dbsz * ls, d), w_o, g_post,
                    ls, ls, 0, t_new).reshape(dbsz, ls, d)
        pn = ps[:, :t_new, :]
        outs_s["hgrn"].append(_bd_to_state(st_a, N_HEADS))
        outs_s["ret"].append(_bd_to_state(st_d, N_HEADS))
        outs_s["fk"].append(pn[:, :, 5 * gw:6 * gw].reshape(dbsz, -1, N_HEADS, hd))
        outs_s["fv"].append(pn[:, :, 6 * gw:7 * gw].reshape(dbsz, -1, N_HEADS, hd))
        outs_s["fl"].append(lf[:, past:past + t_new, BF0:BF0 + FOX_HEADS])
        outs_s["ck"].append(pn[:, :, 11 * gw:11 * gw + hd])
        outs_s["cv"].append(pn[:, :, 11 * gw + hd:11 * gw + 2 * hd])
        outs_s["ci"].append(pn[:, :, misc_col * LANES + IK0:misc_col * LANES + IK0 + D_IDX])

    dt = x_prompt.dtype
    st = lambda xs_list: jnp.stack(xs_list, axis=0).astype(dt)
    order = ("hgrn", "fk", "fv", "fl", "ck", "cv", "ci", "ret")
    return ((xp[:, ROW_BLOCK:, :].astype(dt), xs[:, :t_new, :].astype(dt))
            + tuple(st(outs_p[k]) for k in order) + tuple(st(outs_s[k]) for k in order))
```

```python
import functools

import numpy as np
import jax
import jax.numpy as jnp
from jax import lax
from jax.experimental import pallas as pl
from jax.experimental.pallas import tpu as pltpu

F32 = jnp.float32
BF16 = jnp.bfloat16
I32 = jnp.int32

N_META = 16
CHUNK = 64
N_GROUPS = 4
N_HEADS = 4
H_IDX = 8
D_IDX = 32
TOP_K_MAX = 256
ROPE_BASE = 10000.0
EPS = 1e-6
FOX_HEADS = 4

LANES = 128
SUBLANES = 8
ROW_BLOCK = 128
GROUP_W = 2 * LANES
VMEM_LIMIT_BYTES = 56 * 1024 * 1024

IK0 = 0
IW0 = 32
BF0 = 64

NEG = -1e30
M_FLOOR = -1e20
INT_MIN = np.int32(-2 ** 31)
STEP_BLOCKS = 4
KEY_STEP = STEP_BLOCKS * ROW_BLOCK


def _cparams(*sem):
    return pltpu.CompilerParams(dimension_semantics=sem, vmem_limit_bytes=VMEM_LIMIT_BYTES)


def _split3(x):
    h = x.astype(BF16)
    r = x - h.astype(F32)
    m = r.astype(BF16)
    lo = (r - m.astype(F32)).astype(BF16)
    return h, m, lo


def _dot_exact_lhs(a_bf16, x):
    d = lambda y: jnp.dot(a_bf16, y, preferred_element_type=F32)
    h, m, lo = _split3(x)
    return d(h) + d(m) + d(lo)


def _dot_exact_rhs(x, a_bf16):
    d = lambda y: jnp.dot(y, a_bf16, preferred_element_type=F32)
    h, m, lo = _split3(x)
    return d(h) + d(m) + d(lo)


def _dot_nt(a, b):
    return lax.dot_general(a, b, (((1,), (1,)), ((), ())), preferred_element_type=F32)


def _dot_tn(a, b):
    return lax.dot_general(a, b, (((0,), (0,)), ((), ())), preferred_element_type=F32)


def _log2(n):
    assert n > 0 and n & (n - 1) == 0, n
    return n.bit_length() - 1


def _head_masks(width, n_heads):
    lane = lax.broadcasted_iota(I32, (1, width), 1)
    sh = _log2(width // n_heads)
    return [(lax.shift_right_logical(lane, sh) == h).astype(F32) for h in range(n_heads)]


def _block_diag(width, n_heads, value):
    r = lax.broadcasted_iota(I32, (width, width), 0)
    c = lax.broadcasted_iota(I32, (width, width), 1)
    sh = _log2(width // n_heads)
    same = lax.shift_right_logical(r, sh) == lax.shift_right_logical(c, sh)
    return jnp.where(same, value, 0.0).astype(F32)


def _silu(x):
    return x * jax.nn.sigmoid(x)


def _proj_kernel(x_ref, g_ref, w_ref, o_ref, *, col_chunk):
    x = x_ref[...]
    ms = jnp.mean(x * x, axis=-1, keepdims=True)
    xn = ((x * lax.rsqrt(ms + EPS)) * g_ref[...]).astype(BF16)
    for c in range(o_ref.shape[1] // col_chunk):
        cols = slice(c * col_chunk, (c + 1) * col_chunk)
        o_ref[:, cols] = jnp.dot(xn, w_ref[:, cols], preferred_element_type=F32)


def _project(x2d, g, w_bf16, tm):
    rows, d = x2d.shape
    n = w_bf16.shape[1]
    return pl.pallas_call(
        functools.partial(_proj_kernel, col_chunk=1024),
        grid=(rows // tm,),
        in_specs=[pl.BlockSpec((tm, d), lambda i: (i, 0)),
                  pl.BlockSpec((1, d), lambda i: (0, 0)),
                  pl.BlockSpec((d, n), lambda i: (0, 0))],
        out_specs=pl.BlockSpec((tm, n), lambda i: (i, 0)),
        out_shape=jax.ShapeDtypeStruct((rows, n), F32),
        compiler_params=_cparams("parallel"),
        name="proj",
    )(x2d, g, w_bf16)


def _gla_chunk(qa, qe, ka, ke2, v, decay_end, a_mask, st, hm, bd):
    c = qa.shape[0]
    q_stack = jnp.concatenate([qa * hm[h] for h in range(N_HEADS)], axis=0).astype(BF16)
    att = _dot_nt(q_stack, ka.astype(BF16)) * a_mask
    o_stack = jnp.dot(att.astype(BF16), v.astype(BF16), preferred_element_type=F32)
    o_intra = o_stack[0:c] * hm[0]
    for h in range(1, N_HEADS):
        o_intra = o_intra + o_stack[h * c:(h + 1) * c] * hm[h]
    o_inter = _dot_nt(qe.astype(BF16), st.astype(BF16))
    st_new = st * decay_end + _dot_tn(v.astype(BF16), ke2.astype(BF16)) * bd
    return o_inter + o_intra, st_new


def _head_rms_gate(o, gate, bd_mean_bf16):
    h, m, _ = _split3(o * o)
    ms = (jnp.dot(h, bd_mean_bf16, preferred_element_type=F32)
          + jnp.dot(m, bd_mean_bf16, preferred_element_type=F32))
    return (o * lax.rsqrt(ms + EPS)) * _silu(gate)


def _hgrn_kernel(a_ref, lb_ref, st0_ref, y_ref, st_ref, *, chunk, n_chunks, row0):
    w = lb_ref.shape[1]
    hm = _head_masks(w, N_HEADS)
    bd = _block_diag(w, N_HEADS, 1.0)
    bd_mean = _block_diag(w, N_HEADS, 1.0 / (w // N_HEADS)).astype(BF16)
    r = lax.broadcasted_iota(I32, (chunk, chunk), 0)
    s = lax.broadcasted_iota(I32, (chunk, chunk), 1)
    tri = (r >= s).astype(BF16)
    assert chunk & (chunk - 1) == 0
    rs = lax.broadcasted_iota(I32, (N_HEADS * chunk, chunk), 0) & (chunk - 1)
    ss = lax.broadcasted_iota(I32, (N_HEADS * chunk, chunk), 1)
    causal = (rs >= ss).astype(F32)
    lb = lb_ref[...]

    if row0 > 0:
        y_ref[0:row0, :] = jnp.zeros((row0, w), F32)
    tail = row0 + n_chunks * chunk
    if tail < y_ref.shape[0]:
        y_ref[tail:, :] = jnp.zeros((y_ref.shape[0] - tail, w), F32)
    st_ref[...] = st0_ref[...]

    def body(c, carry):
        rows = pl.ds(pl.multiple_of(row0 + c * chunk, chunk), chunk)
        q = a_ref[rows, 0:w]
        f = lb + (1.0 - lb) * jax.nn.sigmoid(a_ref[rows, w:2 * w])
        k = 1.0 - f
        v = a_ref[rows, 2 * w:3 * w]
        gate = a_ref[rows, 3 * w:4 * w]
        b = _dot_exact_lhs(tri, jnp.log(f))
        b_end = b[chunk - 1:chunk, :]
        qe = q * jnp.exp(b)
        o, st_new = _gla_chunk(qe, qe, k * jnp.exp(-b), k * jnp.exp(b_end - b), v,
                               jnp.exp(b_end), causal, st_ref[...], hm, bd)
        st_ref[...] = st_new
        y_ref[rows, :] = _head_rms_gate(o, gate, bd_mean)
        return carry

    lax.fori_loop(0, n_chunks, body, 0)


def _ret_kernel(d_ref, cos_ref, sin_ref, eb_ref, ke2s_ref, dend_ref, gam_ref, st0_ref, y_ref, st_ref,
                *, chunk, n_chunks, row0):
    w = st0_ref.shape[0]
    hd = w // N_HEADS
    hm = _head_masks(w, N_HEADS)
    bd = _block_diag(w, N_HEADS, 1.0)
    bd_mean = _block_diag(w, N_HEADS, 1.0 / hd).astype(BF16)
    lane = lax.broadcasted_iota(I32, (1, w), 1)
    first_half = (lane & (hd - 1)) < (hd // 2)
    eb = eb_ref[...]
    ke2s = ke2s_ref[...]
    dend = dend_ref[...]
    gam = gam_ref[...]

    if row0 > 0:
        y_ref[0:row0, :] = jnp.zeros((row0, w), F32)
    tail = row0 + n_chunks * chunk
    if tail < y_ref.shape[0]:
        y_ref[tail:, :] = jnp.zeros((y_ref.shape[0] - tail, w), F32)
    st_ref[...] = st0_ref[...]

    def rope(x, cos, sin_signed):
        swapped = jnp.where(first_half, pltpu.roll(x, w - hd // 2, 1), pltpu.roll(x, hd // 2, 1))
        return x * cos + swapped * sin_signed

    def body(c, carry):
        rows = pl.ds(pl.multiple_of(row0 + c * chunk, chunk), chunk)
        cos = cos_ref[rows, :]
        sin = sin_ref[rows, :]
        q = rope(d_ref[rows, 0:w], cos, sin)
        k = rope(d_ref[rows, w:2 * w], cos, sin) * (hd ** -0.5)
        v = d_ref[rows, 2 * w:3 * w]
        gate = d_ref[rows, 3 * w:4 * w]
        o, st_new = _gla_chunk(q, q * eb, k, k * ke2s, v, dend, gam, st_ref[...], hm, bd)
        st_ref[...] = st_new
        y_ref[rows, :] = _head_rms_gate(o, gate, bd_mean)
        return carry

    lax.fori_loop(0, n_chunks, body, 0)


def _hgrn(p3, col_block, lb, st0, chunk, n_chunks, row0):
    bsz, lp, _ = p3.shape
    w = lb.shape[1]
    return pl.pallas_call(
        functools.partial(_hgrn_kernel, chunk=chunk, n_chunks=n_chunks, row0=row0),
        grid=(bsz,),
        in_specs=[pl.BlockSpec((None, lp, 4 * w), lambda b: (b, 0, col_block)),
                  pl.BlockSpec((1, w), lambda b: (0, 0)),
                  pl.BlockSpec((None, w, w), lambda b: (b, 0, 0))],
        out_specs=[pl.BlockSpec((None, lp, w), lambda b: (b, 0, 0)),
                   pl.BlockSpec((None, w, w), lambda b: (b, 0, 0))],
        out_shape=[jax.ShapeDtypeStruct((bsz, lp, w), F32),
                   jax.ShapeDtypeStruct((bsz, w, w), F32)],
        compiler_params=_cparams("parallel"),
        name="hgrn",
    )(p3, lb, st0)


def _ret(p3, col_block, tables, st0, chunk, n_chunks, row0):
    bsz, lp, _ = p3.shape
    w = st0.shape[1]
    cos, sin, eb, ke2s, dend, gam = tables
    full = lambda a: pl.BlockSpec(a.shape, lambda b: (0,) * a.ndim)
    return pl.pallas_call(
        functools.partial(_ret_kernel, chunk=chunk, n_chunks=n_chunks, row0=row0),
        grid=(bsz,),
        in_specs=[pl.BlockSpec((None, lp, 4 * w), lambda b: (b, 0, col_block)),
                  full(cos), full(sin), full(eb), full(ke2s), full(dend), full(gam),
                  pl.BlockSpec((None, w, w), lambda b: (b, 0, 0))],
        out_specs=[pl.BlockSpec((None, lp, w), lambda b: (b, 0, 0)),
                   pl.BlockSpec((None, w, w), lambda b: (b, 0, 0))],
        out_shape=[jax.ShapeDtypeStruct((bsz, lp, w), F32),
                   jax.ShapeDtypeStruct((bsz, w, w), F32)],
        compiler_params=_cparams("parallel"),
        name="ret",
    )(p3, cos, sin, eb, ke2s, dend, gam, st0)


def _log_sigmoid(x):
    return -(jnp.maximum(-x, 0.0) + jnp.log(1.0 + jnp.exp(-jnp.abs(x))))


def _foxprep_kernel(z_ref, bias_ref, lf_ref, cc_ref, cr_ref, *, n_pass):
    nblk = z_ref.shape[0] // ROW_BLOCK
    r = lax.broadcasted_iota(I32, (ROW_BLOCK, ROW_BLOCK), 0)
    s = lax.broadcasted_iota(I32, (ROW_BLOCK, ROW_BLOCK), 1)
    tri = (r >= s).astype(BF16)
    bias = bias_ref[...]

    def body(i, carry):
        rows = pl.ds(pl.multiple_of(i * ROW_BLOCK, ROW_BLOCK), ROW_BLOCK)
        z = z_ref[rows, :]
        rowi = i * ROW_BLOCK + lax.broadcasted_iota(I32, (ROW_BLOCK, 1), 0)
        lf = jnp.where(rowi < n_pass, z, _log_sigmoid(z + bias))
        lf_ref[rows, :] = lf
        cs = _dot_exact_lhs(tri, lf) + carry
        cc_ref[rows, :] = cs
        cr_ref[:, rows] = cs.T[BF0:BF0 + SUBLANES, :]
        return cs[ROW_BLOCK - 1:ROW_BLOCK, :]

    lax.fori_loop(0, nblk, body, jnp.zeros((1, LANES), F32))


def _foxprep(z3, z_col, bias, n_pass):
    bsz, lk, _ = z3.shape
    return pl.pallas_call(
        functools.partial(_foxprep_kernel, n_pass=n_pass),
        grid=(bsz,),
        in_specs=[pl.BlockSpec((None, lk, LANES), lambda b: (b, 0, z_col)),
                  pl.BlockSpec((1, LANES), lambda b: (0, 0))],
        out_specs=[pl.BlockSpec((None, lk, LANES), lambda b: (b, 0, 0)),
                   pl.BlockSpec((None, lk, LANES), lambda b: (b, 0, 0)),
                   pl.BlockSpec((None, SUBLANES, lk), lambda b: (b, 0, 0))],
        out_shape=[jax.ShapeDtypeStruct((bsz, lk, LANES), F32),
                   jax.ShapeDtypeStruct((bsz, lk, LANES), F32),
                   jax.ShapeDtypeStruct((bsz, SUBLANES, lk), F32)],
        compiler_params=_cparams("parallel"),
        name="foxprep",
    )(z3, bias)


def _pair_weights(x_t, rows_per_head, n_heads):
    rowh = lax.shift_right_logical(lax.broadcasted_iota(I32, (x_t.shape[0], 1), 0),
                                   _log2(rows_per_head))
    only = lambda h: jnp.where(rowh == h, x_t, 0.0)
    return [jnp.concatenate([only(2 * p), only(2 * p + 1)], axis=1).astype(BF16)
            for p in range(n_heads // 2)]


def _softmax_block(s, m_old, l_old, shift):
    m_new = jnp.maximum(m_old, jnp.max(s, axis=0, keepdims=True) + shift)
    p = jnp.exp(s - (m_new - shift))
    alpha = jnp.exp(m_old - m_new)
    return p, m_new, alpha, alpha * l_old + jnp.sum(p, axis=0, keepdims=True)


def _fox_kernel(q_ref, k_ref, v_ref, g_ref, cc_ref, cr_ref, y_ref, kbf_scr, vt_scr, ck_scr, acc_scr,
                *, key_lo, q_off, causal_blocks):
    qb, w = q_ref.shape
    hd = w // N_HEADS
    nkb_total = k_ref.shape[0] // ROW_BLOCK
    j = pl.program_id(1)
    nkb = jnp.minimum(j + 1, nkb_total) if causal_blocks else nkb_total

    @pl.when(j == 0)
    def _prepare_batch_row():
        def blk(i, carry):
            rows = pl.ds(pl.multiple_of(i * ROW_BLOCK, ROW_BLOCK), ROW_BLOCK)
            kbf_scr[rows, :] = k_ref[rows, :].astype(BF16)
            vt_scr[:, rows] = v_ref[rows, :].T.astype(BF16)
            cs = cc_ref[rows, :]
            for h in range(N_HEADS):
                ck_scr[h, rows, :] = jnp.broadcast_to(cs[:, BF0 + h:BF0 + h + 1], (ROW_BLOCK, qb))
            return carry
        lax.fori_loop(0, nkb_total, blk, 0)
        pad = vt_scr.shape[1] - nkb_total * ROW_BLOCK
        if pad:
            vt_scr[:, nkb_total * ROW_BLOCK:] = jnp.zeros((w, pad), BF16)

    wq = _pair_weights((q_ref[...] * (hd ** -0.5)).T, hd, N_HEADS)
    qcol = pl.ds(pl.multiple_of(j * qb + q_off, ROW_BLOCK), qb)
    cq = [cr_ref[h:h + 1, qcol] for h in range(N_HEADS)]
    qrow = j * qb + q_off + lax.broadcasted_iota(I32, (1, qb), 1)
    sub = lax.broadcasted_iota(I32, (KEY_STEP, 1), 0)
    acc_scr[...] = jnp.zeros(acc_scr.shape, F32)

    def step(i, carry):
        ms, ls = carry
        k0 = pl.multiple_of(i * KEY_STEP, KEY_STEP)
        krows = pl.ds(k0, KEY_STEP)
        kblk = kbf_scr[krows, :]
        kidx = k0 + sub
        ok = (kidx >= key_lo) & (kidx <= qrow)
        new_ms, new_ls = [], []
        for pair in range(N_HEADS // 2):
            s2 = jnp.dot(kblk, wq[pair], preferred_element_type=F32)
            for i2 in range(2):
                h = 2 * pair + i2
                s = jnp.where(ok, s2[:, i2 * qb:(i2 + 1) * qb] - ck_scr[h, krows, :], NEG)
                p, m_new, alpha, l_new = _softmax_block(s, ms[h], ls[h], cq[h])
                new_ms.append(m_new)
                new_ls.append(l_new)
                hrows = slice(h * hd, (h + 1) * hd)
                pv = jnp.dot(vt_scr[hrows, krows], p.astype(BF16), preferred_element_type=F32)
                acc_scr[hrows, :] = acc_scr[hrows, :] * alpha + pv
        return tuple(new_ms), tuple(new_ls)

    init = (tuple(jnp.full((1, qb), M_FLOOR, F32) for _ in range(N_HEADS)),
            tuple(jnp.zeros((1, qb), F32) for _ in range(N_HEADS)))
    n_steps = lax.shift_right_logical(nkb + (STEP_BLOCKS - 1), _log2(STEP_BLOCKS))
    ms, ls = lax.fori_loop(0, n_steps, step, init)
    for h in range(N_HEADS):
        hrows = slice(h * hd, (h + 1) * hd)
        acc_scr[hrows, :] = acc_scr[hrows, :] / jnp.where(ls[h] > 0.0, ls[h], 1.0)
    y_ref[...] = acc_scr[...].T * _silu(g_ref[...])


def _fox(q_arr, q_col, k_arr, k_col, v_arr, v_col, g_arr, g_col, cc, cr, key_lo, q_off, causal_blocks):
    bsz, lq, _ = q_arr.shape
    lk = k_arr.shape[1]
    w = GROUP_W
    nqb = lq // ROW_BLOCK
    key_rows = pl.cdiv(lk, KEY_STEP) * KEY_STEP
    return pl.pallas_call(
        functools.partial(_fox_kernel, key_lo=key_lo, q_off=q_off, causal_blocks=causal_blocks),
        grid=(bsz, nqb),
        in_specs=[pl.BlockSpec((None, ROW_BLOCK, w), lambda b, j: (b, j, q_col)),
                  pl.BlockSpec((None, lk, w), lambda b, j: (b, 0, k_col)),
                  pl.BlockSpec((None, lk, w), lambda b, j: (b, 0, v_col)),
                  pl.BlockSpec((None, ROW_BLOCK, w), lambda b, j: (b, j, g_col)),
                  pl.BlockSpec((None, lk, LANES), lambda b, j: (b, 0, 0)),
                  pl.BlockSpec((None, SUBLANES, lk), lambda b, j: (b, 0, 0))],
        out_specs=pl.BlockSpec((None, ROW_BLOCK, w), lambda b, j: (b, j, 0)),
        out_shape=jax.ShapeDtypeStruct((bsz, lq, w), F32),
        scratch_shapes=[pltpu.VMEM((key_rows, w), BF16),
                        pltpu.VMEM((w, key_rows), BF16),
                        pltpu.VMEM((N_HEADS, key_rows, ROW_BLOCK), F32),
                        pltpu.VMEM((w, ROW_BLOCK), F32)],
        compiler_params=_cparams("parallel", "arbitrary"),
        name="fox",
    )(q_arr, k_arr, v_arr, g_arr, cc, cr)


def _dsa_kernel(cq_ref, cg_ref, ciq_ref, mq_ref, ckv_ref, mk_ref, y_ref,
                key_scr, kvb_scr, vt_scr, mkb_scr, acc_scr,
                *, k_top, key_lo, key_hi, chunk_causal):
    qb, w = cq_ref.shape
    hd = w // N_HEADS
    nkb_total = ckv_ref.shape[0] // ROW_BLOCK
    j = pl.program_id(1)
    nkb = jnp.minimum(j + 1, nkb_total) if chunk_causal else nkb_total
    idx_scale = (H_IDX * D_IDX) ** -0.5

    @pl.when(j == 0)
    def _prepare_batch_row():
        def blk(i, carry):
            rows = pl.ds(pl.multiple_of(i * ROW_BLOCK, ROW_BLOCK), ROW_BLOCK)
            kv = ckv_ref[rows, :]
            kvb_scr[rows, :] = kv.astype(BF16)
            vt_scr[:, rows] = kv.T[hd:2 * hd, :].astype(BF16)
            mkb_scr[rows, :] = mk_ref[rows, :].astype(BF16)
            return carry
        lax.fori_loop(0, nkb_total, blk, 0)
        pad = vt_scr.shape[1] - nkb_total * ROW_BLOCK
        if pad:
            vt_scr[:, nkb_total * ROW_BLOCK:] = jnp.zeros((hd, pad), BF16)

    def pad_rows(x):
        return jnp.concatenate([x, jnp.zeros((LANES - x.shape[0], qb), F32)], axis=0)

    iq_t = ciq_ref[...].T
    iw_t = mq_ref[...].T[IW0:IW0 + H_IDX, :]
    q_t = (cq_ref[...] * (hd ** -0.5)).T
    iq_rhs = [jnp.concatenate([pad_rows(iq_t[(2 * p + i) * D_IDX:(2 * p + i + 1) * D_IDX, :])
                               for i in range(2)], axis=1).astype(BF16) for p in range(H_IDX // 2)]
    q_rhs = [jnp.concatenate([pad_rows(q_t[(2 * p + i) * hd:(2 * p + i + 1) * hd, :])
                              for i in range(2)], axis=1).astype(BF16) for p in range(N_HEADS // 2)]

    qrow = j * qb + lax.broadcasted_iota(I32, (1, qb), 1)
    if chunk_causal:
        hi = (lax.shift_right_logical(qrow, 6) + 1) * CHUNK
    else:
        hi = jnp.full((1, qb), key_hi, I32)
    sub = lax.broadcasted_iota(I32, (KEY_STEP, 1), 0)
    n_steps = lax.shift_right_logical(nkb + (STEP_BLOCKS - 1), _log2(STEP_BLOCKS))

    def score_step(i, carry):
        k0 = pl.multiple_of(i * KEY_STEP, KEY_STEP)
        mk = mkb_scr[pl.ds(k0, KEY_STEP), :]
        acc = jnp.zeros((KEY_STEP, qb), F32)
        for p in range(H_IDX // 2):
            sc2 = jnp.dot(mk, iq_rhs[p], preferred_element_type=F32)
            for i in range(2):
                h = 2 * p + i
                acc = acc + jnp.maximum(sc2[:, i * qb:(i + 1) * qb], 0.0) * iw_t[h:h + 1, :]
        score = acc * idx_scale + 0.0
        kidx = k0 + sub
        adm = (kidx >= key_lo) & (kidx < hi)
        u = pltpu.bitcast(score, I32)
        key = u ^ (lax.shift_right_arithmetic(u, 31) & np.int32(0x7FFFFFFF))
        key_scr[pl.ds(k0, KEY_STEP), :] = jnp.where(adm, key, INT_MIN)
        return carry

    lax.fori_loop(0, n_steps, score_step, 0)

    def count(pred):
        def cb(i, c8):
            k0 = pl.multiple_of(i * KEY_STEP, KEY_STEP)
            ind = jnp.where(pred(key_scr[pl.ds(k0, KEY_STEP), :], k0), 1, 0).astype(I32)
            return c8 + jnp.sum(ind.reshape(KEY_STEP // SUBLANES, SUBLANES, qb), axis=0)
        c8 = lax.fori_loop(0, n_steps, cb, jnp.zeros((SUBLANES, qb), I32))
        return jnp.sum(c8, axis=0, keepdims=True)

    def bisect(n_bits, enough):
        def bit_body(i, ans):
            cand = ans | lax.shift_left(np.int32(1), (n_bits - 1 - i).astype(I32))
            return jnp.where(enough(cand), cand, ans)
        return lax.fori_loop(0, n_bits, bit_body, jnp.zeros((1, qb), I32))

    ans = bisect(32, lambda cand: count(lambda kk, k0: kk >= (cand ^ INT_MIN)) >= k_top)
    thr = ans ^ INT_MIN
    tie = (count(lambda kk, k0: kk >= thr) > k_top) & (thr != INT_MIN)

    @pl.when(jnp.max(jnp.where(tie, 1, 0)) > 0)
    def _break_ties():
        n_rev_bits = _log2(pl.next_power_of_2(key_scr.shape[0]))
        rev_base = np.int32(2 ** n_rev_bits - 1)
        need = k_top - count(lambda kk, k0: kk > thr)
        ans2 = bisect(n_rev_bits, lambda cand: count(
            lambda kk, k0: (kk == thr) & ((rev_base - (k0 + sub)) >= cand)) >= need)

        def demote(i, carry):
            k0 = pl.multiple_of(i * KEY_STEP, KEY_STEP)
            kk = key_scr[pl.ds(k0, KEY_STEP), :]
            lose = tie & (kk == thr) & ((rev_base - (k0 + sub)) < ans2)
            key_scr[pl.ds(k0, KEY_STEP), :] = jnp.where(lose, INT_MIN, kk)
            return carry

        lax.fori_loop(0, n_steps, demote, 0)

    thr_sel = jnp.maximum(thr, INT_MIN + 1)
    acc_scr[...] = jnp.zeros(acc_scr.shape, F32)

    def attend(i, carry):
        ms, ls = carry
        krows = pl.ds(pl.multiple_of(i * KEY_STEP, KEY_STEP), KEY_STEP)
        sel = key_scr[krows, :] >= thr_sel
        kv = kvb_scr[krows, :]
        new_ms, new_ls = [], []
        for pair in range(N_HEADS // 2):
            s2 = jnp.dot(kv, q_rhs[pair], preferred_element_type=F32)
            for i in range(2):
                h = 2 * pair + i
                s = jnp.where(sel, s2[:, i * qb:(i + 1) * qb], NEG)
                p, m_new, alpha, l_new = _softmax_block(s, ms[h], ls[h], 0.0)
                new_ms.append(m_new)
                new_ls.append(l_new)
                hrows = slice(h * hd, (h + 1) * hd)
                pv = jnp.dot(vt_scr[:, krows], p.astype(BF16), preferred_element_type=F32)
                acc_scr[hrows, :] = acc_scr[hrows, :] * alpha + pv
        return tuple(new_ms), tuple(new_ls)

    init = (tuple(jnp.full((1, qb), M_FLOOR, F32) for _ in range(N_HEADS)),
            tuple(jnp.zeros((1, qb), F32) for _ in range(N_HEADS)))
    ms, ls = lax.fori_loop(0, n_steps, attend, init)
    for h in range(N_HEADS):
        hrows = slice(h * hd, (h + 1) * hd)
        acc_scr[hrows, :] = acc_scr[hrows, :] / jnp.where(ls[h] > 0.0, ls[h], 1.0)
    y_ref[...] = acc_scr[...].T * _silu(cg_ref[...])


def _dsa(p3, cols, ckv_arr, ckv_col, mk_arr, mk_col, k_top, key_lo, key_hi, chunk_causal):
    bsz, lq, _ = p3.shape
    lk = ckv_arr.shape[1]
    w = GROUP_W
    cq_col, cg_col, ciq_col, mq_col = cols
    key_rows = pl.cdiv(lk, KEY_STEP) * KEY_STEP
    return pl.pallas_call(
        functools.partial(_dsa_kernel, k_top=k_top, key_lo=key_lo, key_hi=key_hi,
                          chunk_causal=chunk_causal),
        grid=(bsz, lq // ROW_BLOCK),
        in_specs=[pl.BlockSpec((None, ROW_BLOCK, w), lambda b, j: (b, j, cq_col)),
                  pl.BlockSpec((None, ROW_BLOCK, w), lambda b, j: (b, j, cg_col)),
                  pl.BlockSpec((None, ROW_BLOCK, w), lambda b, j: (b, j, ciq_col)),
                  pl.BlockSpec((None, ROW_BLOCK, LANES), lambda b, j: (b, j, mq_col)),
                  pl.BlockSpec((None, lk, LANES), lambda b, j: (b, 0, ckv_col)),
                  pl.BlockSpec((None, lk, LANES), lambda b, j: (b, 0, mk_col))],
        out_specs=pl.BlockSpec((None, ROW_BLOCK, w), lambda b, j: (b, j, 0)),
        out_shape=jax.ShapeDtypeStruct((bsz, lq, w), F32),
        scratch_shapes=[pltpu.VMEM((key_rows, ROW_BLOCK), I32),
                        pltpu.VMEM((key_rows, LANES), BF16),
                        pltpu.VMEM((w // N_HEADS, key_rows), BF16),
                        pltpu.VMEM((key_rows, LANES), BF16),
                        pltpu.VMEM((w, ROW_BLOCK), F32)],
        compiler_params=_cparams("parallel", "arbitrary"),
        name="dsa",
    )(p3, p3, p3, p3, ckv_arr, mk_arr)


def _merge_kernel(ya_ref, yb_ref, yc_ref, yd_ref, x_ref, w_ref, g_ref, o_ref, *, period, valid_lo, valid_hi):
    tm = x_ref.shape[0]
    gw = ya_ref.shape[1]
    acc = jnp.zeros(o_ref.shape, F32)
    for i, y_ref in enumerate((ya_ref, yb_ref, yc_ref, yd_ref)):
        acc = acc + jnp.dot(y_ref[...].astype(BF16), w_ref[i * gw:(i + 1) * gw, :],
                            preferred_element_type=F32)
    ms = jnp.mean(acc * acc, axis=-1, keepdims=True)
    out = x_ref[...] + (acc * lax.rsqrt(ms + EPS)) * g_ref[...]
    r0 = pl.program_id(0) * tm
    local = (r0 - (r0 // period) * period) + lax.broadcasted_iota(I32, (tm, 1), 0)
    local = jnp.where(local >= period, local - period, local)
    valid = (local >= valid_lo) & (local < valid_hi)
    o_ref[...] = jnp.where(valid, out, 0.0)


def _merge(ys, x2d, w_bf16, g, tm, period, valid_lo, valid_hi):
    rows, d = x2d.shape
    assert tm <= period
    gw = ys[0].shape[1]
    yspec = pl.BlockSpec((tm, gw), lambda i: (i, 0))
    return pl.pallas_call(
        functools.partial(_merge_kernel, period=period, valid_lo=valid_lo, valid_hi=valid_hi),
        grid=(rows // tm,),
        in_specs=[yspec, yspec, yspec, yspec,
                  pl.BlockSpec((tm, d), lambda i: (i, 0)),
                  pl.BlockSpec(w_bf16.shape, lambda i: (0, 0)),
                  pl.BlockSpec((1, d), lambda i: (0, 0))],
        out_specs=pl.BlockSpec((tm, d), lambda i: (i, 0)),
        out_shape=jax.ShapeDtypeStruct((rows, d), F32),
        compiler_params=_cparams("parallel"),
        name="merge",
    )(*ys, x2d, w_bf16, g)


def _row_tile(rows):
    return next(t for t in (512, 256, ROW_BLOCK) if rows % t == 0)


def _column_layout(gw):
    sizes = [gw] * 4 + [gw, gw, gw, FOX_HEADS, gw] + [gw, gw // 4, gw // 4, gw, H_IDX * D_IDX, D_IDX, H_IDX] + [gw] * 4
    names = ["aq", "af", "ai", "ag", "bq", "bk", "bv", "bf", "bg",
             "cq", "ck", "cv", "cg", "ciq", "cik", "ciw", "dq", "dk", "dv", "dg"]
    start = dict(zip(names, np.cumsum([0] + sizes[:-1])))
    size = dict(zip(names, sizes))
    src = -np.ones((16 * gw,), np.int64)
    def put(dst, name, off=0):
        src[dst + off:dst + off + size[name]] = np.arange(start[name], start[name] + size[name])
    for i, n in enumerate(["aq", "af", "ai", "ag", "bq", "bk", "bv", "bg", "cq", "cg", "ciq"]):
        put(i * gw, n)
    ckv0 = 11 * gw
    put(ckv0, "ck")
    put(ckv0 + gw // 4, "cv")
    misc0 = ckv0 + LANES
    put(misc0, "cik", IK0)
    put(misc0, "ciw", IW0)
    put(misc0, "bf", BF0)
    for i, n in enumerate(["dq", "dk", "dv", "dg"]):
        put(12 * gw + i * gw, n)
    return src


def _relayout_w_in(w_in_l, src):
    cols = jnp.take(w_in_l, jnp.asarray(np.maximum(src, 0)), axis=1)
    return jnp.where(jnp.asarray(src >= 0)[None, :], cols, 0.0).astype(BF16)


def _ret_tables(pos, chunk, gw):
    hd = gw // N_HEADS
    half = hd // 2
    inv = ROPE_BASE ** (-jnp.arange(half, dtype=F32) / half)
    ang = pos.astype(F32)[:, None] * inv[None, :]
    cos_h = jnp.concatenate([jnp.cos(ang), jnp.cos(ang)], axis=-1)
    sin_h = jnp.concatenate([-jnp.sin(ang), jnp.sin(ang)], axis=-1)
    cos = jnp.tile(cos_h, (1, N_HEADS))
    sin = jnp.tile(sin_h, (1, N_HEADS))
    lg = jnp.log(1.0 - 2.0 ** (-5.0 - jnp.arange(N_HEADS, dtype=F32)))
    lg_l = jnp.repeat(lg, hd)[None, :]
    t = jnp.arange(chunk, dtype=F32)[:, None]
    eb = jnp.exp((t + 1.0) * lg_l)
    ke2s = jnp.exp((chunk - 1.0 - t) * lg_l)
    dend = jnp.exp(chunk * lg_l)
    dt = jnp.arange(chunk, dtype=F32)[:, None] - jnp.arange(chunk, dtype=F32)[None, :]
    gam = jnp.concatenate([jnp.where(dt >= 0, jnp.exp(dt * lg[h]), 0.0) for h in range(N_HEADS)], axis=0)
    return cos, sin, eb, ke2s, dend, gam


def _state_to_bd(state):
    bsz, h, k, v = state.shape
    eye = jnp.eye(h, dtype=state.dtype)
    st = jnp.einsum('bhkv,hg->bhvgk', state, eye)
    return st.reshape(bsz, h * v, h * k)


def _bd_to_state(st, h):
    bsz, hv, hk = st.shape
    st5 = st.reshape(bsz, h, hv // h, h, hk // h)
    diag = jnp.stack([st5[:, i, :, i, :] for i in range(h)], axis=1)
    return jnp.swapaxes(diag, 2, 3)


def kernel(x_prompt, x_sample, state_hgrn, cache_fox_k, cache_fox_v, cache_fox_logf, cache_dsa_k,
           cache_dsa_v, cache_dsa_idx_k, state_ret, meta_tokens, w_in, w_out, fox_bias, hgrn_lb,
           norm_pre, norm_post):
    bsz, seq, d = x_prompt.shape
    dbsz, t_new, _ = x_sample.shape
    depth = w_in.shape[0]
    past = cache_fox_k.shape[2]
    gw = d // N_GROUPS
    hd = gw // N_HEADS
    assert gw == 2 * LANES and seq % ROW_BLOCK == 0 and past % ROW_BLOCK == 0 and t_new <= ROW_BLOCK
    assert t_new % SUBLANES == 0

    pad_front = ROW_BLOCK - N_META
    lp = ROW_BLOCK + seq
    ls = ROW_BLOCK
    lks = past + ROW_BLOCK
    n_chunks_p = seq // CHUNK + 1
    k_top_p = min(TOP_K_MAX, seq // 4)
    k_top_s = min(TOP_K_MAX, (past + t_new) // 4)

    src = _column_layout(gw)
    n_cols = src.shape[0]
    col = {"a": 0, "bq": 4, "bk": 5, "bv": 6, "bg": 7, "cq": 8, "cg": 9, "ciq": 10, "d": 3}
    ckv_col = (11 * gw) // LANES
    misc_col = ckv_col + 1

    sm = jax.nn.softmax(hgrn_lb.astype(F32), axis=0)
    lbs = jnp.cumsum(sm, axis=0) - sm[0:1]

    xp = jnp.concatenate([jnp.zeros((bsz, pad_front, d), F32),
                          jnp.broadcast_to(meta_tokens.astype(F32)[None], (bsz, N_META, d)),
                          x_prompt], axis=1)
    xs = jnp.concatenate([x_sample, jnp.zeros((dbsz, ls - t_new, d), F32)], axis=1)

    tab_p = _ret_tables(jnp.arange(lp) - ROW_BLOCK, CHUNK, gw)
    tab_s = _ret_tables(past + jnp.arange(ls), t_new, gw)
    zero_state_p = jnp.zeros((bsz, gw, gw), F32)

    outs_p = {k: [] for k in ("hgrn", "fk", "fv", "fl", "ck", "cv", "ci", "ret")}
    outs_s = {k: [] for k in ("hgrn", "fk", "fv", "fl", "ck", "cv", "ci", "ret")}

    for l in range(depth):
        w_l = _relayout_w_in(w_in[l], src)
        w_o = w_out[l].astype(BF16)
        g_pre = norm_pre[l][None, :]
        g_post = norm_post[l][None, :]
        lb = lbs[l][None, :]
        bias = jnp.zeros((1, LANES), F32).at[0, BF0:BF0 + FOX_HEADS].set(fox_bias[l].astype(F32))

        p = _project(xp.reshape(bsz * lp, d), g_pre, w_l, _row_tile(bsz * lp)).reshape(bsz, lp, n_cols)
        ya, st_a = _hgrn(p, 0, lb, zero_state_p, CHUNK, n_chunks_p, CHUNK)
        yd, st_d = _ret(p, 3, tab_p, zero_state_p, CHUNK, n_chunks_p, CHUNK)
        lf, cc, cr = _foxprep(p, misc_col, bias, 0)
        yb = _fox(p, col["bq"], p, col["bk"], p, col["bv"], p, col["bg"], cc, cr, pad_front, 0, True)
        yc = _dsa(p, (col["cq"], col["cg"], col["ciq"], misc_col), p, ckv_col, p, misc_col,
                  k_top_p, pad_front, 0, True)
        flat = lambda a: a.reshape(bsz * lp, gw)
        xp = _merge([flat(ya), flat(yb), flat(yc), flat(yd)], xp.reshape(bsz * lp, d), w_o, g_post,
                    _row_tile(bsz * lp), lp, pad_front, lp).reshape(bsz, lp, d)
        pv = p[:, pad_front:, :]
        outs_p["hgrn"].append(_bd_to_state(st_a, N_HEADS))
        outs_p["ret"].append(_bd_to_state(st_d, N_HEADS))
        outs_p["fk"].append(pv[:, :, 5 * gw:6 * gw].reshape(bsz, -1, N_HEADS, hd))
        outs_p["fv"].append(pv[:, :, 6 * gw:7 * gw].reshape(bsz, -1, N_HEADS, hd))
        outs_p["fl"].append(lf[:, pad_front:, BF0:BF0 + FOX_HEADS])
        outs_p["ck"].append(pv[:, :, 11 * gw:11 * gw + hd])
        outs_p["cv"].append(pv[:, :, 11 * gw + hd:11 * gw + 2 * hd])
        outs_p["ci"].append(pv[:, :, misc_col * LANES + IK0:misc_col * LANES + IK0 + D_IDX])

        ps = _project(xs.reshape(dbsz * ls, d), g_pre, w_l, _row_tile(dbsz * ls)).reshape(dbsz, ls, n_cols)
        ya, st_a = _hgrn(ps, 0, lb, _state_to_bd(state_hgrn[l].astype(F32)), t_new, 1, 0)
        yd, st_d = _ret(ps, 3, tab_s, _state_to_bd(state_ret[l].astype(F32)), t_new, 1, 0)
        z = jnp.concatenate(
            [jnp.pad(cache_fox_logf[l].astype(F32), ((0, 0), (0, 0), (BF0, LANES - BF0 - FOX_HEADS))),
             ps[:, :, misc_col * LANES:(misc_col + 1) * LANES]], axis=1)
        lf, cc, cr = _foxprep(z, 0, bias, past)
        k_all = jnp.concatenate([cache_fox_k[l].reshape(dbsz, past, gw), ps[:, :, 5 * gw:6 * gw]], axis=1)
        v_all = jnp.concatenate([cache_fox_v[l].reshape(dbsz, past, gw), ps[:, :, 6 * gw:7 * gw]], axis=1)
        yb = _fox(ps, col["bq"], k_all, 0, v_all, 0, ps, col["bg"], cc, cr, 0, past, False)
        ckv_all = jnp.concatenate(
            [jnp.concatenate([cache_dsa_k[l], cache_dsa_v[l]], axis=-1).astype(F32),
             ps[:, :, 11 * gw:11 * gw + LANES]], axis=1)
        mk_all = jnp.concatenate(
            [jnp.pad(cache_dsa_idx_k[l].astype(F32), ((0, 0), (0, 0), (IK0, LANES - IK0 - D_IDX))),
             ps[:, :, misc_col * LANES:(misc_col + 1) * LANES]], axis=1)
        yc = _dsa(ps, (col["cq"], col["cg"], col["ciq"], misc_col), ckv_all, 0, mk_all, 0,
                  k_top_s, 0, past + t_new, False)
        flat = lambda a: a.reshape(dbsz * ls, gw)
        xs = _merge([flat(ya), flat(yb), flat(yc), flat(yd)], xs.reshape(dbsz * ls, d), w_o, g_post,
                    ls, ls, 0, t_new).reshape(dbsz, ls, d)
        pn = ps[:, :t_new, :]
        outs_s["hgrn"].append(_bd_to_state(st_a, N_HEADS))
        outs_s["ret"].append(_bd_to_state(st_d, N_HEADS))
        outs_s["fk"].append(pn[:, :, 5 * gw:6 * gw].reshape(dbsz, -1, N_HEADS, hd))
        outs_s["fv"].append(pn[:, :, 6 * gw:7 * gw].reshape(dbsz, -1, N_HEADS, hd))
        outs_s["fl"].append(lf[:, past:past + t_new, BF0:BF0 + FOX_HEADS])
        outs_s["ck"].append(pn[:, :, 11 * gw:11 * gw + hd])
        outs_s["cv"].append(pn[:, :, 11 * gw + hd:11 * gw + 2 * hd])
        outs_s["ci"].append(pn[:, :, misc_col * LANES + IK0:misc_col * LANES + IK0 + D_IDX])

    dt = x_prompt.dtype
    st = lambda xs_list: jnp.stack(xs_list, axis=0).astype(dt)
    order = ("hgrn", "fk", "fv", "fl", "ck", "cv", "ci", "ret")
    return ((xp[:, ROW_BLOCK:, :].astype(dt), xs[:, :t_new, :].astype(dt))
            + tuple(st(outs_p[k]) for k in order) + tuple(st(outs_s[k]) for k in order))
```

```python
import functools

import numpy as np
import jax
import jax.numpy as jnp
from jax import lax
from jax.experimental import pallas as pl
from jax.experimental.pallas import tpu as pltpu

F32 = jnp.float32
BF16 = jnp.bfloat16
I32 = jnp.int32
I16 = jnp.int16
HALF16 = 2 ** 15
LOG2E = float(np.log2(np.e))
ONES_ROWS = 16

N_META = 16
CHUNK = 64
N_GROUPS = 4
N_HEADS = 4
H_IDX = 8
D_IDX = 32
TOP_K_MAX = 256
ROPE_BASE = 10000.0
EPS = 1e-6
FOX_HEADS = 4

LANES = 128
SUBLANES = 8
ROW_BLOCK = 128
GROUP_W = 2 * LANES
VMEM_LIMIT_BYTES = 56 * 1024 * 1024

IK0 = 0
IW0 = 32
BF0 = 64

NEG = -1e30
M_FLOOR = -1e20
INT_MIN = np.int32(-2 ** 31)
STEP_BLOCKS = 4
KEY_STEP = STEP_BLOCKS * ROW_BLOCK


def _cparams(*sem):
    return pltpu.CompilerParams(dimension_semantics=sem, vmem_limit_bytes=VMEM_LIMIT_BYTES)


def _split3(x):
    h = x.astype(BF16)
    r = x - h.astype(F32)
    m = r.astype(BF16)
    lo = (r - m.astype(F32)).astype(BF16)
    return h, m, lo


def _dot_exact_lhs(a_bf16, x):
    d = lambda y: jnp.dot(a_bf16, y, preferred_element_type=F32)
    h, m, lo = _split3(x)
    return d(h) + d(m) + d(lo)


def _dot_exact_rhs(x, a_bf16):
    d = lambda y: jnp.dot(y, a_bf16, preferred_element_type=F32)
    h, m, lo = _split3(x)
    return d(h) + d(m) + d(lo)


def _dot_nt(a, b):
    return lax.dot_general(a, b, (((1,), (1,)), ((), ())), preferred_element_type=F32)


def _dot_tn(a, b):
    return lax.dot_general(a, b, (((0,), (0,)), ((), ())), preferred_element_type=F32)


def _log2(n):
    assert n > 0 and n & (n - 1) == 0, n
    return n.bit_length() - 1


def _head_masks(width, n_heads):
    lane = lax.broadcasted_iota(I32, (1, width), 1)
    sh = _log2(width // n_heads)
    return [(lax.shift_right_logical(lane, sh) == h).astype(F32) for h in range(n_heads)]


def _block_diag(width, n_heads, value):
    r = lax.broadcasted_iota(I32, (width, width), 0)
    c = lax.broadcasted_iota(I32, (width, width), 1)
    sh = _log2(width // n_heads)
    same = lax.shift_right_logical(r, sh) == lax.shift_right_logical(c, sh)
    return jnp.where(same, value, 0.0).astype(F32)


def _silu(x):
    return x * jax.nn.sigmoid(x)


def _proj_kernel(x_ref, g_ref, w_ref, o_ref, *, col_chunk):
    x = x_ref[...]
    ms = jnp.mean(x * x, axis=-1, keepdims=True)
    xn = ((x * lax.rsqrt(ms + EPS)) * g_ref[...]).astype(BF16)
    for c in range(o_ref.shape[1] // col_chunk):
        cols = slice(c * col_chunk, (c + 1) * col_chunk)
        o_ref[:, cols] = jnp.dot(xn, w_ref[:, cols], preferred_element_type=F32)


def _project(x2d, g, w_bf16, tm):
    rows, d = x2d.shape
    n = w_bf16.shape[1]
    return pl.pallas_call(
        functools.partial(_proj_kernel, col_chunk=1024),
        grid=(rows // tm,),
        in_specs=[pl.BlockSpec((tm, d), lambda i: (i, 0)),
                  pl.BlockSpec((1, d), lambda i: (0, 0)),
                  pl.BlockSpec((d, n), lambda i: (0, 0))],
        out_specs=pl.BlockSpec((tm, n), lambda i: (i, 0)),
        out_shape=jax.ShapeDtypeStruct((rows, n), F32),
        compiler_params=_cparams("parallel"),
        name="proj",
    )(x2d, g, w_bf16)


def _gla_chunk(qa, qe, ka, ke2, v, decay_end, a_mask, st, hm, bd):
    c = qa.shape[0]
    q_stack = jnp.concatenate([qa * hm[h] for h in range(N_HEADS)], axis=0).astype(BF16)
    att = _dot_nt(q_stack, ka.astype(BF16)) * a_mask
    o_stack = jnp.dot(att.astype(BF16), v.astype(BF16), preferred_element_type=F32)
    o_intra = o_stack[0:c] * hm[0]
    for h in range(1, N_HEADS):
        o_intra = o_intra + o_stack[h * c:(h + 1) * c] * hm[h]
    o_inter = _dot_nt(qe.astype(BF16), st.astype(BF16))
    st_new = st * decay_end + _dot_tn(v.astype(BF16), ke2.astype(BF16)) * bd
    return o_inter + o_intra, st_new


def _head_rms_gate(o, gate, bd_mean_bf16):
    h, m, _ = _split3(o * o)
    ms = (jnp.dot(h, bd_mean_bf16, preferred_element_type=F32)
          + jnp.dot(m, bd_mean_bf16, preferred_element_type=F32))
    return (o * lax.rsqrt(ms + EPS)) * _silu(gate)


def _hgrn_kernel(a_ref, lb_ref, st0_ref, y_ref, st_ref, *, chunk, n_chunks, row0):
    w = lb_ref.shape[1]
    hm = _head_masks(w, N_HEADS)
    bd = _block_diag(w, N_HEADS, 1.0)
    bd_mean = _block_diag(w, N_HEADS, 1.0 / (w // N_HEADS)).astype(BF16)
    r = lax.broadcasted_iota(I32, (chunk, chunk), 0)
    s = lax.broadcasted_iota(I32, (chunk, chunk), 1)
    tri = (r >= s).astype(BF16)
    assert chunk & (chunk - 1) == 0
    rs = lax.broadcasted_iota(I32, (N_HEADS * chunk, chunk), 0) & (chunk - 1)
    ss = lax.broadcasted_iota(I32, (N_HEADS * chunk, chunk), 1)
    causal = (rs >= ss).astype(F32)
    lb = lb_ref[...]

    if row0 > 0:
        y_ref[0:row0, :] = jnp.zeros((row0, w), F32)
    tail = row0 + n_chunks * chunk
    if tail < y_ref.shape[0]:
        y_ref[tail:, :] = jnp.zeros((y_ref.shape[0] - tail, w), F32)
    st_ref[...] = st0_ref[...]

    def body(c, carry):
        rows = pl.ds(pl.multiple_of(row0 + c * chunk, chunk), chunk)
        q = a_ref[rows, 0:w]
        f = lb + (1.0 - lb) * jax.nn.sigmoid(a_ref[rows, w:2 * w])
        k = 1.0 - f
        v = a_ref[rows, 2 * w:3 * w]
        gate = a_ref[rows, 3 * w:4 * w]
        b = _dot_exact_lhs(tri, jnp.log(f))
        b_end = b[chunk - 1:chunk, :]
        qe = q * jnp.exp(b)
        o, st_new = _gla_chunk(qe, qe, k * jnp.exp(-b), k * jnp.exp(b_end - b), v,
                               jnp.exp(b_end), causal, st_ref[...], hm, bd)
        st_ref[...] = st_new
        y_ref[rows, :] = _head_rms_gate(o, gate, bd_mean)
        return carry

    lax.fori_loop(0, n_chunks, body, 0)


def _ret_kernel(d_ref, cos_ref, sin_ref, eb_ref, ke2s_ref, dend_ref, gam_ref, st0_ref, y_ref, st_ref,
                *, chunk, n_chunks, row0):
    w = st0_ref.shape[0]
    hd = w // N_HEADS
    hm = _head_masks(w, N_HEADS)
    bd = _block_diag(w, N_HEADS, 1.0)
    bd_mean = _block_diag(w, N_HEADS, 1.0 / hd).astype(BF16)
    lane = lax.broadcasted_iota(I32, (1, w), 1)
    first_half = (lane & (hd - 1)) < (hd // 2)
    eb = eb_ref[...]
    ke2s = ke2s_ref[...]
    dend = dend_ref[...]
    gam = gam_ref[...]

    if row0 > 0:
        y_ref[0:row0, :] = jnp.zeros((row0, w), F32)
    tail = row0 + n_chunks * chunk
    if tail < y_ref.shape[0]:
        y_ref[tail:, :] = jnp.zeros((y_ref.shape[0] - tail, w), F32)
    st_ref[...] = st0_ref[...]

    def rope(x, cos, sin_signed):
        swapped = jnp.where(first_half, pltpu.roll(x, w - hd // 2, 1), pltpu.roll(x, hd // 2, 1))
        return x * cos + swapped * sin_signed

    def body(c, carry):
        rows = pl.ds(pl.multiple_of(row0 + c * chunk, chunk), chunk)
        cos = cos_ref[rows, :]
        sin = sin_ref[rows, :]
        q = rope(d_ref[rows, 0:w], cos, sin)
        k = rope(d_ref[rows, w:2 * w], cos, sin) * (hd ** -0.5)
        v = d_ref[rows, 2 * w:3 * w]
        gate = d_ref[rows, 3 * w:4 * w]
        o, st_new = _gla_chunk(q, q * eb, k, k * ke2s, v, dend, gam, st_ref[...], hm, bd)
        st_ref[...] = st_new
        y_ref[rows, :] = _head_rms_gate(o, gate, bd_mean)
        return carry

    lax.fori_loop(0, n_chunks, body, 0)


def _hgrn(p3, col_block, lb, st0, chunk, n_chunks, row0):
    bsz, lp, _ = p3.shape
    w = lb.shape[1]
    return pl.pallas_call(
        functools.partial(_hgrn_kernel, chunk=chunk, n_chunks=n_chunks, row0=row0),
        grid=(bsz,),
        in_specs=[pl.BlockSpec((None, lp, 4 * w), lambda b: (b, 0, col_block)),
                  pl.BlockSpec((1, w), lambda b: (0, 0)),
                  pl.BlockSpec((None, w, w), lambda b: (b, 0, 0))],
        out_specs=[pl.BlockSpec((None, lp, w), lambda b: (b, 0, 0)),
                   pl.BlockSpec((None, w, w), lambda b: (b, 0, 0))],
        out_shape=[jax.ShapeDtypeStruct((bsz, lp, w), F32),
                   jax.ShapeDtypeStruct((bsz, w, w), F32)],
        compiler_params=_cparams("parallel"),
        name="hgrn",
    )(p3, lb, st0)


def _ret(p3, col_block, tables, st0, chunk, n_chunks, row0):
    bsz, lp, _ = p3.shape
    w = st0.shape[1]
    cos, sin, eb, ke2s, dend, gam = tables
    full = lambda a: pl.BlockSpec(a.shape, lambda b: (0,) * a.ndim)
    return pl.pallas_call(
        functools.partial(_ret_kernel, chunk=chunk, n_chunks=n_chunks, row0=row0),
        grid=(bsz,),
        in_specs=[pl.BlockSpec((None, lp, 4 * w), lambda b: (b, 0, col_block)),
                  full(cos), full(sin), full(eb), full(ke2s), full(dend), full(gam),
                  pl.BlockSpec((None, w, w), lambda b: (b, 0, 0))],
        out_specs=[pl.BlockSpec((None, lp, w), lambda b: (b, 0, 0)),
                   pl.BlockSpec((None, w, w), lambda b: (b, 0, 0))],
        out_shape=[jax.ShapeDtypeStruct((bsz, lp, w), F32),
                   jax.ShapeDtypeStruct((bsz, w, w), F32)],
        compiler_params=_cparams("parallel"),
        name="ret",
    )(p3, cos, sin, eb, ke2s, dend, gam, st0)


def _log_sigmoid(x):
    return -(jnp.maximum(-x, 0.0) + jnp.log(1.0 + jnp.exp(-jnp.abs(x))))


def _foxprep_kernel(z_ref, bias_ref, lf_ref, cc_ref, cr_ref, *, n_pass):
    nblk = z_ref.shape[0] // ROW_BLOCK
    r = lax.broadcasted_iota(I32, (ROW_BLOCK, ROW_BLOCK), 0)
    s = lax.broadcasted_iota(I32, (ROW_BLOCK, ROW_BLOCK), 1)
    tri = (r >= s).astype(BF16)
    bias = bias_ref[...]

    def body(i, carry):
        rows = pl.ds(pl.multiple_of(i * ROW_BLOCK, ROW_BLOCK), ROW_BLOCK)
        z = z_ref[rows, :]
        rowi = i * ROW_BLOCK + lax.broadcasted_iota(I32, (ROW_BLOCK, 1), 0)
        lf = jnp.where(rowi < n_pass, z, _log_sigmoid(z + bias))
        lf_ref[rows, :] = lf
        cs = _dot_exact_lhs(tri, lf) + carry
        cc_ref[rows, :] = cs
        cr_ref[:, rows] = cs.T[BF0:BF0 + SUBLANES, :]
        return cs[ROW_BLOCK - 1:ROW_BLOCK, :]

    lax.fori_loop(0, nblk, body, jnp.zeros((1, LANES), F32))


def _foxprep(z3, z_col, bias, n_pass):
    bsz, lk, _ = z3.shape
    return pl.pallas_call(
        functools.partial(_foxprep_kernel, n_pass=n_pass),
        grid=(bsz,),
        in_specs=[pl.BlockSpec((None, lk, LANES), lambda b: (b, 0, z_col)),
                  pl.BlockSpec((1, LANES), lambda b: (0, 0))],
        out_specs=[pl.BlockSpec((None, lk, LANES), lambda b: (b, 0, 0)),
                   pl.BlockSpec((None, lk, LANES), lambda b: (b, 0, 0)),
                   pl.BlockSpec((None, SUBLANES, lk), lambda b: (b, 0, 0))],
        out_shape=[jax.ShapeDtypeStruct((bsz, lk, LANES), F32),
                   jax.ShapeDtypeStruct((bsz, lk, LANES), F32),
                   jax.ShapeDtypeStruct((bsz, SUBLANES, lk), F32)],
        compiler_params=_cparams("parallel"),
        name="foxprep",
    )(z3, bias)


def _pair_weights(x_t, rows_per_head, n_heads):
    rowh = lax.shift_right_logical(lax.broadcasted_iota(I32, (x_t.shape[0], 1), 0),
                                   _log2(rows_per_head))
    only = lambda h: jnp.where(rowh == h, x_t, 0.0)
    return [jnp.concatenate([only(2 * p), only(2 * p + 1)], axis=1).astype(BF16)
            for p in range(n_heads // 2)]


def _attend_two_pass(n_steps, logits_fn, value_t_fn, shifts, s_scr, acc_scr, qb, hd):
    fold = lambda x, op: op(x.reshape(KEY_STEP // SUBLANES, SUBLANES, qb), axis=0)
    hv = hd + ONES_ROWS

    def max_step(i, ms):
        tiles = logits_fn(i)
        for h in range(N_HEADS):
            s_scr[i * N_HEADS + h] = tiles[h]
        return tuple(jnp.maximum(ms[h], fold(tiles[h], jnp.max)) for h in range(N_HEADS))

    ms = lax.fori_loop(0, n_steps, max_step,
                       tuple(jnp.full((SUBLANES, qb), M_FLOOR, F32) for _ in range(N_HEADS)))
    m_logit = [jnp.max(ms[h], axis=0, keepdims=True) + shifts[h] for h in range(N_HEADS)]
    acc_scr[...] = jnp.zeros(acc_scr.shape, F32)

    def sum_step(i, carry):
        for h in range(N_HEADS):
            p = jnp.exp2(s_scr[i * N_HEADS + h] - (m_logit[h] - shifts[h]))
            acc_scr[h * hv:(h + 1) * hv, :] += jnp.dot(value_t_fn(h, i), p.astype(BF16),
                                                       preferred_element_type=F32)
        return carry

    lax.fori_loop(0, n_steps, sum_step, 0)
    outs = []
    for h in range(N_HEADS):
        l = acc_scr[h * hv + hd:h * hv + hd + 1, :]
        outs.append(acc_scr[h * hv:h * hv + hd, :] / jnp.where(l > 0.0, l, 1.0))
    return jnp.concatenate(outs, axis=0)


def _fox_kernel(q_ref, k_ref, v_ref, g_ref, cc_ref, cr_ref, y_ref, kbf_scr, vt_scr, ck_scr, s_scr, acc_scr,
                *, key_lo, q_off, causal_blocks):
    qb, w = q_ref.shape
    hd = w // N_HEADS
    hv = hd + ONES_ROWS
    nkb_total = k_ref.shape[0] // ROW_BLOCK
    j = pl.program_id(1)
    nkb = jnp.minimum(j + 1, nkb_total) if causal_blocks else nkb_total

    @pl.when(j == 0)
    def _prepare_batch_row():
        def blk(i, carry):
            rows = pl.ds(pl.multiple_of(i * ROW_BLOCK, ROW_BLOCK), ROW_BLOCK)
            kbf_scr[rows, :] = k_ref[rows, :].astype(BF16)
            v_t = v_ref[rows, :].T.astype(BF16)
            cs = cc_ref[rows, :] * LOG2E
            for h in range(N_HEADS):
                vt_scr[h * hv:h * hv + hd, rows] = v_t[h * hd:(h + 1) * hd, :]
                ck_scr[h, rows, :] = jnp.broadcast_to(cs[:, BF0 + h:BF0 + h + 1], (ROW_BLOCK, qb))
            return carry
        lax.fori_loop(0, nkb_total, blk, 0)
        pad = vt_scr.shape[1] - nkb_total * ROW_BLOCK
        for h in range(N_HEADS):
            if pad:
                vt_scr[h * hv:h * hv + hd, nkb_total * ROW_BLOCK:] = jnp.zeros((hd, pad), BF16)
            vt_scr[h * hv + hd:(h + 1) * hv, :] = jnp.ones((ONES_ROWS, vt_scr.shape[1]), BF16)

    wq = _pair_weights((q_ref[...] * (hd ** -0.5 * LOG2E)).T, hd, N_HEADS)
    qcol = pl.ds(pl.multiple_of(j * qb + q_off, ROW_BLOCK), qb)
    cq = [cr_ref[h:h + 1, qcol] * LOG2E for h in range(N_HEADS)]
    qrow = j * qb + q_off + lax.broadcasted_iota(I32, (1, qb), 1)
    sub = lax.broadcasted_iota(I32, (KEY_STEP, 1), 0)

    def logits(i):
        k0 = pl.multiple_of(i * KEY_STEP, KEY_STEP)
        krows = pl.ds(k0, KEY_STEP)
        kblk = kbf_scr[krows, :]
        kidx = k0 + sub
        ok = (kidx >= key_lo) & (kidx <= qrow)
        tiles = []
        for pair in range(N_HEADS // 2):
            s2 = jnp.dot(kblk, wq[pair], preferred_element_type=F32)
            for i2 in range(2):
                h = 2 * pair + i2
                tiles.append(jnp.where(ok, s2[:, i2 * qb:(i2 + 1) * qb] - ck_scr[h, krows, :], NEG))
        return tiles

    def value_t(h, i):
        return vt_scr[h * hv:(h + 1) * hv, pl.ds(pl.multiple_of(i * KEY_STEP, KEY_STEP), KEY_STEP)]

    n_steps = lax.shift_right_logical(nkb + (STEP_BLOCKS - 1), _log2(STEP_BLOCKS))
    o_t = _attend_two_pass(n_steps, logits, value_t, cq, s_scr, acc_scr, qb, hd)
    y_ref[...] = o_t.T * _silu(g_ref[...])


def _fox(q_arr, q_col, k_arr, k_col, v_arr, v_col, g_arr, g_col, cc, cr, key_lo, q_off, causal_blocks):
    bsz, lq, _ = q_arr.shape
    lk = k_arr.shape[1]
    w = GROUP_W
    nqb = lq // ROW_BLOCK
    key_rows = pl.cdiv(lk, KEY_STEP) * KEY_STEP
    return pl.pallas_call(
        functools.partial(_fox_kernel, key_lo=key_lo, q_off=q_off, causal_blocks=causal_blocks),
        grid=(bsz, nqb),
        in_specs=[pl.BlockSpec((None, ROW_BLOCK, w), lambda b, j: (b, j, q_col)),
                  pl.BlockSpec((None, lk, w), lambda b, j: (b, 0, k_col)),
                  pl.BlockSpec((None, lk, w), lambda b, j: (b, 0, v_col)),
                  pl.BlockSpec((None, ROW_BLOCK, w), lambda b, j: (b, j, g_col)),
                  pl.BlockSpec((None, lk, LANES), lambda b, j: (b, 0, 0)),
                  pl.BlockSpec((None, SUBLANES, lk), lambda b, j: (b, 0, 0))],
        out_specs=pl.BlockSpec((None, ROW_BLOCK, w), lambda b, j: (b, j, 0)),
        out_shape=jax.ShapeDtypeStruct((bsz, lq, w), F32),
        scratch_shapes=[pltpu.VMEM((key_rows, w), BF16),
                        pltpu.VMEM((w + N_HEADS * ONES_ROWS, key_rows), BF16),
                        pltpu.VMEM((N_HEADS, key_rows, ROW_BLOCK), F32),
                        pltpu.VMEM((N_HEADS * key_rows // KEY_STEP, KEY_STEP, ROW_BLOCK), F32),
                        pltpu.VMEM((w + N_HEADS * ONES_ROWS, ROW_BLOCK), F32)],
        compiler_params=_cparams("parallel", "arbitrary"),
        name="fox",
    )(q_arr, k_arr, v_arr, g_arr, cc, cr)


def _dsa_kernel(cq_ref, cg_ref, ciq_ref, mq_ref, ckv_ref, mk_ref, y_ref,
                key_scr, hi_scr, lo_scr, kvb_scr, vt_scr, mkb_scr, s_scr, acc_scr,
                *, k_top, key_lo, key_hi, chunk_causal):
    qb, w = cq_ref.shape
    hd = w // N_HEADS
    nkb_total = ckv_ref.shape[0] // ROW_BLOCK
    j = pl.program_id(1)
    nkb = jnp.minimum(j + 1, nkb_total) if chunk_causal else nkb_total
    idx_scale = (H_IDX * D_IDX) ** -0.5

    @pl.when(j == 0)
    def _prepare_batch_row():
        def blk(i, carry):
            rows = pl.ds(pl.multiple_of(i * ROW_BLOCK, ROW_BLOCK), ROW_BLOCK)
            kv = ckv_ref[rows, :]
            kvb_scr[rows, :] = kv.astype(BF16)
            vt_scr[0:hd, rows] = kv.T[hd:2 * hd, :].astype(BF16)
            mkb_scr[rows, :] = mk_ref[rows, :].astype(BF16)
            return carry
        lax.fori_loop(0, nkb_total, blk, 0)
        pad = vt_scr.shape[1] - nkb_total * ROW_BLOCK
        if pad:
            vt_scr[0:hd, nkb_total * ROW_BLOCK:] = jnp.zeros((hd, pad), BF16)
        vt_scr[hd:, :] = jnp.ones((ONES_ROWS, vt_scr.shape[1]), BF16)

    def pad_rows(x):
        return jnp.concatenate([x, jnp.zeros((LANES - x.shape[0], qb), F32)], axis=0)

    iq_t = ciq_ref[...].T
    iw_t = mq_ref[...].T[IW0:IW0 + H_IDX, :]
    q_t = (cq_ref[...] * (hd ** -0.5 * LOG2E)).T
    iq_rhs = [jnp.concatenate([pad_rows(iq_t[(2 * p + i) * D_IDX:(2 * p + i + 1) * D_IDX, :])
                               for i in range(2)], axis=1).astype(BF16) for p in range(H_IDX // 2)]
    q_rhs = [jnp.concatenate([pad_rows(q_t[(2 * p + i) * hd:(2 * p + i + 1) * hd, :])
                              for i in range(2)], axis=1).astype(BF16) for p in range(N_HEADS // 2)]

    qrow = j * qb + lax.broadcasted_iota(I32, (1, qb), 1)
    if chunk_causal:
        hi = (lax.shift_right_logical(qrow, 6) + 1) * CHUNK
    else:
        hi = jnp.full((1, qb), key_hi, I32)
    sub = lax.broadcasted_iota(I32, (KEY_STEP, 1), 0)
    n_steps = lax.shift_right_logical(nkb + (STEP_BLOCKS - 1), _log2(STEP_BLOCKS))

    def score_step(i, carry):
        k0 = pl.multiple_of(i * KEY_STEP, KEY_STEP)
        mk = mkb_scr[pl.ds(k0, KEY_STEP), :]
        acc = jnp.zeros((KEY_STEP, qb), F32)
        for p in range(H_IDX // 2):
            sc2 = jnp.dot(mk, iq_rhs[p], preferred_element_type=F32)
            for i in range(2):
                h = 2 * p + i
                acc = acc + jnp.maximum(sc2[:, i * qb:(i + 1) * qb], 0.0) * iw_t[h:h + 1, :]
        score = acc * idx_scale + 0.0
        kidx = k0 + sub
        adm = (kidx >= key_lo) & (kidx < hi)
        u = pltpu.bitcast(score, I32)
        key = u ^ (lax.shift_right_arithmetic(u, 31) & np.int32(0x7FFFFFFF))
        key = jnp.where(adm, key, INT_MIN)
        key_scr[pl.ds(k0, KEY_STEP), :] = key
        hi_scr[pl.ds(k0, KEY_STEP), :] = lax.shift_right_arithmetic(key, 16).astype(I16)
        lo_scr[pl.ds(k0, KEY_STEP), :] = ((key & np.int32(0xFFFF)) - HALF16).astype(I16)
        return carry

    lax.fori_loop(0, n_steps, score_step, 0)

    def count16(plane, pred):
        rows16 = 2 * SUBLANES

        def cb(i, c16):
            k0 = pl.multiple_of(i * KEY_STEP, KEY_STEP)
            ind = jnp.where(pred(plane[pl.ds(k0, KEY_STEP), :]), jnp.int16(1), jnp.int16(0))
            parts = [ind[r * rows16:(r + 1) * rows16, :] for r in range(KEY_STEP // rows16)]
            while len(parts) > 1:
                parts = [parts[a] + parts[a + 1] for a in range(0, len(parts), 2)]
            return c16 + parts[0]
        c16 = lax.fori_loop(0, n_steps, cb, jnp.zeros((rows16, qb), I16))
        return jnp.sum(c16.astype(I32), axis=0, keepdims=True)

    def bisect16(plane, need):
        def bit_body(i, ans):
            cand = ans | lax.shift_left(np.int32(1), jnp.int32(15) - i)
            t16 = (cand - HALF16).astype(I16)
            return jnp.where(count16(plane, lambda v: v >= t16) >= need, cand, ans)
        return lax.fori_loop(0, 16, bit_body, jnp.zeros((1, qb), I32)) - HALF16

    def count(pred):
        def cb(i, c8):
            k0 = pl.multiple_of(i * KEY_STEP, KEY_STEP)
            ind = jnp.where(pred(key_scr[pl.ds(k0, KEY_STEP), :], k0), 1, 0).astype(I32)
            return c8 + jnp.sum(ind.reshape(KEY_STEP // SUBLANES, SUBLANES, qb), axis=0)
        c8 = lax.fori_loop(0, n_steps, cb, jnp.zeros((SUBLANES, qb), I32))
        return jnp.sum(c8, axis=0, keepdims=True)

    def bisect(n_bits, enough):
        def bit_body(i, ans):
            cand = ans | lax.shift_left(np.int32(1), jnp.int32(n_bits - 1) - i)
            return jnp.where(enough(cand), cand, ans)
        return lax.fori_loop(0, n_bits, bit_body, jnp.zeros((1, qb), I32))

    thr_hi = bisect16(hi_scr, k_top)
    thr_hi16 = thr_hi.astype(I16)
    need_lo = k_top - count16(hi_scr, lambda v: v > thr_hi16)

    def keep_members(i, carry):
        rows = pl.ds(pl.multiple_of(i * KEY_STEP, KEY_STEP), KEY_STEP)
        lo_scr[rows, :] = jnp.where(hi_scr[rows, :] == thr_hi16, lo_scr[rows, :], jnp.int16(-HALF16))
        return carry

    lax.fori_loop(0, n_steps, keep_members, 0)
    thr_lo = bisect16(lo_scr, need_lo)
    thr = lax.shift_left(thr_hi, np.int32(16)) | (thr_lo + HALF16)
    tie = (count(lambda kk, k0: kk >= thr) > k_top) & (thr != INT_MIN)

    @pl.when(jnp.max(jnp.where(tie, 1, 0)) > 0)
    def _break_ties():
        n_rev_bits = _log2(pl.next_power_of_2(key_scr.shape[0]))
        rev_base = np.int32(2 ** n_rev_bits - 1)
        need = k_top - count(lambda kk, k0: kk > thr)
        ans2 = bisect(n_rev_bits, lambda cand: count(
            lambda kk, k0: (kk == thr) & ((rev_base - (k0 + sub)) >= cand)) >= need)

        def demote(i, carry):
            k0 = pl.multiple_of(i * KEY_STEP, KEY_STEP)
            kk = key_scr[pl.ds(k0, KEY_STEP), :]
            lose = tie & (kk == thr) & ((rev_base - (k0 + sub)) < ans2)
            key_scr[pl.ds(k0, KEY_STEP), :] = jnp.where(lose, INT_MIN, kk)
            return carry

        lax.fori_loop(0, n_steps, demote, 0)

    thr_sel = jnp.maximum(thr, INT_MIN + 1)

    def logits(i):
        krows = pl.ds(pl.multiple_of(i * KEY_STEP, KEY_STEP), KEY_STEP)
        sel = key_scr[krows, :] >= thr_sel
        kv = kvb_scr[krows, :]
        tiles = []
        for pair in range(N_HEADS // 2):
            s2 = jnp.dot(kv, q_rhs[pair], preferred_element_type=F32)
            tiles += [jnp.where(sel, s2[:, i2 * qb:(i2 + 1) * qb], NEG) for i2 in range(2)]
        return tiles

    def value_t(h, i):
        return vt_scr[:, pl.ds(pl.multiple_of(i * KEY_STEP, KEY_STEP), KEY_STEP)]

    no_shift = [jnp.zeros((1, qb), F32)] * N_HEADS
    o_t = _attend_two_pass(n_steps, logits, value_t, no_shift, s_scr, acc_scr, qb, hd)
    y_ref[...] = o_t.T * _silu(cg_ref[...])


def _dsa(p3, cols, ckv_arr, ckv_col, mk_arr, mk_col, k_top, key_lo, key_hi, chunk_causal):
    bsz, lq, _ = p3.shape
    lk = ckv_arr.shape[1]
    w = GROUP_W
    cq_col, cg_col, ciq_col, mq_col = cols
    key_rows = pl.cdiv(lk, KEY_STEP) * KEY_STEP
    return pl.pallas_call(
        functools.partial(_dsa_kernel, k_top=k_top, key_lo=key_lo, key_hi=key_hi,
                          chunk_causal=chunk_causal),
        grid=(bsz, lq // ROW_BLOCK),
        in_specs=[pl.BlockSpec((None, ROW_BLOCK, w), lambda b, j: (b, j, cq_col)),
                  pl.BlockSpec((None, ROW_BLOCK, w), lambda b, j: (b, j, cg_col)),
                  pl.BlockSpec((None, ROW_BLOCK, w), lambda b, j: (b, j, ciq_col)),
                  pl.BlockSpec((None, ROW_BLOCK, LANES), lambda b, j: (b, j, mq_col)),
                  pl.BlockSpec((None, lk, LANES), lambda b, j: (b, 0, ckv_col)),
                  pl.BlockSpec((None, lk, LANES), lambda b, j: (b, 0, mk_col))],
        out_specs=pl.BlockSpec((None, ROW_BLOCK, w), lambda b, j: (b, j, 0)),
        out_shape=jax.ShapeDtypeStruct((bsz, lq, w), F32),
        scratch_shapes=[pltpu.VMEM((key_rows, ROW_BLOCK), I32),
                        pltpu.VMEM((key_rows, ROW_BLOCK), I16),
                        pltpu.VMEM((key_rows, ROW_BLOCK), I16),
                        pltpu.VMEM((key_rows, LANES), BF16),
                        pltpu.VMEM((w // N_HEADS + ONES_ROWS, key_rows), BF16),
                        pltpu.VMEM((key_rows, LANES), BF16),
                        pltpu.VMEM((N_HEADS * key_rows // KEY_STEP, KEY_STEP, ROW_BLOCK), F32),
                        pltpu.VMEM((w + N_HEADS * ONES_ROWS, ROW_BLOCK), F32)],
        compiler_params=_cparams("parallel", "arbitrary"),
        name="dsa",
    )(p3, p3, p3, p3, ckv_arr, mk_arr)


def _merge_kernel(ya_ref, yb_ref, yc_ref, yd_ref, x_ref, w_ref, g_ref, o_ref, *, period, valid_lo, valid_hi):
    tm = x_ref.shape[0]
    gw = ya_ref.shape[1]
    acc = jnp.zeros(o_ref.shape, F32)
    for i, y_ref in enumerate((ya_ref, yb_ref, yc_ref, yd_ref)):
        acc = acc + jnp.dot(y_ref[...].astype(BF16), w_ref[i * gw:(i + 1) * gw, :],
                            preferred_element_type=F32)
    ms = jnp.mean(acc * acc, axis=-1, keepdims=True)
    out = x_ref[...] + (acc * lax.rsqrt(ms + EPS)) * g_ref[...]
    r0 = pl.program_id(0) * tm
    local = (r0 - (r0 // period) * period) + lax.broadcasted_iota(I32, (tm, 1), 0)
    local = jnp.where(local >= period, local - period, local)
    valid = (local >= valid_lo) & (local < valid_hi)
    o_ref[...] = jnp.where(valid, out, 0.0)


def _merge(ys, x2d, w_bf16, g, tm, period, valid_lo, valid_hi):
    rows, d = x2d.shape
    assert tm <= period
    gw = ys[0].shape[1]
    yspec = pl.BlockSpec((tm, gw), lambda i: (i, 0))
    return pl.pallas_call(
        functools.partial(_merge_kernel, period=period, valid_lo=valid_lo, valid_hi=valid_hi),
        grid=(rows // tm,),
        in_specs=[yspec, yspec, yspec, yspec,
                  pl.BlockSpec((tm, d), lambda i: (i, 0)),
                  pl.BlockSpec(w_bf16.shape, lambda i: (0, 0)),
                  pl.BlockSpec((1, d), lambda i: (0, 0))],
        out_specs=pl.BlockSpec((tm, d), lambda i: (i, 0)),
        out_shape=jax.ShapeDtypeStruct((rows, d), F32),
        compiler_params=_cparams("parallel"),
        name="merge",
    )(*ys, x2d, w_bf16, g)


def _row_tile(rows):
    return next(t for t in (512, 256, ROW_BLOCK) if rows % t == 0)


def _column_layout(gw):
    sizes = [gw] * 4 + [gw, gw, gw, FOX_HEADS, gw] + [gw, gw // 4, gw // 4, gw, H_IDX * D_IDX, D_IDX, H_IDX] + [gw] * 4
    names = ["aq", "af", "ai", "ag", "bq", "bk", "bv", "bf", "bg",
             "cq", "ck", "cv", "cg", "ciq", "cik", "ciw", "dq", "dk", "dv", "dg"]
    start = dict(zip(names, np.cumsum([0] + sizes[:-1])))
    size = dict(zip(names, sizes))
    src = -np.ones((16 * gw,), np.int64)
    def put(dst, name, off=0):
        src[dst + off:dst + off + size[name]] = np.arange(start[name], start[name] + size[name])
    for i, n in enumerate(["aq", "af", "ai", "ag", "bq", "bk", "bv", "bg", "cq", "cg", "ciq"]):
        put(i * gw, n)
    ckv0 = 11 * gw
    put(ckv0, "ck")
    put(ckv0 + gw // 4, "cv")
    misc0 = ckv0 + LANES
    put(misc0, "cik", IK0)
    put(misc0, "ciw", IW0)
    put(misc0, "bf", BF0)
    for i, n in enumerate(["dq", "dk", "dv", "dg"]):
        put(12 * gw + i * gw, n)
    return src


def _relayout_w_in(w_in_l, src):
    cols = jnp.take(w_in_l, jnp.asarray(np.maximum(src, 0)), axis=1)
    return jnp.where(jnp.asarray(src >= 0)[None, :], cols, 0.0).astype(BF16)


def _ret_tables(pos, chunk, gw):
    hd = gw // N_HEADS
    half = hd // 2
    inv = ROPE_BASE ** (-jnp.arange(half, dtype=F32) / half)
    ang = pos.astype(F32)[:, None] * inv[None, :]
    cos_h = jnp.concatenate([jnp.cos(ang), jnp.cos(ang)], axis=-1)
    sin_h = jnp.concatenate([-jnp.sin(ang), jnp.sin(ang)], axis=-1)
    cos = jnp.tile(cos_h, (1, N_HEADS))
    sin = jnp.tile(sin_h, (1, N_HEADS))
    lg = jnp.log(1.0 - 2.0 ** (-5.0 - jnp.arange(N_HEADS, dtype=F32)))
    lg_l = jnp.repeat(lg, hd)[None, :]
    t = jnp.arange(chunk, dtype=F32)[:, None]
    eb = jnp.exp((t + 1.0) * lg_l)
    ke2s = jnp.exp((chunk - 1.0 - t) * lg_l)
    dend = jnp.exp(chunk * lg_l)
    dt = jnp.arange(chunk, dtype=F32)[:, None] - jnp.arange(chunk, dtype=F32)[None, :]
    gam = jnp.concatenate([jnp.where(dt >= 0, jnp.exp(dt * lg[h]), 0.0) for h in range(N_HEADS)], axis=0)
    return cos, sin, eb, ke2s, dend, gam


def _state_to_bd(state):
    bsz, h, k, v = state.shape
    eye = jnp.eye(h, dtype=state.dtype)
    st = jnp.einsum('bhkv,hg->bhvgk', state, eye)
    return st.reshape(bsz, h * v, h * k)


def _bd_to_state(st, h):
    bsz, hv, hk = st.shape
    st5 = st.reshape(bsz, h, hv // h, h, hk // h)
    diag = jnp.stack([st5[:, i, :, i, :] for i in range(h)], axis=1)
    return jnp.swapaxes(diag, 2, 3)


def kernel(x_prompt, x_sample, state_hgrn, cache_fox_k, cache_fox_v, cache_fox_logf, cache_dsa_k,
           cache_dsa_v, cache_dsa_idx_k, state_ret, meta_tokens, w_in, w_out, fox_bias, hgrn_lb,
           norm_pre, norm_post):
    bsz, seq, d = x_prompt.shape
    dbsz, t_new, _ = x_sample.shape
    depth = w_in.shape[0]
    past = cache_fox_k.shape[2]
    gw = d // N_GROUPS
    hd = gw // N_HEADS
    assert gw == 2 * LANES and seq % ROW_BLOCK == 0 and past % ROW_BLOCK == 0 and t_new <= ROW_BLOCK
    assert t_new % SUBLANES == 0

    pad_front = ROW_BLOCK - N_META
    lp = ROW_BLOCK + seq
    ls = ROW_BLOCK
    lks = past + ROW_BLOCK
    n_chunks_p = seq // CHUNK + 1
    k_top_p = min(TOP_K_MAX, seq // 4)
    k_top_s = min(TOP_K_MAX, (past + t_new) // 4)

    src = _column_layout(gw)
    n_cols = src.shape[0]
    col = {"a": 0, "bq": 4, "bk": 5, "bv": 6, "bg": 7, "cq": 8, "cg": 9, "ciq": 10, "d": 3}
    ckv_col = (11 * gw) // LANES
    misc_col = ckv_col + 1

    sm = jax.nn.softmax(hgrn_lb.astype(F32), axis=0)
    lbs = jnp.cumsum(sm, axis=0) - sm[0:1]

    xp = jnp.concatenate([jnp.zeros((bsz, pad_front, d), F32),
                          jnp.broadcast_to(meta_tokens.astype(F32)[None], (bsz, N_META, d)),
                          x_prompt], axis=1)
    xs = jnp.concatenate([x_sample, jnp.zeros((dbsz, ls - t_new, d), F32)], axis=1)

    tab_p = _ret_tables(jnp.arange(lp) - ROW_BLOCK, CHUNK, gw)
    tab_s = _ret_tables(past + jnp.arange(ls), t_new, gw)
    zero_state_p = jnp.zeros((bsz, gw, gw), F32)

    outs_p = {k: [] for k in ("hgrn", "fk", "fv", "fl", "ck", "cv", "ci", "ret")}
    outs_s = {k: [] for k in ("hgrn", "fk", "fv", "fl", "ck", "cv", "ci", "ret")}

    for l in range(depth):
        w_l = _relayout_w_in(w_in[l], src)
        w_o = w_out[l].astype(BF16)
        g_pre = norm_pre[l][None, :]
        g_post = norm_post[l][None, :]
        lb = lbs[l][None, :]
        bias = jnp.zeros((1, LANES), F32).at[0, BF0:BF0 + FOX_HEADS].set(fox_bias[l].astype(F32))

        p = _project(xp.reshape(bsz * lp, d), g_pre, w_l, _row_tile(bsz * lp)).reshape(bsz, lp, n_cols)
        ya, st_a = _hgrn(p, 0, lb, zero_state_p, CHUNK, n_chunks_p, CHUNK)
        yd, st_d = _ret(p, 3, tab_p, zero_state_p, CHUNK, n_chunks_p, CHUNK)
        lf, cc, cr = _foxprep(p, misc_col, bias, 0)
        yb = _fox(p, col["bq"], p, col["bk"], p, col["bv"], p, col["bg"], cc, cr, pad_front, 0, True)
        yc = _dsa(p, (col["cq"], col["cg"], col["ciq"], misc_col), p, ckv_col, p, misc_col,
                  k_top_p, pad_front, 0, True)
        flat = lambda a: a.reshape(bsz * lp, gw)
        xp = _merge([flat(ya), flat(yb), flat(yc), flat(yd)], xp.reshape(bsz * lp, d), w_o, g_post,
                    _row_tile(bsz * lp), lp, pad_front, lp).reshape(bsz, lp, d)
        pv = p[:, pad_front:, :]
        outs_p["hgrn"].append(_bd_to_state(st_a, N_HEADS))
        outs_p["ret"].append(_bd_to_state(st_d, N_HEADS))
        outs_p["fk"].append(pv[:, :, 5 * gw:6 * gw].reshape(bsz, -1, N_HEADS, hd))
        outs_p["fv"].append(pv[:, :, 6 * gw:7 * gw].reshape(bsz, -1, N_HEADS, hd))
        outs_p["fl"].append(lf[:, pad_front:, BF0:BF0 + FOX_HEADS])
        outs_p["ck"].append(pv[:, :, 11 * gw:11 * gw + hd])
        outs_p["cv"].append(pv[:, :, 11 * gw + hd:11 * gw + 2 * hd])
        outs_p["ci"].append(pv[:, :, misc_col * LANES + IK0:misc_col * LANES + IK0 + D_IDX])

        ps = _project(xs.reshape(dbsz * ls, d), g_pre, w_l, _row_tile(dbsz * ls)).reshape(dbsz, ls, n_cols)
        ya, st_a = _hgrn(ps, 0, lb, _state_to_bd(state_hgrn[l].astype(F32)), t_new, 1, 0)
        yd, st_d = _ret(ps, 3, tab_s, _state_to_bd(state_ret[l].astype(F32)), t_new, 1, 0)
        z = jnp.concatenate(
            [jnp.pad(cache_fox_logf[l].astype(F32), ((0, 0), (0, 0), (BF0, LANES - BF0 - FOX_HEADS))),
             ps[:, :, misc_col * LANES:(misc_col + 1) * LANES]], axis=1)
        lf, cc, cr = _foxprep(z, 0, bias, past)
        k_all = jnp.concatenate([cache_fox_k[l].reshape(dbsz, past, gw), ps[:, :, 5 * gw:6 * gw]], axis=1)
        v_all = jnp.concatenate([cache_fox_v[l].reshape(dbsz, past, gw), ps[:, :, 6 * gw:7 * gw]], axis=1)
        yb = _fox(ps, col["bq"], k_all, 0, v_all, 0, ps, col["bg"], cc, cr, 0, past, False)
        ckv_all = jnp.concatenate(
            [jnp.concatenate([cache_dsa_k[l], cache_dsa_v[l]], axis=-1).astype(F32),
             ps[:, :, 11 * gw:11 * gw + LANES]], axis=1)
        mk_all = jnp.concatenate(
            [jnp.pad(cache_dsa_idx_k[l].astype(F32), ((0, 0), (0, 0), (IK0, LANES - IK0 - D_IDX))),
             ps[:, :, misc_col * LANES:(misc_col + 1) * LANES]], axis=1)
        yc = _dsa(ps, (col["cq"], col["cg"], col["ciq"], misc_col), ckv_all, 0, mk_all, 0,
                  k_top_s, 0, past + t_new, False)
        flat = lambda a: a.reshape(dbsz * ls, gw)
        xs = _merge([flat(ya), flat(yb), flat(yc), flat(yd)], xs.reshape(dbsz * ls, d), w_o, g_post,
                    ls, ls, 0, t_new).reshape(dbsz, ls, d)
        pn = ps[:, :t_new, :]
        outs_s["hgrn"].append(_bd_to_state(st_a, N_HEADS))
        outs_s["ret"].append(_bd_to_state(st_d, N_HEADS))
        outs_s["fk"].append(pn[:, :, 5 * gw:6 * gw].reshape(dbsz, -1, N_HEADS, hd))
        outs_s["fv"].append(pn[:, :, 6 * gw:7 * gw].reshape(dbsz, -1, N_HEADS, hd))
        outs_s["fl"].append(lf[:, past:past + t_new, BF0:BF0 + FOX_HEADS])
        outs_s["ck"].append(pn[:, :, 11 * gw:11 * gw + hd])
        outs_s["cv"].append(pn[:, :, 11 * gw + hd:11 * gw + 2 * hd])
        outs_s["ci"].append(pn[:, :, misc_col * LANES + IK0:misc_col * LANES + IK0 + D_IDX])

    dt = x_prompt.dtype
    st = lambda xs_list: jnp.stack(xs_list, axis=0).astype(dt)
    order = ("hgrn", "fk", "fv", "fl", "ck", "cv", "ci", "ret")
    return ((xp[:, ROW_BLOCK:, :].astype(dt), xs[:, :t_new, :].astype(dt))
            + tuple(st(outs_p[k]) for k in order) + tuple(st(outs_s[k]) for k in order))
```

```python
import functools

import numpy as np
import jax
import jax.numpy as jnp
from jax import lax
from jax.experimental import pallas as pl
from jax.experimental.pallas import tpu as pltpu

F32 = jnp.float32
BF16 = jnp.bfloat16
I32 = jnp.int32
LOG2E = float(np.log2(np.e))
ONES_ROWS = 16

N_META = 16
CHUNK = 64
N_GROUPS = 4
N_HEADS = 4
H_IDX = 8
D_IDX = 32
TOP_K_MAX = 256
ROPE_BASE = 10000.0
EPS = 1e-6
FOX_HEADS = 4

LANES = 128
SUBLANES = 8
ROW_BLOCK = 128
GROUP_W = 2 * LANES
VMEM_LIMIT_BYTES = 56 * 1024 * 1024

IK0 = 0
IW0 = 32
BF0 = 64

NEG = -1e30
M_FLOOR = -1e20
INT_MIN = np.int32(-2 ** 31)
STEP_BLOCKS = 4
KEY_STEP = STEP_BLOCKS * ROW_BLOCK


def _cparams(*sem):
    return pltpu.CompilerParams(dimension_semantics=sem, vmem_limit_bytes=VMEM_LIMIT_BYTES)


def _split3(x):
    h = x.astype(BF16)
    r = x - h.astype(F32)
    m = r.astype(BF16)
    lo = (r - m.astype(F32)).astype(BF16)
    return h, m, lo


def _dot_exact_lhs(a_bf16, x):
    d = lambda y: jnp.dot(a_bf16, y, preferred_element_type=F32)
    h, m, lo = _split3(x)
    return d(h) + d(m) + d(lo)


def _dot_exact_rhs(x, a_bf16):
    d = lambda y: jnp.dot(y, a_bf16, preferred_element_type=F32)
    h, m, lo = _split3(x)
    return d(h) + d(m) + d(lo)


def _dot_nt(a, b):
    return lax.dot_general(a, b, (((1,), (1,)), ((), ())), preferred_element_type=F32)


def _dot_tn(a, b):
    return lax.dot_general(a, b, (((0,), (0,)), ((), ())), preferred_element_type=F32)


def _log2(n):
    assert n > 0 and n & (n - 1) == 0, n
    return n.bit_length() - 1


def _head_masks(width, n_heads):
    lane = lax.broadcasted_iota(I32, (1, width), 1)
    sh = _log2(width // n_heads)
    return [(lax.shift_right_logical(lane, sh) == h).astype(F32) for h in range(n_heads)]


def _block_diag(width, n_heads, value):
    r = lax.broadcasted_iota(I32, (width, width), 0)
    c = lax.broadcasted_iota(I32, (width, width), 1)
    sh = _log2(width // n_heads)
    same = lax.shift_right_logical(r, sh) == lax.shift_right_logical(c, sh)
    return jnp.where(same, value, 0.0).astype(F32)


def _silu(x):
    return x * jax.nn.sigmoid(x)


def _proj_kernel(x_ref, g_ref, w_ref, o_ref, *, col_chunk):
    x = x_ref[...]
    ms = jnp.mean(x * x, axis=-1, keepdims=True)
    xn = ((x * lax.rsqrt(ms + EPS)) * g_ref[...]).astype(BF16)
    for c in range(o_ref.shape[1] // col_chunk):
        cols = slice(c * col_chunk, (c + 1) * col_chunk)
        o_ref[:, cols] = jnp.dot(xn, w_ref[:, cols], preferred_element_type=F32)


def _project(x2d, g, w_bf16, tm):
    rows, d = x2d.shape
    n = w_bf16.shape[1]
    return pl.pallas_call(
        functools.partial(_proj_kernel, col_chunk=1024),
        grid=(rows // tm,),
        in_specs=[pl.BlockSpec((tm, d), lambda i: (i, 0)),
                  pl.BlockSpec((1, d), lambda i: (0, 0)),
                  pl.BlockSpec((d, n), lambda i: (0, 0))],
        out_specs=pl.BlockSpec((tm, n), lambda i: (i, 0)),
        out_shape=jax.ShapeDtypeStruct((rows, n), F32),
        compiler_params=_cparams("parallel"),
        name="proj",
    )(x2d, g, w_bf16)


def _gla_chunk(qa, qe, ka, ke2, v, decay_end, a_mask, st, hm, bd):
    c = qa.shape[0]
    q_stack = jnp.concatenate([qa * hm[h] for h in range(N_HEADS)], axis=0).astype(BF16)
    att = _dot_nt(q_stack, ka.astype(BF16)) * a_mask
    o_stack = jnp.dot(att.astype(BF16), v.astype(BF16), preferred_element_type=F32)
    o_intra = o_stack[0:c] * hm[0]
    for h in range(1, N_HEADS):
        o_intra = o_intra + o_stack[h * c:(h + 1) * c] * hm[h]
    o_inter = _dot_nt(qe.astype(BF16), st.astype(BF16))
    st_new = st * decay_end + _dot_tn(v.astype(BF16), ke2.astype(BF16)) * bd
    return o_inter + o_intra, st_new


def _head_rms_gate(o, gate, bd_mean_bf16):
    h, m, _ = _split3(o * o)
    ms = (jnp.dot(h, bd_mean_bf16, preferred_element_type=F32)
          + jnp.dot(m, bd_mean_bf16, preferred_element_type=F32))
    return (o * lax.rsqrt(ms + EPS)) * _silu(gate)


def _hgrn_kernel(a_ref, lb_ref, st0_ref, y_ref, st_ref, *, chunk, n_chunks, row0):
    w = lb_ref.shape[1]
    hm = _head_masks(w, N_HEADS)
    bd = _block_diag(w, N_HEADS, 1.0)
    bd_mean = _block_diag(w, N_HEADS, 1.0 / (w // N_HEADS)).astype(BF16)
    r = lax.broadcasted_iota(I32, (chunk, chunk), 0)
    s = lax.broadcasted_iota(I32, (chunk, chunk), 1)
    tri = (r >= s).astype(BF16)
    assert chunk & (chunk - 1) == 0
    rs = lax.broadcasted_iota(I32, (N_HEADS * chunk, chunk), 0) & (chunk - 1)
    ss = lax.broadcasted_iota(I32, (N_HEADS * chunk, chunk), 1)
    causal = (rs >= ss).astype(F32)
    lb = lb_ref[...]

    if row0 > 0:
        y_ref[0:row0, :] = jnp.zeros((row0, w), F32)
    tail = row0 + n_chunks * chunk
    if tail < y_ref.shape[0]:
        y_ref[tail:, :] = jnp.zeros((y_ref.shape[0] - tail, w), F32)
    st_ref[...] = st0_ref[...]

    def body(c, carry):
        rows = pl.ds(pl.multiple_of(row0 + c * chunk, chunk), chunk)
        q = a_ref[rows, 0:w]
        f = lb + (1.0 - lb) * jax.nn.sigmoid(a_ref[rows, w:2 * w])
        k = 1.0 - f
        v = a_ref[rows, 2 * w:3 * w]
        gate = a_ref[rows, 3 * w:4 * w]
        b = _dot_exact_lhs(tri, jnp.log(f))
        b_end = b[chunk - 1:chunk, :]
        qe = q * jnp.exp(b)
        o, st_new = _gla_chunk(qe, qe, k * jnp.exp(-b), k * jnp.exp(b_end - b), v,
                               jnp.exp(b_end), causal, st_ref[...], hm, bd)
        st_ref[...] = st_new
        y_ref[rows, :] = _head_rms_gate(o, gate, bd_mean)
        return carry

    lax.fori_loop(0, n_chunks, body, 0)


def _ret_kernel(d_ref, cos_ref, sin_ref, eb_ref, ke2s_ref, dend_ref, gam_ref, st0_ref, y_ref, st_ref,
                *, chunk, n_chunks, row0):
    w = st0_ref.shape[0]
    hd = w // N_HEADS
    hm = _head_masks(w, N_HEADS)
    bd = _block_diag(w, N_HEADS, 1.0)
    bd_mean = _block_diag(w, N_HEADS, 1.0 / hd).astype(BF16)
    lane = lax.broadcasted_iota(I32, (1, w), 1)
    first_half = (lane & (hd - 1)) < (hd // 2)
    eb = eb_ref[...]
    ke2s = ke2s_ref[...]
    dend = dend_ref[...]
    gam = gam_ref[...]

    if row0 > 0:
        y_ref[0:row0, :] = jnp.zeros((row0, w), F32)
    tail = row0 + n_chunks * chunk
    if tail < y_ref.shape[0]:
        y_ref[tail:, :] = jnp.zeros((y_ref.shape[0] - tail, w), F32)
    st_ref[...] = st0_ref[...]

    def rope(x, cos, sin_signed):
        swapped = jnp.where(first_half, pltpu.roll(x, w - hd // 2, 1), pltpu.roll(x, hd // 2, 1))
        return x * cos + swapped * sin_signed

    def body(c, carry):
        rows = pl.ds(pl.multiple_of(row0 + c * chunk, chunk), chunk)
        cos = cos_ref[rows, :]
        sin = sin_ref[rows, :]
        q = rope(d_ref[rows, 0:w], cos, sin)
        k = rope(d_ref[rows, w:2 * w], cos, sin) * (hd ** -0.5)
        v = d_ref[rows, 2 * w:3 * w]
        gate = d_ref[rows, 3 * w:4 * w]
        o, st_new = _gla_chunk(q, q * eb, k, k * ke2s, v, dend, gam, st_ref[...], hm, bd)
        st_ref[...] = st_new
        y_ref[rows, :] = _head_rms_gate(o, gate, bd_mean)
        return carry

    lax.fori_loop(0, n_chunks, body, 0)


def _hgrn(p3, col_block, lb, st0, chunk, n_chunks, row0):
    bsz, lp, _ = p3.shape
    w = lb.shape[1]
    return pl.pallas_call(
        functools.partial(_hgrn_kernel, chunk=chunk, n_chunks=n_chunks, row0=row0),
        grid=(bsz,),
        in_specs=[pl.BlockSpec((None, lp, 4 * w), lambda b: (b, 0, col_block)),
                  pl.BlockSpec((1, w), lambda b: (0, 0)),
                  pl.BlockSpec((None, w, w), lambda b: (b, 0, 0))],
        out_specs=[pl.BlockSpec((None, lp, w), lambda b: (b, 0, 0)),
                   pl.BlockSpec((None, w, w), lambda b: (b, 0, 0))],
        out_shape=[jax.ShapeDtypeStruct((bsz, lp, w), F32),
                   jax.ShapeDtypeStruct((bsz, w, w), F32)],
        compiler_params=_cparams("parallel"),
        name="hgrn",
    )(p3, lb, st0)


def _ret(p3, col_block, tables, st0, chunk, n_chunks, row0):
    bsz, lp, _ = p3.shape
    w = st0.shape[1]
    cos, sin, eb, ke2s, dend, gam = tables
    full = lambda a: pl.BlockSpec(a.shape, lambda b: (0,) * a.ndim)
    return pl.pallas_call(
        functools.partial(_ret_kernel, chunk=chunk, n_chunks=n_chunks, row0=row0),
        grid=(bsz,),
        in_specs=[pl.BlockSpec((None, lp, 4 * w), lambda b: (b, 0, col_block)),
                  full(cos), full(sin), full(eb), full(ke2s), full(dend), full(gam),
                  pl.BlockSpec((None, w, w), lambda b: (b, 0, 0))],
        out_specs=[pl.BlockSpec((None, lp, w), lambda b: (b, 0, 0)),
                   pl.BlockSpec((None, w, w), lambda b: (b, 0, 0))],
        out_shape=[jax.ShapeDtypeStruct((bsz, lp, w), F32),
                   jax.ShapeDtypeStruct((bsz, w, w), F32)],
        compiler_params=_cparams("parallel"),
        name="ret",
    )(p3, cos, sin, eb, ke2s, dend, gam, st0)


def _log_sigmoid(x):
    return -(jnp.maximum(-x, 0.0) + jnp.log(1.0 + jnp.exp(-jnp.abs(x))))


def _foxprep_kernel(z_ref, bias_ref, lf_ref, cc_ref, cr_ref, *, n_pass):
    nblk = z_ref.shape[0] // ROW_BLOCK
    r = lax.broadcasted_iota(I32, (ROW_BLOCK, ROW_BLOCK), 0)
    s = lax.broadcasted_iota(I32, (ROW_BLOCK, ROW_BLOCK), 1)
    tri = (r >= s).astype(BF16)
    bias = bias_ref[...]

    def body(i, carry):
        rows = pl.ds(pl.multiple_of(i * ROW_BLOCK, ROW_BLOCK), ROW_BLOCK)
        z = z_ref[rows, :]
        rowi = i * ROW_BLOCK + lax.broadcasted_iota(I32, (ROW_BLOCK, 1), 0)
        lf = jnp.where(rowi < n_pass, z, _log_sigmoid(z + bias))
        lf_ref[rows, :] = lf
        cs = _dot_exact_lhs(tri, lf) + carry
        cc_ref[rows, :] = cs
        cr_ref[:, rows] = cs.T[BF0:BF0 + SUBLANES, :]
        return cs[ROW_BLOCK - 1:ROW_BLOCK, :]

    lax.fori_loop(0, nblk, body, jnp.zeros((1, LANES), F32))


def _foxprep(z3, z_col, bias, n_pass):
    bsz, lk, _ = z3.shape
    return pl.pallas_call(
        functools.partial(_foxprep_kernel, n_pass=n_pass),
        grid=(bsz,),
        in_specs=[pl.BlockSpec((None, lk, LANES), lambda b: (b, 0, z_col)),
                  pl.BlockSpec((1, LANES), lambda b: (0, 0))],
        out_specs=[pl.BlockSpec((None, lk, LANES), lambda b: (b, 0, 0)),
                   pl.BlockSpec((None, lk, LANES), lambda b: (b, 0, 0)),
                   pl.BlockSpec((None, SUBLANES, lk), lambda b: (b, 0, 0))],
        out_shape=[jax.ShapeDtypeStruct((bsz, lk, LANES), F32),
                   jax.ShapeDtypeStruct((bsz, lk, LANES), F32),
                   jax.ShapeDtypeStruct((bsz, SUBLANES, lk), F32)],
        compiler_params=_cparams("parallel"),
        name="foxprep",
    )(z3, bias)


def _pair_weights(x_t, rows_per_head, n_heads):
    rowh = lax.shift_right_logical(lax.broadcasted_iota(I32, (x_t.shape[0], 1), 0),
                                   _log2(rows_per_head))
    only = lambda h: jnp.where(rowh == h, x_t, 0.0)
    return [jnp.concatenate([only(2 * p), only(2 * p + 1)], axis=1).astype(BF16)
            for p in range(n_heads // 2)]


def _attend_two_pass(n_steps, logits_fn, value_t_fn, shifts, s_scr, acc_scr, qb, hd):
    fold = lambda x, op: op(x.reshape(KEY_STEP // SUBLANES, SUBLANES, qb), axis=0)
    hv = hd + ONES_ROWS

    def max_step(i, ms):
        tiles = logits_fn(i)
        for h in range(N_HEADS):
            s_scr[i * N_HEADS + h] = tiles[h]
        return tuple(jnp.maximum(ms[h], fold(tiles[h], jnp.max)) for h in range(N_HEADS))

    ms = lax.fori_loop(0, n_steps, max_step,
                       tuple(jnp.full((SUBLANES, qb), M_FLOOR, F32) for _ in range(N_HEADS)))
    m_logit = [jnp.max(ms[h], axis=0, keepdims=True) + shifts[h] for h in range(N_HEADS)]
    acc_scr[...] = jnp.zeros(acc_scr.shape, F32)

    def sum_step(i, carry):
        for h in range(N_HEADS):
            p = jnp.exp2(s_scr[i * N_HEADS + h] - (m_logit[h] - shifts[h]))
            acc_scr[h * hv:(h + 1) * hv, :] += jnp.dot(value_t_fn(h, i), p.astype(BF16),
                                                       preferred_element_type=F32)
        return carry

    lax.fori_loop(0, n_steps, sum_step, 0)
    outs = []
    for h in range(N_HEADS):
        l = acc_scr[h * hv + hd:h * hv + hd + 1, :]
        outs.append(acc_scr[h * hv:h * hv + hd, :] / jnp.where(l > 0.0, l, 1.0))
    return jnp.concatenate(outs, axis=0)


def _fox_kernel(q_ref, k_ref, v_ref, g_ref, cc_ref, cr_ref, y_ref, kbf_scr, vt_scr, ck_scr, s_scr, acc_scr,
                *, key_lo, q_off, causal_blocks):
    qb, w = q_ref.shape
    hd = w // N_HEADS
    hv = hd + ONES_ROWS
    nkb_total = k_ref.shape[0] // ROW_BLOCK
    j = pl.program_id(1)
    nkb = jnp.minimum(j + 1, nkb_total) if causal_blocks else nkb_total

    @pl.when(j == 0)
    def _prepare_batch_row():
        def blk(i, carry):
            rows = pl.ds(pl.multiple_of(i * ROW_BLOCK, ROW_BLOCK), ROW_BLOCK)
            kbf_scr[rows, :] = k_ref[rows, :].astype(BF16)
            v_t = v_ref[rows, :].T.astype(BF16)
            cs = cc_ref[rows, :] * LOG2E
            for h in range(N_HEADS):
                vt_scr[h * hv:h * hv + hd, rows] = v_t[h * hd:(h + 1) * hd, :]
                ck_scr[h, rows, :] = jnp.broadcast_to(cs[:, BF0 + h:BF0 + h + 1], (ROW_BLOCK, qb))
            return carry
        lax.fori_loop(0, nkb_total, blk, 0)
        pad = vt_scr.shape[1] - nkb_total * ROW_BLOCK
        for h in range(N_HEADS):
            if pad:
                vt_scr[h * hv:h * hv + hd, nkb_total * ROW_BLOCK:] = jnp.zeros((hd, pad), BF16)
            vt_scr[h * hv + hd:(h + 1) * hv, :] = jnp.ones((ONES_ROWS, vt_scr.shape[1]), BF16)

    wq = _pair_weights((q_ref[...] * (hd ** -0.5 * LOG2E)).T, hd, N_HEADS)
    qcol = pl.ds(pl.multiple_of(j * qb + q_off, ROW_BLOCK), qb)
    cq = [cr_ref[h:h + 1, qcol] * LOG2E for h in range(N_HEADS)]
    qrow = j * qb + q_off + lax.broadcasted_iota(I32, (1, qb), 1)
    sub = lax.broadcasted_iota(I32, (KEY_STEP, 1), 0)

    def logits(i):
        k0 = pl.multiple_of(i * KEY_STEP, KEY_STEP)
        krows = pl.ds(k0, KEY_STEP)
        kblk = kbf_scr[krows, :]
        kidx = k0 + sub
        ok = (kidx >= key_lo) & (kidx <= qrow)
        tiles = []
        for pair in range(N_HEADS // 2):
            s2 = jnp.dot(kblk, wq[pair], preferred_element_type=F32)
            for i2 in range(2):
                h = 2 * pair + i2
                tiles.append(jnp.where(ok, s2[:, i2 * qb:(i2 + 1) * qb] - ck_scr[h, krows, :], NEG))
        return tiles

    def value_t(h, i):
        return vt_scr[h * hv:(h + 1) * hv, pl.ds(pl.multiple_of(i * KEY_STEP, KEY_STEP), KEY_STEP)]

    n_steps = lax.shift_right_logical(nkb + (STEP_BLOCKS - 1), _log2(STEP_BLOCKS))
    o_t = _attend_two_pass(n_steps, logits, value_t, cq, s_scr, acc_scr, qb, hd)
    y_ref[...] = o_t.T * _silu(g_ref[...])


def _fox(q_arr, q_col, k_arr, k_col, v_arr, v_col, g_arr, g_col, cc, cr, key_lo, q_off, causal_blocks):
    bsz, lq, _ = q_arr.shape
    lk = k_arr.shape[1]
    w = GROUP_W
    nqb = lq // ROW_BLOCK
    key_rows = pl.cdiv(lk, KEY_STEP) * KEY_STEP
    return pl.pallas_call(
        functools.partial(_fox_kernel, key_lo=key_lo, q_off=q_off, causal_blocks=causal_blocks),
        grid=(bsz, nqb),
        in_specs=[pl.BlockSpec((None, ROW_BLOCK, w), lambda b, j: (b, j, q_col)),
                  pl.BlockSpec((None, lk, w), lambda b, j: (b, 0, k_col)),
                  pl.BlockSpec((None, lk, w), lambda b, j: (b, 0, v_col)),
                  pl.BlockSpec((None, ROW_BLOCK, w), lambda b, j: (b, j, g_col)),
                  pl.BlockSpec((None, lk, LANES), lambda b, j: (b, 0, 0)),
                  pl.BlockSpec((None, SUBLANES, lk), lambda b, j: (b, 0, 0))],
        out_specs=pl.BlockSpec((None, ROW_BLOCK, w), lambda b, j: (b, j, 0)),
        out_shape=jax.ShapeDtypeStruct((bsz, lq, w), F32),
        scratch_shapes=[pltpu.VMEM((key_rows, w), BF16),
                        pltpu.VMEM((w + N_HEADS * ONES_ROWS, key_rows), BF16),
                        pltpu.VMEM((N_HEADS, key_rows, ROW_BLOCK), F32),
                        pltpu.VMEM((N_HEADS * key_rows // KEY_STEP, KEY_STEP, ROW_BLOCK), F32),
                        pltpu.VMEM((w + N_HEADS * ONES_ROWS, ROW_BLOCK), F32)],
        compiler_params=_cparams("parallel", "arbitrary"),
        name="fox",
    )(q_arr, k_arr, v_arr, g_arr, cc, cr)


def _dsa_kernel(cq_ref, cg_ref, ciq_ref, mq_ref, ckv_ref, mk_ref, y_ref,
                key_scr, kvb_scr, vt_scr, mkb_scr, s_scr, acc_scr,
                *, k_top, key_lo, key_hi, chunk_causal):
    qb, w = cq_ref.shape
    hd = w // N_HEADS
    nkb_total = ckv_ref.shape[0] // ROW_BLOCK
    j = pl.program_id(1)
    nkb = jnp.minimum(j + 1, nkb_total) if chunk_causal else nkb_total
    idx_scale = (H_IDX * D_IDX) ** -0.5

    @pl.when(j == 0)
    def _prepare_batch_row():
        def blk(i, carry):
            rows = pl.ds(pl.multiple_of(i * ROW_BLOCK, ROW_BLOCK), ROW_BLOCK)
            kv = ckv_ref[rows, :]
            kvb_scr[rows, :] = kv.astype(BF16)
            vt_scr[0:hd, rows] = kv.T[hd:2 * hd, :].astype(BF16)
            mkb_scr[rows, :] = mk_ref[rows, :].astype(BF16)
            return carry
        lax.fori_loop(0, nkb_total, blk, 0)
        pad = vt_scr.shape[1] - nkb_total * ROW_BLOCK
        if pad:
            vt_scr[0:hd, nkb_total * ROW_BLOCK:] = jnp.zeros((hd, pad), BF16)
        vt_scr[hd:, :] = jnp.ones((ONES_ROWS, vt_scr.shape[1]), BF16)

    def pad_rows(x):
        return jnp.concatenate([x, jnp.zeros((LANES - x.shape[0], qb), F32)], axis=0)

    iq_t = ciq_ref[...].T
    iw_t = mq_ref[...].T[IW0:IW0 + H_IDX, :]
    q_t = (cq_ref[...] * (hd ** -0.5 * LOG2E)).T
    iq_rhs = [jnp.concatenate([pad_rows(iq_t[(2 * p + i) * D_IDX:(2 * p + i + 1) * D_IDX, :])
                               for i in range(2)], axis=1).astype(BF16) for p in range(H_IDX // 2)]
    q_rhs = [jnp.concatenate([pad_rows(q_t[(2 * p + i) * hd:(2 * p + i + 1) * hd, :])
                              for i in range(2)], axis=1).astype(BF16) for p in range(N_HEADS // 2)]

    qrow = j * qb + lax.broadcasted_iota(I32, (1, qb), 1)
    if chunk_causal:
        hi = (lax.shift_right_logical(qrow, 6) + 1) * CHUNK
    else:
        hi = jnp.full((1, qb), key_hi, I32)
    sub = lax.broadcasted_iota(I32, (KEY_STEP, 1), 0)
    n_steps = lax.shift_right_logical(nkb + (STEP_BLOCKS - 1), _log2(STEP_BLOCKS))

    def score_step(i, carry):
        k0 = pl.multiple_of(i * KEY_STEP, KEY_STEP)
        mk = mkb_scr[pl.ds(k0, KEY_STEP), :]
        acc = jnp.zeros((KEY_STEP, qb), F32)
        for p in range(H_IDX // 2):
            sc2 = jnp.dot(mk, iq_rhs[p], preferred_element_type=F32)
            for i in range(2):
                h = 2 * p + i
                acc = acc + jnp.maximum(sc2[:, i * qb:(i + 1) * qb], 0.0) * iw_t[h:h + 1, :]
        score = acc * idx_scale + 0.0
        kidx = k0 + sub
        adm = (kidx >= key_lo) & (kidx < hi)
        u = pltpu.bitcast(score, I32)
        key = u ^ (lax.shift_right_arithmetic(u, 31) & np.int32(0x7FFFFFFF))
        key_scr[pl.ds(k0, KEY_STEP), :] = jnp.where(adm, key, INT_MIN)
        return carry

    lax.fori_loop(0, n_steps, score_step, 0)

    def count(pred):
        def cb(i, c8):
            k0 = pl.multiple_of(i * KEY_STEP, KEY_STEP)
            ind = jnp.where(pred(key_scr[pl.ds(k0, KEY_STEP), :], k0), 1, 0).astype(I32)
            return c8 + jnp.sum(ind.reshape(KEY_STEP // SUBLANES, SUBLANES, qb), axis=0)
        c8 = lax.fori_loop(0, n_steps, cb, jnp.zeros((SUBLANES, qb), I32))
        return jnp.sum(c8, axis=0, keepdims=True)

    def bisect(n_bits, count_ge, need):
        def bit_body(i, carry):
            ans, cnt_ans = carry
            cand = ans | lax.shift_left(np.int32(1), jnp.int32(n_bits - 1) - i)
            cnt = count_ge(cand)
            ok = cnt >= need
            return jnp.where(ok, cand, ans), jnp.where(ok, cnt, cnt_ans)
        return lax.fori_loop(0, n_bits, bit_body,
                             (jnp.zeros((1, qb), I32), jnp.full((1, qb), np.int32(2 ** 30), I32)))

    ans, cnt_thr = bisect(32, lambda cand: count(lambda kk, k0: kk >= (cand ^ INT_MIN)), k_top)
    thr = ans ^ INT_MIN
    tie = (cnt_thr > k_top) & (thr != INT_MIN)

    @pl.when(jnp.max(jnp.where(tie, 1, 0)) > 0)
    def _break_ties():
        n_rev_bits = _log2(pl.next_power_of_2(key_scr.shape[0]))
        rev_base = np.int32(2 ** n_rev_bits - 1)
        need = k_top - count(lambda kk, k0: kk > thr)
        ans2, _ = bisect(n_rev_bits, lambda cand: count(
            lambda kk, k0: (kk == thr) & ((rev_base - (k0 + sub)) >= cand)), need)

        def demote(i, carry):
            k0 = pl.multiple_of(i * KEY_STEP, KEY_STEP)
            kk = key_scr[pl.ds(k0, KEY_STEP), :]
            lose = tie & (kk == thr) & ((rev_base - (k0 + sub)) < ans2)
            key_scr[pl.ds(k0, KEY_STEP), :] = jnp.where(lose, INT_MIN, kk)
            return carry

        lax.fori_loop(0, n_steps, demote, 0)

    thr_sel = jnp.maximum(thr, INT_MIN + 1)

    def logits(i):
        krows = pl.ds(pl.multiple_of(i * KEY_STEP, KEY_STEP), KEY_STEP)
        sel = key_scr[krows, :] >= thr_sel
        kv = kvb_scr[krows, :]
        tiles = []
        for pair in range(N_HEADS // 2):
            s2 = jnp.dot(kv, q_rhs[pair], preferred_element_type=F32)
            tiles += [jnp.where(sel, s2[:, i2 * qb:(i2 + 1) * qb], NEG) for i2 in range(2)]
        return tiles

    def value_t(h, i):
        return vt_scr[:, pl.ds(pl.multiple_of(i * KEY_STEP, KEY_STEP), KEY_STEP)]

    no_shift = [jnp.zeros((1, qb), F32)] * N_HEADS
    o_t = _attend_two_pass(n_steps, logits, value_t, no_shift, s_scr, acc_scr, qb, hd)
    y_ref[...] = o_t.T * _silu(cg_ref[...])


def _dsa(p3, cols, ckv_arr, ckv_col, mk_arr, mk_col, k_top, key_lo, key_hi, chunk_causal):
    bsz, lq, _ = p3.shape
    lk = ckv_arr.shape[1]
    w = GROUP_W
    cq_col, cg_col, ciq_col, mq_col = cols
    key_rows = pl.cdiv(lk, KEY_STEP) * KEY_STEP
    return pl.pallas_call(
        functools.partial(_dsa_kernel, k_top=k_top, key_lo=key_lo, key_hi=key_hi,
                          chunk_causal=chunk_causal),
        grid=(bsz, lq // ROW_BLOCK),
        in_specs=[pl.BlockSpec((None, ROW_BLOCK, w), lambda b, j: (b, j, cq_col)),
                  pl.BlockSpec((None, ROW_BLOCK, w), lambda b, j: (b, j, cg_col)),
                  pl.BlockSpec((None, ROW_BLOCK, w), lambda b, j: (b, j, ciq_col)),
                  pl.BlockSpec((None, ROW_BLOCK, LANES), lambda b, j: (b, j, mq_col)),
                  pl.BlockSpec((None, lk, LANES), lambda b, j: (b, 0, ckv_col)),
                  pl.BlockSpec((None, lk, LANES), lambda b, j: (b, 0, mk_col))],
        out_specs=pl.BlockSpec((None, ROW_BLOCK, w), lambda b, j: (b, j, 0)),
        out_shape=jax.ShapeDtypeStruct((bsz, lq, w), F32),
        scratch_shapes=[pltpu.VMEM((key_rows, ROW_BLOCK), I32),
                        pltpu.VMEM((key_rows, LANES), BF16),
                        pltpu.VMEM((w // N_HEADS + ONES_ROWS, key_rows), BF16),
                        pltpu.VMEM((key_rows, LANES), BF16),
                        pltpu.VMEM((N_HEADS * key_rows // KEY_STEP, KEY_STEP, ROW_BLOCK), F32),
                        pltpu.VMEM((w + N_HEADS * ONES_ROWS, ROW_BLOCK), F32)],
        compiler_params=_cparams("parallel", "arbitrary"),
        name="dsa",
    )(p3, p3, p3, p3, ckv_arr, mk_arr)


def _merge_kernel(ya_ref, yb_ref, yc_ref, yd_ref, x_ref, w_ref, g_ref, o_ref, *, period, valid_lo, valid_hi):
    tm = x_ref.shape[0]
    gw = ya_ref.shape[1]
    acc = jnp.zeros(o_ref.shape, F32)
    for i, y_ref in enumerate((ya_ref, yb_ref, yc_ref, yd_ref)):
        acc = acc + jnp.dot(y_ref[...].astype(BF16), w_ref[i * gw:(i + 1) * gw, :],
                            preferred_element_type=F32)
    ms = jnp.mean(acc * acc, axis=-1, keepdims=True)
    out = x_ref[...] + (acc * lax.rsqrt(ms + EPS)) * g_ref[...]
    r0 = pl.program_id(0) * tm
    local = (r0 - (r0 // period) * period) + lax.broadcasted_iota(I32, (tm, 1), 0)
    local = jnp.where(local >= period, local - period, local)
    valid = (local >= valid_lo) & (local < valid_hi)
    o_ref[...] = jnp.where(valid, out, 0.0)


def _merge(ys, x2d, w_bf16, g, tm, period, valid_lo, valid_hi):
    rows, d = x2d.shape
    assert tm <= period
    gw = ys[0].shape[1]
    yspec = pl.BlockSpec((tm, gw), lambda i: (i, 0))
    return pl.pallas_call(
        functools.partial(_merge_kernel, period=period, valid_lo=valid_lo, valid_hi=valid_hi),
        grid=(rows // tm,),
        in_specs=[yspec, yspec, yspec, yspec,
                  pl.BlockSpec((tm, d), lambda i: (i, 0)),
                  pl.BlockSpec(w_bf16.shape, lambda i: (0, 0)),
                  pl.BlockSpec((1, d), lambda i: (0, 0))],
        out_specs=pl.BlockSpec((tm, d), lambda i: (i, 0)),
        out_shape=jax.ShapeDtypeStruct((rows, d), F32),
        compiler_params=_cparams("parallel"),
        name="merge",
    )(*ys, x2d, w_bf16, g)


def _row_tile(rows):
    return next(t for t in (512, 256, ROW_BLOCK) if rows % t == 0)


def _column_layout(gw):
    sizes = [gw] * 4 + [gw, gw, gw, FOX_HEADS, gw] + [gw, gw // 4, gw // 4, gw, H_IDX * D_IDX, D_IDX, H_IDX] + [gw] * 4
    names = ["aq", "af", "ai", "ag", "bq", "bk", "bv", "bf", "bg",
             "cq", "ck", "cv", "cg", "ciq", "cik", "ciw", "dq", "dk", "dv", "dg"]
    start = dict(zip(names, np.cumsum([0] + sizes[:-1])))
    size = dict(zip(names, sizes))
    src = -np.ones((16 * gw,), np.int64)
    def put(dst, name, off=0):
        src[dst + off:dst + off + size[name]] = np.arange(start[name], start[name] + size[name])
    for i, n in enumerate(["aq", "af", "ai", "ag", "bq", "bk", "bv", "bg", "cq", "cg", "ciq"]):
        put(i * gw, n)
    ckv0 = 11 * gw
    put(ckv0, "ck")
    put(ckv0 + gw // 4, "cv")
    misc0 = ckv0 + LANES
    put(misc0, "cik", IK0)
    put(misc0, "ciw", IW0)
    put(misc0, "bf", BF0)
    for i, n in enumerate(["dq", "dk", "dv", "dg"]):
        put(12 * gw + i * gw, n)
    return src


def _relayout_w_in(w_in_l, src):
    cols = jnp.take(w_in_l, jnp.asarray(np.maximum(src, 0)), axis=1)
    return jnp.where(jnp.asarray(src >= 0)[None, :], cols, 0.0).astype(BF16)


def _ret_tables(pos, chunk, gw):
    hd = gw // N_HEADS
    half = hd // 2
    inv = ROPE_BASE ** (-jnp.arange(half, dtype=F32) / half)
    ang = pos.astype(F32)[:, None] * inv[None, :]
    cos_h = jnp.concatenate([jnp.cos(ang), jnp.cos(ang)], axis=-1)
    sin_h = jnp.concatenate([-jnp.sin(ang), jnp.sin(ang)], axis=-1)
    cos = jnp.tile(cos_h, (1, N_HEADS))
    sin = jnp.tile(sin_h, (1, N_HEADS))
    lg = jnp.log(1.0 - 2.0 ** (-5.0 - jnp.arange(N_HEADS, dtype=F32)))
    lg_l = jnp.repeat(lg, hd)[None, :]
    t = jnp.arange(chunk, dtype=F32)[:, None]
    eb = jnp.exp((t + 1.0) * lg_l)
    ke2s = jnp.exp((chunk - 1.0 - t) * lg_l)
    dend = jnp.exp(chunk * lg_l)
    dt = jnp.arange(chunk, dtype=F32)[:, None] - jnp.arange(chunk, dtype=F32)[None, :]
    gam = jnp.concatenate([jnp.where(dt >= 0, jnp.exp(dt * lg[h]), 0.0) for h in range(N_HEADS)], axis=0)
    return cos, sin, eb, ke2s, dend, gam


def _state_to_bd(state):
    bsz, h, k, v = state.shape
    eye = jnp.eye(h, dtype=state.dtype)
    st = jnp.einsum('bhkv,hg->bhvgk', state, eye)
    return st.reshape(bsz, h * v, h * k)


def _bd_to_state(st, h):
    bsz, hv, hk = st.shape
    st5 = st.reshape(bsz, h, hv // h, h, hk // h)
    diag = jnp.stack([st5[:, i, :, i, :] for i in range(h)], axis=1)
    return jnp.swapaxes(diag, 2, 3)


def kernel(x_prompt, x_sample, state_hgrn, cache_fox_k, cache_fox_v, cache_fox_logf, cache_dsa_k,
           cache_dsa_v, cache_dsa_idx_k, state_ret, meta_tokens, w_in, w_out, fox_bias, hgrn_lb,
           norm_pre, norm_post):
    bsz, seq, d = x_prompt.shape
    dbsz, t_new, _ = x_sample.shape
    depth = w_in.shape[0]
    past = cache_fox_k.shape[2]
    gw = d // N_GROUPS
    hd = gw // N_HEADS
    assert gw == 2 * LANES and seq % ROW_BLOCK == 0 and past % ROW_BLOCK == 0 and t_new <= ROW_BLOCK
    assert t_new % SUBLANES == 0

    pad_front = ROW_BLOCK - N_META
    lp = ROW_BLOCK + seq
    ls = ROW_BLOCK
    lks = past + ROW_BLOCK
    n_chunks_p = seq // CHUNK + 1
    k_top_p = min(TOP_K_MAX, seq // 4)
    k_top_s = min(TOP_K_MAX, (past + t_new) // 4)

    src = _column_layout(gw)
    n_cols = src.shape[0]
    col = {"a": 0, "bq": 4, "bk": 5, "bv": 6, "bg": 7, "cq": 8, "cg": 9, "ciq": 10, "d": 3}
    ckv_col = (11 * gw) // LANES
    misc_col = ckv_col + 1

    sm = jax.nn.softmax(hgrn_lb.astype(F32), axis=0)
    lbs = jnp.cumsum(sm, axis=0) - sm[0:1]

    xp = jnp.concatenate([jnp.zeros((bsz, pad_front, d), F32),
                          jnp.broadcast_to(meta_tokens.astype(F32)[None], (bsz, N_META, d)),
                          x_prompt], axis=1)
    xs = jnp.concatenate([x_sample, jnp.zeros((dbsz, ls - t_new, d), F32)], axis=1)

    tab_p = _ret_tables(jnp.arange(lp) - ROW_BLOCK, CHUNK, gw)
    tab_s = _ret_tables(past + jnp.arange(ls), t_new, gw)
    zero_state_p = jnp.zeros((bsz, gw, gw), F32)

    outs_p = {k: [] for k in ("hgrn", "fk", "fv", "fl", "ck", "cv", "ci", "ret")}
    outs_s = {k: [] for k in ("hgrn", "fk", "fv", "fl", "ck", "cv", "ci", "ret")}

    for l in range(depth):
        w_l = _relayout_w_in(w_in[l], src)
        w_o = w_out[l].astype(BF16)
        g_pre = norm_pre[l][None, :]
        g_post = norm_post[l][None, :]
        lb = lbs[l][None, :]
        bias = jnp.zeros((1, LANES), F32).at[0, BF0:BF0 + FOX_HEADS].set(fox_bias[l].astype(F32))

        p = _project(xp.reshape(bsz * lp, d), g_pre, w_l, _row_tile(bsz * lp)).reshape(bsz, lp, n_cols)
        ya, st_a = _hgrn(p, 0, lb, zero_state_p, CHUNK, n_chunks_p, CHUNK)
        yd, st_d = _ret(p, 3, tab_p, zero_state_p, CHUNK, n_chunks_p, CHUNK)
        lf, cc, cr = _foxprep(p, misc_col, bias, 0)
        yb = _fox(p, col["bq"], p, col["bk"], p, col["bv"], p, col["bg"], cc, cr, pad_front, 0, True)
        yc = _dsa(p, (col["cq"], col["cg"], col["ciq"], misc_col), p, ckv_col, p, misc_col,
                  k_top_p, pad_front, 0, True)
        flat = lambda a: a.reshape(bsz * lp, gw)
        xp = _merge([flat(ya), flat(yb), flat(yc), flat(yd)], xp.reshape(bsz * lp, d), w_o, g_post,
                    _row_tile(bsz * lp), lp, pad_front, lp).reshape(bsz, lp, d)
        pv = p[:, pad_front:, :]
        outs_p["hgrn"].append(_bd_to_state(st_a, N_HEADS))
        outs_p["ret"].append(_bd_to_state(st_d, N_HEADS))
        outs_p["fk"].append(pv[:, :, 5 * gw:6 * gw].reshape(bsz, -1, N_HEADS, hd))
        outs_p["fv"].append(pv[:, :, 6 * gw:7 * gw].reshape(bsz, -1, N_HEADS, hd))
        outs_p["fl"].append(lf[:, pad_front:, BF0:BF0 + FOX_HEADS])
        outs_p["ck"].append(pv[:, :, 11 * gw:11 * gw + hd])
        outs_p["cv"].append(pv[:, :, 11 * gw + hd:11 * gw + 2 * hd])
        outs_p["ci"].append(pv[:, :, misc_col * LANES + IK0:misc_col * LANES + IK0 + D_IDX])

        ps = _project(xs.reshape(dbsz * ls, d), g_pre, w_l, _row_tile(dbsz * ls)).reshape(dbsz, ls, n_cols)
        ya, st_a = _hgrn(ps, 0, lb, _state_to_bd(state_hgrn[l].astype(F32)), t_new, 1, 0)
        yd, st_d = _ret(ps, 3, tab_s, _state_to_bd(state_ret[l].astype(F32)), t_new, 1, 0)
        z = jnp.concatenate(
            [jnp.pad(cache_fox_logf[l].astype(F32), ((0, 0), (0, 0), (BF0, LANES - BF0 - FOX_HEADS))),
             ps[:, :, misc_col * LANES:(misc_col + 1) * LANES]], axis=1)
        lf, cc, cr = _foxprep(z, 0, bias, past)
        k_all = jnp.concatenate([cache_fox_k[l].reshape(dbsz, past, gw), ps[:, :, 5 * gw:6 * gw]], axis=1)
        v_all = jnp.concatenate([cache_fox_v[l].reshape(dbsz, past, gw), ps[:, :, 6 * gw:7 * gw]], axis=1)
        yb = _fox(ps, col["bq"], k_all, 0, v_all, 0, ps, col["bg"], cc, cr, 0, past, False)
        ckv_all = jnp.concatenate(
            [jnp.concatenate([cache_dsa_k[l], cache_dsa_v[l]], axis=-1).astype(F32),
             ps[:, :, 11 * gw:11 * gw + LANES]], axis=1)
        mk_all = jnp.concatenate(
            [jnp.pad(cache_dsa_idx_k[l].astype(F32), ((0, 0), (0, 0), (IK0, LANES - IK0 - D_IDX))),
             ps[:, :, misc_col * LANES:(misc_col + 1) * LANES]], axis=1)
        yc = _dsa(ps, (col["cq"], col["cg"], col["ciq"], misc_col), ckv_all, 0, mk_all, 0,
                  k_top_s, 0, past + t_new, False)
        flat = lambda a: a.reshape(dbsz * ls, gw)
        xs = _merge([flat(ya), flat(yb), flat(yc), flat(yd)], xs.reshape(dbsz * ls, d), w_o, g_post,
                    ls, ls, 0, t_new).reshape(dbsz, ls, d)
        pn = ps[:, :t_new, :]
        outs_s["hgrn"].append(_bd_to_state(st_a, N_HEADS))
        outs_s["ret"].append(_bd_to_state(st_d, N_HEADS))
        outs_s["fk"].append(pn[:, :, 5 * gw:6 * gw].reshape(dbsz, -1, N_HEADS, hd))
        outs_s["fv"].append(pn[:, :, 6 * gw:7 * gw].reshape(dbsz, -1, N_HEADS, hd))
        outs_s["fl"].append(lf[:, past:past + t_new, BF0:BF0 + FOX_HEADS])
        outs_s["ck"].append(pn[:, :, 11 * gw:11 * gw + hd])
        outs_s["cv"].append(pn[:, :, 11 * gw + hd:11 * gw + 2 * hd])
        outs_s["ci"].append(pn[:, :, misc_col * LANES + IK0:misc_col * LANES + IK0 + D_IDX])

    dt = x_prompt.dtype
    st = lambda xs_list: jnp.stack(xs_list, axis=0).astype(dt)
    order = ("hgrn", "fk", "fv", "fl", "ck", "cv", "ci", "ret")
    return ((xp[:, ROW_BLOCK:, :].astype(dt), xs[:, :t_new, :].astype(dt))
            + tuple(st(outs_p[k]) for k in order) + tuple(st(outs_s[k]) for k in order))
```

```python
import functools

import numpy as np
import jax
import jax.numpy as jnp
from jax import lax
from jax.experimental import pallas as pl
from jax.experimental.pallas import tpu as pltpu

F32 = jnp.float32
BF16 = jnp.bfloat16
I32 = jnp.int32
LOG2E = float(np.log2(np.e))
ONES_ROWS = 16

N_META = 16
CHUNK = 64
N_GROUPS = 4
N_HEADS = 4
H_IDX = 8
D_IDX = 32
TOP_K_MAX = 256
ROPE_BASE = 10000.0
EPS = 1e-6
FOX_HEADS = 4

LANES = 128
SUBLANES = 8
ROW_BLOCK = 128
GROUP_W = 2 * LANES
VMEM_LIMIT_BYTES = 56 * 1024 * 1024

IK0 = 0
IW0 = 32
BF0 = 64

NEG = -1e30
M_FLOOR = -1e20
INT_MIN = np.int32(-2 ** 31)
STEP_BLOCKS = 4
KEY_STEP = STEP_BLOCKS * ROW_BLOCK


def _cparams(*sem):
    return pltpu.CompilerParams(dimension_semantics=sem, vmem_limit_bytes=VMEM_LIMIT_BYTES)


def _split3(x):
    h = x.astype(BF16)
    r = x - h.astype(F32)
    m = r.astype(BF16)
    lo = (r - m.astype(F32)).astype(BF16)
    return h, m, lo


def _dot_exact_lhs(a_bf16, x):
    d = lambda y: jnp.dot(a_bf16, y, preferred_element_type=F32)
    h, m, lo = _split3(x)
    return d(h) + d(m) + d(lo)


def _dot_exact_rhs(x, a_bf16):
    d = lambda y: jnp.dot(y, a_bf16, preferred_element_type=F32)
    h, m, lo = _split3(x)
    return d(h) + d(m) + d(lo)


def _dot_nt(a, b):
    return lax.dot_general(a, b, (((1,), (1,)), ((), ())), preferred_element_type=F32)


def _dot_tn(a, b):
    return lax.dot_general(a, b, (((0,), (0,)), ((), ())), preferred_element_type=F32)


def _log2(n):
    assert n > 0 and n & (n - 1) == 0, n
    return n.bit_length() - 1


def _head_masks(width, n_heads):
    lane = lax.broadcasted_iota(I32, (1, width), 1)
    sh = _log2(width // n_heads)
    return [(lax.shift_right_logical(lane, sh) == h).astype(F32) for h in range(n_heads)]


def _block_diag(width, n_heads, value):
    r = lax.broadcasted_iota(I32, (width, width), 0)
    c = lax.broadcasted_iota(I32, (width, width), 1)
    sh = _log2(width // n_heads)
    same = lax.shift_right_logical(r, sh) == lax.shift_right_logical(c, sh)
    return jnp.where(same, value, 0.0).astype(F32)


def _silu(x):
    return x * jax.nn.sigmoid(x)


def _proj_kernel(x_ref, g_ref, w_ref, o_ref, *, col_chunk):
    x = x_ref[...]
    ms = jnp.mean(x * x, axis=-1, keepdims=True)
    xn = ((x * lax.rsqrt(ms + EPS)) * g_ref[...]).astype(BF16)
    for c in range(o_ref.shape[1] // col_chunk):
        cols = slice(c * col_chunk, (c + 1) * col_chunk)
        o_ref[:, cols] = jnp.dot(xn, w_ref[:, cols], preferred_element_type=F32)


def _project(x2d, g, w_bf16, tm):
    rows, d = x2d.shape
    n = w_bf16.shape[1]
    return pl.pallas_call(
        functools.partial(_proj_kernel, col_chunk=1024),
        grid=(rows // tm,),
        in_specs=[pl.BlockSpec((tm, d), lambda i: (i, 0)),
                  pl.BlockSpec((1, d), lambda i: (0, 0)),
                  pl.BlockSpec((d, n), lambda i: (0, 0))],
        out_specs=pl.BlockSpec((tm, n), lambda i: (i, 0)),
        out_shape=jax.ShapeDtypeStruct((rows, n), F32),
        compiler_params=_cparams("parallel"),
        name="proj",
    )(x2d, g, w_bf16)


def _head_scores(qa, ka, hm):
    q_stack = jnp.concatenate([qa * hm[h] for h in range(N_HEADS)], axis=0).astype(BF16)
    return _dot_nt(q_stack, ka.astype(BF16))


def _hier_constants(chunk):
    halves = [chunk >> (i + 1) for i in range(_log2(chunk))]
    r = lax.broadcasted_iota(I32, (chunk, chunk), 0)
    c = lax.broadcasted_iota(I32, (chunk, chunk), 1)
    ts = lax.broadcasted_iota(I32, (N_HEADS * chunk, chunk), 0) & (chunk - 1)
    ss = lax.broadcasted_iota(I32, (N_HEADS * chunk, chunk), 1)
    sels, masks = [], []
    for h in halves:
        sh = _log2(h)
        grp = lambda x: lax.shift_right_logical(x, sh + 1)
        sels.append((c == lax.shift_left(grp(r), sh + 1) + (h - 1)).astype(BF16))
        upper_t = (lax.shift_right_logical(ts, sh) & 1).astype(F32)
        lower_s = 1.0 - (lax.shift_right_logical(ss, sh) & 1).astype(F32)
        masks.append(jnp.where(grp(ts) == grp(ss), upper_t * lower_s, 0.0))
    return jnp.concatenate(sels, axis=0), masks, (ts == ss).astype(F32)


def _decayed_scores(q, k, b, sel_all, masks, diag, hm):
    chunk = q.shape[0]
    refs = _dot_exact_lhs(sel_all, b)
    att = _head_scores(q, k, hm) * diag
    for lvl, mask in enumerate(masks):
        ref = refs[lvl * chunk:(lvl + 1) * chunk, :]
        qa = q * jnp.exp(jnp.minimum(b - ref, 0.0))
        ka = k * jnp.exp(jnp.minimum(ref - b, 0.0))
        att = att + _head_scores(qa, ka, hm) * mask
    return att


def _gla_chunk(att, qe, ke2, v, decay_end, st, hm, bd):
    c = qe.shape[0]
    o_stack = jnp.dot(att.astype(BF16), v.astype(BF16), preferred_element_type=F32)
    o_intra = o_stack[0:c] * hm[0]
    for h in range(1, N_HEADS):
        o_intra = o_intra + o_stack[h * c:(h + 1) * c] * hm[h]
    o_inter = _dot_nt(qe.astype(BF16), st.astype(BF16))
    st_new = st * decay_end + _dot_tn(v.astype(BF16), ke2.astype(BF16)) * bd
    return o_inter + o_intra, st_new


def _head_rms_gate(o, gate, bd_mean_bf16):
    h, m, _ = _split3(o * o)
    ms = (jnp.dot(h, bd_mean_bf16, preferred_element_type=F32)
          + jnp.dot(m, bd_mean_bf16, preferred_element_type=F32))
    return (o * lax.rsqrt(ms + EPS)) * _silu(gate)


def _gla_prologue(st0_ref, y_ref, st_ref, chunk, n_chunks):
    @pl.when(pl.program_id(1) == 0)
    def _():
        st_ref[...] = st0_ref[...]
    tail = n_chunks * chunk
    if tail < y_ref.shape[1]:
        y_ref[:, tail:, :] = jnp.zeros((y_ref.shape[0], y_ref.shape[1] - tail, y_ref.shape[2]), F32)


def _hgrn_kernel(a_ref, lb_ref, st0_ref, y_ref, st_ref, *, chunk, n_chunks):
    n_b, _, w = y_ref.shape
    hm = _head_masks(w, N_HEADS)
    bd = _block_diag(w, N_HEADS, 1.0)
    bd_mean = _block_diag(w, N_HEADS, 1.0 / (w // N_HEADS)).astype(BF16)
    r = lax.broadcasted_iota(I32, (chunk, chunk), 0)
    s = lax.broadcasted_iota(I32, (chunk, chunk), 1)
    tri = (r >= s).astype(BF16)
    sel_all, masks, diag = _hier_constants(chunk)
    lb = lb_ref[...]
    _gla_prologue(st0_ref, y_ref, st_ref, chunk, n_chunks)

    def body(c, carry):
        rows = pl.ds(pl.multiple_of(c * chunk, chunk), chunk)
        for g in range(n_b):
            q = a_ref[g, rows, 0:w]
            f = lb + (1.0 - lb) * jax.nn.sigmoid(a_ref[g, rows, w:2 * w])
            k = 1.0 - f
            v = a_ref[g, rows, 2 * w:3 * w]
            gate = a_ref[g, rows, 3 * w:4 * w]
            b = _dot_exact_lhs(tri, jnp.log(f))
            b_end = b[chunk - 1:chunk, :]
            att = _decayed_scores(q, k, b, sel_all, masks, diag, hm)
            o, st_new = _gla_chunk(att, q * jnp.exp(b), k * jnp.exp(b_end - b), v, jnp.exp(b_end),
                                   st_ref[g], hm, bd)
            st_ref[g] = st_new
            y_ref[g, rows, :] = _head_rms_gate(o, gate, bd_mean)
        return carry

    lax.fori_loop(0, n_chunks, body, 0)


def _ret_kernel(d_ref, cos_ref, sin_ref, eb_ref, ke2s_ref, dend_ref, gam_ref, st0_ref, y_ref, st_ref,
                *, chunk, n_chunks):
    n_b, _, w = y_ref.shape
    hd = w // N_HEADS
    hm = _head_masks(w, N_HEADS)
    bd = _block_diag(w, N_HEADS, 1.0)
    bd_mean = _block_diag(w, N_HEADS, 1.0 / hd).astype(BF16)
    lane = lax.broadcasted_iota(I32, (1, w), 1)
    first_half = (lane & (hd - 1)) < (hd // 2)
    eb = eb_ref[...]
    ke2s = ke2s_ref[...]
    dend = dend_ref[...]
    gam = gam_ref[...]
    _gla_prologue(st0_ref, y_ref, st_ref, chunk, n_chunks)

    def rope(x, cos, sin_signed):
        swapped = jnp.where(first_half, pltpu.roll(x, w - hd // 2, 1), pltpu.roll(x, hd // 2, 1))
        return x * cos + swapped * sin_signed

    def body(c, carry):
        rows = pl.ds(pl.multiple_of(c * chunk, chunk), chunk)
        cos = cos_ref[rows, :]
        sin = sin_ref[rows, :]
        for g in range(n_b):
            q = rope(d_ref[g, rows, 0:w], cos, sin)
            k = rope(d_ref[g, rows, w:2 * w], cos, sin) * (hd ** -0.5)
            v = d_ref[g, rows, 2 * w:3 * w]
            gate = d_ref[g, rows, 3 * w:4 * w]
            att = _head_scores(q, k, hm) * gam
            o, st_new = _gla_chunk(att, q * eb, k * ke2s, v, dend, st_ref[g], hm, bd)
            st_ref[g] = st_new
            y_ref[g, rows, :] = _head_rms_gate(o, gate, bd_mean)
        return carry

    lax.fori_loop(0, n_chunks, body, 0)


def _gla_grid(bsz, lp, chunk, n_chunks):
    n_b = 2 if bsz % 2 == 0 else 1
    halves = n_chunks * chunk == lp and n_chunks % 2 == 0 and (lp // 2) % SUBLANES == 0
    n_seq = 2 if halves else 1
    return n_b, n_seq, n_chunks // n_seq


def _hgrn(p3, col_block, lb, st0, chunk, n_chunks):
    bsz, lp, _ = p3.shape
    w = lb.shape[1]
    n_b, n_seq, n_chunks_blk = _gla_grid(bsz, lp, chunk, n_chunks)
    rows = lp // n_seq
    return pl.pallas_call(
        functools.partial(_hgrn_kernel, chunk=chunk, n_chunks=n_chunks_blk),
        grid=(bsz // n_b, n_seq),
        in_specs=[pl.BlockSpec((n_b, rows, 4 * w), lambda b, s: (b, s, col_block)),
                  pl.BlockSpec((1, w), lambda b, s: (0, 0)),
                  pl.BlockSpec((n_b, w, w), lambda b, s: (b, 0, 0))],
        out_specs=[pl.BlockSpec((n_b, rows, w), lambda b, s: (b, s, 0)),
                   pl.BlockSpec((n_b, w, w), lambda b, s: (b, 0, 0))],
        out_shape=[jax.ShapeDtypeStruct((bsz, lp, w), F32),
                   jax.ShapeDtypeStruct((bsz, w, w), F32)],
        compiler_params=_cparams("parallel", "arbitrary"),
        name="hgrn",
    )(p3, lb, st0)


def _ret(p3, col_block, tables, st0, chunk, n_chunks):
    bsz, lp, _ = p3.shape
    w = st0.shape[1]
    n_b, n_seq, n_chunks_blk = _gla_grid(bsz, lp, chunk, n_chunks)
    rows = lp // n_seq
    cos, sin, eb, ke2s, dend, gam = tables
    full = lambda a: pl.BlockSpec(a.shape, lambda b, s: (0,) * a.ndim)
    per_seq = pl.BlockSpec((rows, w), lambda b, s: (s, 0))
    return pl.pallas_call(
        functools.partial(_ret_kernel, chunk=chunk, n_chunks=n_chunks_blk),
        grid=(bsz // n_b, n_seq),
        in_specs=[pl.BlockSpec((n_b, rows, 4 * w), lambda b, s: (b, s, col_block)),
                  per_seq, per_seq, full(eb), full(ke2s), full(dend), full(gam),
                  pl.BlockSpec((n_b, w, w), lambda b, s: (b, 0, 0))],
        out_specs=[pl.BlockSpec((n_b, rows, w), lambda b, s: (b, s, 0)),
                   pl.BlockSpec((n_b, w, w), lambda b, s: (b, 0, 0))],
        out_shape=[jax.ShapeDtypeStruct((bsz, lp, w), F32),
                   jax.ShapeDtypeStruct((bsz, w, w), F32)],
        compiler_params=_cparams("parallel", "arbitrary"),
        name="ret",
    )(p3, cos, sin, eb, ke2s, dend, gam, st0)


def _log_sigmoid(x):
    return -(jnp.maximum(-x, 0.0) + jnp.log(1.0 + jnp.exp(-jnp.abs(x))))


def _foxprep_kernel(z_ref, bias_ref, lf_ref, cc_ref, cr_ref, *, n_pass):
    nblk = z_ref.shape[0] // ROW_BLOCK
    r = lax.broadcasted_iota(I32, (ROW_BLOCK, ROW_BLOCK), 0)
    s = lax.broadcasted_iota(I32, (ROW_BLOCK, ROW_BLOCK), 1)
    tri = (r >= s).astype(BF16)
    bias = bias_ref[...]

    def body(i, carry):
        rows = pl.ds(pl.multiple_of(i * ROW_BLOCK, ROW_BLOCK), ROW_BLOCK)
        z = z_ref[rows, :]
        rowi = i * ROW_BLOCK + lax.broadcasted_iota(I32, (ROW_BLOCK, 1), 0)
        lf = jnp.where(rowi < n_pass, z, _log_sigmoid(z + bias))
        lf_ref[rows, :] = lf
        cs = _dot_exact_lhs(tri, lf) + carry
        cc_ref[rows, :] = cs
        cr_ref[:, rows] = cs.T[BF0:BF0 + SUBLANES, :]
        return cs[ROW_BLOCK - 1:ROW_BLOCK, :]

    lax.fori_loop(0, nblk, body, jnp.zeros((1, LANES), F32))


def _foxprep(z3, z_col, bias, n_pass):
    bsz, lk, _ = z3.shape
    return pl.pallas_call(
        functools.partial(_foxprep_kernel, n_pass=n_pass),
        grid=(bsz,),
        in_specs=[pl.BlockSpec((None, lk, LANES), lambda b: (b, 0, z_col)),
                  pl.BlockSpec((1, LANES), lambda b: (0, 0))],
        out_specs=[pl.BlockSpec((None, lk, LANES), lambda b: (b, 0, 0)),
                   pl.BlockSpec((None, lk, LANES), lambda b: (b, 0, 0)),
                   pl.BlockSpec((None, SUBLANES, lk), lambda b: (b, 0, 0))],
        out_shape=[jax.ShapeDtypeStruct((bsz, lk, LANES), F32),
                   jax.ShapeDtypeStruct((bsz, lk, LANES), F32),
                   jax.ShapeDtypeStruct((bsz, SUBLANES, lk), F32)],
        compiler_params=_cparams("parallel"),
        name="foxprep",
    )(z3, bias)


def _pair_weights(x_t, rows_per_head, n_heads):
    rowh = lax.shift_right_logical(lax.broadcasted_iota(I32, (x_t.shape[0], 1), 0),
                                   _log2(rows_per_head))
    only = lambda h: jnp.where(rowh == h, x_t, 0.0)
    return [jnp.concatenate([only(2 * p), only(2 * p + 1)], axis=1).astype(BF16)
            for p in range(n_heads // 2)]


def _attend_two_pass(n_steps, logits_fn, value_t_fn, shifts, s_scr, acc_scr, qb, hd):
    fold = lambda x, op: op(x.reshape(KEY_STEP // SUBLANES, SUBLANES, qb), axis=0)
    hv = hd + ONES_ROWS

    def max_step(i, ms):
        tiles = logits_fn(i)
        for h in range(N_HEADS):
            s_scr[i * N_HEADS + h] = tiles[h]
        return tuple(jnp.maximum(ms[h], fold(tiles[h], jnp.max)) for h in range(N_HEADS))

    ms = lax.fori_loop(0, n_steps, max_step,
                       tuple(jnp.full((SUBLANES, qb), M_FLOOR, F32) for _ in range(N_HEADS)))
    m_logit = [jnp.max(ms[h], axis=0, keepdims=True) + shifts[h] for h in range(N_HEADS)]
    acc_scr[...] = jnp.zeros(acc_scr.shape, F32)

    def sum_step(i, carry):
        for h in range(N_HEADS):
            p = jnp.exp2(s_scr[i * N_HEADS + h] - (m_logit[h] - shifts[h]))
            acc_scr[h * hv:(h + 1) * hv, :] += jnp.dot(value_t_fn(h, i), p.astype(BF16),
                                                       preferred_element_type=F32)
        return carry

    lax.fori_loop(0, n_steps, sum_step, 0)
    outs = []
    for h in range(N_HEADS):
        l = acc_scr[h * hv + hd:h * hv + hd + 1, :]
        outs.append(acc_scr[h * hv:h * hv + hd, :] / jnp.where(l > 0.0, l, 1.0))
    return jnp.concatenate(outs, axis=0)


def _fox_kernel(q_ref, k_ref, v_ref, g_ref, cc_ref, cr_ref, y_ref, kbf_scr, vt_scr, ck_scr, s_scr, acc_scr,
                *, key_lo, q_off, causal_blocks):
    qb, w = q_ref.shape
    hd = w // N_HEADS
    hv = hd + ONES_ROWS
    nkb_total = k_ref.shape[0] // ROW_BLOCK
    j = pl.program_id(1)
    nkb = jnp.minimum(j + 1, nkb_total) if causal_blocks else nkb_total

    @pl.when(j == 0)
    def _prepare_batch_row():
        def blk(i, carry):
            rows = pl.ds(pl.multiple_of(i * ROW_BLOCK, ROW_BLOCK), ROW_BLOCK)
            kbf_scr[rows, :] = k_ref[rows, :].astype(BF16)
            v_t = v_ref[rows, :].T.astype(BF16)
            cs = cc_ref[rows, :] * LOG2E
            for h in range(N_HEADS):
                vt_scr[h * hv:h * hv + hd, rows] = v_t[h * hd:(h + 1) * hd, :]
                ck_scr[h, rows, :] = jnp.broadcast_to(cs[:, BF0 + h:BF0 + h + 1], (ROW_BLOCK, qb))
            return carry
        lax.fori_loop(0, nkb_total, blk, 0)
        pad = vt_scr.shape[1] - nkb_total * ROW_BLOCK
        for h in range(N_HEADS):
            if pad:
                vt_scr[h * hv:h * hv + hd, nkb_total * ROW_BLOCK:] = jnp.zeros((hd, pad), BF16)
            vt_scr[h * hv + hd:(h + 1) * hv, :] = jnp.ones((ONES_ROWS, vt_scr.shape[1]), BF16)

    wq = _pair_weights((q_ref[...] * (hd ** -0.5 * LOG2E)).T, hd, N_HEADS)
    qcol = pl.ds(pl.multiple_of(j * qb + q_off, ROW_BLOCK), qb)
    cq = [cr_ref[h:h + 1, qcol] * LOG2E for h in range(N_HEADS)]
    qrow = j * qb + q_off + lax.broadcasted_iota(I32, (1, qb), 1)
    sub = lax.broadcasted_iota(I32, (KEY_STEP, 1), 0)

    def logits(i):
        k0 = pl.multiple_of(i * KEY_STEP, KEY_STEP)
        krows = pl.ds(k0, KEY_STEP)
        kblk = kbf_scr[krows, :]
        kidx = k0 + sub
        ok = (kidx >= key_lo) & (kidx <= qrow)
        tiles = []
        for pair in range(N_HEADS // 2):
            s2 = jnp.dot(kblk, wq[pair], preferred_element_type=F32)
            for i2 in range(2):
                h = 2 * pair + i2
                tiles.append(jnp.where(ok, s2[:, i2 * qb:(i2 + 1) * qb] - ck_scr[h, krows, :], NEG))
        return tiles

    def value_t(h, i):
        return vt_scr[h * hv:(h + 1) * hv, pl.ds(pl.multiple_of(i * KEY_STEP, KEY_STEP), KEY_STEP)]

    n_steps = lax.shift_right_logical(nkb + (STEP_BLOCKS - 1), _log2(STEP_BLOCKS))
    o_t = _attend_two_pass(n_steps, logits, value_t, cq, s_scr, acc_scr, qb, hd)
    y_ref[...] = o_t.T * _silu(g_ref[...])


def _fox(q_arr, q_col, k_arr, k_col, v_arr, v_col, g_arr, g_col, cc, cr, key_lo, q_off, causal_blocks):
    bsz, lq, _ = q_arr.shape
    lk = k_arr.shape[1]
    w = GROUP_W
    nqb = lq // ROW_BLOCK
    key_rows = pl.cdiv(lk, KEY_STEP) * KEY_STEP
    return pl.pallas_call(
        functools.partial(_fox_kernel, key_lo=key_lo, q_off=q_off, causal_blocks=causal_blocks),
        grid=(bsz, nqb),
        in_specs=[pl.BlockSpec((None, ROW_BLOCK, w), lambda b, j: (b, j, q_col)),
                  pl.BlockSpec((None, lk, w), lambda b, j: (b, 0, k_col)),
                  pl.BlockSpec((None, lk, w), lambda b, j: (b, 0, v_col)),
                  pl.BlockSpec((None, ROW_BLOCK, w), lambda b, j: (b, j, g_col)),
                  pl.BlockSpec((None, lk, LANES), lambda b, j: (b, 0, 0)),
                  pl.BlockSpec((None, SUBLANES, lk), lambda b, j: (b, 0, 0))],
        out_specs=pl.BlockSpec((None, ROW_BLOCK, w), lambda b, j: (b, j, 0)),
        out_shape=jax.ShapeDtypeStruct((bsz, lq, w), F32),
        scratch_shapes=[pltpu.VMEM((key_rows, w), BF16),
                        pltpu.VMEM((w + N_HEADS * ONES_ROWS, key_rows), BF16),
                        pltpu.VMEM((N_HEADS, key_rows, ROW_BLOCK), F32),
                        pltpu.VMEM((N_HEADS * key_rows // KEY_STEP, KEY_STEP, ROW_BLOCK), F32),
                        pltpu.VMEM((w + N_HEADS * ONES_ROWS, ROW_BLOCK), F32)],
        compiler_params=_cparams("parallel", "arbitrary"),
        name="fox",
    )(q_arr, k_arr, v_arr, g_arr, cc, cr)


def _dsa_kernel(cq_ref, cg_ref, ciq_ref, mq_ref, ckv_ref, mk_ref, y_ref,
                key_scr, kvb_scr, vt_scr, mkb_scr, s_scr, acc_scr,
                *, k_top, key_lo, key_hi, chunk_causal):
    qb, w = cq_ref.shape
    hd = w // N_HEADS
    nkb_total = ckv_ref.shape[0] // ROW_BLOCK
    j = pl.program_id(1)
    nkb = jnp.minimum(j + 1, nkb_total) if chunk_causal else nkb_total
    idx_scale = (H_IDX * D_IDX) ** -0.5

    @pl.when(j == 0)
    def _prepare_batch_row():
        def blk(i, carry):
            rows = pl.ds(pl.multiple_of(i * ROW_BLOCK, ROW_BLOCK), ROW_BLOCK)
            kv = ckv_ref[rows, :]
            kvb_scr[rows, :] = kv.astype(BF16)
            vt_scr[0:hd, rows] = kv.T[hd:2 * hd, :].astype(BF16)
            mkb_scr[rows, :] = mk_ref[rows, :].astype(BF16)
            return carry
        lax.fori_loop(0, nkb_total, blk, 0)
        pad = vt_scr.shape[1] - nkb_total * ROW_BLOCK
        if pad:
            vt_scr[0:hd, nkb_total * ROW_BLOCK:] = jnp.zeros((hd, pad), BF16)
        vt_scr[hd:, :] = jnp.ones((ONES_ROWS, vt_scr.shape[1]), BF16)

    def pad_rows(x):
        return jnp.concatenate([x, jnp.zeros((LANES - x.shape[0], qb), F32)], axis=0)

    iq_t = ciq_ref[...].T
    iw_t = mq_ref[...].T[IW0:IW0 + H_IDX, :]
    q_t = (cq_ref[...] * (hd ** -0.5 * LOG2E)).T
    iq_rhs = [jnp.concatenate([pad_rows(iq_t[(2 * p + i) * D_IDX:(2 * p + i + 1) * D_IDX, :])
                               for i in range(2)], axis=1).astype(BF16) for p in range(H_IDX // 2)]
    q_rhs = [jnp.concatenate([pad_rows(q_t[(2 * p + i) * hd:(2 * p + i + 1) * hd, :])
                              for i in range(2)], axis=1).astype(BF16) for p in range(N_HEADS // 2)]

    qrow = j * qb + lax.broadcasted_iota(I32, (1, qb), 1)
    if chunk_causal:
        hi = (lax.shift_right_logical(qrow, 6) + 1) * CHUNK
    else:
        hi = jnp.full((1, qb), key_hi, I32)
    sub = lax.broadcasted_iota(I32, (KEY_STEP, 1), 0)
    n_steps = lax.shift_right_logical(nkb + (STEP_BLOCKS - 1), _log2(STEP_BLOCKS))

    def score_step(i, carry):
        k0 = pl.multiple_of(i * KEY_STEP, KEY_STEP)
        mk = mkb_scr[pl.ds(k0, KEY_STEP), :]
        acc = jnp.zeros((KEY_STEP, qb), F32)
        for p in range(H_IDX // 2):
            sc2 = jnp.dot(mk, iq_rhs[p], preferred_element_type=F32)
            for i in range(2):
                h = 2 * p + i
                acc = acc + jnp.maximum(sc2[:, i * qb:(i + 1) * qb], 0.0) * iw_t[h:h + 1, :]
        score = acc * idx_scale + 0.0
        kidx = k0 + sub
        adm = (kidx >= key_lo) & (kidx < hi)
        u = pltpu.bitcast(score, I32)
        key = u ^ (lax.shift_right_arithmetic(u, 31) & np.int32(0x7FFFFFFF))
        key_scr[pl.ds(k0, KEY_STEP), :] = jnp.where(adm, key, INT_MIN)
        return carry

    lax.fori_loop(0, n_steps, score_step, 0)

    def count(pred):
        def cb(i, c8):
            k0 = pl.multiple_of(i * KEY_STEP, KEY_STEP)
            ind = jnp.where(pred(key_scr[pl.ds(k0, KEY_STEP), :], k0), 1, 0).astype(I32)
            return c8 + jnp.sum(ind.reshape(KEY_STEP // SUBLANES, SUBLANES, qb), axis=0)
        c8 = lax.fori_loop(0, n_steps, cb, jnp.zeros((SUBLANES, qb), I32))
        return jnp.sum(c8, axis=0, keepdims=True)

    def bisect(n_bits, count_ge, need):
        def bit_body(i, carry):
            ans, cnt_ans = carry
            cand = ans | lax.shift_left(np.int32(1), jnp.int32(n_bits - 1) - i)
            cnt = count_ge(cand)
            ok = cnt >= need
            return jnp.where(ok, cand, ans), jnp.where(ok, cnt, cnt_ans)
        return lax.fori_loop(0, n_bits, bit_body,
                             (jnp.zeros((1, qb), I32), jnp.full((1, qb), np.int32(2 ** 30), I32)))

    ans, cnt_thr = bisect(32, lambda cand: count(lambda kk, k0: kk >= (cand ^ INT_MIN)), k_top)
    thr = ans ^ INT_MIN
    tie = (cnt_thr > k_top) & (thr != INT_MIN)

    @pl.when(jnp.max(jnp.where(tie, 1, 0)) > 0)
    def _break_ties():
        n_rev_bits = _log2(pl.next_power_of_2(key_scr.shape[0]))
        rev_base = np.int32(2 ** n_rev_bits - 1)
        need = k_top - count(lambda kk, k0: kk > thr)
        ans2, _ = bisect(n_rev_bits, lambda cand: count(
            lambda kk, k0: (kk == thr) & ((rev_base - (k0 + sub)) >= cand)), need)

        def demote(i, carry):
            k0 = pl.multiple_of(i * KEY_STEP, KEY_STEP)
            kk = key_scr[pl.ds(k0, KEY_STEP), :]
            lose = tie & (kk == thr) & ((rev_base - (k0 + sub)) < ans2)
            key_scr[pl.ds(k0, KEY_STEP), :] = jnp.where(lose, INT_MIN, kk)
            return carry

        lax.fori_loop(0, n_steps, demote, 0)

    thr_sel = jnp.maximum(thr, INT_MIN + 1)

    def logits(i):
        krows = pl.ds(pl.multiple_of(i * KEY_STEP, KEY_STEP), KEY_STEP)
        sel = key_scr[krows, :] >= thr_sel
        kv = kvb_scr[krows, :]
        tiles = []
        for pair in range(N_HEADS // 2):
            s2 = jnp.dot(kv, q_rhs[pair], preferred_element_type=F32)
            tiles += [jnp.where(sel, s2[:, i2 * qb:(i2 + 1) * qb], NEG) for i2 in range(2)]
        return tiles

    def value_t(h, i):
        return vt_scr[:, pl.ds(pl.multiple_of(i * KEY_STEP, KEY_STEP), KEY_STEP)]

    no_shift = [jnp.zeros((1, qb), F32)] * N_HEADS
    o_t = _attend_two_pass(n_steps, logits, value_t, no_shift, s_scr, acc_scr, qb, hd)
    y_ref[...] = o_t.T * _silu(cg_ref[...])


def _dsa(p3, cols, ckv_arr, ckv_col, mk_arr, mk_col, k_top, key_lo, key_hi, chunk_causal):
    bsz, lq, _ = p3.shape
    lk = ckv_arr.shape[1]
    w = GROUP_W
    cq_col, cg_col, ciq_col, mq_col = cols
    key_rows = pl.cdiv(lk, KEY_STEP) * KEY_STEP
    return pl.pallas_call(
        functools.partial(_dsa_kernel, k_top=k_top, key_lo=key_lo, key_hi=key_hi,
                          chunk_causal=chunk_causal),
        grid=(bsz, lq // ROW_BLOCK),
        in_specs=[pl.BlockSpec((None, ROW_BLOCK, w), lambda b, j: (b, j, cq_col)),
                  pl.BlockSpec((None, ROW_BLOCK, w), lambda b, j: (b, j, cg_col)),
                  pl.BlockSpec((None, ROW_BLOCK, w), lambda b, j: (b, j, ciq_col)),
                  pl.BlockSpec((None, ROW_BLOCK, LANES), lambda b, j: (b, j, mq_col)),
                  pl.BlockSpec((None, lk, LANES), lambda b, j: (b, 0, ckv_col)),
                  pl.BlockSpec((None, lk, LANES), lambda b, j: (b, 0, mk_col))],
        out_specs=pl.BlockSpec((None, ROW_BLOCK, w), lambda b, j: (b, j, 0)),
        out_shape=jax.ShapeDtypeStruct((bsz, lq, w), F32),
        scratch_shapes=[pltpu.VMEM((key_rows, ROW_BLOCK), I32),
                        pltpu.VMEM((key_rows, LANES), BF16),
                        pltpu.VMEM((w // N_HEADS + ONES_ROWS, key_rows), BF16),
                        pltpu.VMEM((key_rows, LANES), BF16),
                        pltpu.VMEM((N_HEADS * key_rows // KEY_STEP, KEY_STEP, ROW_BLOCK), F32),
                        pltpu.VMEM((w + N_HEADS * ONES_ROWS, ROW_BLOCK), F32)],
        compiler_params=_cparams("parallel", "arbitrary"),
        name="dsa",
    )(p3, p3, p3, p3, ckv_arr, mk_arr)


def _merge_kernel(ya_ref, yb_ref, yc_ref, yd_ref, x_ref, w_ref, g_ref, o_ref, *, period, valid_lo, valid_hi):
    tm = x_ref.shape[0]
    gw = ya_ref.shape[1]
    acc = jnp.zeros(o_ref.shape, F32)
    for i, y_ref in enumerate((ya_ref, yb_ref, yc_ref, yd_ref)):
        acc = acc + jnp.dot(y_ref[...].astype(BF16), w_ref[i * gw:(i + 1) * gw, :],
                            preferred_element_type=F32)
    ms = jnp.mean(acc * acc, axis=-1, keepdims=True)
    out = x_ref[...] + (acc * lax.rsqrt(ms + EPS)) * g_ref[...]
    r0 = pl.program_id(0) * tm
    local = (r0 - (r0 // period) * period) + lax.broadcasted_iota(I32, (tm, 1), 0)
    local = jnp.where(local >= period, local - period, local)
    valid = (local >= valid_lo) & (local < valid_hi)
    o_ref[...] = jnp.where(valid, out, 0.0)


def _merge(ys, x2d, w_bf16, g, tm, period, valid_lo, valid_hi):
    rows, d = x2d.shape
    assert tm <= period
    gw = ys[0].shape[1]
    yspec = pl.BlockSpec((tm, gw), lambda i: (i, 0))
    return pl.pallas_call(
        functools.partial(_merge_kernel, period=period, valid_lo=valid_lo, valid_hi=valid_hi),
        grid=(rows // tm,),
        in_specs=[yspec, yspec, yspec, yspec,
                  pl.BlockSpec((tm, d), lambda i: (i, 0)),
                  pl.BlockSpec(w_bf16.shape, lambda i: (0, 0)),
                  pl.BlockSpec((1, d), lambda i: (0, 0))],
        out_specs=pl.BlockSpec((tm, d), lambda i: (i, 0)),
        out_shape=jax.ShapeDtypeStruct((rows, d), F32),
        compiler_params=_cparams("parallel"),
        name="merge",
    )(*ys, x2d, w_bf16, g)


def _row_tile(rows):
    return next(t for t in (512, 256, ROW_BLOCK) if rows % t == 0)


def _column_layout(gw):
    sizes = [gw] * 4 + [gw, gw, gw, FOX_HEADS, gw] + [gw, gw // 4, gw // 4, gw, H_IDX * D_IDX, D_IDX, H_IDX] + [gw] * 4
    names = ["aq", "af", "ai", "ag", "bq", "bk", "bv", "bf", "bg",
             "cq", "ck", "cv", "cg", "ciq", "cik", "ciw", "dq", "dk", "dv", "dg"]
    start = dict(zip(names, np.cumsum([0] + sizes[:-1])))
    size = dict(zip(names, sizes))
    src = -np.ones((16 * gw,), np.int64)
    def put(dst, name, off=0):
        src[dst + off:dst + off + size[name]] = np.arange(start[name], start[name] + size[name])
    for i, n in enumerate(["aq", "af", "ai", "ag", "bq", "bk", "bv", "bg", "cq", "cg", "ciq"]):
        put(i * gw, n)
    ckv0 = 11 * gw
    put(ckv0, "ck")
    put(ckv0 + gw // 4, "cv")
    misc0 = ckv0 + LANES
    put(misc0, "cik", IK0)
    put(misc0, "ciw", IW0)
    put(misc0, "bf", BF0)
    for i, n in enumerate(["dq", "dk", "dv", "dg"]):
        put(12 * gw + i * gw, n)
    return src


def _relayout_w_in(w_in_l, src):
    cols = jnp.take(w_in_l, jnp.asarray(np.maximum(src, 0)), axis=1)
    return jnp.where(jnp.asarray(src >= 0)[None, :], cols, 0.0).astype(BF16)


def _ret_tables(pos, chunk, gw):
    hd = gw // N_HEADS
    half = hd // 2
    inv = ROPE_BASE ** (-jnp.arange(half, dtype=F32) / half)
    ang = pos.astype(F32)[:, None] * inv[None, :]
    cos_h = jnp.concatenate([jnp.cos(ang), jnp.cos(ang)], axis=-1)
    sin_h = jnp.concatenate([-jnp.sin(ang), jnp.sin(ang)], axis=-1)
    cos = jnp.tile(cos_h, (1, N_HEADS))
    sin = jnp.tile(sin_h, (1, N_HEADS))
    lg = jnp.log(1.0 - 2.0 ** (-5.0 - jnp.arange(N_HEADS, dtype=F32)))
    lg_l = jnp.repeat(lg, hd)[None, :]
    t = jnp.arange(chunk, dtype=F32)[:, None]
    eb = jnp.exp((t + 1.0) * lg_l)
    ke2s = jnp.exp((chunk - 1.0 - t) * lg_l)
    dend = jnp.exp(chunk * lg_l)
    dt = jnp.arange(chunk, dtype=F32)[:, None] - jnp.arange(chunk, dtype=F32)[None, :]
    gam = jnp.concatenate([jnp.where(dt >= 0, jnp.exp(dt * lg[h]), 0.0) for h in range(N_HEADS)], axis=0)
    return cos, sin, eb, ke2s, dend, gam


def _state_to_bd(state):
    bsz, h, k, v = state.shape
    eye = jnp.eye(h, dtype=state.dtype)
    st = jnp.einsum('bhkv,hg->bhvgk', state, eye)
    return st.reshape(bsz, h * v, h * k)


def _bd_to_state(st, h):
    bsz, hv, hk = st.shape
    st5 = st.reshape(bsz, h, hv // h, h, hk // h)
    diag = jnp.stack([st5[:, i, :, i, :] for i in range(h)], axis=1)
    return jnp.swapaxes(diag, 2, 3)


def kernel(x_prompt, x_sample, state_hgrn, cache_fox_k, cache_fox_v, cache_fox_logf, cache_dsa_k,
           cache_dsa_v, cache_dsa_idx_k, state_ret, meta_tokens, w_in, w_out, fox_bias, hgrn_lb,
           norm_pre, norm_post):
    bsz, seq, d = x_prompt.shape
    dbsz, t_new, _ = x_sample.shape
    depth = w_in.shape[0]
    past = cache_fox_k.shape[2]
    gw = d // N_GROUPS
    hd = gw // N_HEADS
    assert gw == 2 * LANES and seq % ROW_BLOCK == 0 and past % ROW_BLOCK == 0 and t_new <= ROW_BLOCK
    assert t_new % SUBLANES == 0

    pad_front = ROW_BLOCK - N_META
    lp = ROW_BLOCK + seq
    ls = ROW_BLOCK
    lks = past + ROW_BLOCK
    n_chunks_p = lp // CHUNK
    k_top_p = min(TOP_K_MAX, seq // 4)
    k_top_s = min(TOP_K_MAX, (past + t_new) // 4)

    src = _column_layout(gw)
    n_cols = src.shape[0]
    col = {"a": 0, "bq": 4, "bk": 5, "bv": 6, "bg": 7, "cq": 8, "cg": 9, "ciq": 10, "d": 3}
    ckv_col = (11 * gw) // LANES
    misc_col = ckv_col + 1

    sm = jax.nn.softmax(hgrn_lb.astype(F32), axis=0)
    lbs = jnp.cumsum(sm, axis=0) - sm[0:1]

    xp = jnp.concatenate([jnp.zeros((bsz, pad_front, d), F32),
                          jnp.broadcast_to(meta_tokens.astype(F32)[None], (bsz, N_META, d)),
                          x_prompt], axis=1)
    xs = jnp.concatenate([x_sample, jnp.zeros((dbsz, ls - t_new, d), F32)], axis=1)

    tab_p = _ret_tables(jnp.arange(lp) - ROW_BLOCK, CHUNK, gw)
    tab_s = _ret_tables(past + jnp.arange(ls), t_new, gw)
    zero_state_p = jnp.zeros((bsz, gw, gw), F32)

    outs_p = {k: [] for k in ("hgrn", "fk", "fv", "fl", "ck", "cv", "ci", "ret")}
    outs_s = {k: [] for k in ("hgrn", "fk", "fv", "fl", "ck", "cv", "ci", "ret")}

    for l in range(depth):
        w_l = _relayout_w_in(w_in[l], src)
        w_o = w_out[l].astype(BF16)
        g_pre = norm_pre[l][None, :]
        g_post = norm_post[l][None, :]
        lb = lbs[l][None, :]
        bias = jnp.zeros((1, LANES), F32).at[0, BF0:BF0 + FOX_HEADS].set(fox_bias[l].astype(F32))

        p = _project(xp.reshape(bsz * lp, d), g_pre, w_l, _row_tile(bsz * lp)).reshape(bsz, lp, n_cols)
        ya, st_a = _hgrn(p, 0, lb, zero_state_p, CHUNK, n_chunks_p)
        yd, st_d = _ret(p, 3, tab_p, zero_state_p, CHUNK, n_chunks_p)
        lf, cc, cr = _foxprep(p, misc_col, bias, 0)
        yb = _fox(p, col["bq"], p, col["bk"], p, col["bv"], p, col["bg"], cc, cr, pad_front, 0, True)
        yc = _dsa(p, (col["cq"], col["cg"], col["ciq"], misc_col), p, ckv_col, p, misc_col,
                  k_top_p, pad_front, 0, True)
        flat = lambda a: a.reshape(bsz * lp, gw)
        xp = _merge([flat(ya), flat(yb), flat(yc), flat(yd)], xp.reshape(bsz * lp, d), w_o, g_post,
                    _row_tile(bsz * lp), lp, pad_front, lp).reshape(bsz, lp, d)
        pv = p[:, pad_front:, :]
        outs_p["hgrn"].append(_bd_to_state(st_a, N_HEADS))
        outs_p["ret"].append(_bd_to_state(st_d, N_HEADS))
        outs_p["fk"].append(pv[:, :, 5 * gw:6 * gw].reshape(bsz, -1, N_HEADS, hd))
        outs_p["fv"].append(pv[:, :, 6 * gw:7 * gw].reshape(bsz, -1, N_HEADS, hd))
        outs_p["fl"].append(lf[:, pad_front:, BF0:BF0 + FOX_HEADS])
        outs_p["ck"].append(pv[:, :, 11 * gw:11 * gw + hd])
        outs_p["cv"].append(pv[:, :, 11 * gw + hd:11 * gw + 2 * hd])
        outs_p["ci"].append(pv[:, :, misc_col * LANES + IK0:misc_col * LANES + IK0 + D_IDX])

        ps = _project(xs.reshape(dbsz * ls, d), g_pre, w_l, _row_tile(dbsz * ls)).reshape(dbsz, ls, n_cols)
        ya, st_a = _hgrn(ps, 0, lb, _state_to_bd(state_hgrn[l].astype(F32)), t_new, 1)
        yd, st_d = _ret(ps, 3, tab_s, _state_to_bd(state_ret[l].astype(F32)), t_new, 1)
        z = jnp.concatenate(
            [jnp.pad(cache_fox_logf[l].astype(F32), ((0, 0), (0, 0), (BF0, LANES - BF0 - FOX_HEADS))),
             ps[:, :, misc_col * LANES:(misc_col + 1) * LANES]], axis=1)
        lf, cc, cr = _foxprep(z, 0, bias, past)
        k_all = jnp.concatenate([cache_fox_k[l].reshape(dbsz, past, gw), ps[:, :, 5 * gw:6 * gw]], axis=1)
        v_all = jnp.concatenate([cache_fox_v[l].reshape(dbsz, past, gw), ps[:, :, 6 * gw:7 * gw]], axis=1)
        yb = _fox(ps, col["bq"], k_all, 0, v_all, 0, ps, col["bg"], cc, cr, 0, past, False)
        ckv_all = jnp.concatenate(
            [jnp.concatenate([cache_dsa_k[l], cache_dsa_v[l]], axis=-1).astype(F32),
             ps[:, :, 11 * gw:11 * gw + LANES]], axis=1)
        mk_all = jnp.concatenate(
            [jnp.pad(cache_dsa_idx_k[l].astype(F32), ((0, 0), (0, 0), (IK0, LANES - IK0 - D_IDX))),
             ps[:, :, misc_col * LANES:(misc_col + 1) * LANES]], axis=1)
        yc = _dsa(ps, (col["cq"], col["cg"], col["ciq"], misc_col), ckv_all, 0, mk_all, 0,
                  k_top_s, 0, past + t_new, False)
        flat = lambda a: a.reshape(dbsz * ls, gw)
        xs = _merge([flat(ya), flat(yb), flat(yc), flat(yd)], xs.reshape(dbsz * ls, d), w_o, g_post,
                    ls, ls, 0, t_new).reshape(dbsz, ls, d)
        pn = ps[:, :t_new, :]
        outs_s["hgrn"].append(_bd_to_state(st_a, N_HEADS))
        outs_s["ret"].append(_bd_to_state(st_d, N_HEADS))
        outs_s["fk"].append(pn[:, :, 5 * gw:6 * gw].reshape(dbsz, -1, N_HEADS, hd))
        outs_s["fv"].append(pn[:, :, 6 * gw:7 * gw].reshape(dbsz, -1, N_HEADS, hd))
        outs_s["fl"].append(lf[:, past:past + t_new, BF0:BF0 + FOX_HEADS])
        outs_s["ck"].append(pn[:, :, 11 * gw:11 * gw + hd])
        outs_s["cv"].append(pn[:, :, 11 * gw + hd:11 * gw + 2 * hd])
        outs_s["ci"].append(pn[:, :, misc_col * LANES + IK0:misc_col * LANES + IK0 + D_IDX])

    dt = x_prompt.dtype
    st = lambda xs_list: jnp.stack(xs_list, axis=0).astype(dt)
    order = ("hgrn", "fk", "fv", "fl", "ck", "cv", "ci", "ret")
    return ((xp[:, ROW_BLOCK:, :].astype(dt), xs[:, :t_new, :].astype(dt))
            + tuple(st(outs_p[k]) for k in order) + tuple(st(outs_s[k]) for k in order))
```

```python
import functools

import numpy as np
import jax
import jax.numpy as jnp
from jax import lax
from jax.experimental import pallas as pl
from jax.experimental.pallas import tpu as pltpu

F32 = jnp.float32
BF16 = jnp.bfloat16
Y_DTYPE = BF16
I32 = jnp.int32
LOG2E = float(np.log2(np.e))
ONES_ROWS = 16

N_META = 16
CHUNK = 64
N_GROUPS = 4
N_HEADS = 4
H_IDX = 8
D_IDX = 32
TOP_K_MAX = 256
ROPE_BASE = 10000.0
EPS = 1e-6
FOX_HEADS = 4

LANES = 128
SUBLANES = 8
ROW_BLOCK = 128
GROUP_W = 2 * LANES
VMEM_LIMIT_BYTES = 56 * 1024 * 1024

IK0 = 0
IW0 = 32
BF0 = 64

NEG = -1e30
M_FLOOR = -1e20
INT_MIN = np.int32(-2 ** 31)
STEP_BLOCKS = 4
KEY_STEP = STEP_BLOCKS * ROW_BLOCK


def _cparams(*sem):
    return pltpu.CompilerParams(dimension_semantics=sem, vmem_limit_bytes=VMEM_LIMIT_BYTES)


def _split3(x):
    h = x.astype(BF16)
    r = x - h.astype(F32)
    m = r.astype(BF16)
    lo = (r - m.astype(F32)).astype(BF16)
    return h, m, lo


def _dot_exact_lhs(a_bf16, x):
    d = lambda y: jnp.dot(a_bf16, y, preferred_element_type=F32)
    h, m, lo = _split3(x)
    return d(h) + d(m) + d(lo)


def _dot_exact_rhs(x, a_bf16):
    d = lambda y: jnp.dot(y, a_bf16, preferred_element_type=F32)
    h, m, lo = _split3(x)
    return d(h) + d(m) + d(lo)


def _dot_nt(a, b):
    return lax.dot_general(a, b, (((1,), (1,)), ((), ())), preferred_element_type=F32)


def _dot_tn(a, b):
    return lax.dot_general(a, b, (((0,), (0,)), ((), ())), preferred_element_type=F32)


def _log2(n):
    assert n > 0 and n & (n - 1) == 0, n
    return n.bit_length() - 1


def _head_masks(width, n_heads):
    lane = lax.broadcasted_iota(I32, (1, width), 1)
    sh = _log2(width // n_heads)
    return [(lax.shift_right_logical(lane, sh) == h).astype(F32) for h in range(n_heads)]


def _block_diag(width, n_heads, value):
    r = lax.broadcasted_iota(I32, (width, width), 0)
    c = lax.broadcasted_iota(I32, (width, width), 1)
    sh = _log2(width // n_heads)
    same = lax.shift_right_logical(r, sh) == lax.shift_right_logical(c, sh)
    return jnp.where(same, value, 0.0).astype(F32)


def _silu(x):
    return x * jax.nn.sigmoid(x)


def _proj_kernel(x_ref, g_ref, w_ref, o_ref, *, col_chunk):
    x = x_ref[...]
    ms = jnp.mean(x * x, axis=-1, keepdims=True)
    xn = ((x * lax.rsqrt(ms + EPS)) * g_ref[...]).astype(BF16)
    for c in range(o_ref.shape[1] // col_chunk):
        cols = slice(c * col_chunk, (c + 1) * col_chunk)
        o_ref[:, cols] = jnp.dot(xn, w_ref[:, cols], preferred_element_type=F32)


def _project(x2d, g, w_bf16, tm):
    rows, d = x2d.shape
    n = w_bf16.shape[1]
    return pl.pallas_call(
        functools.partial(_proj_kernel, col_chunk=1024),
        grid=(rows // tm,),
        in_specs=[pl.BlockSpec((tm, d), lambda i: (i, 0)),
                  pl.BlockSpec((1, d), lambda i: (0, 0)),
                  pl.BlockSpec((d, n), lambda i: (0, 0))],
        out_specs=pl.BlockSpec((tm, n), lambda i: (i, 0)),
        out_shape=jax.ShapeDtypeStruct((rows, n), F32),
        compiler_params=_cparams("parallel"),
        name="proj",
    )(x2d, g, w_bf16)


def _head_scores(qa, ka, hm):
    q_stack = jnp.concatenate([qa * hm[h] for h in range(N_HEADS)], axis=0).astype(BF16)
    return _dot_nt(q_stack, ka.astype(BF16))


def _hier_constants(chunk):
    halves = [chunk >> (i + 1) for i in range(_log2(chunk))]
    r = lax.broadcasted_iota(I32, (chunk, chunk), 0)
    c = lax.broadcasted_iota(I32, (chunk, chunk), 1)
    ts = lax.broadcasted_iota(I32, (N_HEADS * chunk, chunk), 0) & (chunk - 1)
    ss = lax.broadcasted_iota(I32, (N_HEADS * chunk, chunk), 1)
    sels, masks = [], []
    for h in halves:
        sh = _log2(h)
        grp = lambda x: lax.shift_right_logical(x, sh + 1)
        if h < SUBLANES:
            sels.append((c == lax.shift_left(grp(r), sh + 1) + (h - 1)).astype(BF16))
        upper_t = (lax.shift_right_logical(ts, sh) & 1).astype(F32)
        lower_s = 1.0 - (lax.shift_right_logical(ss, sh) & 1).astype(F32)
        masks.append(jnp.where(grp(ts) == grp(ss), upper_t * lower_s, 0.0))
    return halves, jnp.concatenate(sels, axis=0), masks


def _decayed_scores(q, k, b, hier, hm):
    halves, sel_small, masks = hier
    chunk, w = q.shape
    small_refs = _dot_exact_lhs(sel_small, b)
    att, n_small = None, 0
    for h, mask in zip(halves, masks):
        if h < SUBLANES:
            ref = small_refs[n_small * chunk:(n_small + 1) * chunk, :]
            n_small += 1
        else:
            ref = jnp.concatenate([jnp.broadcast_to(b[g0 + h - 1:g0 + h, :], (2 * h, w))
                                   for g0 in range(0, chunk, 2 * h)], axis=0)
        qa = q * jnp.exp(jnp.minimum(b - ref, 0.0))
        ka = k * jnp.exp(jnp.minimum(ref - b, 0.0))
        term = _head_scores(qa, ka, hm) * mask
        att = term if att is None else att + term
    return att


def _gla_chunk(att, qe, ke2, v, decay_end, st, hm, bd):
    c = qe.shape[0]
    o_stack = jnp.dot(att.astype(BF16), v.astype(BF16), preferred_element_type=F32)
    o_intra = o_stack[0:c] * hm[0]
    for h in range(1, N_HEADS):
        o_intra = o_intra + o_stack[h * c:(h + 1) * c] * hm[h]
    o_inter = _dot_nt(qe.astype(BF16), st.astype(BF16))
    st_new = st * decay_end + _dot_tn(v.astype(BF16), ke2.astype(BF16)) * bd
    return o_inter + o_intra, st_new


def _head_rms_gate(o, gate, bd_mean_bf16):
    h, m, _ = _split3(o * o)
    ms = (jnp.dot(h, bd_mean_bf16, preferred_element_type=F32)
          + jnp.dot(m, bd_mean_bf16, preferred_element_type=F32))
    return (o * lax.rsqrt(ms + EPS)) * _silu(gate)


def _gla_prologue(st0_ref, y_ref, st_ref, chunk, n_chunks):
    @pl.when(pl.program_id(1) == 0)
    def _():
        st_ref[...] = st0_ref[...]
    tail = n_chunks * chunk
    if tail < y_ref.shape[1]:
        y_ref[:, tail:, :] = jnp.zeros((y_ref.shape[0], y_ref.shape[1] - tail, y_ref.shape[2]),
                                       y_ref.dtype)


def _hgrn_kernel(a_ref, lb_ref, st0_ref, y_ref, st_ref, *, chunk, n_chunks):
    n_b, _, w = y_ref.shape
    hm = _head_masks(w, N_HEADS)
    bd = _block_diag(w, N_HEADS, 1.0)
    bd_mean = _block_diag(w, N_HEADS, 1.0 / (w // N_HEADS)).astype(BF16)
    r = lax.broadcasted_iota(I32, (chunk, chunk), 0)
    s = lax.broadcasted_iota(I32, (chunk, chunk), 1)
    tri = (r >= s).astype(BF16)
    hier = _hier_constants(chunk)
    bd_ones = bd.astype(BF16)
    lb = lb_ref[...]
    _gla_prologue(st0_ref, y_ref, st_ref, chunk, n_chunks)

    def body(c, carry):
        rows = pl.ds(pl.multiple_of(c * chunk, chunk), chunk)
        for g in range(n_b):
            q = a_ref[g, rows, 0:w]
            f = lb + (1.0 - lb) * jax.nn.sigmoid(a_ref[g, rows, w:2 * w])
            k = 1.0 - f
            v = a_ref[g, rows, 2 * w:3 * w]
            gate = a_ref[g, rows, 3 * w:4 * w]
            b = _dot_exact_lhs(tri, jnp.log(f))
            b_end = b[chunk - 1:chunk, :]
            att = _decayed_scores(q, k, b, hier, hm)
            o, st_new = _gla_chunk(att, q * jnp.exp(b), k * jnp.exp(b_end - b), v, jnp.exp(b_end),
                                   st_ref[g], hm, bd)
            o = o + jnp.dot((q * k).astype(BF16), bd_ones, preferred_element_type=F32) * v
            st_ref[g] = st_new
            y_ref[g, rows, :] = _head_rms_gate(o, gate, bd_mean).astype(y_ref.dtype)
        return carry

    lax.fori_loop(0, n_chunks, body, 0)


def _ret_kernel(d_ref, cos_ref, sin_ref, eb_ref, ke2s_ref, dend_ref, gam_ref, st0_ref, y_ref, st_ref,
                *, chunk, n_chunks):
    n_b, _, w = y_ref.shape
    hd = w // N_HEADS
    hm = _head_masks(w, N_HEADS)
    bd = _block_diag(w, N_HEADS, 1.0)
    bd_mean = _block_diag(w, N_HEADS, 1.0 / hd).astype(BF16)
    lane = lax.broadcasted_iota(I32, (1, w), 1)
    first_half = (lane & (hd - 1)) < (hd // 2)
    eb = eb_ref[...]
    ke2s = ke2s_ref[...]
    dend = dend_ref[...]
    gam = gam_ref[...]
    _gla_prologue(st0_ref, y_ref, st_ref, chunk, n_chunks)

    def rope(x, cos, sin_signed):
        swapped = jnp.where(first_half, pltpu.roll(x, w - hd // 2, 1), pltpu.roll(x, hd // 2, 1))
        return x * cos + swapped * sin_signed

    def body(c, carry):
        rows = pl.ds(pl.multiple_of(c * chunk, chunk), chunk)
        cos = cos_ref[rows, :]
        sin = sin_ref[rows, :]
        for g in range(n_b):
            q = rope(d_ref[g, rows, 0:w], cos, sin)
            k = rope(d_ref[g, rows, w:2 * w], cos, sin) * (hd ** -0.5)
            v = d_ref[g, rows, 2 * w:3 * w]
            gate = d_ref[g, rows, 3 * w:4 * w]
            att = _head_scores(q, k, hm) * gam
            o, st_new = _gla_chunk(att, q * eb, k * ke2s, v, dend, st_ref[g], hm, bd)
            st_ref[g] = st_new
            y_ref[g, rows, :] = _head_rms_gate(o, gate, bd_mean).astype(y_ref.dtype)
        return carry

    lax.fori_loop(0, n_chunks, body, 0)


def _gla_grid(bsz, lp, chunk, n_chunks):
    n_b = 2 if bsz % 2 == 0 else 1
    halves = n_chunks * chunk == lp and n_chunks % 2 == 0 and (lp // 2) % SUBLANES == 0
    n_seq = 2 if halves else 1
    return n_b, n_seq, n_chunks // n_seq


def _hgrn(p3, col_block, lb, st0, chunk, n_chunks):
    bsz, lp, _ = p3.shape
    w = lb.shape[1]
    n_b, n_seq, n_chunks_blk = _gla_grid(bsz, lp, chunk, n_chunks)
    rows = lp // n_seq
    return pl.pallas_call(
        functools.partial(_hgrn_kernel, chunk=chunk, n_chunks=n_chunks_blk),
        grid=(bsz // n_b, n_seq),
        in_specs=[pl.BlockSpec((n_b, rows, 4 * w), lambda b, s: (b, s, col_block)),
                  pl.BlockSpec((1, w), lambda b, s: (0, 0)),
                  pl.BlockSpec((n_b, w, w), lambda b, s: (b, 0, 0))],
        out_specs=[pl.BlockSpec((n_b, rows, w), lambda b, s: (b, s, 0)),
                   pl.BlockSpec((n_b, w, w), lambda b, s: (b, 0, 0))],
        out_shape=[jax.ShapeDtypeStruct((bsz, lp, w), Y_DTYPE),
                   jax.ShapeDtypeStruct((bsz, w, w), F32)],
        compiler_params=_cparams("parallel", "arbitrary"),
        name="hgrn",
    )(p3, lb, st0)


def _ret(p3, col_block, tables, st0, chunk, n_chunks):
    bsz, lp, _ = p3.shape
    w = st0.shape[1]
    n_b, n_seq, n_chunks_blk = _gla_grid(bsz, lp, chunk, n_chunks)
    rows = lp // n_seq
    cos, sin, eb, ke2s, dend, gam = tables
    full = lambda a: pl.BlockSpec(a.shape, lambda b, s: (0,) * a.ndim)
    per_seq = pl.BlockSpec((rows, w), lambda b, s: (s, 0))
    return pl.pallas_call(
        functools.partial(_ret_kernel, chunk=chunk, n_chunks=n_chunks_blk),
        grid=(bsz // n_b, n_seq),
        in_specs=[pl.BlockSpec((n_b, rows, 4 * w), lambda b, s: (b, s, col_block)),
                  per_seq, per_seq, full(eb), full(ke2s), full(dend), full(gam),
                  pl.BlockSpec((n_b, w, w), lambda b, s: (b, 0, 0))],
        out_specs=[pl.BlockSpec((n_b, rows, w), lambda b, s: (b, s, 0)),
                   pl.BlockSpec((n_b, w, w), lambda b, s: (b, 0, 0))],
        out_shape=[jax.ShapeDtypeStruct((bsz, lp, w), Y_DTYPE),
                   jax.ShapeDtypeStruct((bsz, w, w), F32)],
        compiler_params=_cparams("parallel", "arbitrary"),
        name="ret",
    )(p3, cos, sin, eb, ke2s, dend, gam, st0)


def _log_sigmoid(x):
    return -(jnp.maximum(-x, 0.0) + jnp.log(1.0 + jnp.exp(-jnp.abs(x))))


def _foxprep_kernel(z_ref, bias_ref, lf_ref, cc_ref, cr_ref, *, n_pass):
    nblk = z_ref.shape[0] // ROW_BLOCK
    r = lax.broadcasted_iota(I32, (ROW_BLOCK, ROW_BLOCK), 0)
    s = lax.broadcasted_iota(I32, (ROW_BLOCK, ROW_BLOCK), 1)
    tri = (r >= s).astype(BF16)
    bias = bias_ref[...]

    def body(i, carry):
        rows = pl.ds(pl.multiple_of(i * ROW_BLOCK, ROW_BLOCK), ROW_BLOCK)
        z = z_ref[rows, :]
        rowi = i * ROW_BLOCK + lax.broadcasted_iota(I32, (ROW_BLOCK, 1), 0)
        lf = jnp.where(rowi < n_pass, z, _log_sigmoid(z + bias))
        lf_ref[rows, :] = lf
        cs = _dot_exact_lhs(tri, lf) + carry
        cc_ref[rows, :] = cs
        cr_ref[:, rows] = cs.T[BF0:BF0 + SUBLANES, :]
        return cs[ROW_BLOCK - 1:ROW_BLOCK, :]

    lax.fori_loop(0, nblk, body, jnp.zeros((1, LANES), F32))


def _foxprep(z3, z_col, bias, n_pass):
    bsz, lk, _ = z3.shape
    return pl.pallas_call(
        functools.partial(_foxprep_kernel, n_pass=n_pass),
        grid=(bsz,),
        in_specs=[pl.BlockSpec((None, lk, LANES), lambda b: (b, 0, z_col)),
                  pl.BlockSpec((1, LANES), lambda b: (0, 0))],
        out_specs=[pl.BlockSpec((None, lk, LANES), lambda b: (b, 0, 0)),
                   pl.BlockSpec((None, lk, LANES), lambda b: (b, 0, 0)),
                   pl.BlockSpec((None, SUBLANES, lk), lambda b: (b, 0, 0))],
        out_shape=[jax.ShapeDtypeStruct((bsz, lk, LANES), F32),
                   jax.ShapeDtypeStruct((bsz, lk, LANES), F32),
                   jax.ShapeDtypeStruct((bsz, SUBLANES, lk), F32)],
        compiler_params=_cparams("parallel"),
        name="foxprep",
    )(z3, bias)


def _pair_weights(x_t, rows_per_head, n_heads):
    rowh = lax.shift_right_logical(lax.broadcasted_iota(I32, (x_t.shape[0], 1), 0),
                                   _log2(rows_per_head))
    only = lambda h: jnp.where(rowh == h, x_t, 0.0)
    return [jnp.concatenate([only(2 * p), only(2 * p + 1)], axis=1).astype(BF16)
            for p in range(n_heads // 2)]


def _attend_two_pass(n_steps, logits_fn, value_t_fn, shifts, s_scr, acc_scr, qb, hd):
    fold = lambda x, op: op(x.reshape(KEY_STEP // SUBLANES, SUBLANES, qb), axis=0)
    hv = hd + ONES_ROWS

    def max_step(i, ms):
        tiles = logits_fn(i)
        for h in range(N_HEADS):
            s_scr[i * N_HEADS + h] = tiles[h]
        return tuple(jnp.maximum(ms[h], fold(tiles[h], jnp.max)) for h in range(N_HEADS))

    ms = lax.fori_loop(0, n_steps, max_step,
                       tuple(jnp.full((SUBLANES, qb), M_FLOOR, F32) for _ in range(N_HEADS)))
    m_logit = [jnp.max(ms[h], axis=0, keepdims=True) + shifts[h] for h in range(N_HEADS)]
    acc_scr[...] = jnp.zeros(acc_scr.shape, F32)

    def sum_step(i, carry):
        for h in range(N_HEADS):
            p = jnp.exp2(s_scr[i * N_HEADS + h] - (m_logit[h] - shifts[h]))
            acc_scr[h * hv:(h + 1) * hv, :] += jnp.dot(value_t_fn(h, i), p.astype(BF16),
                                                       preferred_element_type=F32)
        return carry

    lax.fori_loop(0, n_steps, sum_step, 0)
    outs = []
    for h in range(N_HEADS):
        l = acc_scr[h * hv + hd:h * hv + hd + 1, :]
        outs.append(acc_scr[h * hv:h * hv + hd, :] / jnp.where(l > 0.0, l, 1.0))
    return jnp.concatenate(outs, axis=0)


def _fox_kernel(q_ref, k_ref, v_ref, g_ref, cc_ref, cr_ref, y_ref, kbf_scr, vt_scr, ck_scr, s_scr, acc_scr,
                *, key_lo, q_off, causal_blocks):
    qb, w = q_ref.shape
    hd = w // N_HEADS
    hv = hd + ONES_ROWS
    nkb_total = k_ref.shape[0] // ROW_BLOCK
    j = pl.program_id(1)
    nkb = jnp.minimum(j + 1, nkb_total) if causal_blocks else nkb_total

    @pl.when(j == 0)
    def _prepare_batch_row():
        def blk(i, carry):
            rows = pl.ds(pl.multiple_of(i * ROW_BLOCK, ROW_BLOCK), ROW_BLOCK)
            kbf_scr[rows, :] = k_ref[rows, :].astype(BF16)
            v_t = v_ref[rows, :].T.astype(BF16)
            cs = cc_ref[rows, :] * LOG2E
            for h in range(N_HEADS):
                vt_scr[h * hv:h * hv + hd, rows] = v_t[h * hd:(h + 1) * hd, :]
                ck_scr[h, rows, :] = jnp.broadcast_to(cs[:, BF0 + h:BF0 + h + 1], (ROW_BLOCK, qb))
            return carry
        lax.fori_loop(0, nkb_total, blk, 0)
        pad = vt_scr.shape[1] - nkb_total * ROW_BLOCK
        for h in range(N_HEADS):
            if pad:
                vt_scr[h * hv:h * hv + hd, nkb_total * ROW_BLOCK:] = jnp.zeros((hd, pad), BF16)
            vt_scr[h * hv + hd:(h + 1) * hv, :] = jnp.ones((ONES_ROWS, vt_scr.shape[1]), BF16)

    wq = _pair_weights((q_ref[...] * (hd ** -0.5 * LOG2E)).T, hd, N_HEADS)
    qcol = pl.ds(pl.multiple_of(j * qb + q_off, ROW_BLOCK), qb)
    cq = [cr_ref[h:h + 1, qcol] * LOG2E for h in range(N_HEADS)]
    qrow = j * qb + q_off + lax.broadcasted_iota(I32, (1, qb), 1)
    sub = lax.broadcasted_iota(I32, (KEY_STEP, 1), 0)

    def logits(i):
        k0 = pl.multiple_of(i * KEY_STEP, KEY_STEP)
        krows = pl.ds(k0, KEY_STEP)
        kblk = kbf_scr[krows, :]
        kidx = k0 + sub
        ok = (kidx >= key_lo) & (kidx <= qrow)
        tiles = []
        for pair in range(N_HEADS // 2):
            s2 = jnp.dot(kblk, wq[pair], preferred_element_type=F32)
            for i2 in range(2):
                h = 2 * pair + i2
                tiles.append(jnp.where(ok, s2[:, i2 * qb:(i2 + 1) * qb] - ck_scr[h, krows, :], NEG))
        return tiles

    def value_t(h, i):
        return vt_scr[h * hv:(h + 1) * hv, pl.ds(pl.multiple_of(i * KEY_STEP, KEY_STEP), KEY_STEP)]

    n_steps = lax.shift_right_logical(nkb + (STEP_BLOCKS - 1), _log2(STEP_BLOCKS))
    o_t = _attend_two_pass(n_steps, logits, value_t, cq, s_scr, acc_scr, qb, hd)
    y_ref[...] = (o_t.T * _silu(g_ref[...])).astype(y_ref.dtype)


def _fox(q_arr, q_col, k_arr, k_col, v_arr, v_col, g_arr, g_col, cc, cr, key_lo, q_off, causal_blocks):
    bsz, lq, _ = q_arr.shape
    lk = k_arr.shape[1]
    w = GROUP_W
    nqb = lq // ROW_BLOCK
    key_rows = pl.cdiv(lk, KEY_STEP) * KEY_STEP
    return pl.pallas_call(
        functools.partial(_fox_kernel, key_lo=key_lo, q_off=q_off, causal_blocks=causal_blocks),
        grid=(bsz, nqb),
        in_specs=[pl.BlockSpec((None, ROW_BLOCK, w), lambda b, j: (b, j, q_col)),
                  pl.BlockSpec((None, lk, w), lambda b, j: (b, 0, k_col)),
                  pl.BlockSpec((None, lk, w), lambda b, j: (b, 0, v_col)),
                  pl.BlockSpec((None, ROW_BLOCK, w), lambda b, j: (b, j, g_col)),
                  pl.BlockSpec((None, lk, LANES), lambda b, j: (b, 0, 0)),
                  pl.BlockSpec((None, SUBLANES, lk), lambda b, j: (b, 0, 0))],
        out_specs=pl.BlockSpec((None, ROW_BLOCK, w), lambda b, j: (b, j, 0)),
        out_shape=jax.ShapeDtypeStruct((bsz, lq, w), Y_DTYPE),
        scratch_shapes=[pltpu.VMEM((key_rows, w), BF16),
                        pltpu.VMEM((w + N_HEADS * ONES_ROWS, key_rows), BF16),
                        pltpu.VMEM((N_HEADS, key_rows, ROW_BLOCK), F32),
                        pltpu.VMEM((N_HEADS * key_rows // KEY_STEP, KEY_STEP, ROW_BLOCK), F32),
                        pltpu.VMEM((w + N_HEADS * ONES_ROWS, ROW_BLOCK), F32)],
        compiler_params=_cparams("parallel", "arbitrary"),
        name="fox",
    )(q_arr, k_arr, v_arr, g_arr, cc, cr)


def _dsa_kernel(cq_ref, cg_ref, ciq_ref, mq_ref, ckv_ref, mk_ref, y_ref,
                key_scr, kvb_scr, vt_scr, mkb_scr, s_scr, acc_scr,
                *, k_top, key_lo, key_hi, chunk_causal):
    qb, w = cq_ref.shape
    hd = w // N_HEADS
    nkb_total = ckv_ref.shape[0] // ROW_BLOCK
    j = pl.program_id(1)
    nkb = jnp.minimum(j + 1, nkb_total) if chunk_causal else nkb_total
    idx_scale = (H_IDX * D_IDX) ** -0.5

    @pl.when(j == 0)
    def _prepare_batch_row():
        def blk(i, carry):
            rows = pl.ds(pl.multiple_of(i * ROW_BLOCK, ROW_BLOCK), ROW_BLOCK)
            kv = ckv_ref[rows, :]
            kvb_scr[rows, :] = kv.astype(BF16)
            vt_scr[0:hd, rows] = kv.T[hd:2 * hd, :].astype(BF16)
            mkb_scr[rows, :] = mk_ref[rows, :].astype(BF16)
            return carry
        lax.fori_loop(0, nkb_total, blk, 0)
        pad = vt_scr.shape[1] - nkb_total * ROW_BLOCK
        if pad:
            vt_scr[0:hd, nkb_total * ROW_BLOCK:] = jnp.zeros((hd, pad), BF16)
        vt_scr[hd:, :] = jnp.ones((ONES_ROWS, vt_scr.shape[1]), BF16)

    def pad_rows(x):
        return jnp.concatenate([x, jnp.zeros((LANES - x.shape[0], qb), F32)], axis=0)

    iq_t = ciq_ref[...].T
    iw_t = mq_ref[...].T[IW0:IW0 + H_IDX, :]
    q_t = (cq_ref[...] * (hd ** -0.5 * LOG2E)).T
    iq_rhs = [jnp.concatenate([pad_rows(iq_t[(2 * p + i) * D_IDX:(2 * p + i + 1) * D_IDX, :])
                               for i in range(2)], axis=1).astype(BF16) for p in range(H_IDX // 2)]
    q_rhs = [jnp.concatenate([pad_rows(q_t[(2 * p + i) * hd:(2 * p + i + 1) * hd, :])
                              for i in range(2)], axis=1).astype(BF16) for p in range(N_HEADS // 2)]

    qrow = j * qb + lax.broadcasted_iota(I32, (1, qb), 1)
    if chunk_causal:
        hi = (lax.shift_right_logical(qrow, 6) + 1) * CHUNK
    else:
        hi = jnp.full((1, qb), key_hi, I32)
    sub = lax.broadcasted_iota(I32, (KEY_STEP, 1), 0)
    n_steps = lax.shift_right_logical(nkb + (STEP_BLOCKS - 1), _log2(STEP_BLOCKS))

    def score_step(i, carry):
        k0 = pl.multiple_of(i * KEY_STEP, KEY_STEP)
        mk = mkb_scr[pl.ds(k0, KEY_STEP), :]
        acc = jnp.zeros((KEY_STEP, qb), F32)
        for p in range(H_IDX // 2):
            sc2 = jnp.dot(mk, iq_rhs[p], preferred_element_type=F32)
            for i in range(2):
                h = 2 * p + i
                acc = acc + jnp.maximum(sc2[:, i * qb:(i + 1) * qb], 0.0) * iw_t[h:h + 1, :]
        score = acc * idx_scale + 0.0
        kidx = k0 + sub
        adm = (kidx >= key_lo) & (kidx < hi)
        u = pltpu.bitcast(score, I32)
        key = u ^ (lax.shift_right_arithmetic(u, 31) & np.int32(0x7FFFFFFF))
        key_scr[pl.ds(k0, KEY_STEP), :] = jnp.where(adm, key, INT_MIN)
        return carry

    lax.fori_loop(0, n_steps, score_step, 0)

    def count(pred):
        def cb(i, c8):
            k0 = pl.multiple_of(i * KEY_STEP, KEY_STEP)
            ind = jnp.where(pred(key_scr[pl.ds(k0, KEY_STEP), :], k0), 1, 0).astype(I32)
            return c8 + jnp.sum(ind.reshape(KEY_STEP // SUBLANES, SUBLANES, qb), axis=0)
        c8 = lax.fori_loop(0, n_steps, cb, jnp.zeros((SUBLANES, qb), I32))
        return jnp.sum(c8, axis=0, keepdims=True)

    def bisect(n_bits, count_ge, need):
        def bit_body(i, carry):
            ans, cnt_ans = carry
            cand = ans | lax.shift_left(np.int32(1), jnp.int32(n_bits - 1) - i)
            cnt = count_ge(cand)
            ok = cnt >= need
            return jnp.where(ok, cand, ans), jnp.where(ok, cnt, cnt_ans)
        return lax.fori_loop(0, n_bits, bit_body,
                             (jnp.zeros((1, qb), I32), jnp.full((1, qb), np.int32(2 ** 30), I32)))

    ans, cnt_thr = bisect(32, lambda cand: count(lambda kk, k0: kk >= (cand ^ INT_MIN)), k_top)
    thr = ans ^ INT_MIN
    tie = (cnt_thr > k_top) & (thr != INT_MIN)

    @pl.when(jnp.max(jnp.where(tie, 1, 0)) > 0)
    def _break_ties():
        n_rev_bits = _log2(pl.next_power_of_2(key_scr.shape[0]))
        rev_base = np.int32(2 ** n_rev_bits - 1)
        need = k_top - count(lambda kk, k0: kk > thr)
        ans2, _ = bisect(n_rev_bits, lambda cand: count(
            lambda kk, k0: (kk == thr) & ((rev_base - (k0 + sub)) >= cand)), need)

        def demote(i, carry):
            k0 = pl.multiple_of(i * KEY_STEP, KEY_STEP)
            kk = key_scr[pl.ds(k0, KEY_STEP), :]
            lose = tie & (kk == thr) & ((rev_base - (k0 + sub)) < ans2)
            key_scr[pl.ds(k0, KEY_STEP), :] = jnp.where(lose, INT_MIN, kk)
            return carry

        lax.fori_loop(0, n_steps, demote, 0)

    thr_sel = jnp.maximum(thr, INT_MIN + 1)

    def logits(i):
        krows = pl.ds(pl.multiple_of(i * KEY_STEP, KEY_STEP), KEY_STEP)
        sel = key_scr[krows, :] >= thr_sel
        kv = kvb_scr[krows, :]
        tiles = []
        for pair in range(N_HEADS // 2):
            s2 = jnp.dot(kv, q_rhs[pair], preferred_element_type=F32)
            tiles += [jnp.where(sel, s2[:, i2 * qb:(i2 + 1) * qb], NEG) for i2 in range(2)]
        return tiles

    def value_t(h, i):
        return vt_scr[:, pl.ds(pl.multiple_of(i * KEY_STEP, KEY_STEP), KEY_STEP)]

    no_shift = [jnp.zeros((1, qb), F32)] * N_HEADS
    o_t = _attend_two_pass(n_steps, logits, value_t, no_shift, s_scr, acc_scr, qb, hd)
    y_ref[...] = (o_t.T * _silu(cg_ref[...])).astype(y_ref.dtype)


def _dsa(p3, cols, ckv_arr, ckv_col, mk_arr, mk_col, k_top, key_lo, key_hi, chunk_causal):
    bsz, lq, _ = p3.shape
    lk = ckv_arr.shape[1]
    w = GROUP_W
    cq_col, cg_col, ciq_col, mq_col = cols
    key_rows = pl.cdiv(lk, KEY_STEP) * KEY_STEP
    return pl.pallas_call(
        functools.partial(_dsa_kernel, k_top=k_top, key_lo=key_lo, key_hi=key_hi,
                          chunk_causal=chunk_causal),
        grid=(bsz, lq // ROW_BLOCK),
        in_specs=[pl.BlockSpec((None, ROW_BLOCK, w), lambda b, j: (b, j, cq_col)),
                  pl.BlockSpec((None, ROW_BLOCK, w), lambda b, j: (b, j, cg_col)),
                  pl.BlockSpec((None, ROW_BLOCK, w), lambda b, j: (b, j, ciq_col)),
                  pl.BlockSpec((None, ROW_BLOCK, LANES), lambda b, j: (b, j, mq_col)),
                  pl.BlockSpec((None, lk, LANES), lambda b, j: (b, 0, ckv_col)),
                  pl.BlockSpec((None, lk, LANES), lambda b, j: (b, 0, mk_col))],
        out_specs=pl.BlockSpec((None, ROW_BLOCK, w), lambda b, j: (b, j, 0)),
        out_shape=jax.ShapeDtypeStruct((bsz, lq, w), Y_DTYPE),
        scratch_shapes=[pltpu.VMEM((key_rows, ROW_BLOCK), I32),
                        pltpu.VMEM((key_rows, LANES), BF16),
                        pltpu.VMEM((w // N_HEADS + ONES_ROWS, key_rows), BF16),
                        pltpu.VMEM((key_rows, LANES), BF16),
                        pltpu.VMEM((N_HEADS * key_rows // KEY_STEP, KEY_STEP, ROW_BLOCK), F32),
                        pltpu.VMEM((w + N_HEADS * ONES_ROWS, ROW_BLOCK), F32)],
        compiler_params=_cparams("parallel", "arbitrary"),
        name="dsa",
    )(p3, p3, p3, p3, ckv_arr, mk_arr)


def _merge_kernel(ya_ref, yb_ref, yc_ref, yd_ref, x_ref, w_ref, g_ref, o_ref, *, period, valid_lo, valid_hi):
    tm = x_ref.shape[0]
    gw = ya_ref.shape[1]
    acc = jnp.zeros(o_ref.shape, F32)
    for i, y_ref in enumerate((ya_ref, yb_ref, yc_ref, yd_ref)):
        acc = acc + jnp.dot(y_ref[...].astype(BF16), w_ref[i * gw:(i + 1) * gw, :],
                            preferred_element_type=F32)
    ms = jnp.mean(acc * acc, axis=-1, keepdims=True)
    out = x_ref[...] + (acc * lax.rsqrt(ms + EPS)) * g_ref[...]
    r0 = pl.program_id(0) * tm
    local = (r0 - (r0 // period) * period) + lax.broadcasted_iota(I32, (tm, 1), 0)
    local = jnp.where(local >= period, local - period, local)
    valid = (local >= valid_lo) & (local < valid_hi)
    o_ref[...] = jnp.where(valid, out, 0.0)


def _merge(ys, x2d, w_bf16, g, tm, period, valid_lo, valid_hi):
    rows, d = x2d.shape
    assert tm <= period
    gw = ys[0].shape[1]
    yspec = pl.BlockSpec((tm, gw), lambda i: (i, 0))
    return pl.pallas_call(
        functools.partial(_merge_kernel, period=period, valid_lo=valid_lo, valid_hi=valid_hi),
        grid=(rows // tm,),
        in_specs=[yspec, yspec, yspec, yspec,
                  pl.BlockSpec((tm, d), lambda i: (i, 0)),
                  pl.BlockSpec(w_bf16.shape, lambda i: (0, 0)),
                  pl.BlockSpec((1, d), lambda i: (0, 0))],
        out_specs=pl.BlockSpec((tm, d), lambda i: (i, 0)),
        out_shape=jax.ShapeDtypeStruct((rows, d), F32),
        compiler_params=_cparams("parallel"),
        name="merge",
    )(*ys, x2d, w_bf16, g)


def _row_tile(rows):
    return next(t for t in (512, 256, ROW_BLOCK) if rows % t == 0)


def _column_layout(gw):
    sizes = [gw] * 4 + [gw, gw, gw, FOX_HEADS, gw] + [gw, gw // 4, gw // 4, gw, H_IDX * D_IDX, D_IDX, H_IDX] + [gw] * 4
    names = ["aq", "af", "ai", "ag", "bq", "bk", "bv", "bf", "bg",
             "cq", "ck", "cv", "cg", "ciq", "cik", "ciw", "dq", "dk", "dv", "dg"]
    start = dict(zip(names, np.cumsum([0] + sizes[:-1])))
    size = dict(zip(names, sizes))
    src = -np.ones((16 * gw,), np.int64)
    def put(dst, name, off=0):
        src[dst + off:dst + off + size[name]] = np.arange(start[name], start[name] + size[name])
    for i, n in enumerate(["aq", "af", "ai", "ag", "bq", "bk", "bv", "bg", "cq", "cg", "ciq"]):
        put(i * gw, n)
    ckv0 = 11 * gw
    put(ckv0, "ck")
    put(ckv0 + gw // 4, "cv")
    misc0 = ckv0 + LANES
    put(misc0, "cik", IK0)
    put(misc0, "ciw", IW0)
    put(misc0, "bf", BF0)
    for i, n in enumerate(["dq", "dk", "dv", "dg"]):
        put(12 * gw + i * gw, n)
    return src


def _relayout_w_in(w_in_l, src):
    cols = jnp.take(w_in_l, jnp.asarray(np.maximum(src, 0)), axis=1)
    return jnp.where(jnp.asarray(src >= 0)[None, :], cols, 0.0).astype(BF16)


def _ret_tables(pos, chunk, gw):
    hd = gw // N_HEADS
    half = hd // 2
    inv = ROPE_BASE ** (-jnp.arange(half, dtype=F32) / half)
    ang = pos.astype(F32)[:, None] * inv[None, :]
    cos_h = jnp.concatenate([jnp.cos(ang), jnp.cos(ang)], axis=-1)
    sin_h = jnp.concatenate([-jnp.sin(ang), jnp.sin(ang)], axis=-1)
    cos = jnp.tile(cos_h, (1, N_HEADS))
    sin = jnp.tile(sin_h, (1, N_HEADS))
    lg = jnp.log(1.0 - 2.0 ** (-5.0 - jnp.arange(N_HEADS, dtype=F32)))
    lg_l = jnp.repeat(lg, hd)[None, :]
    t = jnp.arange(chunk, dtype=F32)[:, None]
    eb = jnp.exp((t + 1.0) * lg_l)
    ke2s = jnp.exp((chunk - 1.0 - t) * lg_l)
    dend = jnp.exp(chunk * lg_l)
    dt = jnp.arange(chunk, dtype=F32)[:, None] - jnp.arange(chunk, dtype=F32)[None, :]
    gam = jnp.concatenate([jnp.where(dt >= 0, jnp.exp(dt * lg[h]), 0.0) for h in range(N_HEADS)], axis=0)
    return cos, sin, eb, ke2s, dend, gam


def _state_to_bd(state):
    bsz, h, k, v = state.shape
    eye = jnp.eye(h, dtype=state.dtype)
    st = jnp.einsum('bhkv,hg->bhvgk', state, eye)
    return st.reshape(bsz, h * v, h * k)


def _bd_to_state(st, h):
    bsz, hv, hk = st.shape
    st5 = st.reshape(bsz, h, hv // h, h, hk // h)
    diag = jnp.stack([st5[:, i, :, i, :] for i in range(h)], axis=1)
    return jnp.swapaxes(diag, 2, 3)


def kernel(x_prompt, x_sample, state_hgrn, cache_fox_k, cache_fox_v, cache_fox_logf, cache_dsa_k,
           cache_dsa_v, cache_dsa_idx_k, state_ret, meta_tokens, w_in, w_out, fox_bias, hgrn_lb,
           norm_pre, norm_post):
    bsz, seq, d = x_prompt.shape
    dbsz, t_new, _ = x_sample.shape
    depth = w_in.shape[0]
    past = cache_fox_k.shape[2]
    gw = d // N_GROUPS
    hd = gw // N_HEADS
    assert gw == 2 * LANES and seq % ROW_BLOCK == 0 and past % ROW_BLOCK == 0 and t_new <= ROW_BLOCK
    assert t_new % SUBLANES == 0

    pad_front = ROW_BLOCK - N_META
    lp = ROW_BLOCK + seq
    ls = ROW_BLOCK
    lks = past + ROW_BLOCK
    n_chunks_p = lp // CHUNK
    k_top_p = min(TOP_K_MAX, seq // 4)
    k_top_s = min(TOP_K_MAX, (past + t_new) // 4)

    src = _column_layout(gw)
    n_cols = src.shape[0]
    col = {"a": 0, "bq": 4, "bk": 5, "bv": 6, "bg": 7, "cq": 8, "cg": 9, "ciq": 10, "d": 3}
    ckv_col = (11 * gw) // LANES
    misc_col = ckv_col + 1

    sm = jax.nn.softmax(hgrn_lb.astype(F32), axis=0)
    lbs = jnp.cumsum(sm, axis=0) - sm[0:1]

    xp = jnp.concatenate([jnp.zeros((bsz, pad_front, d), F32),
                          jnp.broadcast_to(meta_tokens.astype(F32)[None], (bsz, N_META, d)),
                          x_prompt], axis=1)
    xs = jnp.concatenate([x_sample, jnp.zeros((dbsz, ls - t_new, d), F32)], axis=1)

    tab_p = _ret_tables(jnp.arange(lp) - ROW_BLOCK, CHUNK, gw)
    tab_s = _ret_tables(past + jnp.arange(ls), t_new, gw)
    zero_state_p = jnp.zeros((bsz, gw, gw), F32)

    outs_p = {k: [] for k in ("hgrn", "fk", "fv", "fl", "ck", "cv", "ci", "ret")}
    outs_s = {k: [] for k in ("hgrn", "fk", "fv", "fl", "ck", "cv", "ci", "ret")}

    for l in range(depth):
        w_l = _relayout_w_in(w_in[l], src)
        w_o = w_out[l].astype(BF16)
        g_pre = norm_pre[l][None, :]
        g_post = norm_post[l][None, :]
        lb = lbs[l][None, :]
        bias = jnp.zeros((1, LANES), F32).at[0, BF0:BF0 + FOX_HEADS].set(fox_bias[l].astype(F32))

        p = _project(xp.reshape(bsz * lp, d), g_pre, w_l, _row_tile(bsz * lp)).reshape(bsz, lp, n_cols)
        ya, st_a = _hgrn(p, 0, lb, zero_state_p, CHUNK, n_chunks_p)
        yd, st_d = _ret(p, 3, tab_p, zero_state_p, CHUNK, n_chunks_p)
        lf, cc, cr = _foxprep(p, misc_col, bias, 0)
        yb = _fox(p, col["bq"], p, col["bk"], p, col["bv"], p, col["bg"], cc, cr, pad_front, 0, True)
        yc = _dsa(p, (col["cq"], col["cg"], col["ciq"], misc_col), p, ckv_col, p, misc_col,
                  k_top_p, pad_front, 0, True)
        flat = lambda a: a.reshape(bsz * lp, gw)
        xp = _merge([flat(ya), flat(yb), flat(yc), flat(yd)], xp.reshape(bsz * lp, d), w_o, g_post,
                    _row_tile(bsz * lp), lp, pad_front, lp).reshape(bsz, lp, d)
        pv = p[:, pad_front:, :]
        outs_p["hgrn"].append(_bd_to_state(st_a, N_HEADS))
        outs_p["ret"].append(_bd_to_state(st_d, N_HEADS))
        outs_p["fk"].append(pv[:, :, 5 * gw:6 * gw].reshape(bsz, -1, N_HEADS, hd))
        outs_p["fv"].append(pv[:, :, 6 * gw:7 * gw].reshape(bsz, -1, N_HEADS, hd))
        outs_p["fl"].append(lf[:, pad_front:, BF0:BF0 + FOX_HEADS])
        outs_p["ck"].append(pv[:, :, 11 * gw:11 * gw + hd])
        outs_p["cv"].append(pv[:, :, 11 * gw + hd:11 * gw + 2 * hd])
        outs_p["ci"].append(pv[:, :, misc_col * LANES + IK0:misc_col * LANES + IK0 + D_IDX])

        ps = _project(xs.reshape(dbsz * ls, d), g_pre, w_l, _row_tile(dbsz * ls)).reshape(dbsz, ls, n_cols)
        ya, st_a = _hgrn(ps, 0, lb, _state_to_bd(state_hgrn[l].astype(F32)), t_new, 1)
        yd, st_d = _ret(ps, 3, tab_s, _state_to_bd(state_ret[l].astype(F32)), t_new, 1)
        z = jnp.concatenate(
            [jnp.pad(cache_fox_logf[l].astype(F32), ((0, 0), (0, 0), (BF0, LANES - BF0 - FOX_HEADS))),
             ps[:, :, misc_col * LANES:(misc_col + 1) * LANES]], axis=1)
        lf, cc, cr = _foxprep(z, 0, bias, past)
        k_all = jnp.concatenate([cache_fox_k[l].reshape(dbsz, past, gw), ps[:, :, 5 * gw:6 * gw]], axis=1)
        v_all = jnp.concatenate([cache_fox_v[l].reshape(dbsz, past, gw), ps[:, :, 6 * gw:7 * gw]], axis=1)
        yb = _fox(ps, col["bq"], k_all, 0, v_all, 0, ps, col["bg"], cc, cr, 0, past, False)
        ckv_all = jnp.concatenate(
            [jnp.concatenate([cache_dsa_k[l], cache_dsa_v[l]], axis=-1).astype(F32),
             ps[:, :, 11 * gw:11 * gw + LANES]], axis=1)
        mk_all = jnp.concatenate(
            [jnp.pad(cache_dsa_idx_k[l].astype(F32), ((0, 0), (0, 0), (IK0, LANES - IK0 - D_IDX))),
             ps[:, :, misc_col * LANES:(misc_col + 1) * LANES]], axis=1)
        yc = _dsa(ps, (col["cq"], col["cg"], col["ciq"], misc_col), ckv_all, 0, mk_all, 0,
                  k_top_s, 0, past + t_new, False)
        flat = lambda a: a.reshape(dbsz * ls, gw)
        xs = _merge([flat(ya), flat(yb), flat(yc), flat(yd)], xs.reshape(dbsz * ls, d), w_o, g_post,
                    ls, ls, 0, t_new).reshape(dbsz, ls, d)
        pn = ps[:, :t_new, :]
        outs_s["hgrn"].append(_bd_to_state(st_a, N_HEADS))
        outs_s["ret"].append(_bd_to_state(st_d, N_HEADS))
        outs_s["fk"].append(pn[:, :, 5 * gw:6 * gw].reshape(dbsz, -1, N_HEADS, hd))
        outs_s["fv"].append(pn[:, :, 6 * gw:7 * gw].reshape(dbsz, -1, N_HEADS, hd))
        outs_s["fl"].append(lf[:, past:past + t_new, BF0:BF0 + FOX_HEADS])
        outs_s["ck"].append(pn[:, :, 11 * gw:11 * gw + hd])
        outs_s["cv"].append(pn[:, :, 11 * gw + hd:11 * gw + 2 * hd])
        outs_s["ci"].append(pn[:, :, misc_col * LANES + IK0:misc_col * LANES + IK0 + D_IDX])

    dt = x_prompt.dtype
    st = lambda xs_list: jnp.stack(xs_list, axis=0).astype(dt)
    order = ("hgrn", "fk", "fv", "fl", "ck", "cv", "ci", "ret")
    return ((xp[:, ROW_BLOCK:, :].astype(dt), xs[:, :t_new, :].astype(dt))
            + tuple(st(outs_p[k]) for k in order) + tuple(st(outs_s[k]) for k in order))
```

```python
import functools

import numpy as np
import jax
import jax.numpy as jnp
from jax import lax
from jax.experimental import pallas as pl
from jax.experimental.pallas import tpu as pltpu

F32 = jnp.float32
BF16 = jnp.bfloat16
Y_DTYPE = BF16
I32 = jnp.int32
LOG2E = float(np.log2(np.e))
ONES_ROWS = 16

N_META = 16
CHUNK = 64
N_GROUPS = 4
N_HEADS = 4
H_IDX = 8
D_IDX = 32
TOP_K_MAX = 256
ROPE_BASE = 10000.0
EPS = 1e-6
FOX_HEADS = 4

LANES = 128
SUBLANES = 8
ROW_BLOCK = 128
GROUP_W = 2 * LANES
VMEM_LIMIT_BYTES = 56 * 1024 * 1024

IK0 = 0
IW0 = 32
BF0 = 64

NEG = -1e30
M_FLOOR = -1e20
INT_MIN = np.int32(-2 ** 31)
STEP_BLOCKS = 4
KEY_STEP = STEP_BLOCKS * ROW_BLOCK


def _cparams(*sem):
    return pltpu.CompilerParams(dimension_semantics=sem, vmem_limit_bytes=VMEM_LIMIT_BYTES)


def _split3(x):
    h = x.astype(BF16)
    r = x - h.astype(F32)
    m = r.astype(BF16)
    lo = (r - m.astype(F32)).astype(BF16)
    return h, m, lo


def _dot_exact_lhs(a_bf16, x):
    d = lambda y: jnp.dot(a_bf16, y, preferred_element_type=F32)
    h, m, lo = _split3(x)
    return d(h) + d(m) + d(lo)


def _dot_exact_rhs(x, a_bf16):
    d = lambda y: jnp.dot(y, a_bf16, preferred_element_type=F32)
    h, m, lo = _split3(x)
    return d(h) + d(m) + d(lo)


def _dot_nt(a, b):
    return lax.dot_general(a, b, (((1,), (1,)), ((), ())), preferred_element_type=F32)


def _dot_tn(a, b):
    return lax.dot_general(a, b, (((0,), (0,)), ((), ())), preferred_element_type=F32)


def _log2(n):
    assert n > 0 and n & (n - 1) == 0, n
    return n.bit_length() - 1


def _head_masks(width, n_heads):
    lane = lax.broadcasted_iota(I32, (1, width), 1)
    sh = _log2(width // n_heads)
    return [(lax.shift_right_logical(lane, sh) == h).astype(F32) for h in range(n_heads)]


def _block_diag(width, n_heads, value):
    r = lax.broadcasted_iota(I32, (width, width), 0)
    c = lax.broadcasted_iota(I32, (width, width), 1)
    sh = _log2(width // n_heads)
    same = lax.shift_right_logical(r, sh) == lax.shift_right_logical(c, sh)
    return jnp.where(same, value, 0.0).astype(F32)


def _silu(x):
    return x * jax.nn.sigmoid(x)


def _proj_kernel(x_ref, g_ref, w_ref, o_ref, *, col_chunk):
    x = x_ref[...]
    ms = jnp.mean(x * x, axis=-1, keepdims=True)
    xn = ((x * lax.rsqrt(ms + EPS)) * g_ref[...]).astype(BF16)
    for c in range(o_ref.shape[1] // col_chunk):
        cols = slice(c * col_chunk, (c + 1) * col_chunk)
        o_ref[:, cols] = jnp.dot(xn, w_ref[:, cols], preferred_element_type=F32)


def _project(x2d, g, w_bf16, tm):
    rows, d = x2d.shape
    n = w_bf16.shape[1]
    return pl.pallas_call(
        functools.partial(_proj_kernel, col_chunk=1024),
        grid=(rows // tm,),
        in_specs=[pl.BlockSpec((tm, d), lambda i: (i, 0)),
                  pl.BlockSpec((1, d), lambda i: (0, 0)),
                  pl.BlockSpec((d, n), lambda i: (0, 0))],
        out_specs=pl.BlockSpec((tm, n), lambda i: (i, 0)),
        out_shape=jax.ShapeDtypeStruct((rows, n), F32),
        compiler_params=_cparams("parallel"),
        name="proj",
    )(x2d, g, w_bf16)


def _head_scores(qa, ka, hm):
    q_stack = jnp.concatenate([qa * hm[h] for h in range(N_HEADS)], axis=0).astype(BF16)
    return _dot_nt(q_stack, ka.astype(BF16))


def _hier_constants(chunk):
    halves = [chunk >> (i + 1) for i in range(_log2(chunk))]
    r = lax.broadcasted_iota(I32, (chunk, chunk), 0)
    c = lax.broadcasted_iota(I32, (chunk, chunk), 1)
    ts = lax.broadcasted_iota(I32, (N_HEADS * chunk, chunk), 0) & (chunk - 1)
    ss = lax.broadcasted_iota(I32, (N_HEADS * chunk, chunk), 1)
    sels, masks = [], []
    for h in halves:
        sh = _log2(h)
        grp = lambda x: lax.shift_right_logical(x, sh + 1)
        if h < SUBLANES:
            sels.append((c == lax.shift_left(grp(r), sh + 1) + (h - 1)).astype(BF16))
        upper_t = (lax.shift_right_logical(ts, sh) & 1).astype(F32)
        lower_s = 1.0 - (lax.shift_right_logical(ss, sh) & 1).astype(F32)
        masks.append(jnp.where(grp(ts) == grp(ss), upper_t * lower_s, 0.0))
    return halves, jnp.concatenate(sels, axis=0), masks


def _decayed_scores(q, k, b, hier, hm):
    halves, sel_small, masks = hier
    chunk, w = q.shape
    small_refs = _dot_exact_lhs(sel_small, b)
    att, n_small = None, 0
    for h, mask in zip(halves, masks):
        if h < SUBLANES:
            ref = small_refs[n_small * chunk:(n_small + 1) * chunk, :]
            n_small += 1
        else:
            ref = jnp.concatenate([jnp.broadcast_to(b[g0 + h - 1:g0 + h, :], (2 * h, w))
                                   for g0 in range(0, chunk, 2 * h)], axis=0)
        qa = q * jnp.exp(jnp.minimum(b - ref, 0.0))
        ka = k * jnp.exp(jnp.minimum(ref - b, 0.0))
        term = _head_scores(qa, ka, hm) * mask
        att = term if att is None else att + term
    return att


def _gla_chunk(att, qe, ke2, v, decay_end, st, hm, bd):
    c = qe.shape[0]
    o_stack = jnp.dot(att.astype(BF16), v.astype(BF16), preferred_element_type=F32)
    o_intra = o_stack[0:c] * hm[0]
    for h in range(1, N_HEADS):
        o_intra = o_intra + o_stack[h * c:(h + 1) * c] * hm[h]
    o_inter = _dot_nt(qe.astype(BF16), st.astype(BF16))
    st_new = st * decay_end + _dot_tn(v.astype(BF16), ke2.astype(BF16)) * bd
    return o_inter + o_intra, st_new


def _head_rms_gate(o, gate, bd_mean_bf16):
    h, m, _ = _split3(o * o)
    ms = (jnp.dot(h, bd_mean_bf16, preferred_element_type=F32)
          + jnp.dot(m, bd_mean_bf16, preferred_element_type=F32))
    return (o * lax.rsqrt(ms + EPS)) * _silu(gate)


def _gla_prologue(st0_ref, y_ref, st_ref, chunk, n_chunks):
    @pl.when(pl.program_id(1) == 0)
    def _():
        st_ref[...] = st0_ref[...]
    tail = n_chunks * chunk
    if tail < y_ref.shape[1]:
        y_ref[:, tail:, :] = jnp.zeros((y_ref.shape[0], y_ref.shape[1] - tail, y_ref.shape[2]),
                                       y_ref.dtype)


def _hgrn_kernel(a_ref, lb_ref, st0_ref, y_ref, st_ref, *, chunk, n_chunks):
    n_b, _, w = y_ref.shape
    hm = _head_masks(w, N_HEADS)
    bd = _block_diag(w, N_HEADS, 1.0)
    bd_mean = _block_diag(w, N_HEADS, 1.0 / (w // N_HEADS)).astype(BF16)
    r = lax.broadcasted_iota(I32, (chunk, chunk), 0)
    s = lax.broadcasted_iota(I32, (chunk, chunk), 1)
    tri = (r >= s).astype(BF16)
    hier = _hier_constants(chunk)
    bd_ones = bd.astype(BF16)
    lb = lb_ref[...]
    _gla_prologue(st0_ref, y_ref, st_ref, chunk, n_chunks)

    def body(c, carry):
        rows = pl.ds(pl.multiple_of(c * chunk, chunk), chunk)
        for g in range(n_b):
            q = a_ref[g, rows, 0:w]
            f = lb + (1.0 - lb) * jax.nn.sigmoid(a_ref[g, rows, w:2 * w])
            k = 1.0 - f
            v = a_ref[g, rows, 2 * w:3 * w]
            gate = a_ref[g, rows, 3 * w:4 * w]
            b = _dot_exact_lhs(tri, jnp.log(f))
            b_end = b[chunk - 1:chunk, :]
            att = _decayed_scores(q, k, b, hier, hm)
            o, st_new = _gla_chunk(att, q * jnp.exp(b), k * jnp.exp(b_end - b), v, jnp.exp(b_end),
                                   st_ref[g], hm, bd)
            o = o + jnp.dot((q * k).astype(BF16), bd_ones, preferred_element_type=F32) * v
            st_ref[g] = st_new
            y_ref[g, rows, :] = _head_rms_gate(o, gate, bd_mean).astype(y_ref.dtype)
        return carry

    lax.fori_loop(0, n_chunks, body, 0)


def _ret_kernel(d_ref, cos_ref, sin_ref, eb_ref, ke2s_ref, dend_ref, gam_ref, st0_ref, y_ref, st_ref,
                *, chunk, n_chunks):
    n_b, _, w = y_ref.shape
    hd = w // N_HEADS
    hm = _head_masks(w, N_HEADS)
    bd = _block_diag(w, N_HEADS, 1.0)
    bd_mean = _block_diag(w, N_HEADS, 1.0 / hd).astype(BF16)
    lane = lax.broadcasted_iota(I32, (1, w), 1)
    first_half = (lane & (hd - 1)) < (hd // 2)
    eb = eb_ref[...]
    ke2s = ke2s_ref[...]
    dend = dend_ref[...]
    gam = gam_ref[...]
    _gla_prologue(st0_ref, y_ref, st_ref, chunk, n_chunks)

    def rope(x, cos, sin_signed):
        swapped = jnp.where(first_half, pltpu.roll(x, w - hd // 2, 1), pltpu.roll(x, hd // 2, 1))
        return x * cos + swapped * sin_signed

    def body(c, carry):
        rows = pl.ds(pl.multiple_of(c * chunk, chunk), chunk)
        cos = cos_ref[rows, :]
        sin = sin_ref[rows, :]
        for g in range(n_b):
            q = rope(d_ref[g, rows, 0:w], cos, sin)
            k = rope(d_ref[g, rows, w:2 * w], cos, sin) * (hd ** -0.5)
            v = d_ref[g, rows, 2 * w:3 * w]
            gate = d_ref[g, rows, 3 * w:4 * w]
            att = _head_scores(q, k, hm) * gam
            o, st_new = _gla_chunk(att, q * eb, k * ke2s, v, dend, st_ref[g], hm, bd)
            st_ref[g] = st_new
            y_ref[g, rows, :] = _head_rms_gate(o, gate, bd_mean).astype(y_ref.dtype)
        return carry

    lax.fori_loop(0, n_chunks, body, 0)


def _gla_grid(bsz, lp, chunk, n_chunks):
    n_b = 2 if bsz % 2 == 0 else 1
    halves = n_chunks * chunk == lp and n_chunks % 2 == 0 and (lp // 2) % SUBLANES == 0
    n_seq = 2 if halves else 1
    return n_b, n_seq, n_chunks // n_seq


def _hgrn(p3, col_block, lb, st0, chunk, n_chunks):
    bsz, lp, _ = p3.shape
    w = lb.shape[1]
    n_b, n_seq, n_chunks_blk = _gla_grid(bsz, lp, chunk, n_chunks)
    rows = lp // n_seq
    return pl.pallas_call(
        functools.partial(_hgrn_kernel, chunk=chunk, n_chunks=n_chunks_blk),
        grid=(bsz // n_b, n_seq),
        in_specs=[pl.BlockSpec((n_b, rows, 4 * w), lambda b, s: (b, s, col_block)),
                  pl.BlockSpec((1, w), lambda b, s: (0, 0)),
                  pl.BlockSpec((n_b, w, w), lambda b, s: (b, 0, 0))],
        out_specs=[pl.BlockSpec((n_b, rows, w), lambda b, s: (b, s, 0)),
                   pl.BlockSpec((n_b, w, w), lambda b, s: (b, 0, 0))],
        out_shape=[jax.ShapeDtypeStruct((bsz, lp, w), Y_DTYPE),
                   jax.ShapeDtypeStruct((bsz, w, w), F32)],
        compiler_params=_cparams("parallel", "arbitrary"),
        name="hgrn",
    )(p3, lb, st0)


def _ret(p3, col_block, tables, st0, chunk, n_chunks):
    bsz, lp, _ = p3.shape
    w = st0.shape[1]
    n_b, n_seq, n_chunks_blk = _gla_grid(bsz, lp, chunk, n_chunks)
    rows = lp // n_seq
    cos, sin, eb, ke2s, dend, gam = tables
    full = lambda a: pl.BlockSpec(a.shape, lambda b, s: (0,) * a.ndim)
    per_seq = pl.BlockSpec((rows, w), lambda b, s: (s, 0))
    return pl.pallas_call(
        functools.partial(_ret_kernel, chunk=chunk, n_chunks=n_chunks_blk),
        grid=(bsz // n_b, n_seq),
        in_specs=[pl.BlockSpec((n_b, rows, 4 * w), lambda b, s: (b, s, col_block)),
                  per_seq, per_seq, full(eb), full(ke2s), full(dend), full(gam),
                  pl.BlockSpec((n_b, w, w), lambda b, s: (b, 0, 0))],
        out_specs=[pl.BlockSpec((n_b, rows, w), lambda b, s: (b, s, 0)),
                   pl.BlockSpec((n_b, w, w), lambda b, s: (b, 0, 0))],
        out_shape=[jax.ShapeDtypeStruct((bsz, lp, w), Y_DTYPE),
                   jax.ShapeDtypeStruct((bsz, w, w), F32)],
        compiler_params=_cparams("parallel", "arbitrary"),
        name="ret",
    )(p3, cos, sin, eb, ke2s, dend, gam, st0)


def _log_sigmoid(x):
    return -(jnp.maximum(-x, 0.0) + jnp.log(1.0 + jnp.exp(-jnp.abs(x))))


def _foxprep_kernel(z_ref, bias_ref, lf_ref, cc_ref, cr_ref, *, n_pass):
    nblk = z_ref.shape[0] // ROW_BLOCK
    r = lax.broadcasted_iota(I32, (ROW_BLOCK, ROW_BLOCK), 0)
    s = lax.broadcasted_iota(I32, (ROW_BLOCK, ROW_BLOCK), 1)
    tri = (r >= s).astype(BF16)
    bias = bias_ref[...]

    def body(i, carry):
        rows = pl.ds(pl.multiple_of(i * ROW_BLOCK, ROW_BLOCK), ROW_BLOCK)
        z = z_ref[rows, :]
        rowi = i * ROW_BLOCK + lax.broadcasted_iota(I32, (ROW_BLOCK, 1), 0)
        lf = jnp.where(rowi < n_pass, z, _log_sigmoid(z + bias))
        lf_ref[rows, :] = lf
        cs = _dot_exact_lhs(tri, lf) + carry
        cc_ref[rows, :] = cs
        cr_ref[:, rows] = cs.T[BF0:BF0 + SUBLANES, :]
        return cs[ROW_BLOCK - 1:ROW_BLOCK, :]

    lax.fori_loop(0, nblk, body, jnp.zeros((1, LANES), F32))
    pad = cr_ref.shape[1] - nblk * ROW_BLOCK
    if pad:
        cr_ref[:, nblk * ROW_BLOCK:] = jnp.zeros((SUBLANES, pad), F32)


def _foxprep(z3, z_col, bias, n_pass):
    bsz, lk, _ = z3.shape
    cr_cols = pl.cdiv(lk, KEY_STEP) * KEY_STEP
    return pl.pallas_call(
        functools.partial(_foxprep_kernel, n_pass=n_pass),
        grid=(bsz,),
        in_specs=[pl.BlockSpec((None, lk, LANES), lambda b: (b, 0, z_col)),
                  pl.BlockSpec((1, LANES), lambda b: (0, 0))],
        out_specs=[pl.BlockSpec((None, lk, LANES), lambda b: (b, 0, 0)),
                   pl.BlockSpec((None, lk, LANES), lambda b: (b, 0, 0)),
                   pl.BlockSpec((None, SUBLANES, cr_cols), lambda b: (b, 0, 0))],
        out_shape=[jax.ShapeDtypeStruct((bsz, lk, LANES), F32),
                   jax.ShapeDtypeStruct((bsz, lk, LANES), F32),
                   jax.ShapeDtypeStruct((bsz, SUBLANES, cr_cols), F32)],
        compiler_params=_cparams("parallel"),
        name="foxprep",
    )(z3, bias)


MXU_COLS = 2 * LANES


def _query_block(lq):
    return MXU_COLS if lq > ROW_BLOCK else ROW_BLOCK


def _heads_per_dot(qb):
    assert MXU_COLS % qb == 0
    return MXU_COLS // qb


def _head_weights(x_t, rows_per_head, n_heads):
    hpd = _heads_per_dot(x_t.shape[1])
    rowh = lax.shift_right_logical(lax.broadcasted_iota(I32, (x_t.shape[0], 1), 0),
                                   _log2(rows_per_head))
    only = lambda h: jnp.where(rowh == h, x_t, 0.0)
    return [jnp.concatenate([only(d * hpd + i) for i in range(hpd)], axis=1).astype(BF16)
            for d in range(n_heads // hpd)]


def _attend_two_pass(n_steps, logits_fn, value_t_fn, shifts, s_scr, acc_scr, qb, hd):
    fold = lambda x, op: op(x.reshape(KEY_STEP // SUBLANES, SUBLANES, qb), axis=0)
    hv = hd + ONES_ROWS

    def max_step(i, ms):
        tiles = logits_fn(i)
        for h in range(N_HEADS):
            s_scr[i * N_HEADS + h] = tiles[h]
        return tuple(jnp.maximum(ms[h], fold(tiles[h], jnp.max)) for h in range(N_HEADS))

    ms = lax.fori_loop(0, n_steps, max_step,
                       tuple(jnp.full((SUBLANES, qb), M_FLOOR, F32) for _ in range(N_HEADS)))
    m_logit = [jnp.max(ms[h], axis=0, keepdims=True) + shifts[h] for h in range(N_HEADS)]
    acc_scr[...] = jnp.zeros(acc_scr.shape, F32)

    def sum_step(i, carry):
        for h in range(N_HEADS):
            p = jnp.exp2(s_scr[i * N_HEADS + h] - (m_logit[h] - shifts[h]))
            acc_scr[h * hv:(h + 1) * hv, :] += jnp.dot(value_t_fn(h, i), p.astype(BF16),
                                                       preferred_element_type=F32)
        return carry

    lax.fori_loop(0, n_steps, sum_step, 0)
    outs = []
    for h in range(N_HEADS):
        l = acc_scr[h * hv + hd:h * hv + hd + 1, :]
        outs.append(acc_scr[h * hv:h * hv + hd, :] / jnp.where(l > 0.0, l, 1.0))
    return jnp.concatenate(outs, axis=0)


def _fox_kernel(q_ref, k_ref, v_ref, g_ref, cc_ref, cr_ref, y_ref, kbf_scr, vt_scr, ck_scr, s_scr, acc_scr,
                *, key_lo, q_off, causal_blocks):
    qb, w = q_ref.shape
    hd = w // N_HEADS
    hv = hd + ONES_ROWS
    nkb_total = k_ref.shape[0] // ROW_BLOCK
    hpd = _heads_per_dot(qb)
    j = pl.program_id(1)
    nkb = jnp.minimum((j + 1) * (qb // ROW_BLOCK), nkb_total) if causal_blocks else nkb_total

    @pl.when(j == 0)
    def _prepare_batch_row():
        def blk(i, carry):
            rows = pl.ds(pl.multiple_of(i * ROW_BLOCK, ROW_BLOCK), ROW_BLOCK)
            kbf_scr[rows, :] = k_ref[rows, :].astype(BF16)
            v_t = v_ref[rows, :].T.astype(BF16)
            cs = cc_ref[rows, :] * LOG2E
            for h in range(N_HEADS):
                vt_scr[h * hv:h * hv + hd, rows] = v_t[h * hd:(h + 1) * hd, :]
                ck_scr[h, rows, :] = jnp.broadcast_to(cs[:, BF0 + h:BF0 + h + 1], (ROW_BLOCK, LANES))
            return carry
        lax.fori_loop(0, nkb_total, blk, 0)
        pad = vt_scr.shape[1] - nkb_total * ROW_BLOCK
        for h in range(N_HEADS):
            if pad:
                vt_scr[h * hv:h * hv + hd, nkb_total * ROW_BLOCK:] = jnp.zeros((hd, pad), BF16)
            vt_scr[h * hv + hd:(h + 1) * hv, :] = jnp.ones((ONES_ROWS, vt_scr.shape[1]), BF16)

    wq = _head_weights((q_ref[...] * (hd ** -0.5 * LOG2E)).T, hd, N_HEADS)
    qcol = pl.ds(pl.multiple_of(j * qb + q_off, ROW_BLOCK), qb)
    cq = [cr_ref[h:h + 1, qcol] * LOG2E for h in range(N_HEADS)]
    qrow = j * qb + q_off + lax.broadcasted_iota(I32, (1, qb), 1)
    sub = lax.broadcasted_iota(I32, (KEY_STEP, 1), 0)

    def logits(i):
        k0 = pl.multiple_of(i * KEY_STEP, KEY_STEP)
        krows = pl.ds(k0, KEY_STEP)
        kblk = kbf_scr[krows, :]
        kidx = k0 + sub
        ok = (kidx >= key_lo) & (kidx <= qrow)
        tiles = []
        for d in range(N_HEADS // hpd):
            s2 = jnp.dot(kblk, wq[d], preferred_element_type=F32)
            for i2 in range(hpd):
                h = d * hpd + i2
                ck = jnp.concatenate([ck_scr[h, krows, :]] * (qb // LANES), axis=1)
                tiles.append(jnp.where(ok, s2[:, i2 * qb:(i2 + 1) * qb] - ck, NEG))
        return tiles

    def value_t(h, i):
        return vt_scr[h * hv:(h + 1) * hv, pl.ds(pl.multiple_of(i * KEY_STEP, KEY_STEP), KEY_STEP)]

    n_steps = lax.shift_right_logical(nkb + (STEP_BLOCKS - 1), _log2(STEP_BLOCKS))
    o_t = _attend_two_pass(n_steps, logits, value_t, cq, s_scr, acc_scr, qb, hd)
    y_ref[...] = (o_t.T * _silu(g_ref[...])).astype(y_ref.dtype)


def _fox(q_arr, q_col, k_arr, k_col, v_arr, v_col, g_arr, g_col, cc, cr, key_lo, q_off, causal_blocks):
    bsz, lq, _ = q_arr.shape
    lk = k_arr.shape[1]
    w = GROUP_W
    qb = _query_block(lq)
    key_rows = pl.cdiv(lk, KEY_STEP) * KEY_STEP
    assert cr.shape[2] >= q_off + pl.cdiv(lq, qb) * qb
    return pl.pallas_call(
        functools.partial(_fox_kernel, key_lo=key_lo, q_off=q_off, causal_blocks=causal_blocks),
        grid=(bsz, pl.cdiv(lq, qb)),
        in_specs=[pl.BlockSpec((None, qb, w), lambda b, j: (b, j, q_col)),
                  pl.BlockSpec((None, lk, w), lambda b, j: (b, 0, k_col)),
                  pl.BlockSpec((None, lk, w), lambda b, j: (b, 0, v_col)),
                  pl.BlockSpec((None, qb, w), lambda b, j: (b, j, g_col)),
                  pl.BlockSpec((None, lk, LANES), lambda b, j: (b, 0, 0)),
                  pl.BlockSpec((None, SUBLANES, cr.shape[2]), lambda b, j: (b, 0, 0))],
        out_specs=pl.BlockSpec((None, qb, w), lambda b, j: (b, j, 0)),
        out_shape=jax.ShapeDtypeStruct((bsz, lq, w), Y_DTYPE),
        scratch_shapes=[pltpu.VMEM((key_rows, w), BF16),
                        pltpu.VMEM((w + N_HEADS * ONES_ROWS, key_rows), BF16),
                        pltpu.VMEM((N_HEADS, key_rows, LANES), F32),
                        pltpu.VMEM((N_HEADS * key_rows // KEY_STEP, KEY_STEP, qb), F32),
                        pltpu.VMEM((w + N_HEADS * ONES_ROWS, qb), F32)],
        compiler_params=_cparams("parallel", "arbitrary"),
        name="fox",
    )(q_arr, k_arr, v_arr, g_arr, cc, cr)


def _dsa_kernel(cq_ref, cg_ref, ciq_ref, mq_ref, ckv_ref, mk_ref, y_ref,
                key_scr, kvb_scr, vt_scr, mkb_scr, s_scr, acc_scr,
                *, k_top, key_lo, key_hi, chunk_causal, q_lo, q_hi):
    qb, w = cq_ref.shape
    hd = w // N_HEADS
    hpd = _heads_per_dot(qb)
    nkb_total = ckv_ref.shape[0] // ROW_BLOCK
    j = pl.program_id(1)
    nkb = jnp.minimum((j + 1) * (qb // ROW_BLOCK), nkb_total) if chunk_causal else nkb_total
    idx_scale = (H_IDX * D_IDX) ** -0.5

    @pl.when(j == 0)
    def _prepare_batch_row():
        def blk(i, carry):
            rows = pl.ds(pl.multiple_of(i * ROW_BLOCK, ROW_BLOCK), ROW_BLOCK)
            kv = ckv_ref[rows, :]
            kvb_scr[rows, :] = kv.astype(BF16)
            vt_scr[0:hd, rows] = kv.T[hd:2 * hd, :].astype(BF16)
            mkb_scr[rows, :] = mk_ref[rows, :].astype(BF16)
            return carry
        lax.fori_loop(0, nkb_total, blk, 0)
        pad = vt_scr.shape[1] - nkb_total * ROW_BLOCK
        if pad:
            vt_scr[0:hd, nkb_total * ROW_BLOCK:] = jnp.zeros((hd, pad), BF16)
        vt_scr[hd:, :] = jnp.ones((ONES_ROWS, vt_scr.shape[1]), BF16)

    def pad_rows(x):
        return jnp.concatenate([x, jnp.zeros((LANES - x.shape[0], qb), F32)], axis=0)

    iq_t = ciq_ref[...].T
    iw_t = mq_ref[...].T[IW0:IW0 + H_IDX, :]
    q_t = (cq_ref[...] * (hd ** -0.5 * LOG2E)).T
    def side_by_side(x_t, rows, n_heads):
        return [jnp.concatenate([pad_rows(x_t[(d * hpd + i) * rows:(d * hpd + i + 1) * rows, :])
                                 for i in range(hpd)], axis=1).astype(BF16)
                for d in range(n_heads // hpd)]

    iq_rhs = side_by_side(iq_t, D_IDX, H_IDX)
    q_rhs = side_by_side(q_t, hd, N_HEADS)

    qrow = j * qb + lax.broadcasted_iota(I32, (1, qb), 1)
    if chunk_causal:
        hi = (lax.shift_right_logical(qrow, 6) + 1) * CHUNK
    else:
        hi = jnp.full((1, qb), key_hi, I32)
    sub = lax.broadcasted_iota(I32, (KEY_STEP, 1), 0)
    n_steps = lax.shift_right_logical(nkb + (STEP_BLOCKS - 1), _log2(STEP_BLOCKS))

    def score_step(i, carry):
        k0 = pl.multiple_of(i * KEY_STEP, KEY_STEP)
        mk = mkb_scr[pl.ds(k0, KEY_STEP), :]
        acc = jnp.zeros((KEY_STEP, qb), F32)
        for d in range(H_IDX // hpd):
            sc2 = jnp.dot(mk, iq_rhs[d], preferred_element_type=F32)
            for i2 in range(hpd):
                h = d * hpd + i2
                acc = acc + jnp.maximum(sc2[:, i2 * qb:(i2 + 1) * qb], 0.0) * iw_t[h:h + 1, :]
        score = acc * idx_scale + 0.0
        kidx = k0 + sub
        adm = (kidx >= key_lo) & (kidx < hi)
        u = pltpu.bitcast(score, I32)
        key = u ^ (lax.shift_right_arithmetic(u, 31) & np.int32(0x7FFFFFFF))
        key_scr[pl.ds(k0, KEY_STEP), :] = jnp.where(adm, key, INT_MIN)
        return carry

    lax.fori_loop(0, n_steps, score_step, 0)

    def count(pred):
        groups = qb // LANES

        def cb(i, c8s):
            k0 = pl.multiple_of(i * KEY_STEP, KEY_STEP)
            out = []
            for lg in range(groups):
                lanes = lambda x, lg=lg: x[:, lg * LANES:(lg + 1) * LANES]
                kk = key_scr[pl.ds(k0, KEY_STEP), lg * LANES:(lg + 1) * LANES]
                ind = jnp.where(pred(kk, k0, lanes), 1, 0).astype(I32)
                out.append(c8s[lg] + jnp.sum(ind.reshape(KEY_STEP // SUBLANES, SUBLANES, LANES), axis=0))
            return tuple(out)
        c8s = lax.fori_loop(0, n_steps, cb,
                            tuple(jnp.zeros((SUBLANES, LANES), I32) for _ in range(groups)))
        return jnp.concatenate([jnp.sum(c, axis=0, keepdims=True) for c in c8s], axis=1)

    def bisect(n_bits, count_ge, need):
        def bit_body(i, carry):
            ans, cnt_ans = carry
            cand = ans | lax.shift_left(np.int32(1), jnp.int32(n_bits - 1) - i)
            cnt = count_ge(cand)
            ok = cnt >= need
            return jnp.where(ok, cand, ans), jnp.where(ok, cnt, cnt_ans)
        return lax.fori_loop(0, n_bits, bit_body,
                             (jnp.zeros((1, qb), I32), jnp.full((1, qb), np.int32(2 ** 30), I32)))

    ans, cnt_thr = bisect(32, lambda cand: count(
        lambda kk, k0, lanes: kk >= lanes(cand ^ INT_MIN)), k_top)
    thr = ans ^ INT_MIN
    tie = (cnt_thr > k_top) & (thr != INT_MIN) & (qrow >= q_lo) & (qrow < q_hi)

    @pl.when(jnp.max(jnp.where(tie, 1, 0)) > 0)
    def _break_ties():
        n_rev_bits = _log2(pl.next_power_of_2(key_scr.shape[0]))
        rev_base = np.int32(2 ** n_rev_bits - 1)
        need = k_top - count(lambda kk, k0, lanes: kk > lanes(thr))
        ans2, _ = bisect(n_rev_bits, lambda cand: count(
            lambda kk, k0, lanes: (kk == lanes(thr)) & ((rev_base - (k0 + sub)) >= lanes(cand))), need)

        def demote(i, carry):
            k0 = pl.multiple_of(i * KEY_STEP, KEY_STEP)
            kk = key_scr[pl.ds(k0, KEY_STEP), :]
            lose = tie & (kk == thr) & ((rev_base - (k0 + sub)) < ans2)
            key_scr[pl.ds(k0, KEY_STEP), :] = jnp.where(lose, INT_MIN, kk)
            return carry

        lax.fori_loop(0, n_steps, demote, 0)

    thr_sel = jnp.maximum(thr, INT_MIN + 1)

    def logits(i):
        krows = pl.ds(pl.multiple_of(i * KEY_STEP, KEY_STEP), KEY_STEP)
        sel = key_scr[krows, :] >= thr_sel
        kv = kvb_scr[krows, :]
        tiles = []
        for d in range(N_HEADS // hpd):
            s2 = jnp.dot(kv, q_rhs[d], preferred_element_type=F32)
            tiles += [jnp.where(sel, s2[:, i2 * qb:(i2 + 1) * qb], NEG) for i2 in range(hpd)]
        return tiles

    def value_t(h, i):
        return vt_scr[:, pl.ds(pl.multiple_of(i * KEY_STEP, KEY_STEP), KEY_STEP)]

    no_shift = [jnp.zeros((1, qb), F32)] * N_HEADS
    o_t = _attend_two_pass(n_steps, logits, value_t, no_shift, s_scr, acc_scr, qb, hd)
    y_ref[...] = (o_t.T * _silu(cg_ref[...])).astype(y_ref.dtype)


def _dsa(p3, cols, ckv_arr, ckv_col, mk_arr, mk_col, k_top, key_lo, key_hi, chunk_causal, q_lo, q_hi):
    bsz, lq, _ = p3.shape
    lk = ckv_arr.shape[1]
    w = GROUP_W
    qb = _query_block(lq)
    cq_col, cg_col, ciq_col, mq_col = cols
    key_rows = pl.cdiv(lk, KEY_STEP) * KEY_STEP
    return pl.pallas_call(
        functools.partial(_dsa_kernel, k_top=k_top, key_lo=key_lo, key_hi=key_hi,
                          chunk_causal=chunk_causal, q_lo=q_lo, q_hi=q_hi),
        grid=(bsz, pl.cdiv(lq, qb)),
        in_specs=[pl.BlockSpec((None, qb, w), lambda b, j: (b, j, cq_col)),
                  pl.BlockSpec((None, qb, w), lambda b, j: (b, j, cg_col)),
                  pl.BlockSpec((None, qb, w), lambda b, j: (b, j, ciq_col)),
                  pl.BlockSpec((None, qb, LANES), lambda b, j: (b, j, mq_col)),
                  pl.BlockSpec((None, lk, LANES), lambda b, j: (b, 0, ckv_col)),
                  pl.BlockSpec((None, lk, LANES), lambda b, j: (b, 0, mk_col))],
        out_specs=pl.BlockSpec((None, qb, w), lambda b, j: (b, j, 0)),
        out_shape=jax.ShapeDtypeStruct((bsz, lq, w), Y_DTYPE),
        scratch_shapes=[pltpu.VMEM((key_rows, qb), I32),
                        pltpu.VMEM((key_rows, LANES), BF16),
                        pltpu.VMEM((w // N_HEADS + ONES_ROWS, key_rows), BF16),
                        pltpu.VMEM((key_rows, LANES), BF16),
                        pltpu.VMEM((N_HEADS * key_rows // KEY_STEP, KEY_STEP, qb), F32),
                        pltpu.VMEM((w + N_HEADS * ONES_ROWS, qb), F32)],
        compiler_params=_cparams("parallel", "arbitrary"),
        name="dsa",
    )(p3, p3, p3, p3, ckv_arr, mk_arr)


def _merge_kernel(ya_ref, yb_ref, yc_ref, yd_ref, x_ref, w_ref, g_ref, o_ref, *, period, valid_lo, valid_hi):
    tm = x_ref.shape[0]
    gw = ya_ref.shape[1]
    acc = jnp.zeros(o_ref.shape, F32)
    for i, y_ref in enumerate((ya_ref, yb_ref, yc_ref, yd_ref)):
        acc = acc + jnp.dot(y_ref[...].astype(BF16), w_ref[i * gw:(i + 1) * gw, :],
                            preferred_element_type=F32)
    ms = jnp.mean(acc * acc, axis=-1, keepdims=True)
    out = x_ref[...] + (acc * lax.rsqrt(ms + EPS)) * g_ref[...]
    r0 = pl.program_id(0) * tm
    local = (r0 - (r0 // period) * period) + lax.broadcasted_iota(I32, (tm, 1), 0)
    local = jnp.where(local >= period, local - period, local)
    valid = (local >= valid_lo) & (local < valid_hi)
    o_ref[...] = jnp.where(valid, out, 0.0)


def _merge(ys, x2d, w_bf16, g, tm, period, valid_lo, valid_hi):
    rows, d = x2d.shape
    assert tm <= period
    gw = ys[0].shape[1]
    yspec = pl.BlockSpec((tm, gw), lambda i: (i, 0))
    return pl.pallas_call(
        functools.partial(_merge_kernel, period=period, valid_lo=valid_lo, valid_hi=valid_hi),
        grid=(rows // tm,),
        in_specs=[yspec, yspec, yspec, yspec,
                  pl.BlockSpec((tm, d), lambda i: (i, 0)),
                  pl.BlockSpec(w_bf16.shape, lambda i: (0, 0)),
                  pl.BlockSpec((1, d), lambda i: (0, 0))],
        out_specs=pl.BlockSpec((tm, d), lambda i: (i, 0)),
        out_shape=jax.ShapeDtypeStruct((rows, d), F32),
        compiler_params=_cparams("parallel"),
        name="merge",
    )(*ys, x2d, w_bf16, g)


def _row_tile(rows):
    return next(t for t in (512, 256, ROW_BLOCK) if rows % t == 0)


def _column_layout(gw):
    sizes = [gw] * 4 + [gw, gw, gw, FOX_HEADS, gw] + [gw, gw // 4, gw // 4, gw, H_IDX * D_IDX, D_IDX, H_IDX] + [gw] * 4
    names = ["aq", "af", "ai", "ag", "bq", "bk", "bv", "bf", "bg",
             "cq", "ck", "cv", "cg", "ciq", "cik", "ciw", "dq", "dk", "dv", "dg"]
    start = dict(zip(names, np.cumsum([0] + sizes[:-1])))
    size = dict(zip(names, sizes))
    src = -np.ones((16 * gw,), np.int64)
    def put(dst, name, off=0):
        src[dst + off:dst + off + size[name]] = np.arange(start[name], start[name] + size[name])
    for i, n in enumerate(["aq", "af", "ai", "ag", "bq", "bk", "bv", "bg", "cq", "cg", "ciq"]):
        put(i * gw, n)
    ckv0 = 11 * gw
    put(ckv0, "ck")
    put(ckv0 + gw // 4, "cv")
    misc0 = ckv0 + LANES
    put(misc0, "cik", IK0)
    put(misc0, "ciw", IW0)
    put(misc0, "bf", BF0)
    for i, n in enumerate(["dq", "dk", "dv", "dg"]):
        put(12 * gw + i * gw, n)
    return src


def _relayout_w_in(w_in_l, src):
    cols = jnp.take(w_in_l, jnp.asarray(np.maximum(src, 0)), axis=1)
    return jnp.where(jnp.asarray(src >= 0)[None, :], cols, 0.0).astype(BF16)


def _ret_tables(pos, chunk, gw):
    hd = gw // N_HEADS
    half = hd // 2
    inv = ROPE_BASE ** (-jnp.arange(half, dtype=F32) / half)
    ang = pos.astype(F32)[:, None] * inv[None, :]
    cos_h = jnp.concatenate([jnp.cos(ang), jnp.cos(ang)], axis=-1)
    sin_h = jnp.concatenate([-jnp.sin(ang), jnp.sin(ang)], axis=-1)
    cos = jnp.tile(cos_h, (1, N_HEADS))
    sin = jnp.tile(sin_h, (1, N_HEADS))
    lg = jnp.log(1.0 - 2.0 ** (-5.0 - jnp.arange(N_HEADS, dtype=F32)))
    lg_l = jnp.repeat(lg, hd)[None, :]
    t = jnp.arange(chunk, dtype=F32)[:, None]
    eb = jnp.exp((t + 1.0) * lg_l)
    ke2s = jnp.exp((chunk - 1.0 - t) * lg_l)
    dend = jnp.exp(chunk * lg_l)
    dt = jnp.arange(chunk, dtype=F32)[:, None] - jnp.arange(chunk, dtype=F32)[None, :]
    gam = jnp.concatenate([jnp.where(dt >= 0, jnp.exp(dt * lg[h]), 0.0) for h in range(N_HEADS)], axis=0)
    return cos, sin, eb, ke2s, dend, gam


def _state_to_bd(state):
    bsz, h, k, v = state.shape
    eye = jnp.eye(h, dtype=state.dtype)
    st = jnp.einsum('bhkv,hg->bhvgk', state, eye)
    return st.reshape(bsz, h * v, h * k)


def _bd_to_state(st, h):
    bsz, hv, hk = st.shape
    st5 = st.reshape(bsz, h, hv // h, h, hk // h)
    diag = jnp.stack([st5[:, i, :, i, :] for i in range(h)], axis=1)
    return jnp.swapaxes(diag, 2, 3)


def kernel(x_prompt, x_sample, state_hgrn, cache_fox_k, cache_fox_v, cache_fox_logf, cache_dsa_k,
           cache_dsa_v, cache_dsa_idx_k, state_ret, meta_tokens, w_in, w_out, fox_bias, hgrn_lb,
           norm_pre, norm_post):
    bsz, seq, d = x_prompt.shape
    dbsz, t_new, _ = x_sample.shape
    depth = w_in.shape[0]
    past = cache_fox_k.shape[2]
    gw = d // N_GROUPS
    hd = gw // N_HEADS
    assert gw == 2 * LANES and seq % ROW_BLOCK == 0 and past % ROW_BLOCK == 0 and t_new <= ROW_BLOCK
    assert t_new % SUBLANES == 0

    pad_front = ROW_BLOCK - N_META
    lp = ROW_BLOCK + seq
    ls = ROW_BLOCK
    lks = past + ROW_BLOCK
    n_chunks_p = lp // CHUNK
    k_top_p = min(TOP_K_MAX, seq // 4)
    k_top_s = min(TOP_K_MAX, (past + t_new) // 4)

    src = _column_layout(gw)
    n_cols = src.shape[0]
    col = {"a": 0, "bq": 4, "bk": 5, "bv": 6, "bg": 7, "cq": 8, "cg": 9, "ciq": 10, "d": 3}
    ckv_col = (11 * gw) // LANES
    misc_col = ckv_col + 1

    sm = jax.nn.softmax(hgrn_lb.astype(F32), axis=0)
    lbs = jnp.cumsum(sm, axis=0) - sm[0:1]

    xp = jnp.concatenate([jnp.zeros((bsz, pad_front, d), F32),
                          jnp.broadcast_to(meta_tokens.astype(F32)[None], (bsz, N_META, d)),
                          x_prompt], axis=1)
    xs = jnp.concatenate([x_sample, jnp.zeros((dbsz, ls - t_new, d), F32)], axis=1)

    tab_p = _ret_tables(jnp.arange(lp) - ROW_BLOCK, CHUNK, gw)
    tab_s = _ret_tables(past + jnp.arange(ls), t_new, gw)
    zero_state_p = jnp.zeros((bsz, gw, gw), F32)

    outs_p = {k: [] for k in ("hgrn", "fk", "fv", "fl", "ck", "cv", "ci", "ret")}
    outs_s = {k: [] for k in ("hgrn", "fk", "fv", "fl", "ck", "cv", "ci", "ret")}

    for l in range(depth):
        w_l = _relayout_w_in(w_in[l], src)
        w_o = w_out[l].astype(BF16)
        g_pre = norm_pre[l][None, :]
        g_post = norm_post[l][None, :]
        lb = lbs[l][None, :]
        bias = jnp.zeros((1, LANES), F32).at[0, BF0:BF0 + FOX_HEADS].set(fox_bias[l].astype(F32))

        p = _project(xp.reshape(bsz * lp, d), g_pre, w_l, _row_tile(bsz * lp)).reshape(bsz, lp, n_cols)
        ya, st_a = _hgrn(p, 0, lb, zero_state_p, CHUNK, n_chunks_p)
        yd, st_d = _ret(p, 3, tab_p, zero_state_p, CHUNK, n_chunks_p)
        lf, cc, cr = _foxprep(p, misc_col, bias, 0)
        yb = _fox(p, col["bq"], p, col["bk"], p, col["bv"], p, col["bg"], cc, cr, pad_front, 0, True)
        yc = _dsa(p, (col["cq"], col["cg"], col["ciq"], misc_col), p, ckv_col, p, misc_col,
                  k_top_p, pad_front, 0, True, pad_front, lp)
        flat = lambda a: a.reshape(bsz * lp, gw)
        xp = _merge([flat(ya), flat(yb), flat(yc), flat(yd)], xp.reshape(bsz * lp, d), w_o, g_post,
                    _row_tile(bsz * lp), lp, pad_front, lp).reshape(bsz, lp, d)
        pv = p[:, pad_front:, :]
        outs_p["hgrn"].append(_bd_to_state(st_a, N_HEADS))
        outs_p["ret"].append(_bd_to_state(st_d, N_HEADS))
        outs_p["fk"].append(pv[:, :, 5 * gw:6 * gw].reshape(bsz, -1, N_HEADS, hd))
        outs_p["fv"].append(pv[:, :, 6 * gw:7 * gw].reshape(bsz, -1, N_HEADS, hd))
        outs_p["fl"].append(lf[:, pad_front:, BF0:BF0 + FOX_HEADS])
        outs_p["ck"].append(pv[:, :, 11 * gw:11 * gw + hd])
        outs_p["cv"].append(pv[:, :, 11 * gw + hd:11 * gw + 2 * hd])
        outs_p["ci"].append(pv[:, :, misc_col * LANES + IK0:misc_col * LANES + IK0 + D_IDX])

        ps = _project(xs.reshape(dbsz * ls, d), g_pre, w_l, _row_tile(dbsz * ls)).reshape(dbsz, ls, n_cols)
        ya, st_a = _hgrn(ps, 0, lb, _state_to_bd(state_hgrn[l].astype(F32)), t_new, 1)
        yd, st_d = _ret(ps, 3, tab_s, _state_to_bd(state_ret[l].astype(F32)), t_new, 1)
        z = jnp.concatenate(
            [jnp.pad(cache_fox_logf[l].astype(F32), ((0, 0), (0, 0), (BF0, LANES - BF0 - FOX_HEADS))),
             ps[:, :, misc_col * LANES:(misc_col + 1) * LANES]], axis=1)
        lf, cc, cr = _foxprep(z, 0, bias, past)
        k_all = jnp.concatenate([cache_fox_k[l].reshape(dbsz, past, gw), ps[:, :, 5 * gw:6 * gw]], axis=1)
        v_all = jnp.concatenate([cache_fox_v[l].reshape(dbsz, past, gw), ps[:, :, 6 * gw:7 * gw]], axis=1)
        yb = _fox(ps, col["bq"], k_all, 0, v_all, 0, ps, col["bg"], cc, cr, 0, past, False)
        ckv_all = jnp.concatenate(
            [jnp.concatenate([cache_dsa_k[l], cache_dsa_v[l]], axis=-1).astype(F32),
             ps[:, :, 11 * gw:11 * gw + LANES]], axis=1)
        mk_all = jnp.concatenate(
            [jnp.pad(cache_dsa_idx_k[l].astype(F32), ((0, 0), (0, 0), (IK0, LANES - IK0 - D_IDX))),
             ps[:, :, misc_col * LANES:(misc_col + 1) * LANES]], axis=1)
        yc = _dsa(ps, (col["cq"], col["cg"], col["ciq"], misc_col), ckv_all, 0, mk_all, 0,
                  k_top_s, 0, past + t_new, False, 0, t_new)
        flat = lambda a: a.reshape(dbsz * ls, gw)
        xs = _merge([flat(ya), flat(yb), flat(yc), flat(yd)], xs.reshape(dbsz * ls, d), w_o, g_post,
                    ls, ls, 0, t_new).reshape(dbsz, ls, d)
        pn = ps[:, :t_new, :]
        outs_s["hgrn"].append(_bd_to_state(st_a, N_HEADS))
        outs_s["ret"].append(_bd_to_state(st_d, N_HEADS))
        outs_s["fk"].append(pn[:, :, 5 * gw:6 * gw].reshape(dbsz, -1, N_HEADS, hd))
        outs_s["fv"].append(pn[:, :, 6 * gw:7 * gw].reshape(dbsz, -1, N_HEADS, hd))
        outs_s["fl"].append(lf[:, past:past + t_new, BF0:BF0 + FOX_HEADS])
        outs_s["ck"].append(pn[:, :, 11 * gw:11 * gw + hd])
        outs_s["cv"].append(pn[:, :, 11 * gw + hd:11 * gw + 2 * hd])
        outs_s["ci"].append(pn[:, :, misc_col * LANES + IK0:misc_col * LANES + IK0 + D_IDX])

    dt = x_prompt.dtype
    st = lambda xs_list: jnp.stack(xs_list, axis=0).astype(dt)
    order = ("hgrn", "fk", "fv", "fl", "ck", "cv", "ci", "ret")
    return ((xp[:, ROW_BLOCK:, :].astype(dt), xs[:, :t_new, :].astype(dt))
            + tuple(st(outs_p[k]) for k in order) + tuple(st(outs_s[k]) for k in order))
```

```python
import functools

import numpy as np
import jax
import jax.numpy as jnp
from jax import lax
from jax.experimental import pallas as pl
from jax.experimental.pallas import tpu as pltpu

F32 = jnp.float32
BF16 = jnp.bfloat16
Y_DTYPE = BF16
I32 = jnp.int32
LOG2E = float(np.log2(np.e))
ONES_ROWS = 16

N_META = 16
CHUNK = 64
N_GROUPS = 4
N_HEADS = 4
H_IDX = 8
D_IDX = 32
TOP_K_MAX = 256
ROPE_BASE = 10000.0
EPS = 1e-6
FOX_HEADS = 4

LANES = 128
SUBLANES = 8
ROW_BLOCK = 128
GROUP_W = 2 * LANES
VMEM_LIMIT_BYTES = 56 * 1024 * 1024

IK0 = 0
IW0 = 32
BF0 = 64

NEG = -1e30
M_FLOOR = -1e20
INT_MIN = np.int32(-2 ** 31)
STEP_BLOCKS = 4
KEY_STEP = STEP_BLOCKS * ROW_BLOCK
GLA_BATCH_ROWS = 4


def _cparams(*sem):
    return pltpu.CompilerParams(dimension_semantics=sem, vmem_limit_bytes=VMEM_LIMIT_BYTES)


def _split3(x):
    h = x.astype(BF16)
    r = x - h.astype(F32)
    m = r.astype(BF16)
    lo = (r - m.astype(F32)).astype(BF16)
    return h, m, lo


def _dot_exact_lhs(a_bf16, x):
    d = lambda y: jnp.dot(a_bf16, y, preferred_element_type=F32)
    h, m, lo = _split3(x)
    return d(h) + d(m) + d(lo)


def _dot_exact_rhs(x, a_bf16):
    d = lambda y: jnp.dot(y, a_bf16, preferred_element_type=F32)
    h, m, lo = _split3(x)
    return d(h) + d(m) + d(lo)


def _dot_nt(a, b):
    return lax.dot_general(a, b, (((1,), (1,)), ((), ())), preferred_element_type=F32)


def _dot_tn(a, b):
    return lax.dot_general(a, b, (((0,), (0,)), ((), ())), preferred_element_type=F32)


def _log2(n):
    assert n > 0 and n & (n - 1) == 0, n
    return n.bit_length() - 1


def _head_masks(width, n_heads):
    lane = lax.broadcasted_iota(I32, (1, width), 1)
    sh = _log2(width // n_heads)
    return [(lax.shift_right_logical(lane, sh) == h).astype(F32) for h in range(n_heads)]


def _block_diag(width, n_heads, value):
    r = lax.broadcasted_iota(I32, (width, width), 0)
    c = lax.broadcasted_iota(I32, (width, width), 1)
    sh = _log2(width // n_heads)
    same = lax.shift_right_logical(r, sh) == lax.shift_right_logical(c, sh)
    return jnp.where(same, value, 0.0).astype(F32)


def _silu(x):
    return x * jax.nn.sigmoid(x)


def _proj_kernel(x_ref, g_ref, w_ref, o_ref, *, col_chunk):
    x = x_ref[...]
    ms = jnp.mean(x * x, axis=-1, keepdims=True)
    xn = ((x * lax.rsqrt(ms + EPS)) * g_ref[...]).astype(BF16)
    for c in range(o_ref.shape[1] // col_chunk):
        cols = slice(c * col_chunk, (c + 1) * col_chunk)
        o_ref[:, cols] = jnp.dot(xn, w_ref[:, cols], preferred_element_type=F32)


def _project(x2d, g, w_bf16, tm):
    rows, d = x2d.shape
    n = w_bf16.shape[1]
    return pl.pallas_call(
        functools.partial(_proj_kernel, col_chunk=1024),
        grid=(rows // tm,),
        in_specs=[pl.BlockSpec((tm, d), lambda i: (i, 0)),
                  pl.BlockSpec((1, d), lambda i: (0, 0)),
                  pl.BlockSpec((d, n), lambda i: (0, 0))],
        out_specs=pl.BlockSpec((tm, n), lambda i: (i, 0)),
        out_shape=jax.ShapeDtypeStruct((rows, n), F32),
        compiler_params=_cparams("parallel"),
        name="proj",
    )(x2d, g, w_bf16)


def _head_scores(qa, ka, hm):
    q_stack = jnp.concatenate([qa * hm[h] for h in range(N_HEADS)], axis=0).astype(BF16)
    return _dot_nt(q_stack, ka.astype(BF16))


def _hier_constants(chunk):
    halves = [chunk >> (i + 1) for i in range(_log2(chunk))]
    r = lax.broadcasted_iota(I32, (chunk, chunk), 0)
    c = lax.broadcasted_iota(I32, (chunk, chunk), 1)
    ts = lax.broadcasted_iota(I32, (N_HEADS * chunk, chunk), 0) & (chunk - 1)
    ss = lax.broadcasted_iota(I32, (N_HEADS * chunk, chunk), 1)
    sels, masks = [], []
    for h in halves:
        sh = _log2(h)
        grp = lambda x: lax.shift_right_logical(x, sh + 1)
        if h < SUBLANES:
            sels.append((c == lax.shift_left(grp(r), sh + 1) + (h - 1)).astype(BF16))
        upper_t = (lax.shift_right_logical(ts, sh) & 1).astype(F32)
        lower_s = 1.0 - (lax.shift_right_logical(ss, sh) & 1).astype(F32)
        masks.append(jnp.where(grp(ts) == grp(ss), upper_t * lower_s, 0.0))
    return halves, jnp.concatenate(sels, axis=0), masks


def _decayed_scores(q, k, b, hier, hm):
    halves, sel_small, masks = hier
    chunk, w = q.shape
    small_refs = _dot_exact_lhs(sel_small, b)
    att, n_small = None, 0
    for h, mask in zip(halves, masks):
        if h < SUBLANES:
            ref = small_refs[n_small * chunk:(n_small + 1) * chunk, :]
            n_small += 1
        else:
            ref = jnp.concatenate([jnp.broadcast_to(b[g0 + h - 1:g0 + h, :], (2 * h, w))
                                   for g0 in range(0, chunk, 2 * h)], axis=0)
        qa = q * jnp.exp(jnp.minimum(b - ref, 0.0))
        ka = k * jnp.exp(jnp.minimum(ref - b, 0.0))
        term = _head_scores(qa, ka, hm) * mask
        att = term if att is None else att + term
    return att


def _gla_chunk(att, qe, ke2, v, decay_end, st, hm, bd):
    c = qe.shape[0]
    o_stack = jnp.dot(att.astype(BF16), v.astype(BF16), preferred_element_type=F32)
    o_intra = o_stack[0:c] * hm[0]
    for h in range(1, N_HEADS):
        o_intra = o_intra + o_stack[h * c:(h + 1) * c] * hm[h]
    o_inter = _dot_nt(qe.astype(BF16), st.astype(BF16))
    st_new = st * decay_end + _dot_tn(v.astype(BF16), ke2.astype(BF16)) * bd
    return o_inter + o_intra, st_new


def _head_rms_gate(o, gate, bd_mean_bf16):
    h, m, _ = _split3(o * o)
    ms = (jnp.dot(h, bd_mean_bf16, preferred_element_type=F32)
          + jnp.dot(m, bd_mean_bf16, preferred_element_type=F32))
    return (o * lax.rsqrt(ms + EPS)) * _silu(gate)


def _gla_prologue(st0_ref, y_ref, st_ref, chunk, n_chunks):
    @pl.when(pl.program_id(1) == 0)
    def _():
        st_ref[...] = st0_ref[...]
    tail = n_chunks * chunk
    if tail < y_ref.shape[1]:
        y_ref[:, tail:, :] = jnp.zeros((y_ref.shape[0], y_ref.shape[1] - tail, y_ref.shape[2]),
                                       y_ref.dtype)


def _hgrn_kernel(a_ref, lb_ref, st0_ref, y_ref, st_ref, *, chunk, n_chunks):
    n_b, _, w = y_ref.shape
    hm = _head_masks(w, N_HEADS)
    bd = _block_diag(w, N_HEADS, 1.0)
    bd_mean = _block_diag(w, N_HEADS, 1.0 / (w // N_HEADS)).astype(BF16)
    r = lax.broadcasted_iota(I32, (chunk, chunk), 0)
    s = lax.broadcasted_iota(I32, (chunk, chunk), 1)
    tri = (r >= s).astype(BF16)
    hier = _hier_constants(chunk)
    bd_ones = bd.astype(BF16)
    lb = lb_ref[...]
    _gla_prologue(st0_ref, y_ref, st_ref, chunk, n_chunks)

    def body(c, carry):
        rows = pl.ds(pl.multiple_of(c * chunk, chunk), chunk)
        for g in range(n_b):
            q = a_ref[g, rows, 0:w]
            f = lb + (1.0 - lb) * jax.nn.sigmoid(a_ref[g, rows, w:2 * w])
            k = 1.0 - f
            v = a_ref[g, rows, 2 * w:3 * w]
            gate = a_ref[g, rows, 3 * w:4 * w]
            b = _dot_exact_lhs(tri, jnp.log(f))
            b_end = b[chunk - 1:chunk, :]
            att = _decayed_scores(q, k, b, hier, hm)
            o, st_new = _gla_chunk(att, q * jnp.exp(b), k * jnp.exp(b_end - b), v, jnp.exp(b_end),
                                   st_ref[g], hm, bd)
            o = o + jnp.dot((q * k).astype(BF16), bd_ones, preferred_element_type=F32) * v
            st_ref[g] = st_new
            y_ref[g, rows, :] = _head_rms_gate(o, gate, bd_mean).astype(y_ref.dtype)
        return carry

    lax.fori_loop(0, n_chunks, body, 0)


def _ret_kernel(d_ref, cos_ref, sin_ref, eb_ref, ke2s_ref, dend_ref, gam_ref, st0_ref, y_ref, st_ref,
                *, chunk, n_chunks):
    n_b, _, w = y_ref.shape
    hd = w // N_HEADS
    hm = _head_masks(w, N_HEADS)
    bd = _block_diag(w, N_HEADS, 1.0)
    bd_mean = _block_diag(w, N_HEADS, 1.0 / hd).astype(BF16)
    lane = lax.broadcasted_iota(I32, (1, w), 1)
    first_half = (lane & (hd - 1)) < (hd // 2)
    eb = eb_ref[...]
    ke2s = ke2s_ref[...]
    dend = dend_ref[...]
    gam = gam_ref[...]
    _gla_prologue(st0_ref, y_ref, st_ref, chunk, n_chunks)

    def rope(x, cos, sin_signed):
        swapped = jnp.where(first_half, pltpu.roll(x, w - hd // 2, 1), pltpu.roll(x, hd // 2, 1))
        return x * cos + swapped * sin_signed

    def body(c, carry):
        rows = pl.ds(pl.multiple_of(c * chunk, chunk), chunk)
        cos = cos_ref[rows, :]
        sin = sin_ref[rows, :]
        for g in range(n_b):
            q = rope(d_ref[g, rows, 0:w], cos, sin)
            k = rope(d_ref[g, rows, w:2 * w], cos, sin) * (hd ** -0.5)
            v = d_ref[g, rows, 2 * w:3 * w]
            gate = d_ref[g, rows, 3 * w:4 * w]
            att = _head_scores(q, k, hm) * gam
            o, st_new = _gla_chunk(att, q * eb, k * ke2s, v, dend, st_ref[g], hm, bd)
            st_ref[g] = st_new
            y_ref[g, rows, :] = _head_rms_gate(o, gate, bd_mean).astype(y_ref.dtype)
        return carry

    lax.fori_loop(0, n_chunks, body, 0)


def _gla_grid(bsz, lp, chunk, n_chunks):
    n_b = next(n for n in (GLA_BATCH_ROWS, 2, 1) if bsz % n == 0)
    halves = n_chunks * chunk == lp and n_chunks % 2 == 0 and (lp // 2) % SUBLANES == 0
    n_seq = 2 if halves else 1
    return n_b, n_seq, n_chunks // n_seq


def _hgrn(p3, col_block, lb, st0, chunk, n_chunks):
    bsz, lp, _ = p3.shape
    w = lb.shape[1]
    n_b, n_seq, n_chunks_blk = _gla_grid(bsz, lp, chunk, n_chunks)
    rows = lp // n_seq
    return pl.pallas_call(
        functools.partial(_hgrn_kernel, chunk=chunk, n_chunks=n_chunks_blk),
        grid=(bsz // n_b, n_seq),
        in_specs=[pl.BlockSpec((n_b, rows, 4 * w), lambda b, s: (b, s, col_block)),
                  pl.BlockSpec((1, w), lambda b, s: (0, 0)),
                  pl.BlockSpec((n_b, w, w), lambda b, s: (b, 0, 0))],
        out_specs=[pl.BlockSpec((n_b, rows, w), lambda b, s: (b, s, 0)),
                   pl.BlockSpec((n_b, w, w), lambda b, s: (b, 0, 0))],
        out_shape=[jax.ShapeDtypeStruct((bsz, lp, w), Y_DTYPE),
                   jax.ShapeDtypeStruct((bsz, w, w), F32)],
        compiler_params=_cparams("parallel", "arbitrary"),
        name="hgrn",
    )(p3, lb, st0)


def _ret(p3, col_block, tables, st0, chunk, n_chunks):
    bsz, lp, _ = p3.shape
    w = st0.shape[1]
    n_b, n_seq, n_chunks_blk = _gla_grid(bsz, lp, chunk, n_chunks)
    rows = lp // n_seq
    cos, sin, eb, ke2s, dend, gam = tables
    full = lambda a: pl.BlockSpec(a.shape, lambda b, s: (0,) * a.ndim)
    per_seq = pl.BlockSpec((rows, w), lambda b, s: (s, 0))
    return pl.pallas_call(
        functools.partial(_ret_kernel, chunk=chunk, n_chunks=n_chunks_blk),
        grid=(bsz // n_b, n_seq),
        in_specs=[pl.BlockSpec((n_b, rows, 4 * w), lambda b, s: (b, s, col_block)),
                  per_seq, per_seq, full(eb), full(ke2s), full(dend), full(gam),
                  pl.BlockSpec((n_b, w, w), lambda b, s: (b, 0, 0))],
        out_specs=[pl.BlockSpec((n_b, rows, w), lambda b, s: (b, s, 0)),
                   pl.BlockSpec((n_b, w, w), lambda b, s: (b, 0, 0))],
        out_shape=[jax.ShapeDtypeStruct((bsz, lp, w), Y_DTYPE),
                   jax.ShapeDtypeStruct((bsz, w, w), F32)],
        compiler_params=_cparams("parallel", "arbitrary"),
        name="ret",
    )(p3, cos, sin, eb, ke2s, dend, gam, st0)


def _log_sigmoid(x):
    return -(jnp.maximum(-x, 0.0) + jnp.log(1.0 + jnp.exp(-jnp.abs(x))))


def _foxprep_kernel(z_ref, bias_ref, lf_ref, cc_ref, cr_ref, *, n_pass):
    nblk = z_ref.shape[0] // ROW_BLOCK
    r = lax.broadcasted_iota(I32, (ROW_BLOCK, ROW_BLOCK), 0)
    s = lax.broadcasted_iota(I32, (ROW_BLOCK, ROW_BLOCK), 1)
    tri = (r >= s).astype(BF16)
    bias = bias_ref[...]

    def body(i, carry):
        rows = pl.ds(pl.multiple_of(i * ROW_BLOCK, ROW_BLOCK), ROW_BLOCK)
        z = z_ref[rows, :]
        rowi = i * ROW_BLOCK + lax.broadcasted_iota(I32, (ROW_BLOCK, 1), 0)
        lf = jnp.where(rowi < n_pass, z, _log_sigmoid(z + bias))
        lf_ref[rows, :] = lf
        cs = _dot_exact_lhs(tri, lf) + carry
        cc_ref[rows, :] = cs
        cr_ref[:, rows] = cs.T[BF0:BF0 + SUBLANES, :]
        return cs[ROW_BLOCK - 1:ROW_BLOCK, :]

    lax.fori_loop(0, nblk, body, jnp.zeros((1, LANES), F32))
    pad = cr_ref.shape[1] - nblk * ROW_BLOCK
    if pad:
        cr_ref[:, nblk * ROW_BLOCK:] = jnp.zeros((SUBLANES, pad), F32)


def _foxprep(z3, z_col, bias, n_pass):
    bsz, lk, _ = z3.shape
    cr_cols = pl.cdiv(lk, KEY_STEP) * KEY_STEP
    return pl.pallas_call(
        functools.partial(_foxprep_kernel, n_pass=n_pass),
        grid=(bsz,),
        in_specs=[pl.BlockSpec((None, lk, LANES), lambda b: (b, 0, z_col)),
                  pl.BlockSpec((1, LANES), lambda b: (0, 0))],
        out_specs=[pl.BlockSpec((None, lk, LANES), lambda b: (b, 0, 0)),
                   pl.BlockSpec((None, lk, LANES), lambda b: (b, 0, 0)),
                   pl.BlockSpec((None, SUBLANES, cr_cols), lambda b: (b, 0, 0))],
        out_shape=[jax.ShapeDtypeStruct((bsz, lk, LANES), F32),
                   jax.ShapeDtypeStruct((bsz, lk, LANES), F32),
                   jax.ShapeDtypeStruct((bsz, SUBLANES, cr_cols), F32)],
        compiler_params=_cparams("parallel"),
        name="foxprep",
    )(z3, bias)


MXU_COLS = 2 * LANES


def _query_block(lq):
    return MXU_COLS if lq > ROW_BLOCK else ROW_BLOCK


def _heads_per_dot(qb):
    assert MXU_COLS % qb == 0
    return MXU_COLS // qb


def _head_weights(x_t, rows_per_head, n_heads):
    hpd = _heads_per_dot(x_t.shape[1])
    rowh = lax.shift_right_logical(lax.broadcasted_iota(I32, (x_t.shape[0], 1), 0),
                                   _log2(rows_per_head))
    only = lambda h: jnp.where(rowh == h, x_t, 0.0)
    return [jnp.concatenate([only(d * hpd + i) for i in range(hpd)], axis=1).astype(BF16)
            for d in range(n_heads // hpd)]


def _attend_two_pass(n_steps, logits_fn, value_t_fn, shifts, s_scr, acc_scr, qb, hd):
    fold = lambda x, op: op(x.reshape(KEY_STEP // SUBLANES, SUBLANES, qb), axis=0)
    hv = hd + ONES_ROWS

    def max_step(i, ms):
        tiles = logits_fn(i)
        for h in range(N_HEADS):
            s_scr[i * N_HEADS + h] = tiles[h]
        return tuple(jnp.maximum(ms[h], fold(tiles[h], jnp.max)) for h in range(N_HEADS))

    ms = lax.fori_loop(0, n_steps, max_step,
                       tuple(jnp.full((SUBLANES, qb), M_FLOOR, F32) for _ in range(N_HEADS)))
    m_logit = [jnp.max(ms[h], axis=0, keepdims=True) + shifts[h] for h in range(N_HEADS)]
    acc_scr[...] = jnp.zeros(acc_scr.shape, F32)

    def sum_step(i, carry):
        for h in range(N_HEADS):
            p = jnp.exp2(s_scr[i * N_HEADS + h] - (m_logit[h] - shifts[h]))
            acc_scr[h * hv:(h + 1) * hv, :] += jnp.dot(value_t_fn(h, i), p.astype(BF16),
                                                       preferred_element_type=F32)
        return carry

    lax.fori_loop(0, n_steps, sum_step, 0)
    outs = []
    for h in range(N_HEADS):
        l = acc_scr[h * hv + hd:h * hv + hd + 1, :]
        outs.append(acc_scr[h * hv:h * hv + hd, :] / jnp.where(l > 0.0, l, 1.0))
    return jnp.concatenate(outs, axis=0)


def _fox_kernel(q_ref, k_ref, v_ref, g_ref, cc_ref, cr_ref, y_ref, kbf_scr, vt_scr, ck_scr, s_scr, acc_scr,
                *, key_lo, q_off, causal_blocks):
    qb, w = q_ref.shape
    hd = w // N_HEADS
    hv = hd + ONES_ROWS
    nkb_total = k_ref.shape[0] // ROW_BLOCK
    hpd = _heads_per_dot(qb)
    j = pl.program_id(1)
    nkb = jnp.minimum((j + 1) * (qb // ROW_BLOCK), nkb_total) if causal_blocks else nkb_total

    @pl.when(j == 0)
    def _prepare_batch_row():
        def blk(i, carry):
            rows = pl.ds(pl.multiple_of(i * ROW_BLOCK, ROW_BLOCK), ROW_BLOCK)
            kbf_scr[rows, :] = k_ref[rows, :].astype(BF16)
            v_t = v_ref[rows, :].T.astype(BF16)
            cs = cc_ref[rows, :] * LOG2E
            for h in range(N_HEADS):
                vt_scr[h * hv:h * hv + hd, rows] = v_t[h * hd:(h + 1) * hd, :]
                ck_scr[h, rows, :] = jnp.broadcast_to(cs[:, BF0 + h:BF0 + h + 1], (ROW_BLOCK, LANES))
            return carry
        lax.fori_loop(0, nkb_total, blk, 0)
        pad = vt_scr.shape[1] - nkb_total * ROW_BLOCK
        for h in range(N_HEADS):
            if pad:
                vt_scr[h * hv:h * hv + hd, nkb_total * ROW_BLOCK:] = jnp.zeros((hd, pad), BF16)
            vt_scr[h * hv + hd:(h + 1) * hv, :] = jnp.ones((ONES_ROWS, vt_scr.shape[1]), BF16)

    wq = _head_weights((q_ref[...] * (hd ** -0.5 * LOG2E)).T, hd, N_HEADS)
    qcol = pl.ds(pl.multiple_of(j * qb + q_off, ROW_BLOCK), qb)
    cq = [cr_ref[h:h + 1, qcol] * LOG2E for h in range(N_HEADS)]
    qrow = j * qb + q_off + lax.broadcasted_iota(I32, (1, qb), 1)
    sub = lax.broadcasted_iota(I32, (KEY_STEP, 1), 0)

    def logits(i):
        k0 = pl.multiple_of(i * KEY_STEP, KEY_STEP)
        krows = pl.ds(k0, KEY_STEP)
        kblk = kbf_scr[krows, :]
        kidx = k0 + sub
        ok = (kidx >= key_lo) & (kidx <= qrow)
        tiles = []
        for d in range(N_HEADS // hpd):
            s2 = jnp.dot(kblk, wq[d], preferred_element_type=F32)
            for i2 in range(hpd):
                h = d * hpd + i2
                ck = jnp.concatenate([ck_scr[h, krows, :]] * (qb // LANES), axis=1)
                tiles.append(jnp.where(ok, s2[:, i2 * qb:(i2 + 1) * qb] - ck, NEG))
        return tiles

    def value_t(h, i):
        return vt_scr[h * hv:(h + 1) * hv, pl.ds(pl.multiple_of(i * KEY_STEP, KEY_STEP), KEY_STEP)]

    n_steps = lax.shift_right_logical(nkb + (STEP_BLOCKS - 1), _log2(STEP_BLOCKS))
    o_t = _attend_two_pass(n_steps, logits, value_t, cq, s_scr, acc_scr, qb, hd)
    y_ref[...] = (o_t.T * _silu(g_ref[...])).astype(y_ref.dtype)


def _fox(q_arr, q_col, k_arr, k_col, v_arr, v_col, g_arr, g_col, cc, cr, key_lo, q_off, causal_blocks):
    bsz, lq, _ = q_arr.shape
    lk = k_arr.shape[1]
    w = GROUP_W
    qb = _query_block(lq)
    key_rows = pl.cdiv(lk, KEY_STEP) * KEY_STEP
    assert cr.shape[2] >= q_off + pl.cdiv(lq, qb) * qb
    return pl.pallas_call(
        functools.partial(_fox_kernel, key_lo=key_lo, q_off=q_off, causal_blocks=causal_blocks),
        grid=(bsz, pl.cdiv(lq, qb)),
        in_specs=[pl.BlockSpec((None, qb, w), lambda b, j: (b, j, q_col)),
                  pl.BlockSpec((None, lk, w), lambda b, j: (b, 0, k_col)),
                  pl.BlockSpec((None, lk, w), lambda b, j: (b, 0, v_col)),
                  pl.BlockSpec((None, qb, w), lambda b, j: (b, j, g_col)),
                  pl.BlockSpec((None, lk, LANES), lambda b, j: (b, 0, 0)),
                  pl.BlockSpec((None, SUBLANES, cr.shape[2]), lambda b, j: (b, 0, 0))],
        out_specs=pl.BlockSpec((None, qb, w), lambda b, j: (b, j, 0)),
        out_shape=jax.ShapeDtypeStruct((bsz, lq, w), Y_DTYPE),
        scratch_shapes=[pltpu.VMEM((key_rows, w), BF16),
                        pltpu.VMEM((w + N_HEADS * ONES_ROWS, key_rows), BF16),
                        pltpu.VMEM((N_HEADS, key_rows, LANES), F32),
                        pltpu.VMEM((N_HEADS * key_rows // KEY_STEP, KEY_STEP, qb), F32),
                        pltpu.VMEM((w + N_HEADS * ONES_ROWS, qb), F32)],
        compiler_params=_cparams("parallel", "arbitrary"),
        name="fox",
    )(q_arr, k_arr, v_arr, g_arr, cc, cr)


def _dsa_kernel(cq_ref, cg_ref, ciq_ref, mq_ref, ckv_ref, mk_ref, y_ref,
                key_scr, kvb_scr, vt_scr, mkb_scr, s_scr, acc_scr,
                *, k_top, key_lo, key_hi, chunk_causal, q_lo, q_hi):
    qb, w = cq_ref.shape
    hd = w // N_HEADS
    hpd = _heads_per_dot(qb)
    nkb_total = ckv_ref.shape[0] // ROW_BLOCK
    j = pl.program_id(1)
    nkb = jnp.minimum((j + 1) * (qb // ROW_BLOCK), nkb_total) if chunk_causal else nkb_total
    idx_scale = (H_IDX * D_IDX) ** -0.5

    @pl.when(j == 0)
    def _prepare_batch_row():
        def blk(i, carry):
            rows = pl.ds(pl.multiple_of(i * ROW_BLOCK, ROW_BLOCK), ROW_BLOCK)
            kv = ckv_ref[rows, :]
            kvb_scr[rows, :] = kv.astype(BF16)
            vt_scr[0:hd, rows] = kv.T[hd:2 * hd, :].astype(BF16)
            mkb_scr[rows, :] = mk_ref[rows, :].astype(BF16)
            return carry
        lax.fori_loop(0, nkb_total, blk, 0)
        pad = vt_scr.shape[1] - nkb_total * ROW_BLOCK
        if pad:
            vt_scr[0:hd, nkb_total * ROW_BLOCK:] = jnp.zeros((hd, pad), BF16)
        vt_scr[hd:, :] = jnp.ones((ONES_ROWS, vt_scr.shape[1]), BF16)

    def pad_rows(x):
        return jnp.concatenate([x, jnp.zeros((LANES - x.shape[0], qb), F32)], axis=0)

    iq_t = ciq_ref[...].T
    iw_t = mq_ref[...].T[IW0:IW0 + H_IDX, :]
    q_t = (cq_ref[...] * (hd ** -0.5 * LOG2E)).T
    def side_by_side(x_t, rows, n_heads):
        return [jnp.concatenate([pad_rows(x_t[(d * hpd + i) * rows:(d * hpd + i + 1) * rows, :])
                                 for i in range(hpd)], axis=1).astype(BF16)
                for d in range(n_heads // hpd)]

    iq_rhs = side_by_side(iq_t, D_IDX, H_IDX)
    q_rhs = side_by_side(q_t, hd, N_HEADS)

    qrow = j * qb + lax.broadcasted_iota(I32, (1, qb), 1)
    if chunk_causal:
        hi = (lax.shift_right_logical(qrow, 6) + 1) * CHUNK
    else:
        hi = jnp.full((1, qb), key_hi, I32)
    sub = lax.broadcasted_iota(I32, (KEY_STEP, 1), 0)
    n_steps = lax.shift_right_logical(nkb + (STEP_BLOCKS - 1), _log2(STEP_BLOCKS))

    def score_step(i, carry):
        k0 = pl.multiple_of(i * KEY_STEP, KEY_STEP)
        mk = mkb_scr[pl.ds(k0, KEY_STEP), :]
        acc = jnp.zeros((KEY_STEP, qb), F32)
        for d in range(H_IDX // hpd):
            sc2 = jnp.dot(mk, iq_rhs[d], preferred_element_type=F32)
            for i2 in range(hpd):
                h = d * hpd + i2
                acc = acc + jnp.maximum(sc2[:, i2 * qb:(i2 + 1) * qb], 0.0) * iw_t[h:h + 1, :]
        score = acc * idx_scale + 0.0
        kidx = k0 + sub
        adm = (kidx >= key_lo) & (kidx < hi)
        u = pltpu.bitcast(score, I32)
        key = u ^ (lax.shift_right_arithmetic(u, 31) & np.int32(0x7FFFFFFF))
        key_scr[pl.ds(k0, KEY_STEP), :] = jnp.where(adm, key, INT_MIN)
        return carry

    lax.fori_loop(0, n_steps, score_step, 0)

    def count(pred):
        groups = qb // LANES

        def cb(i, c8s):
            k0 = pl.multiple_of(i * KEY_STEP, KEY_STEP)
            out = []
            for lg in range(groups):
                lanes = lambda x, lg=lg: x[:, lg * LANES:(lg + 1) * LANES]
                kk = key_scr[pl.ds(k0, KEY_STEP), lg * LANES:(lg + 1) * LANES]
                ind = jnp.where(pred(kk, k0, lanes), 1, 0).astype(I32)
                out.append(c8s[lg] + jnp.sum(ind.reshape(KEY_STEP // SUBLANES, SUBLANES, LANES), axis=0))
            return tuple(out)
        c8s = lax.fori_loop(0, n_steps, cb,
                            tuple(jnp.zeros((SUBLANES, LANES), I32) for _ in range(groups)))
        return jnp.concatenate([jnp.sum(c, axis=0, keepdims=True) for c in c8s], axis=1)

    def bisect(n_bits, count_ge, need):
        def bit_body(i, carry):
            ans, cnt_ans = carry
            cand = ans | lax.shift_left(np.int32(1), jnp.int32(n_bits - 1) - i)
            cnt = count_ge(cand)
            ok = cnt >= need
            return jnp.where(ok, cand, ans), jnp.where(ok, cnt, cnt_ans)
        return lax.fori_loop(0, n_bits, bit_body,
                             (jnp.zeros((1, qb), I32), jnp.full((1, qb), np.int32(2 ** 30), I32)))

    ans, cnt_thr = bisect(32, lambda cand: count(
        lambda kk, k0, lanes: kk >= lanes(cand ^ INT_MIN)), k_top)
    thr = ans ^ INT_MIN
    tie = (cnt_thr > k_top) & (thr != INT_MIN) & (qrow >= q_lo) & (qrow < q_hi)

    @pl.when(jnp.max(jnp.where(tie, 1, 0)) > 0)
    def _break_ties():
        n_rev_bits = _log2(pl.next_power_of_2(key_scr.shape[0]))
        rev_base = np.int32(2 ** n_rev_bits - 1)
        need = k_top - count(lambda kk, k0, lanes: kk > lanes(thr))
        ans2, _ = bisect(n_rev_bits, lambda cand: count(
            lambda kk, k0, lanes: (kk == lanes(thr)) & ((rev_base - (k0 + sub)) >= lanes(cand))), need)

        def demote(i, carry):
            k0 = pl.multiple_of(i * KEY_STEP, KEY_STEP)
            kk = key_scr[pl.ds(k0, KEY_STEP), :]
            lose = tie & (kk == thr) & ((rev_base - (k0 + sub)) < ans2)
            key_scr[pl.ds(k0, KEY_STEP), :] = jnp.where(lose, INT_MIN, kk)
            return carry

        lax.fori_loop(0, n_steps, demote, 0)

    thr_sel = jnp.maximum(thr, INT_MIN + 1)

    def logits(i):
        krows = pl.ds(pl.multiple_of(i * KEY_STEP, KEY_STEP), KEY_STEP)
        sel = key_scr[krows, :] >= thr_sel
        kv = kvb_scr[krows, :]
        tiles = []
        for d in range(N_HEADS // hpd):
            s2 = jnp.dot(kv, q_rhs[d], preferred_element_type=F32)
            tiles += [jnp.where(sel, s2[:, i2 * qb:(i2 + 1) * qb], NEG) for i2 in range(hpd)]
        return tiles

    def value_t(h, i):
        return vt_scr[:, pl.ds(pl.multiple_of(i * KEY_STEP, KEY_STEP), KEY_STEP)]

    no_shift = [jnp.zeros((1, qb), F32)] * N_HEADS
    o_t = _attend_two_pass(n_steps, logits, value_t, no_shift, s_scr, acc_scr, qb, hd)
    y_ref[...] = (o_t.T * _silu(cg_ref[...])).astype(y_ref.dtype)


def _dsa(p3, cols, ckv_arr, ckv_col, mk_arr, mk_col, k_top, key_lo, key_hi, chunk_causal, q_lo, q_hi):
    bsz, lq, _ = p3.shape
    lk = ckv_arr.shape[1]
    w = GROUP_W
    qb = _query_block(lq)
    cq_col, cg_col, ciq_col, mq_col = cols
    key_rows = pl.cdiv(lk, KEY_STEP) * KEY_STEP
    return pl.pallas_call(
        functools.partial(_dsa_kernel, k_top=k_top, key_lo=key_lo, key_hi=key_hi,
                          chunk_causal=chunk_causal, q_lo=q_lo, q_hi=q_hi),
        grid=(bsz, pl.cdiv(lq, qb)),
        in_specs=[pl.BlockSpec((None, qb, w), lambda b, j: (b, j, cq_col)),
                  pl.BlockSpec((None, qb, w), lambda b, j: (b, j, cg_col)),
                  pl.BlockSpec((None, qb, w), lambda b, j: (b, j, ciq_col)),
                  pl.BlockSpec((None, qb, LANES), lambda b, j: (b, j, mq_col)),
                  pl.BlockSpec((None, lk, LANES), lambda b, j: (b, 0, ckv_col)),
                  pl.BlockSpec((None, lk, LANES), lambda b, j: (b, 0, mk_col))],
        out_specs=pl.BlockSpec((None, qb, w), lambda b, j: (b, j, 0)),
        out_shape=jax.ShapeDtypeStruct((bsz, lq, w), Y_DTYPE),
        scratch_shapes=[pltpu.VMEM((key_rows, qb), I32),
                        pltpu.VMEM((key_rows, LANES), BF16),
                        pltpu.VMEM((w // N_HEADS + ONES_ROWS, key_rows), BF16),
                        pltpu.VMEM((key_rows, LANES), BF16),
                        pltpu.VMEM((N_HEADS * key_rows // KEY_STEP, KEY_STEP, qb), F32),
                        pltpu.VMEM((w + N_HEADS * ONES_ROWS, qb), F32)],
        compiler_params=_cparams("parallel", "arbitrary"),
        name="dsa",
    )(p3, p3, p3, p3, ckv_arr, mk_arr)


def _merge_kernel(ya_ref, yb_ref, yc_ref, yd_ref, x_ref, w_ref, g_ref, o_ref, *, period, valid_lo, valid_hi):
    tm = x_ref.shape[0]
    gw = ya_ref.shape[1]
    acc = jnp.zeros(o_ref.shape, F32)
    for i, y_ref in enumerate((ya_ref, yb_ref, yc_ref, yd_ref)):
        acc = acc + jnp.dot(y_ref[...].astype(BF16), w_ref[i * gw:(i + 1) * gw, :],
                            preferred_element_type=F32)
    ms = jnp.mean(acc * acc, axis=-1, keepdims=True)
    out = x_ref[...] + (acc * lax.rsqrt(ms + EPS)) * g_ref[...]
    r0 = pl.program_id(0) * tm
    local = (r0 - (r0 // period) * period) + lax.broadcasted_iota(I32, (tm, 1), 0)
    local = jnp.where(local >= period, local - period, local)
    valid = (local >= valid_lo) & (local < valid_hi)
    o_ref[...] = jnp.where(valid, out, 0.0)


def _merge(ys, x2d, w_bf16, g, tm, period, valid_lo, valid_hi):
    rows, d = x2d.shape
    assert tm <= period
    gw = ys[0].shape[1]
    yspec = pl.BlockSpec((tm, gw), lambda i: (i, 0))
    return pl.pallas_call(
        functools.partial(_merge_kernel, period=period, valid_lo=valid_lo, valid_hi=valid_hi),
        grid=(rows // tm,),
        in_specs=[yspec, yspec, yspec, yspec,
                  pl.BlockSpec((tm, d), lambda i: (i, 0)),
                  pl.BlockSpec(w_bf16.shape, lambda i: (0, 0)),
                  pl.BlockSpec((1, d), lambda i: (0, 0))],
        out_specs=pl.BlockSpec((tm, d), lambda i: (i, 0)),
        out_shape=jax.ShapeDtypeStruct((rows, d), F32),
        compiler_params=_cparams("parallel"),
        name="merge",
    )(*ys, x2d, w_bf16, g)


def _row_tile(rows):
    return next(t for t in (512, 256, ROW_BLOCK) if rows % t == 0)


def _column_layout(gw):
    sizes = [gw] * 4 + [gw, gw, gw, FOX_HEADS, gw] + [gw, gw // 4, gw // 4, gw, H_IDX * D_IDX, D_IDX, H_IDX] + [gw] * 4
    names = ["aq", "af", "ai", "ag", "bq", "bk", "bv", "bf", "bg",
             "cq", "ck", "cv", "cg", "ciq", "cik", "ciw", "dq", "dk", "dv", "dg"]
    start = dict(zip(names, np.cumsum([0] + sizes[:-1])))
    size = dict(zip(names, sizes))
    src = -np.ones((16 * gw,), np.int64)
    def put(dst, name, off=0):
        src[dst + off:dst + off + size[name]] = np.arange(start[name], start[name] + size[name])
    for i, n in enumerate(["aq", "af", "ai", "ag", "bq", "bk", "bv", "bg", "cq", "cg", "ciq"]):
        put(i * gw, n)
    ckv0 = 11 * gw
    put(ckv0, "ck")
    put(ckv0 + gw // 4, "cv")
    misc0 = ckv0 + LANES
    put(misc0, "cik", IK0)
    put(misc0, "ciw", IW0)
    put(misc0, "bf", BF0)
    for i, n in enumerate(["dq", "dk", "dv", "dg"]):
        put(12 * gw + i * gw, n)
    return src


def _relayout_w_in(w_in_l, src):
    cols = jnp.take(w_in_l, jnp.asarray(np.maximum(src, 0)), axis=1)
    return jnp.where(jnp.asarray(src >= 0)[None, :], cols, 0.0).astype(BF16)


def _ret_tables(pos, chunk, gw):
    hd = gw // N_HEADS
    half = hd // 2
    inv = ROPE_BASE ** (-jnp.arange(half, dtype=F32) / half)
    ang = pos.astype(F32)[:, None] * inv[None, :]
    cos_h = jnp.concatenate([jnp.cos(ang), jnp.cos(ang)], axis=-1)
    sin_h = jnp.concatenate([-jnp.sin(ang), jnp.sin(ang)], axis=-1)
    cos = jnp.tile(cos_h, (1, N_HEADS))
    sin = jnp.tile(sin_h, (1, N_HEADS))
    lg = jnp.log(1.0 - 2.0 ** (-5.0 - jnp.arange(N_HEADS, dtype=F32)))
    lg_l = jnp.repeat(lg, hd)[None, :]
    t = jnp.arange(chunk, dtype=F32)[:, None]
    eb = jnp.exp((t + 1.0) * lg_l)
    ke2s = jnp.exp((chunk - 1.0 - t) * lg_l)
    dend = jnp.exp(chunk * lg_l)
    dt = jnp.arange(chunk, dtype=F32)[:, None] - jnp.arange(chunk, dtype=F32)[None, :]
    gam = jnp.concatenate([jnp.where(dt >= 0, jnp.exp(dt * lg[h]), 0.0) for h in range(N_HEADS)], axis=0)
    return cos, sin, eb, ke2s, dend, gam


def _state_to_bd(state):
    bsz, h, k, v = state.shape
    eye = jnp.eye(h, dtype=state.dtype)
    st = jnp.einsum('bhkv,hg->bhvgk', state, eye)
    return st.reshape(bsz, h * v, h * k)


def _bd_to_state(st, h):
    bsz, hv, hk = st.shape
    st5 = st.reshape(bsz, h, hv // h, h, hk // h)
    diag = jnp.stack([st5[:, i, :, i, :] for i in range(h)], axis=1)
    return jnp.swapaxes(diag, 2, 3)


def kernel(x_prompt, x_sample, state_hgrn, cache_fox_k, cache_fox_v, cache_fox_logf, cache_dsa_k,
           cache_dsa_v, cache_dsa_idx_k, state_ret, meta_tokens, w_in, w_out, fox_bias, hgrn_lb,
           norm_pre, norm_post):
    bsz, seq, d = x_prompt.shape
    dbsz, t_new, _ = x_sample.shape
    depth = w_in.shape[0]
    past = cache_fox_k.shape[2]
    gw = d // N_GROUPS
    hd = gw // N_HEADS
    assert gw == 2 * LANES and seq % ROW_BLOCK == 0 and past % ROW_BLOCK == 0 and t_new <= ROW_BLOCK
    assert t_new % SUBLANES == 0

    pad_front = ROW_BLOCK - N_META
    lp = ROW_BLOCK + seq
    ls = ROW_BLOCK
    lks = past + ROW_BLOCK
    n_chunks_p = lp // CHUNK
    k_top_p = min(TOP_K_MAX, seq // 4)
    k_top_s = min(TOP_K_MAX, (past + t_new) // 4)

    src = _column_layout(gw)
    n_cols = src.shape[0]
    col = {"a": 0, "bq": 4, "bk": 5, "bv": 6, "bg": 7, "cq": 8, "cg": 9, "ciq": 10, "d": 3}
    ckv_col = (11 * gw) // LANES
    misc_col = ckv_col + 1

    sm = jax.nn.softmax(hgrn_lb.astype(F32), axis=0)
    lbs = jnp.cumsum(sm, axis=0) - sm[0:1]

    xp = jnp.concatenate([jnp.zeros((bsz, pad_front, d), F32),
                          jnp.broadcast_to(meta_tokens.astype(F32)[None], (bsz, N_META, d)),
                          x_prompt], axis=1)
    xs = jnp.concatenate([x_sample, jnp.zeros((dbsz, ls - t_new, d), F32)], axis=1)

    tab_p = _ret_tables(jnp.arange(lp) - ROW_BLOCK, CHUNK, gw)
    tab_s = _ret_tables(past + jnp.arange(ls), t_new, gw)
    zero_state_p = jnp.zeros((bsz, gw, gw), F32)

    outs_p = {k: [] for k in ("hgrn", "fk", "fv", "fl", "ck", "cv", "ci", "ret")}
    outs_s = {k: [] for k in ("hgrn", "fk", "fv", "fl", "ck", "cv", "ci", "ret")}

    for l in range(depth):
        w_l = _relayout_w_in(w_in[l], src)
        w_o = w_out[l].astype(BF16)
        g_pre = norm_pre[l][None, :]
        g_post = norm_post[l][None, :]
        lb = lbs[l][None, :]
        bias = jnp.zeros((1, LANES), F32).at[0, BF0:BF0 + FOX_HEADS].set(fox_bias[l].astype(F32))

        p = _project(xp.reshape(bsz * lp, d), g_pre, w_l, _row_tile(bsz * lp)).reshape(bsz, lp, n_cols)
        ya, st_a = _hgrn(p, 0, lb, zero_state_p, CHUNK, n_chunks_p)
        yd, st_d = _ret(p, 3, tab_p, zero_state_p, CHUNK, n_chunks_p)
        lf, cc, cr = _foxprep(p, misc_col, bias, 0)
        yb = _fox(p, col["bq"], p, col["bk"], p, col["bv"], p, col["bg"], cc, cr, pad_front, 0, True)
        yc = _dsa(p, (col["cq"], col["cg"], col["ciq"], misc_col), p, ckv_col, p, misc_col,
                  k_top_p, pad_front, 0, True, pad_front, lp)
        flat = lambda a: a.reshape(bsz * lp, gw)
        xp = _merge([flat(ya), flat(yb), flat(yc), flat(yd)], xp.reshape(bsz * lp, d), w_o, g_post,
                    _row_tile(bsz * lp), lp, pad_front, lp).reshape(bsz, lp, d)
        pv = p[:, pad_front:, :]
        outs_p["hgrn"].append(_bd_to_state(st_a, N_HEADS))
        outs_p["ret"].append(_bd_to_state(st_d, N_HEADS))
        outs_p["fk"].append(pv[:, :, 5 * gw:6 * gw].reshape(bsz, -1, N_HEADS, hd))
        outs_p["fv"].append(pv[:, :, 6 * gw:7 * gw].reshape(bsz, -1, N_HEADS, hd))
        outs_p["fl"].append(lf[:, pad_front:, BF0:BF0 + FOX_HEADS])
        outs_p["ck"].append(pv[:, :, 11 * gw:11 * gw + hd])
        outs_p["cv"].append(pv[:, :, 11 * gw + hd:11 * gw + 2 * hd])
        outs_p["ci"].append(pv[:, :, misc_col * LANES + IK0:misc_col * LANES + IK0 + D_IDX])

        ps = _project(xs.reshape(dbsz * ls, d), g_pre, w_l, _row_tile(dbsz * ls)).reshape(dbsz, ls, n_cols)
        ya, st_a = _hgrn(ps, 0, lb, _state_to_bd(state_hgrn[l].astype(F32)), t_new, 1)
        yd, st_d = _ret(ps, 3, tab_s, _state_to_bd(state_ret[l].astype(F32)), t_new, 1)
        z = jnp.concatenate(
            [jnp.pad(cache_fox_logf[l].astype(F32), ((0, 0), (0, 0), (BF0, LANES - BF0 - FOX_HEADS))),
             ps[:, :, misc_col * LANES:(misc_col + 1) * LANES]], axis=1)
        lf, cc, cr = _foxprep(z, 0, bias, past)
        k_all = jnp.concatenate([cache_fox_k[l].reshape(dbsz, past, gw), ps[:, :, 5 * gw:6 * gw]], axis=1)
        v_all = jnp.concatenate([cache_fox_v[l].reshape(dbsz, past, gw), ps[:, :, 6 * gw:7 * gw]], axis=1)
        yb = _fox(ps, col["bq"], k_all, 0, v_all, 0, ps, col["bg"], cc, cr, 0, past, False)
        ckv_all = jnp.concatenate(
            [jnp.concatenate([cache_dsa_k[l], cache_dsa_v[l]], axis=-1).astype(F32),
             ps[:, :, 11 * gw:11 * gw + LANES]], axis=1)
        mk_all = jnp.concatenate(
            [jnp.pad(cache_dsa_idx_k[l].astype(F32), ((0, 0), (0, 0), (IK0, LANES - IK0 - D_IDX))),
             ps[:, :, misc_col * LANES:(misc_col + 1) * LANES]], axis=1)
        yc = _dsa(ps, (col["cq"], col["cg"], col["ciq"], misc_col), ckv_all, 0, mk_all, 0,
                  k_top_s, 0, past + t_new, False, 0, t_new)
        flat = lambda a: a.reshape(dbsz * ls, gw)
        xs = _merge([flat(ya), flat(yb), flat(yc), flat(yd)], xs.reshape(dbsz * ls, d), w_o, g_post,
                    ls, ls, 0, t_new).reshape(dbsz, ls, d)
        pn = ps[:, :t_new, :]
        outs_s["hgrn"].append(_bd_to_state(st_a, N_HEADS))
        outs_s["ret"].append(_bd_to_state(st_d, N_HEADS))
        outs_s["fk"].append(pn[:, :, 5 * gw:6 * gw].reshape(dbsz, -1, N_HEADS, hd))
        outs_s["fv"].append(pn[:, :, 6 * gw:7 * gw].reshape(dbsz, -1, N_HEADS, hd))
        outs_s["fl"].append(lf[:, past:past + t_new, BF0:BF0 + FOX_HEADS])
        outs_s["ck"].append(pn[:, :, 11 * gw:11 * gw + hd])
        outs_s["cv"].append(pn[:, :, 11 * gw + hd:11 * gw + 2 * hd])
        outs_s["ci"].append(pn[:, :, misc_col * LANES + IK0:misc_col * LANES + IK0 + D_IDX])

    dt = x_prompt.dtype
    st = lambda xs_list: jnp.stack(xs_list, axis=0).astype(dt)
    order = ("hgrn", "fk", "fv", "fl", "ck", "cv", "ci", "ret")
    return ((xp[:, ROW_BLOCK:, :].astype(dt), xs[:, :t_new, :].astype(dt))
            + tuple(st(outs_p[k]) for k in order) + tuple(st(outs_s[k]) for k in order))
```

```python
import functools

import numpy as np
import jax
import jax.numpy as jnp
from jax import lax
from jax.experimental import pallas as pl
from jax.experimental.pallas import tpu as pltpu

F32 = jnp.float32
BF16 = jnp.bfloat16
Y_DTYPE = BF16
I32 = jnp.int32
LOG2E = float(np.log2(np.e))
ONES_ROWS = 16

N_META = 16
CHUNK = 64
N_GROUPS = 4
N_HEADS = 4
H_IDX = 8
D_IDX = 32
TOP_K_MAX = 256
ROPE_BASE = 10000.0
EPS = 1e-6
FOX_HEADS = 4

LANES = 128
SUBLANES = 8
ROW_BLOCK = 128
GROUP_W = 2 * LANES
VMEM_LIMIT_BYTES = 56 * 1024 * 1024

IK0 = 0
IW0 = 32
BF0 = 64

NEG = -1e30
M_FLOOR = -1e20
INT_MIN = np.int32(-2 ** 31)
STEP_BLOCKS = 4
KEY_STEP = STEP_BLOCKS * ROW_BLOCK
GLA_BATCH_ROWS = 4


def _cparams(*sem):
    return pltpu.CompilerParams(dimension_semantics=sem, vmem_limit_bytes=VMEM_LIMIT_BYTES)


def _split3(x):
    h = x.astype(BF16)
    r = x - h.astype(F32)
    m = r.astype(BF16)
    lo = (r - m.astype(F32)).astype(BF16)
    return h, m, lo


def _dot_exact_lhs(a_bf16, x):
    d = lambda y: jnp.dot(a_bf16, y, preferred_element_type=F32)
    h, m, lo = _split3(x)
    return d(h) + d(m) + d(lo)


def _dot_exact_rhs(x, a_bf16):
    d = lambda y: jnp.dot(y, a_bf16, preferred_element_type=F32)
    h, m, lo = _split3(x)
    return d(h) + d(m) + d(lo)


def _dot_nt(a, b):
    return lax.dot_general(a, b, (((1,), (1,)), ((), ())), preferred_element_type=F32)


def _dot_tn(a, b):
    return lax.dot_general(a, b, (((0,), (0,)), ((), ())), preferred_element_type=F32)


def _log2(n):
    assert n > 0 and n & (n - 1) == 0, n
    return n.bit_length() - 1


def _head_masks(width, n_heads):
    lane = lax.broadcasted_iota(I32, (1, width), 1)
    sh = _log2(width // n_heads)
    return [(lax.shift_right_logical(lane, sh) == h).astype(F32) for h in range(n_heads)]


def _block_diag(width, n_heads, value):
    r = lax.broadcasted_iota(I32, (width, width), 0)
    c = lax.broadcasted_iota(I32, (width, width), 1)
    sh = _log2(width // n_heads)
    same = lax.shift_right_logical(r, sh) == lax.shift_right_logical(c, sh)
    return jnp.where(same, value, 0.0).astype(F32)


def _silu(x):
    return x * jax.nn.sigmoid(x)


def _proj_kernel(x_ref, g_ref, w_ref, o_ref, *, col_chunk):
    x = x_ref[...]
    ms = jnp.mean(x * x, axis=-1, keepdims=True)
    xn = ((x * lax.rsqrt(ms + EPS)) * g_ref[...]).astype(BF16)
    for c in range(o_ref.shape[1] // col_chunk):
        cols = slice(c * col_chunk, (c + 1) * col_chunk)
        o_ref[:, cols] = jnp.dot(xn, w_ref[:, cols], preferred_element_type=F32)


def _project(x2d, g, w_bf16, tm):
    rows, d = x2d.shape
    n = w_bf16.shape[1]
    return pl.pallas_call(
        functools.partial(_proj_kernel, col_chunk=1024),
        grid=(rows // tm,),
        in_specs=[pl.BlockSpec((tm, d), lambda i: (i, 0)),
                  pl.BlockSpec((1, d), lambda i: (0, 0)),
                  pl.BlockSpec((d, n), lambda i: (0, 0))],
        out_specs=pl.BlockSpec((tm, n), lambda i: (i, 0)),
        out_shape=jax.ShapeDtypeStruct((rows, n), F32),
        compiler_params=_cparams("parallel"),
        name="proj",
    )(x2d, g, w_bf16)


def _head_scores(qa, ka, hm):
    q_stack = jnp.concatenate([qa * hm[h] for h in range(N_HEADS)], axis=0).astype(BF16)
    return _dot_nt(q_stack, ka.astype(BF16))


def _hier_constants(chunk):
    halves = [chunk >> (i + 1) for i in range(_log2(chunk))]
    r = lax.broadcasted_iota(I32, (chunk, chunk), 0)
    c = lax.broadcasted_iota(I32, (chunk, chunk), 1)
    ts = lax.broadcasted_iota(I32, (N_HEADS * chunk, chunk), 0) & (chunk - 1)
    ss = lax.broadcasted_iota(I32, (N_HEADS * chunk, chunk), 1)
    sels, masks = [], []
    for h in halves:
        sh = _log2(h)
        grp = lambda x: lax.shift_right_logical(x, sh + 1)
        if h < SUBLANES:
            sels.append((c == lax.shift_left(grp(r), sh + 1) + (h - 1)).astype(BF16))
        upper_t = (lax.shift_right_logical(ts, sh) & 1).astype(F32)
        lower_s = 1.0 - (lax.shift_right_logical(ss, sh) & 1).astype(F32)
        masks.append(jnp.where(grp(ts) == grp(ss), upper_t * lower_s, 0.0))
    return halves, jnp.concatenate(sels, axis=0), masks


def _decayed_scores(q, k, b, hier, hm):
    halves, sel_small, masks = hier
    chunk, w = q.shape
    small_refs = _dot_exact_lhs(sel_small, b)
    att, n_small = None, 0
    for h, mask in zip(halves, masks):
        if h < SUBLANES:
            ref = small_refs[n_small * chunk:(n_small + 1) * chunk, :]
            n_small += 1
        else:
            ref = jnp.concatenate([jnp.broadcast_to(b[g0 + h - 1:g0 + h, :], (2 * h, w))
                                   for g0 in range(0, chunk, 2 * h)], axis=0)
        qa = q * jnp.exp(jnp.minimum(b - ref, 0.0))
        ka = k * jnp.exp(jnp.minimum(ref - b, 0.0))
        term = _head_scores(qa, ka, hm) * mask
        att = term if att is None else att + term
    return att


def _gla_chunk(att, qe, ke2, v, decay_end, st, hm, bd):
    c = qe.shape[0]
    o_stack = jnp.dot(att.astype(BF16), v.astype(BF16), preferred_element_type=F32)
    o_intra = o_stack[0:c] * hm[0]
    for h in range(1, N_HEADS):
        o_intra = o_intra + o_stack[h * c:(h + 1) * c] * hm[h]
    o_inter = _dot_nt(qe.astype(BF16), st.astype(BF16))
    st_new = st * decay_end + _dot_tn(v.astype(BF16), ke2.astype(BF16)) * bd
    return o_inter + o_intra, st_new


def _head_rms_gate(o, gate, bd_mean_bf16):
    h, m, _ = _split3(o * o)
    ms = (jnp.dot(h, bd_mean_bf16, preferred_element_type=F32)
          + jnp.dot(m, bd_mean_bf16, preferred_element_type=F32))
    return (o * lax.rsqrt(ms + EPS)) * _silu(gate)


def _gla_prologue(st0_ref, y_ref, st_ref, chunk, n_chunks):
    @pl.when(pl.program_id(1) == 0)
    def _():
        st_ref[...] = st0_ref[...]
    tail = n_chunks * chunk
    if tail < y_ref.shape[1]:
        y_ref[:, tail:, :] = jnp.zeros((y_ref.shape[0], y_ref.shape[1] - tail, y_ref.shape[2]),
                                       y_ref.dtype)


def _hgrn_kernel(a_ref, lb_ref, st0_ref, y_ref, st_ref, *, chunk, n_chunks):
    n_b, _, w = y_ref.shape
    hm = _head_masks(w, N_HEADS)
    bd = _block_diag(w, N_HEADS, 1.0)
    bd_mean = _block_diag(w, N_HEADS, 1.0 / (w // N_HEADS)).astype(BF16)
    r = lax.broadcasted_iota(I32, (chunk, chunk), 0)
    s = lax.broadcasted_iota(I32, (chunk, chunk), 1)
    tri = (r >= s).astype(BF16)
    hier = _hier_constants(chunk)
    bd_ones = bd.astype(BF16)
    lb = lb_ref[...]
    _gla_prologue(st0_ref, y_ref, st_ref, chunk, n_chunks)

    def body(c, carry):
        rows = pl.ds(pl.multiple_of(c * chunk, chunk), chunk)
        for g in range(n_b):
            q = a_ref[g, rows, 0:w]
            f = lb + (1.0 - lb) * jax.nn.sigmoid(a_ref[g, rows, w:2 * w])
            k = 1.0 - f
            v = a_ref[g, rows, 2 * w:3 * w]
            gate = a_ref[g, rows, 3 * w:4 * w]
            b = _dot_exact_lhs(tri, jnp.log(f))
            b_end = b[chunk - 1:chunk, :]
            att = _decayed_scores(q, k, b, hier, hm)
            o, st_new = _gla_chunk(att, q * jnp.exp(b), k * jnp.exp(b_end - b), v, jnp.exp(b_end),
                                   st_ref[g], hm, bd)
            o = o + jnp.dot((q * k).astype(BF16), bd_ones, preferred_element_type=F32) * v
            st_ref[g] = st_new
            y_ref[g, rows, :] = _head_rms_gate(o, gate, bd_mean).astype(y_ref.dtype)
        return carry

    lax.fori_loop(0, n_chunks, body, 0)


def _ret_kernel(d_ref, cos_ref, sin_ref, eb_ref, ke2s_ref, dend_ref, gam_ref, st0_ref, y_ref, st_ref,
                *, chunk, n_chunks):
    n_b, _, w = y_ref.shape
    hd = w // N_HEADS
    hm = _head_masks(w, N_HEADS)
    bd = _block_diag(w, N_HEADS, 1.0)
    bd_mean = _block_diag(w, N_HEADS, 1.0 / hd).astype(BF16)
    lane = lax.broadcasted_iota(I32, (1, w), 1)
    first_half = (lane & (hd - 1)) < (hd // 2)
    eb = eb_ref[...]
    ke2s = ke2s_ref[...]
    dend = dend_ref[...]
    gam = gam_ref[...]
    _gla_prologue(st0_ref, y_ref, st_ref, chunk, n_chunks)

    def rope(x, cos, sin_signed):
        swapped = jnp.where(first_half, pltpu.roll(x, w - hd // 2, 1), pltpu.roll(x, hd // 2, 1))
        return x * cos + swapped * sin_signed

    def body(c, carry):
        rows = pl.ds(pl.multiple_of(c * chunk, chunk), chunk)
        cos = cos_ref[rows, :]
        sin = sin_ref[rows, :]
        for g in range(n_b):
            q = rope(d_ref[g, rows, 0:w], cos, sin)
            k = rope(d_ref[g, rows, w:2 * w], cos, sin) * (hd ** -0.5)
            v = d_ref[g, rows, 2 * w:3 * w]
            gate = d_ref[g, rows, 3 * w:4 * w]
            att = _head_scores(q, k, hm) * gam
            o, st_new = _gla_chunk(att, q * eb, k * ke2s, v, dend, st_ref[g], hm, bd)
            st_ref[g] = st_new
            y_ref[g, rows, :] = _head_rms_gate(o, gate, bd_mean).astype(y_ref.dtype)
        return carry

    lax.fori_loop(0, n_chunks, body, 0)


def _gla_grid(bsz, lp, chunk, n_chunks):
    n_b = next(n for n in (GLA_BATCH_ROWS, 2, 1) if bsz % n == 0)
    halves = n_chunks * chunk == lp and n_chunks % 2 == 0 and (lp // 2) % SUBLANES == 0
    n_seq = 2 if halves else 1
    return n_b, n_seq, n_chunks // n_seq


def _hgrn(p3, col_block, lb, st0, chunk, n_chunks):
    bsz, lp, _ = p3.shape
    w = lb.shape[1]
    n_b, n_seq, n_chunks_blk = _gla_grid(bsz, lp, chunk, n_chunks)
    rows = lp // n_seq
    return pl.pallas_call(
        functools.partial(_hgrn_kernel, chunk=chunk, n_chunks=n_chunks_blk),
        grid=(bsz // n_b, n_seq),
        in_specs=[pl.BlockSpec((n_b, rows, 4 * w), lambda b, s: (b, s, col_block)),
                  pl.BlockSpec((1, w), lambda b, s: (0, 0)),
                  pl.BlockSpec((n_b, w, w), lambda b, s: (b, 0, 0))],
        out_specs=[pl.BlockSpec((n_b, rows, w), lambda b, s: (b, s, 0)),
                   pl.BlockSpec((n_b, w, w), lambda b, s: (b, 0, 0))],
        out_shape=[jax.ShapeDtypeStruct((bsz, lp, w), Y_DTYPE),
                   jax.ShapeDtypeStruct((bsz, w, w), F32)],
        compiler_params=_cparams("parallel", "arbitrary"),
        name="hgrn",
    )(p3, lb, st0)


def _ret(p3, col_block, tables, st0, chunk, n_chunks):
    bsz, lp, _ = p3.shape
    w = st0.shape[1]
    n_b, n_seq, n_chunks_blk = _gla_grid(bsz, lp, chunk, n_chunks)
    rows = lp // n_seq
    cos, sin, eb, ke2s, dend, gam = tables
    full = lambda a: pl.BlockSpec(a.shape, lambda b, s: (0,) * a.ndim)
    per_seq = pl.BlockSpec((rows, w), lambda b, s: (s, 0))
    return pl.pallas_call(
        functools.partial(_ret_kernel, chunk=chunk, n_chunks=n_chunks_blk),
        grid=(bsz // n_b, n_seq),
        in_specs=[pl.BlockSpec((n_b, rows, 4 * w), lambda b, s: (b, s, col_block)),
                  per_seq, per_seq, full(eb), full(ke2s), full(dend), full(gam),
                  pl.BlockSpec((n_b, w, w), lambda b, s: (b, 0, 0))],
        out_specs=[pl.BlockSpec((n_b, rows, w), lambda b, s: (b, s, 0)),
                   pl.BlockSpec((n_b, w, w), lambda b, s: (b, 0, 0))],
        out_shape=[jax.ShapeDtypeStruct((bsz, lp, w), Y_DTYPE),
                   jax.ShapeDtypeStruct((bsz, w, w), F32)],
        compiler_params=_cparams("parallel", "arbitrary"),
        name="ret",
    )(p3, cos, sin, eb, ke2s, dend, gam, st0)


def _log_sigmoid(x):
    return -(jnp.maximum(-x, 0.0) + jnp.log(1.0 + jnp.exp(-jnp.abs(x))))


def _foxprep_kernel(z_ref, bias_ref, lf_ref, cc_ref, cr_ref, *, n_pass):
    nblk = z_ref.shape[0] // ROW_BLOCK
    r = lax.broadcasted_iota(I32, (ROW_BLOCK, ROW_BLOCK), 0)
    s = lax.broadcasted_iota(I32, (ROW_BLOCK, ROW_BLOCK), 1)
    tri = (r >= s).astype(BF16)
    bias = bias_ref[...]

    def body(i, carry):
        rows = pl.ds(pl.multiple_of(i * ROW_BLOCK, ROW_BLOCK), ROW_BLOCK)
        z = z_ref[rows, :]
        rowi = i * ROW_BLOCK + lax.broadcasted_iota(I32, (ROW_BLOCK, 1), 0)
        lf = jnp.where(rowi < n_pass, z, _log_sigmoid(z + bias))
        lf_ref[rows, :] = lf
        cs = _dot_exact_lhs(tri, lf) + carry
        cc_ref[rows, :] = cs
        cr_ref[:, rows] = cs.T[BF0:BF0 + SUBLANES, :]
        return cs[ROW_BLOCK - 1:ROW_BLOCK, :]

    lax.fori_loop(0, nblk, body, jnp.zeros((1, LANES), F32))
    pad = cr_ref.shape[1] - nblk * ROW_BLOCK
    if pad:
        cr_ref[:, nblk * ROW_BLOCK:] = jnp.zeros((SUBLANES, pad), F32)


def _foxprep(z3, z_col, bias, n_pass):
    bsz, lk, _ = z3.shape
    cr_cols = pl.cdiv(lk, KEY_STEP) * KEY_STEP
    return pl.pallas_call(
        functools.partial(_foxprep_kernel, n_pass=n_pass),
        grid=(bsz,),
        in_specs=[pl.BlockSpec((None, lk, LANES), lambda b: (b, 0, z_col)),
                  pl.BlockSpec((1, LANES), lambda b: (0, 0))],
        out_specs=[pl.BlockSpec((None, lk, LANES), lambda b: (b, 0, 0)),
                   pl.BlockSpec((None, lk, LANES), lambda b: (b, 0, 0)),
                   pl.BlockSpec((None, SUBLANES, cr_cols), lambda b: (b, 0, 0))],
        out_shape=[jax.ShapeDtypeStruct((bsz, lk, LANES), F32),
                   jax.ShapeDtypeStruct((bsz, lk, LANES), F32),
                   jax.ShapeDtypeStruct((bsz, SUBLANES, cr_cols), F32)],
        compiler_params=_cparams("parallel"),
        name="foxprep",
    )(z3, bias)


MXU_COLS = 2 * LANES


def _query_block(lq):
    return MXU_COLS if lq > ROW_BLOCK else ROW_BLOCK


def _heads_per_dot(qb):
    assert MXU_COLS % qb == 0
    return MXU_COLS // qb


def _head_weights(x_t, rows_per_head, n_heads):
    hpd = _heads_per_dot(x_t.shape[1])
    rowh = lax.shift_right_logical(lax.broadcasted_iota(I32, (x_t.shape[0], 1), 0),
                                   _log2(rows_per_head))
    only = lambda h: jnp.where(rowh == h, x_t, 0.0)
    return [jnp.concatenate([only(d * hpd + i) for i in range(hpd)], axis=1).astype(BF16)
            for d in range(n_heads // hpd)]


def _attend_two_pass(n_steps, logits_fn, value_t_fn, shifts, s_scr, acc_scr, qb, hd):
    fold = lambda x, op: op(x.reshape(KEY_STEP // SUBLANES, SUBLANES, qb), axis=0)
    hv = hd + ONES_ROWS

    def max_step(i, ms):
        tiles = logits_fn(i)
        for h in range(N_HEADS):
            s_scr[i * N_HEADS + h] = tiles[h]
        return tuple(jnp.maximum(ms[h], fold(tiles[h], jnp.max)) for h in range(N_HEADS))

    ms = lax.fori_loop(0, n_steps, max_step,
                       tuple(jnp.full((SUBLANES, qb), M_FLOOR, F32) for _ in range(N_HEADS)))
    m_logit = [jnp.max(ms[h], axis=0, keepdims=True) + shifts[h] for h in range(N_HEADS)]
    acc_scr[...] = jnp.zeros(acc_scr.shape, F32)

    def sum_step(i, carry):
        for h in range(N_HEADS):
            p = jnp.exp2(s_scr[i * N_HEADS + h] - (m_logit[h] - shifts[h]))
            acc_scr[h * hv:(h + 1) * hv, :] += jnp.dot(value_t_fn(h, i), p.astype(BF16),
                                                       preferred_element_type=F32)
        return carry

    lax.fori_loop(0, n_steps, sum_step, 0)
    outs = []
    for h in range(N_HEADS):
        l = acc_scr[h * hv + hd:h * hv + hd + 1, :]
        outs.append(acc_scr[h * hv:h * hv + hd, :] / jnp.where(l > 0.0, l, 1.0))
    return jnp.concatenate(outs, axis=0)


def _fox_kernel(q_ref, k_ref, v_ref, g_ref, cc_ref, cr_ref, y_ref, *rest, key_lo, q_off, causal_blocks,
                emit_caches):
    if emit_caches:
        kc_ref, vc_ref, kbf_scr, vt_scr, ck_scr, s_scr, acc_scr = rest
    else:
        kbf_scr, vt_scr, ck_scr, s_scr, acc_scr = rest
    qb, w = q_ref.shape
    hd = w // N_HEADS
    hv = hd + ONES_ROWS
    nkb_total = k_ref.shape[0] // ROW_BLOCK
    hpd = _heads_per_dot(qb)
    j = pl.program_id(1)
    nkb = jnp.minimum((j + 1) * (qb // ROW_BLOCK), nkb_total) if causal_blocks else nkb_total

    @pl.when(j == 0)
    def _prepare_batch_row():
        def blk(i, carry):
            rows = pl.ds(pl.multiple_of(i * ROW_BLOCK, ROW_BLOCK), ROW_BLOCK)
            kbf_scr[rows, :] = k_ref[rows, :].astype(BF16)
            v_t = v_ref[rows, :].T.astype(BF16)
            cs = cc_ref[rows, :] * LOG2E
            for h in range(N_HEADS):
                vt_scr[h * hv:h * hv + hd, rows] = v_t[h * hd:(h + 1) * hd, :]
                ck_scr[h, rows, :] = jnp.broadcast_to(cs[:, BF0 + h:BF0 + h + 1], (ROW_BLOCK, LANES))
            return carry
        lax.fori_loop(0, nkb_total, blk, 0)
        if emit_caches:
            kc_ref[...] = k_ref[key_lo:, :]
            vc_ref[...] = v_ref[key_lo:, :]
        pad = vt_scr.shape[1] - nkb_total * ROW_BLOCK
        for h in range(N_HEADS):
            if pad:
                vt_scr[h * hv:h * hv + hd, nkb_total * ROW_BLOCK:] = jnp.zeros((hd, pad), BF16)
            vt_scr[h * hv + hd:(h + 1) * hv, :] = jnp.ones((ONES_ROWS, vt_scr.shape[1]), BF16)

    wq = _head_weights((q_ref[...] * (hd ** -0.5 * LOG2E)).T, hd, N_HEADS)
    qcol = pl.ds(pl.multiple_of(j * qb + q_off, ROW_BLOCK), qb)
    cq = [cr_ref[h:h + 1, qcol] * LOG2E for h in range(N_HEADS)]
    qrow = j * qb + q_off + lax.broadcasted_iota(I32, (1, qb), 1)
    sub = lax.broadcasted_iota(I32, (KEY_STEP, 1), 0)

    def logits(i):
        k0 = pl.multiple_of(i * KEY_STEP, KEY_STEP)
        krows = pl.ds(k0, KEY_STEP)
        kblk = kbf_scr[krows, :]
        kidx = k0 + sub
        ok = (kidx >= key_lo) & (kidx <= qrow)
        tiles = []
        for d in range(N_HEADS // hpd):
            s2 = jnp.dot(kblk, wq[d], preferred_element_type=F32)
            for i2 in range(hpd):
                h = d * hpd + i2
                ck = jnp.concatenate([ck_scr[h, krows, :]] * (qb // LANES), axis=1)
                tiles.append(jnp.where(ok, s2[:, i2 * qb:(i2 + 1) * qb] - ck, NEG))
        return tiles

    def value_t(h, i):
        return vt_scr[h * hv:(h + 1) * hv, pl.ds(pl.multiple_of(i * KEY_STEP, KEY_STEP), KEY_STEP)]

    n_steps = lax.shift_right_logical(nkb + (STEP_BLOCKS - 1), _log2(STEP_BLOCKS))
    o_t = _attend_two_pass(n_steps, logits, value_t, cq, s_scr, acc_scr, qb, hd)
    y_ref[...] = (o_t.T * _silu(g_ref[...])).astype(y_ref.dtype)


def _fox(q_arr, q_col, k_arr, k_col, v_arr, v_col, g_arr, g_col, cc, cr, key_lo, q_off, causal_blocks,
         emit_caches):
    bsz, lq, _ = q_arr.shape
    lk = k_arr.shape[1]
    w = GROUP_W
    qb = _query_block(lq)
    key_rows = pl.cdiv(lk, KEY_STEP) * KEY_STEP
    assert cr.shape[2] >= q_off + pl.cdiv(lq, qb) * qb
    out_specs = [pl.BlockSpec((None, qb, w), lambda b, j: (b, j, 0))]
    out_shape = [jax.ShapeDtypeStruct((bsz, lq, w), Y_DTYPE)]
    if emit_caches:
        out_specs += [pl.BlockSpec((None, lk - key_lo, w), lambda b, j: (b, 0, 0))] * 2
        out_shape += [jax.ShapeDtypeStruct((bsz, lk - key_lo, w), F32)] * 2
    return pl.pallas_call(
        functools.partial(_fox_kernel, key_lo=key_lo, q_off=q_off, causal_blocks=causal_blocks,
                          emit_caches=emit_caches),
        grid=(bsz, pl.cdiv(lq, qb)),
        in_specs=[pl.BlockSpec((None, qb, w), lambda b, j: (b, j, q_col)),
                  pl.BlockSpec((None, lk, w), lambda b, j: (b, 0, k_col)),
                  pl.BlockSpec((None, lk, w), lambda b, j: (b, 0, v_col)),
                  pl.BlockSpec((None, qb, w), lambda b, j: (b, j, g_col)),
                  pl.BlockSpec((None, lk, LANES), lambda b, j: (b, 0, 0)),
                  pl.BlockSpec((None, SUBLANES, cr.shape[2]), lambda b, j: (b, 0, 0))],
        out_specs=out_specs,
        out_shape=out_shape,
        scratch_shapes=[pltpu.VMEM((key_rows, w), BF16),
                        pltpu.VMEM((w + N_HEADS * ONES_ROWS, key_rows), BF16),
                        pltpu.VMEM((N_HEADS, key_rows, LANES), F32),
                        pltpu.VMEM((N_HEADS * key_rows // KEY_STEP, KEY_STEP, qb), F32),
                        pltpu.VMEM((w + N_HEADS * ONES_ROWS, qb), F32)],
        compiler_params=_cparams("parallel", "arbitrary"),
        name="fox",
    )(q_arr, k_arr, v_arr, g_arr, cc, cr)


def _dsa_kernel(cq_ref, cg_ref, ciq_ref, mq_ref, ckv_ref, mk_ref, y_ref, *rest,
                k_top, key_lo, key_hi, chunk_causal, q_lo, q_hi, emit_caches):
    if emit_caches:
        kc_ref, vc_ref, key_scr, kvb_scr, vt_scr, mkb_scr, s_scr, acc_scr = rest
    else:
        key_scr, kvb_scr, vt_scr, mkb_scr, s_scr, acc_scr = rest
    qb, w = cq_ref.shape
    hd = w // N_HEADS
    hpd = _heads_per_dot(qb)
    nkb_total = ckv_ref.shape[0] // ROW_BLOCK
    j = pl.program_id(1)
    nkb = jnp.minimum((j + 1) * (qb // ROW_BLOCK), nkb_total) if chunk_causal else nkb_total
    idx_scale = (H_IDX * D_IDX) ** -0.5

    @pl.when(j == 0)
    def _prepare_batch_row():
        def blk(i, carry):
            rows = pl.ds(pl.multiple_of(i * ROW_BLOCK, ROW_BLOCK), ROW_BLOCK)
            kv = ckv_ref[rows, :]
            kvb_scr[rows, :] = kv.astype(BF16)
            vt_scr[0:hd, rows] = kv.T[hd:2 * hd, :].astype(BF16)
            mkb_scr[rows, :] = mk_ref[rows, :].astype(BF16)
            return carry
        lax.fori_loop(0, nkb_total, blk, 0)
        if emit_caches:
            kv_new = ckv_ref[key_lo:, :]
            kc_ref[...] = kv_new[:, 0:hd]
            vc_ref[...] = kv_new[:, hd:2 * hd]
        pad = vt_scr.shape[1] - nkb_total * ROW_BLOCK
        if pad:
            vt_scr[0:hd, nkb_total * ROW_BLOCK:] = jnp.zeros((hd, pad), BF16)
        vt_scr[hd:, :] = jnp.ones((ONES_ROWS, vt_scr.shape[1]), BF16)

    def pad_rows(x):
        return jnp.concatenate([x, jnp.zeros((LANES - x.shape[0], qb), F32)], axis=0)

    iq_t = ciq_ref[...].T
    iw_t = mq_ref[...].T[IW0:IW0 + H_IDX, :]
    q_t = (cq_ref[...] * (hd ** -0.5 * LOG2E)).T
    def side_by_side(x_t, rows, n_heads):
        return [jnp.concatenate([pad_rows(x_t[(d * hpd + i) * rows:(d * hpd + i + 1) * rows, :])
                                 for i in range(hpd)], axis=1).astype(BF16)
                for d in range(n_heads // hpd)]

    iq_rhs = side_by_side(iq_t, D_IDX, H_IDX)
    q_rhs = side_by_side(q_t, hd, N_HEADS)

    qrow = j * qb + lax.broadcasted_iota(I32, (1, qb), 1)
    if chunk_causal:
        hi = (lax.shift_right_logical(qrow, 6) + 1) * CHUNK
    else:
        hi = jnp.full((1, qb), key_hi, I32)
    sub = lax.broadcasted_iota(I32, (KEY_STEP, 1), 0)
    n_steps = lax.shift_right_logical(nkb + (STEP_BLOCKS - 1), _log2(STEP_BLOCKS))

    def score_step(i, carry):
        k0 = pl.multiple_of(i * KEY_STEP, KEY_STEP)
        mk = mkb_scr[pl.ds(k0, KEY_STEP), :]
        acc = jnp.zeros((KEY_STEP, qb), F32)
        for d in range(H_IDX // hpd):
            sc2 = jnp.dot(mk, iq_rhs[d], preferred_element_type=F32)
            for i2 in range(hpd):
                h = d * hpd + i2
                acc = acc + jnp.maximum(sc2[:, i2 * qb:(i2 + 1) * qb], 0.0) * iw_t[h:h + 1, :]
        score = acc * idx_scale + 0.0
        kidx = k0 + sub
        adm = (kidx >= key_lo) & (kidx < hi)
        u = pltpu.bitcast(score, I32)
        key = u ^ (lax.shift_right_arithmetic(u, 31) & np.int32(0x7FFFFFFF))
        key_scr[pl.ds(k0, KEY_STEP), :] = jnp.where(adm, key, INT_MIN)
        return carry

    lax.fori_loop(0, n_steps, score_step, 0)

    def count(pred):
        groups = qb // LANES

        def cb(i, c8s):
            k0 = pl.multiple_of(i * KEY_STEP, KEY_STEP)
            out = []
            for lg in range(groups):
                lanes = lambda x, lg=lg: x[:, lg * LANES:(lg + 1) * LANES]
                kk = key_scr[pl.ds(k0, KEY_STEP), lg * LANES:(lg + 1) * LANES]
                ind = jnp.where(pred(kk, k0, lanes), 1, 0).astype(I32)
                out.append(c8s[lg] + jnp.sum(ind.reshape(KEY_STEP // SUBLANES, SUBLANES, LANES), axis=0))
            return tuple(out)
        c8s = lax.fori_loop(0, n_steps, cb,
                            tuple(jnp.zeros((SUBLANES, LANES), I32) for _ in range(groups)))
        return jnp.concatenate([jnp.sum(c, axis=0, keepdims=True) for c in c8s], axis=1)

    def bisect(n_bits, count_ge, need):
        def bit_body(i, carry):
            ans, cnt_ans = carry
            cand = ans | lax.shift_left(np.int32(1), jnp.int32(n_bits - 1) - i)
            cnt = count_ge(cand)
            ok = cnt >= need
            return jnp.where(ok, cand, ans), jnp.where(ok, cnt, cnt_ans)
        return lax.fori_loop(0, n_bits, bit_body,
                             (jnp.zeros((1, qb), I32), jnp.full((1, qb), np.int32(2 ** 30), I32)))

    ans, cnt_thr = bisect(32, lambda cand: count(
        lambda kk, k0, lanes: kk >= lanes(cand ^ INT_MIN)), k_top)
    thr = ans ^ INT_MIN
    tie = (cnt_thr > k_top) & (thr != INT_MIN) & (qrow >= q_lo) & (qrow < q_hi)

    @pl.when(jnp.max(jnp.where(tie, 1, 0)) > 0)
    def _break_ties():
        n_rev_bits = _log2(pl.next_power_of_2(key_scr.shape[0]))
        rev_base = np.int32(2 ** n_rev_bits - 1)
        need = k_top - count(lambda kk, k0, lanes: kk > lanes(thr))
        ans2, _ = bisect(n_rev_bits, lambda cand: count(
            lambda kk, k0, lanes: (kk == lanes(thr)) & ((rev_base - (k0 + sub)) >= lanes(cand))), need)

        def demote(i, carry):
            k0 = pl.multiple_of(i * KEY_STEP, KEY_STEP)
            kk = key_scr[pl.ds(k0, KEY_STEP), :]
            lose = tie & (kk == thr) & ((rev_base - (k0 + sub)) < ans2)
            key_scr[pl.ds(k0, KEY_STEP), :] = jnp.where(lose, INT_MIN, kk)
            return carry

        lax.fori_loop(0, n_steps, demote, 0)

    thr_sel = jnp.maximum(thr, INT_MIN + 1)

    def logits(i):
        krows = pl.ds(pl.multiple_of(i * KEY_STEP, KEY_STEP), KEY_STEP)
        sel = key_scr[krows, :] >= thr_sel
        kv = kvb_scr[krows, :]
        tiles = []
        for d in range(N_HEADS // hpd):
            s2 = jnp.dot(kv, q_rhs[d], preferred_element_type=F32)
            tiles += [jnp.where(sel, s2[:, i2 * qb:(i2 + 1) * qb], NEG) for i2 in range(hpd)]
        return tiles

    def value_t(h, i):
        return vt_scr[:, pl.ds(pl.multiple_of(i * KEY_STEP, KEY_STEP), KEY_STEP)]

    no_shift = [jnp.zeros((1, qb), F32)] * N_HEADS
    o_t = _attend_two_pass(n_steps, logits, value_t, no_shift, s_scr, acc_scr, qb, hd)
    y_ref[...] = (o_t.T * _silu(cg_ref[...])).astype(y_ref.dtype)


def _dsa(p3, cols, ckv_arr, ckv_col, mk_arr, mk_col, k_top, key_lo, key_hi, chunk_causal, q_lo, q_hi,
         emit_caches):
    bsz, lq, _ = p3.shape
    lk = ckv_arr.shape[1]
    w = GROUP_W
    hd = w // N_HEADS
    qb = _query_block(lq)
    cq_col, cg_col, ciq_col, mq_col = cols
    key_rows = pl.cdiv(lk, KEY_STEP) * KEY_STEP
    out_specs = [pl.BlockSpec((None, qb, w), lambda b, j: (b, j, 0))]
    out_shape = [jax.ShapeDtypeStruct((bsz, lq, w), Y_DTYPE)]
    if emit_caches:
        out_specs += [pl.BlockSpec((None, lk - key_lo, hd), lambda b, j: (b, 0, 0))] * 2
        out_shape += [jax.ShapeDtypeStruct((bsz, lk - key_lo, hd), F32)] * 2
    return pl.pallas_call(
        functools.partial(_dsa_kernel, k_top=k_top, key_lo=key_lo, key_hi=key_hi,
                          chunk_causal=chunk_causal, q_lo=q_lo, q_hi=q_hi, emit_caches=emit_caches),
        grid=(bsz, pl.cdiv(lq, qb)),
        in_specs=[pl.BlockSpec((None, qb, w), lambda b, j: (b, j, cq_col)),
                  pl.BlockSpec((None, qb, w), lambda b, j: (b, j, cg_col)),
                  pl.BlockSpec((None, qb, w), lambda b, j: (b, j, ciq_col)),
                  pl.BlockSpec((None, qb, LANES), lambda b, j: (b, j, mq_col)),
                  pl.BlockSpec((None, lk, LANES), lambda b, j: (b, 0, ckv_col)),
                  pl.BlockSpec((None, lk, LANES), lambda b, j: (b, 0, mk_col))],
        out_specs=out_specs,
        out_shape=out_shape,
        scratch_shapes=[pltpu.VMEM((key_rows, qb), I32),
                        pltpu.VMEM((key_rows, LANES), BF16),
                        pltpu.VMEM((w // N_HEADS + ONES_ROWS, key_rows), BF16),
                        pltpu.VMEM((key_rows, LANES), BF16),
                        pltpu.VMEM((N_HEADS * key_rows // KEY_STEP, KEY_STEP, qb), F32),
                        pltpu.VMEM((w + N_HEADS * ONES_ROWS, qb), F32)],
        compiler_params=_cparams("parallel", "arbitrary"),
        name="dsa",
    )(p3, p3, p3, p3, ckv_arr, mk_arr)


def _merge_kernel(ya_ref, yb_ref, yc_ref, yd_ref, x_ref, w_ref, g_ref, o_ref, *, period, valid_lo, valid_hi):
    tm = x_ref.shape[0]
    gw = ya_ref.shape[1]
    acc = jnp.zeros(o_ref.shape, F32)
    for i, y_ref in enumerate((ya_ref, yb_ref, yc_ref, yd_ref)):
        acc = acc + jnp.dot(y_ref[...].astype(BF16), w_ref[i * gw:(i + 1) * gw, :],
                            preferred_element_type=F32)
    ms = jnp.mean(acc * acc, axis=-1, keepdims=True)
    out = x_ref[...] + (acc * lax.rsqrt(ms + EPS)) * g_ref[...]
    r0 = pl.program_id(0) * tm
    local = (r0 - (r0 // period) * period) + lax.broadcasted_iota(I32, (tm, 1), 0)
    local = jnp.where(local >= period, local - period, local)
    valid = (local >= valid_lo) & (local < valid_hi)
    o_ref[...] = jnp.where(valid, out, 0.0)


def _merge(ys, x2d, w_bf16, g, tm, period, valid_lo, valid_hi):
    rows, d = x2d.shape
    assert tm <= period
    gw = ys[0].shape[1]
    yspec = pl.BlockSpec((tm, gw), lambda i: (i, 0))
    return pl.pallas_call(
        functools.partial(_merge_kernel, period=period, valid_lo=valid_lo, valid_hi=valid_hi),
        grid=(rows // tm,),
        in_specs=[yspec, yspec, yspec, yspec,
                  pl.BlockSpec((tm, d), lambda i: (i, 0)),
                  pl.BlockSpec(w_bf16.shape, lambda i: (0, 0)),
                  pl.BlockSpec((1, d), lambda i: (0, 0))],
        out_specs=pl.BlockSpec((tm, d), lambda i: (i, 0)),
        out_shape=jax.ShapeDtypeStruct((rows, d), F32),
        compiler_params=_cparams("parallel"),
        name="merge",
    )(*ys, x2d, w_bf16, g)


def _row_tile(rows, at_most=512):
    return next(t for t in (512, 256, ROW_BLOCK) if rows % t == 0 and t <= at_most)


def _column_layout(gw):
    sizes = [gw] * 4 + [gw, gw, gw, FOX_HEADS, gw] + [gw, gw // 4, gw // 4, gw, H_IDX * D_IDX, D_IDX, H_IDX] + [gw] * 4
    names = ["aq", "af", "ai", "ag", "bq", "bk", "bv", "bf", "bg",
             "cq", "ck", "cv", "cg", "ciq", "cik", "ciw", "dq", "dk", "dv", "dg"]
    start = dict(zip(names, np.cumsum([0] + sizes[:-1])))
    size = dict(zip(names, sizes))
    src = -np.ones((16 * gw,), np.int64)
    def put(dst, name, off=0):
        src[dst + off:dst + off + size[name]] = np.arange(start[name], start[name] + size[name])
    for i, n in enumerate(["aq", "af", "ai", "ag", "bq", "bk", "bv", "bg", "cq", "cg", "ciq"]):
        put(i * gw, n)
    ckv0 = 11 * gw
    put(ckv0, "ck")
    put(ckv0 + gw // 4, "cv")
    misc0 = ckv0 + LANES
    put(misc0, "cik", IK0)
    put(misc0, "ciw", IW0)
    put(misc0, "bf", BF0)
    for i, n in enumerate(["dq", "dk", "dv", "dg"]):
        put(12 * gw + i * gw, n)
    return src


def _relayout_w_in(w_in_l, src):
    cols = jnp.take(w_in_l, jnp.asarray(np.maximum(src, 0)), axis=1)
    return jnp.where(jnp.asarray(src >= 0)[None, :], cols, 0.0).astype(BF16)


def _ret_tables(pos, chunk, gw):
    hd = gw // N_HEADS
    half = hd // 2
    inv = ROPE_BASE ** (-jnp.arange(half, dtype=F32) / half)
    ang = pos.astype(F32)[:, None] * inv[None, :]
    cos_h = jnp.concatenate([jnp.cos(ang), jnp.cos(ang)], axis=-1)
    sin_h = jnp.concatenate([-jnp.sin(ang), jnp.sin(ang)], axis=-1)
    cos = jnp.tile(cos_h, (1, N_HEADS))
    sin = jnp.tile(sin_h, (1, N_HEADS))
    lg = jnp.log(1.0 - 2.0 ** (-5.0 - jnp.arange(N_HEADS, dtype=F32)))
    lg_l = jnp.repeat(lg, hd)[None, :]
    t = jnp.arange(chunk, dtype=F32)[:, None]
    eb = jnp.exp((t + 1.0) * lg_l)
    ke2s = jnp.exp((chunk - 1.0 - t) * lg_l)
    dend = jnp.exp(chunk * lg_l)
    dt = jnp.arange(chunk, dtype=F32)[:, None] - jnp.arange(chunk, dtype=F32)[None, :]
    gam = jnp.concatenate([jnp.where(dt >= 0, jnp.exp(dt * lg[h]), 0.0) for h in range(N_HEADS)], axis=0)
    return cos, sin, eb, ke2s, dend, gam


def _state_to_bd(state):
    bsz, h, k, v = state.shape
    eye = jnp.eye(h, dtype=state.dtype)
    st = jnp.einsum('bhkv,hg->bhvgk', state, eye)
    return st.reshape(bsz, h * v, h * k)


def _bd_to_state(st, h):
    bsz, hv, hk = st.shape
    st5 = st.reshape(bsz, h, hv // h, h, hk // h)
    diag = jnp.stack([st5[:, i, :, i, :] for i in range(h)], axis=1)
    return jnp.swapaxes(diag, 2, 3)


def kernel(x_prompt, x_sample, state_hgrn, cache_fox_k, cache_fox_v, cache_fox_logf, cache_dsa_k,
           cache_dsa_v, cache_dsa_idx_k, state_ret, meta_tokens, w_in, w_out, fox_bias, hgrn_lb,
           norm_pre, norm_post):
    bsz, seq, d = x_prompt.shape
    dbsz, t_new, _ = x_sample.shape
    depth = w_in.shape[0]
    past = cache_fox_k.shape[2]
    gw = d // N_GROUPS
    hd = gw // N_HEADS
    assert gw == 2 * LANES and seq % ROW_BLOCK == 0 and past % ROW_BLOCK == 0 and t_new <= ROW_BLOCK
    assert t_new % SUBLANES == 0

    pad_front = ROW_BLOCK - N_META
    lp = ROW_BLOCK + seq
    ls = ROW_BLOCK
    lks = past + ROW_BLOCK
    n_chunks_p = lp // CHUNK
    k_top_p = min(TOP_K_MAX, seq // 4)
    k_top_s = min(TOP_K_MAX, (past + t_new) // 4)

    src = _column_layout(gw)
    n_cols = src.shape[0]
    col = {"a": 0, "bq": 4, "bk": 5, "bv": 6, "bg": 7, "cq": 8, "cg": 9, "ciq": 10, "d": 3}
    ckv_col = (11 * gw) // LANES
    misc_col = ckv_col + 1

    sm = jax.nn.softmax(hgrn_lb.astype(F32), axis=0)
    lbs = jnp.cumsum(sm, axis=0) - sm[0:1]

    xp = jnp.concatenate([jnp.zeros((bsz, pad_front, d), F32),
                          jnp.broadcast_to(meta_tokens.astype(F32)[None], (bsz, N_META, d)),
                          x_prompt], axis=1)
    xs = jnp.concatenate([x_sample, jnp.zeros((dbsz, ls - t_new, d), F32)], axis=1)

    tab_p = _ret_tables(jnp.arange(lp) - ROW_BLOCK, CHUNK, gw)
    tab_s = _ret_tables(past + jnp.arange(ls), t_new, gw)
    zero_state_p = jnp.zeros((bsz, gw, gw), F32)

    outs_p = {k: [] for k in ("hgrn", "fk", "fv", "fl", "ck", "cv", "ci", "ret")}
    outs_s = {k: [] for k in ("hgrn", "fk", "fv", "fl", "ck", "cv", "ci", "ret")}

    for l in range(depth):
        w_l = _relayout_w_in(w_in[l], src)
        w_o = w_out[l].astype(BF16)
        g_pre = norm_pre[l][None, :]
        g_post = norm_post[l][None, :]
        lb = lbs[l][None, :]
        bias = jnp.zeros((1, LANES), F32).at[0, BF0:BF0 + FOX_HEADS].set(fox_bias[l].astype(F32))

        p = _project(xp.reshape(bsz * lp, d), g_pre, w_l, _row_tile(bsz * lp)).reshape(bsz, lp, n_cols)
        ya, st_a = _hgrn(p, 0, lb, zero_state_p, CHUNK, n_chunks_p)
        yd, st_d = _ret(p, 3, tab_p, zero_state_p, CHUNK, n_chunks_p)
        lf, cc, cr = _foxprep(p, misc_col, bias, 0)
        yb, fox_k_rows, fox_v_rows = _fox(p, col["bq"], p, col["bk"], p, col["bv"], p, col["bg"], cc, cr,
                                          pad_front, 0, True, True)
        yc, dsa_k_rows, dsa_v_rows = _dsa(p, (col["cq"], col["cg"], col["ciq"], misc_col), p, ckv_col,
                                          p, misc_col, k_top_p, pad_front, 0, True, pad_front, lp, True)
        flat = lambda a: a.reshape(bsz * lp, gw)
        xp = _merge([flat(ya), flat(yb), flat(yc), flat(yd)], xp.reshape(bsz * lp, d), w_o, g_post,
                    _row_tile(bsz * lp, lp), lp, pad_front, lp).reshape(bsz, lp, d)
        pv = p[:, pad_front:, :]
        outs_p["hgrn"].append(_bd_to_state(st_a, N_HEADS))
        outs_p["ret"].append(_bd_to_state(st_d, N_HEADS))
        outs_p["fk"].append(fox_k_rows.reshape(bsz, -1, N_HEADS, hd))
        outs_p["fv"].append(fox_v_rows.reshape(bsz, -1, N_HEADS, hd))
        outs_p["fl"].append(lf[:, pad_front:, BF0:BF0 + FOX_HEADS])
        outs_p["ck"].append(dsa_k_rows)
        outs_p["cv"].append(dsa_v_rows)
        outs_p["ci"].append(pv[:, :, misc_col * LANES + IK0:misc_col * LANES + IK0 + D_IDX])

        ps = _project(xs.reshape(dbsz * ls, d), g_pre, w_l, _row_tile(dbsz * ls)).reshape(dbsz, ls, n_cols)
        ya, st_a = _hgrn(ps, 0, lb, _state_to_bd(state_hgrn[l].astype(F32)), t_new, 1)
        yd, st_d = _ret(ps, 3, tab_s, _state_to_bd(state_ret[l].astype(F32)), t_new, 1)
        z = jnp.concatenate(
            [jnp.pad(cache_fox_logf[l].astype(F32), ((0, 0), (0, 0), (BF0, LANES - BF0 - FOX_HEADS))),
             ps[:, :, misc_col * LANES:(misc_col + 1) * LANES]], axis=1)
        lf, cc, cr = _foxprep(z, 0, bias, past)
        k_all = jnp.concatenate([cache_fox_k[l].reshape(dbsz, past, gw), ps[:, :, 5 * gw:6 * gw]], axis=1)
        v_all = jnp.concatenate([cache_fox_v[l].reshape(dbsz, past, gw), ps[:, :, 6 * gw:7 * gw]], axis=1)
        (yb,) = _fox(ps, col["bq"], k_all, 0, v_all, 0, ps, col["bg"], cc, cr, 0, past, False, False)
        ckv_all = jnp.concatenate(
            [jnp.concatenate([cache_dsa_k[l], cache_dsa_v[l]], axis=-1).astype(F32),
             ps[:, :, 11 * gw:11 * gw + LANES]], axis=1)
        mk_all = jnp.concatenate(
            [jnp.pad(cache_dsa_idx_k[l].astype(F32), ((0, 0), (0, 0), (IK0, LANES - IK0 - D_IDX))),
             ps[:, :, misc_col * LANES:(misc_col + 1) * LANES]], axis=1)
        (yc,) = _dsa(ps, (col["cq"], col["cg"], col["ciq"], misc_col), ckv_all, 0, mk_all, 0,
                     k_top_s, 0, past + t_new, False, 0, t_new, False)
        flat = lambda a: a.reshape(dbsz * ls, gw)
        xs = _merge([flat(ya), flat(yb), flat(yc), flat(yd)], xs.reshape(dbsz * ls, d), w_o, g_post,
                    ls, ls, 0, t_new).reshape(dbsz, ls, d)
        pn = ps[:, :t_new, :]
        outs_s["hgrn"].append(_bd_to_state(st_a, N_HEADS))
        outs_s["ret"].append(_bd_to_state(st_d, N_HEADS))
        outs_s["fk"].append(pn[:, :, 5 * gw:6 * gw].reshape(dbsz, -1, N_HEADS, hd))
        outs_s["fv"].append(pn[:, :, 6 * gw:7 * gw].reshape(dbsz, -1, N_HEADS, hd))
        outs_s["fl"].append(lf[:, past:past + t_new, BF0:BF0 + FOX_HEADS])
        outs_s["ck"].append(pn[:, :, 11 * gw:11 * gw + hd])
        outs_s["cv"].append(pn[:, :, 11 * gw + hd:11 * gw + 2 * hd])
        outs_s["ci"].append(pn[:, :, misc_col * LANES + IK0:misc_col * LANES + IK0 + D_IDX])

    dt = x_prompt.dtype
    st = lambda xs_list: jnp.stack(xs_list, axis=0).astype(dt)
    order = ("hgrn", "fk", "fv", "fl", "ck", "cv", "ci", "ret")
    return ((xp[:, ROW_BLOCK:, :].astype(dt), xs[:, :t_new, :].astype(dt))
            + tuple(st(outs_p[k]) for k in order) + tuple(st(outs_s[k]) for k in order))
```

```python
import functools

import numpy as np
import jax
import jax.numpy as jnp
from jax import lax
from jax.experimental import pallas as pl
from jax.experimental.pallas import tpu as pltpu

F32 = jnp.float32
BF16 = jnp.bfloat16
Y_DTYPE = BF16
I32 = jnp.int32
LOG2E = float(np.log2(np.e))
ONES_ROWS = 16

N_META = 16
CHUNK = 64
N_GROUPS = 4
N_HEADS = 4
H_IDX = 8
D_IDX = 32
TOP_K_MAX = 256
ROPE_BASE = 10000.0
EPS = 1e-6
FOX_HEADS = 4

LANES = 128
SUBLANES = 8
ROW_BLOCK = 128
GROUP_W = 2 * LANES
VMEM_LIMIT_BYTES = 56 * 1024 * 1024

IK0 = 0
IW0 = 32
BF0 = 64

NEG = -1e30
M_FLOOR = -1e20
INT_MIN = np.int32(-2 ** 31)
STEP_BLOCKS = 4
KEY_STEP = STEP_BLOCKS * ROW_BLOCK
GLA_BATCH_ROWS = 4


def _cparams(*sem):
    return pltpu.CompilerParams(dimension_semantics=sem, vmem_limit_bytes=VMEM_LIMIT_BYTES)


def _split3(x):
    h = x.astype(BF16)
    r = x - h.astype(F32)
    m = r.astype(BF16)
    lo = (r - m.astype(F32)).astype(BF16)
    return h, m, lo


def _dot_exact_lhs(a_bf16, x):
    d = lambda y: jnp.dot(a_bf16, y, preferred_element_type=F32)
    h, m, lo = _split3(x)
    return d(h) + d(m) + d(lo)


def _dot_nt(a, b):
    return lax.dot_general(a, b, (((1,), (1,)), ((), ())), preferred_element_type=F32)


def _dot_tn(a, b):
    return lax.dot_general(a, b, (((0,), (0,)), ((), ())), preferred_element_type=F32)


def _log2(n):
    assert n > 0 and n & (n - 1) == 0, n
    return n.bit_length() - 1


def _head_masks(width, n_heads):
    lane = lax.broadcasted_iota(I32, (1, width), 1)
    sh = _log2(width // n_heads)
    return [(lax.shift_right_logical(lane, sh) == h).astype(F32) for h in range(n_heads)]


def _block_diag(width, n_heads, value):
    r = lax.broadcasted_iota(I32, (width, width), 0)
    c = lax.broadcasted_iota(I32, (width, width), 1)
    sh = _log2(width // n_heads)
    same = lax.shift_right_logical(r, sh) == lax.shift_right_logical(c, sh)
    return jnp.where(same, value, 0.0).astype(F32)


def _silu(x):
    return x * jax.nn.sigmoid(x)


def _proj_kernel(x_ref, g_ref, w_ref, o_ref, *, col_chunk):
    x = x_ref[...]
    ms = jnp.mean(x * x, axis=-1, keepdims=True)
    xn = ((x * lax.rsqrt(ms + EPS)) * g_ref[...]).astype(BF16)
    for c in range(o_ref.shape[1] // col_chunk):
        cols = slice(c * col_chunk, (c + 1) * col_chunk)
        o_ref[:, cols] = jnp.dot(xn, w_ref[:, cols], preferred_element_type=F32)


def _project(x2d, g, w_bf16, tm):
    rows, d = x2d.shape
    n = w_bf16.shape[1]
    return pl.pallas_call(
        functools.partial(_proj_kernel, col_chunk=1024),
        grid=(rows // tm,),
        in_specs=[pl.BlockSpec((tm, d), lambda i: (i, 0)),
                  pl.BlockSpec((1, d), lambda i: (0, 0)),
                  pl.BlockSpec((d, n), lambda i: (0, 0))],
        out_specs=pl.BlockSpec((tm, n), lambda i: (i, 0)),
        out_shape=jax.ShapeDtypeStruct((rows, n), F32),
        compiler_params=_cparams("parallel"),
        name="proj",
    )(x2d, g, w_bf16)


def _head_scores(qa, ka, hm):
    q_stack = jnp.concatenate([qa * hm[h] for h in range(N_HEADS)], axis=0).astype(BF16)
    return _dot_nt(q_stack, ka.astype(BF16))


def _hier_constants(chunk):
    halves = [chunk >> (i + 1) for i in range(_log2(chunk))]
    r = lax.broadcasted_iota(I32, (chunk, chunk), 0)
    c = lax.broadcasted_iota(I32, (chunk, chunk), 1)
    ts = lax.broadcasted_iota(I32, (N_HEADS * chunk, chunk), 0) & (chunk - 1)
    ss = lax.broadcasted_iota(I32, (N_HEADS * chunk, chunk), 1)
    sels, masks = [], []
    for h in halves:
        sh = _log2(h)
        grp = lambda x: lax.shift_right_logical(x, sh + 1)
        if h < SUBLANES:
            sels.append((c == lax.shift_left(grp(r), sh + 1) + (h - 1)).astype(BF16))
        upper_t = (lax.shift_right_logical(ts, sh) & 1).astype(F32)
        lower_s = 1.0 - (lax.shift_right_logical(ss, sh) & 1).astype(F32)
        masks.append(jnp.where(grp(ts) == grp(ss), upper_t * lower_s, 0.0))
    return halves, jnp.concatenate(sels, axis=0), masks


def _decayed_scores(q, k, b, hier, hm):
    halves, sel_small, masks = hier
    chunk, w = q.shape
    small_refs = _dot_exact_lhs(sel_small, b)
    att, n_small = None, 0
    for h, mask in zip(halves, masks):
        if h < SUBLANES:
            ref = small_refs[n_small * chunk:(n_small + 1) * chunk, :]
            n_small += 1
        else:
            ref = jnp.concatenate([jnp.broadcast_to(b[g0 + h - 1:g0 + h, :], (2 * h, w))
                                   for g0 in range(0, chunk, 2 * h)], axis=0)
        qa = q * jnp.exp(jnp.minimum(b - ref, 0.0))
        ka = k * jnp.exp(jnp.minimum(ref - b, 0.0))
        term = _head_scores(qa, ka, hm) * mask
        att = term if att is None else att + term
    return att


def _gla_chunk(att, qe, ke2, v, decay_end, st, hm, bd):
    c = qe.shape[0]
    o_stack = jnp.dot(att.astype(BF16), v.astype(BF16), preferred_element_type=F32)
    o_intra = o_stack[0:c] * hm[0]
    for h in range(1, N_HEADS):
        o_intra = o_intra + o_stack[h * c:(h + 1) * c] * hm[h]
    o_inter = _dot_nt(qe.astype(BF16), st.astype(BF16))
    st_new = st * decay_end + _dot_tn(v.astype(BF16), ke2.astype(BF16)) * bd
    return o_inter + o_intra, st_new


def _head_rms_gate(o, gate, bd_mean_bf16):
    h, m, _ = _split3(o * o)
    ms = (jnp.dot(h, bd_mean_bf16, preferred_element_type=F32)
          + jnp.dot(m, bd_mean_bf16, preferred_element_type=F32))
    return (o * lax.rsqrt(ms + EPS)) * _silu(gate)


def _gla_prologue(st0_ref, y_ref, st_ref, chunk, n_chunks):
    @pl.when(pl.program_id(1) == 0)
    def _():
        st_ref[...] = st0_ref[...]
    tail = n_chunks * chunk
    if tail < y_ref.shape[1]:
        y_ref[:, tail:, :] = jnp.zeros((y_ref.shape[0], y_ref.shape[1] - tail, y_ref.shape[2]),
                                       y_ref.dtype)


def _hgrn_kernel(a_ref, lb_ref, st0_ref, y_ref, st_ref, *, chunk, n_chunks):
    n_b, _, w = y_ref.shape
    hm = _head_masks(w, N_HEADS)
    bd = _block_diag(w, N_HEADS, 1.0)
    bd_mean = _block_diag(w, N_HEADS, 1.0 / (w // N_HEADS)).astype(BF16)
    r = lax.broadcasted_iota(I32, (chunk, chunk), 0)
    s = lax.broadcasted_iota(I32, (chunk, chunk), 1)
    tri = (r >= s).astype(BF16)
    hier = _hier_constants(chunk)
    bd_ones = bd.astype(BF16)
    lb = lb_ref[...]
    _gla_prologue(st0_ref, y_ref, st_ref, chunk, n_chunks)

    def body(c, carry):
        rows = pl.ds(pl.multiple_of(c * chunk, chunk), chunk)
        for g in range(n_b):
            q = a_ref[g, rows, 0:w]
            f = lb + (1.0 - lb) * jax.nn.sigmoid(a_ref[g, rows, w:2 * w])
            k = 1.0 - f
            v = a_ref[g, rows, 2 * w:3 * w]
            gate = a_ref[g, rows, 3 * w:4 * w]
            b = _dot_exact_lhs(tri, jnp.log(f))
            b_end = b[chunk - 1:chunk, :]
            att = _decayed_scores(q, k, b, hier, hm)
            o, st_new = _gla_chunk(att, q * jnp.exp(b), k * jnp.exp(b_end - b), v, jnp.exp(b_end),
                                   st_ref[g], hm, bd)
            o = o + jnp.dot((q * k).astype(BF16), bd_ones, preferred_element_type=F32) * v
            st_ref[g] = st_new
            y_ref[g, rows, :] = _head_rms_gate(o, gate, bd_mean).astype(y_ref.dtype)
        return carry

    lax.fori_loop(0, n_chunks, body, 0)


def _ret_kernel(d_ref, cos_ref, sin_ref, eb_ref, ke2s_ref, dend_ref, gam_ref, st0_ref, y_ref, st_ref,
                *, chunk, n_chunks):
    n_b, _, w = y_ref.shape
    hd = w // N_HEADS
    hm = _head_masks(w, N_HEADS)
    bd = _block_diag(w, N_HEADS, 1.0)
    bd_mean = _block_diag(w, N_HEADS, 1.0 / hd).astype(BF16)
    lane = lax.broadcasted_iota(I32, (1, w), 1)
    first_half = (lane & (hd - 1)) < (hd // 2)
    eb = eb_ref[...]
    ke2s = ke2s_ref[...]
    dend = dend_ref[...]
    gam = gam_ref[...]
    _gla_prologue(st0_ref, y_ref, st_ref, chunk, n_chunks)

    def rope(x, cos, sin_signed):
        swapped = jnp.where(first_half, pltpu.roll(x, w - hd // 2, 1), pltpu.roll(x, hd // 2, 1))
        return x * cos + swapped * sin_signed

    def body(c, carry):
        rows = pl.ds(pl.multiple_of(c * chunk, chunk), chunk)
        cos = cos_ref[rows, :]
        sin = sin_ref[rows, :]
        for g in range(n_b):
            q = rope(d_ref[g, rows, 0:w], cos, sin)
            k = rope(d_ref[g, rows, w:2 * w], cos, sin) * (hd ** -0.5)
            v = d_ref[g, rows, 2 * w:3 * w]
            gate = d_ref[g, rows, 3 * w:4 * w]
            att = _head_scores(q, k, hm) * gam
            o, st_new = _gla_chunk(att, q * eb, k * ke2s, v, dend, st_ref[g], hm, bd)
            st_ref[g] = st_new
            y_ref[g, rows, :] = _head_rms_gate(o, gate, bd_mean).astype(y_ref.dtype)
        return carry

    lax.fori_loop(0, n_chunks, body, 0)


def _gla_grid(bsz, lp, chunk, n_chunks):
    n_b = next(n for n in (GLA_BATCH_ROWS, 2, 1) if bsz % n == 0)
    halves = n_chunks * chunk == lp and n_chunks % 2 == 0 and (lp // 2) % SUBLANES == 0
    n_seq = 2 if halves else 1
    return n_b, n_seq, n_chunks // n_seq


def _hgrn(p3, col_block, lb, st0, chunk, n_chunks):
    bsz, lp, _ = p3.shape
    w = lb.shape[1]
    n_b, n_seq, n_chunks_blk = _gla_grid(bsz, lp, chunk, n_chunks)
    rows = lp // n_seq
    return pl.pallas_call(
        functools.partial(_hgrn_kernel, chunk=chunk, n_chunks=n_chunks_blk),
        grid=(bsz // n_b, n_seq),
        in_specs=[pl.BlockSpec((n_b, rows, 4 * w), lambda b, s: (b, s, col_block)),
                  pl.BlockSpec((1, w), lambda b, s: (0, 0)),
                  pl.BlockSpec((n_b, w, w), lambda b, s: (b, 0, 0))],
        out_specs=[pl.BlockSpec((n_b, rows, w), lambda b, s: (b, s, 0)),
                   pl.BlockSpec((n_b, w, w), lambda b, s: (b, 0, 0))],
        out_shape=[jax.ShapeDtypeStruct((bsz, lp, w), Y_DTYPE),
                   jax.ShapeDtypeStruct((bsz, w, w), F32)],
        compiler_params=_cparams("parallel", "arbitrary"),
        name="hgrn",
    )(p3, lb, st0)


def _ret(p3, col_block, tables, st0, chunk, n_chunks):
    bsz, lp, _ = p3.shape
    w = st0.shape[1]
    n_b, n_seq, n_chunks_blk = _gla_grid(bsz, lp, chunk, n_chunks)
    rows = lp // n_seq
    cos, sin, eb, ke2s, dend, gam = tables
    full = lambda a: pl.BlockSpec(a.shape, lambda b, s: (0,) * a.ndim)
    per_seq = pl.BlockSpec((rows, w), lambda b, s: (s, 0))
    return pl.pallas_call(
        functools.partial(_ret_kernel, chunk=chunk, n_chunks=n_chunks_blk),
        grid=(bsz // n_b, n_seq),
        in_specs=[pl.BlockSpec((n_b, rows, 4 * w), lambda b, s: (b, s, col_block)),
                  per_seq, per_seq, full(eb), full(ke2s), full(dend), full(gam),
                  pl.BlockSpec((n_b, w, w), lambda b, s: (b, 0, 0))],
        out_specs=[pl.BlockSpec((n_b, rows, w), lambda b, s: (b, s, 0)),
                   pl.BlockSpec((n_b, w, w), lambda b, s: (b, 0, 0))],
        out_shape=[jax.ShapeDtypeStruct((bsz, lp, w), Y_DTYPE),
                   jax.ShapeDtypeStruct((bsz, w, w), F32)],
        compiler_params=_cparams("parallel", "arbitrary"),
        name="ret",
    )(p3, cos, sin, eb, ke2s, dend, gam, st0)


def _log_sigmoid(x):
    return -(jnp.maximum(-x, 0.0) + jnp.log(1.0 + jnp.exp(-jnp.abs(x))))


def _foxprep_kernel(z_ref, bias_ref, lf_ref, cc_ref, cr_ref, *, n_pass):
    n_b = z_ref.shape[0]
    nblk = z_ref.shape[1] // ROW_BLOCK
    r = lax.broadcasted_iota(I32, (ROW_BLOCK, ROW_BLOCK), 0)
    s = lax.broadcasted_iota(I32, (ROW_BLOCK, ROW_BLOCK), 1)
    tri = (r >= s).astype(BF16)
    bias = bias_ref[...]

    def body(i, carries):
        rows = pl.ds(pl.multiple_of(i * ROW_BLOCK, ROW_BLOCK), ROW_BLOCK)
        rowi = i * ROW_BLOCK + lax.broadcasted_iota(I32, (ROW_BLOCK, 1), 0)
        out = []
        for g in range(n_b):
            z = z_ref[g, rows, :]
            lf = jnp.where(rowi < n_pass, z, _log_sigmoid(z + bias))
            lf_ref[g, rows, :] = lf
            cs = _dot_exact_lhs(tri, lf) + carries[g]
            cc_ref[g, rows, :] = cs
            cr_ref[g, :, rows] = cs.T[BF0:BF0 + SUBLANES, :]
            out.append(cs[ROW_BLOCK - 1:ROW_BLOCK, :])
        return tuple(out)

    lax.fori_loop(0, nblk, body, tuple(jnp.zeros((1, LANES), F32) for _ in range(n_b)))
    pad = cr_ref.shape[2] - nblk * ROW_BLOCK
    if pad:
        cr_ref[:, :, nblk * ROW_BLOCK:] = jnp.zeros((n_b, SUBLANES, pad), F32)


def _foxprep(z3, z_col, bias, n_pass):
    bsz, lk, _ = z3.shape
    cr_cols = pl.cdiv(lk, KEY_STEP) * KEY_STEP
    n_b = next(n for n in (GLA_BATCH_ROWS, 2, 1) if bsz % n == 0)
    return pl.pallas_call(
        functools.partial(_foxprep_kernel, n_pass=n_pass),
        grid=(bsz // n_b,),
        in_specs=[pl.BlockSpec((n_b, lk, LANES), lambda b: (b, 0, z_col)),
                  pl.BlockSpec((1, LANES), lambda b: (0, 0))],
        out_specs=[pl.BlockSpec((n_b, lk, LANES), lambda b: (b, 0, 0)),
                   pl.BlockSpec((n_b, lk, LANES), lambda b: (b, 0, 0)),
                   pl.BlockSpec((n_b, SUBLANES, cr_cols), lambda b: (b, 0, 0))],
        out_shape=[jax.ShapeDtypeStruct((bsz, lk, LANES), F32),
                   jax.ShapeDtypeStruct((bsz, lk, LANES), F32),
                   jax.ShapeDtypeStruct((bsz, SUBLANES, cr_cols), F32)],
        compiler_params=_cparams("parallel"),
        name="foxprep",
    )(z3, bias)


MXU_COLS = 2 * LANES


def _query_block(lq):
    return MXU_COLS if lq > ROW_BLOCK else ROW_BLOCK


def _heads_per_dot(qb):
    assert MXU_COLS % qb == 0
    return MXU_COLS // qb


def _head_weights(x_t, rows_per_head, n_heads):
    hpd = _heads_per_dot(x_t.shape[1])
    rowh = lax.shift_right_logical(lax.broadcasted_iota(I32, (x_t.shape[0], 1), 0),
                                   _log2(rows_per_head))
    only = lambda h: jnp.where(rowh == h, x_t, 0.0)
    return [jnp.concatenate([only(d * hpd + i) for i in range(hpd)], axis=1).astype(BF16)
            for d in range(n_heads // hpd)]


def _attend_two_pass(n_steps, logits_fn, value_t_fn, shifts, s_scr, acc_scr, qb, hd):
    fold = lambda x, op: op(x.reshape(KEY_STEP // SUBLANES, SUBLANES, qb), axis=0)
    hv = hd + ONES_ROWS

    def max_step(i, ms):
        tiles = logits_fn(i)
        for h in range(N_HEADS):
            s_scr[i * N_HEADS + h] = tiles[h]
        return tuple(jnp.maximum(ms[h], fold(tiles[h], jnp.max)) for h in range(N_HEADS))

    ms = lax.fori_loop(0, n_steps, max_step,
                       tuple(jnp.full((SUBLANES, qb), M_FLOOR, F32) for _ in range(N_HEADS)))
    m_logit = [jnp.max(ms[h], axis=0, keepdims=True) + shifts[h] for h in range(N_HEADS)]
    acc_scr[...] = jnp.zeros(acc_scr.shape, F32)

    def sum_step(i, carry):
        for h in range(N_HEADS):
            p = jnp.exp2(s_scr[i * N_HEADS + h] - (m_logit[h] - shifts[h]))
            acc_scr[h * hv:(h + 1) * hv, :] += jnp.dot(value_t_fn(h, i), p.astype(BF16),
                                                       preferred_element_type=F32)
        return carry

    lax.fori_loop(0, n_steps, sum_step, 0)
    outs = []
    for h in range(N_HEADS):
        l = acc_scr[h * hv + hd:h * hv + hd + 1, :]
        outs.append(acc_scr[h * hv:h * hv + hd, :] / jnp.where(l > 0.0, l, 1.0))
    return jnp.concatenate(outs, axis=0)


def _fox_kernel(q_ref, k_ref, v_ref, g_ref, cc_ref, cr_ref, y_ref, *rest, key_lo, q_off, causal_blocks,
                emit_caches):
    if emit_caches:
        kc_ref, vc_ref, kbf_scr, vt_scr, ck_scr, s_scr, acc_scr = rest
    else:
        kbf_scr, vt_scr, ck_scr, s_scr, acc_scr = rest
    qb, w = q_ref.shape
    hd = w // N_HEADS
    hv = hd + ONES_ROWS
    nkb_total = k_ref.shape[0] // ROW_BLOCK
    hpd = _heads_per_dot(qb)
    j = pl.program_id(1)
    nkb = jnp.minimum((j + 1) * (qb // ROW_BLOCK), nkb_total) if causal_blocks else nkb_total

    @pl.when(j == 0)
    def _prepare_batch_row():
        def blk(i, carry):
            rows = pl.ds(pl.multiple_of(i * ROW_BLOCK, ROW_BLOCK), ROW_BLOCK)
            kbf_scr[rows, :] = k_ref[rows, :].astype(BF16)
            v_t = v_ref[rows, :].T.astype(BF16)
            cs = cc_ref[rows, :] * LOG2E
            for h in range(N_HEADS):
                vt_scr[h * hv:h * hv + hd, rows] = v_t[h * hd:(h + 1) * hd, :]
                ck_scr[h, rows, :] = jnp.broadcast_to(cs[:, BF0 + h:BF0 + h + 1], (ROW_BLOCK, LANES))
            return carry
        lax.fori_loop(0, nkb_total, blk, 0)
        if emit_caches:
            kc_ref[...] = k_ref[key_lo:, :]
            vc_ref[...] = v_ref[key_lo:, :]
        pad = vt_scr.shape[1] - nkb_total * ROW_BLOCK
        for h in range(N_HEADS):
            if pad:
                vt_scr[h * hv:h * hv + hd, nkb_total * ROW_BLOCK:] = jnp.zeros((hd, pad), BF16)
            vt_scr[h * hv + hd:(h + 1) * hv, :] = jnp.ones((ONES_ROWS, vt_scr.shape[1]), BF16)

    wq = _head_weights((q_ref[...] * (hd ** -0.5 * LOG2E)).T, hd, N_HEADS)
    qcol = pl.ds(pl.multiple_of(j * qb + q_off, ROW_BLOCK), qb)
    cq = [cr_ref[h:h + 1, qcol] * LOG2E for h in range(N_HEADS)]
    qrow = j * qb + q_off + lax.broadcasted_iota(I32, (1, qb), 1)
    sub = lax.broadcasted_iota(I32, (KEY_STEP, 1), 0)

    def logits(i):
        k0 = pl.multiple_of(i * KEY_STEP, KEY_STEP)
        krows = pl.ds(k0, KEY_STEP)
        kblk = kbf_scr[krows, :]
        kidx = k0 + sub
        ok = (kidx >= key_lo) & (kidx <= qrow)
        tiles = []
        for d in range(N_HEADS // hpd):
            s2 = jnp.dot(kblk, wq[d], preferred_element_type=F32)
            for i2 in range(hpd):
                h = d * hpd + i2
                ck = jnp.concatenate([ck_scr[h, krows, :]] * (qb // LANES), axis=1)
                tiles.append(jnp.where(ok, s2[:, i2 * qb:(i2 + 1) * qb] - ck, NEG))
        return tiles

    def value_t(h, i):
        return vt_scr[h * hv:(h + 1) * hv, pl.ds(pl.multiple_of(i * KEY_STEP, KEY_STEP), KEY_STEP)]

    n_steps = lax.shift_right_logical(nkb + (STEP_BLOCKS - 1), _log2(STEP_BLOCKS))
    o_t = _attend_two_pass(n_steps, logits, value_t, cq, s_scr, acc_scr, qb, hd)
    y_ref[...] = (o_t.T * _silu(g_ref[...])).astype(y_ref.dtype)


def _fox(q_arr, q_col, k_arr, k_col, v_arr, v_col, g_arr, g_col, cc, cr, key_lo, q_off, causal_blocks,
         emit_caches):
    bsz, lq, _ = q_arr.shape
    lk = k_arr.shape[1]
    w = GROUP_W
    qb = _query_block(lq)
    key_rows = pl.cdiv(lk, KEY_STEP) * KEY_STEP
    assert cr.shape[2] >= q_off + pl.cdiv(lq, qb) * qb
    out_specs = [pl.BlockSpec((None, qb, w), lambda b, j: (b, j, 0))]
    out_shape = [jax.ShapeDtypeStruct((bsz, lq, w), Y_DTYPE)]
    if emit_caches:
        out_specs += [pl.BlockSpec((None, lk - key_lo, w), lambda b, j: (b, 0, 0))] * 2
        out_shape += [jax.ShapeDtypeStruct((bsz, lk - key_lo, w), F32)] * 2
    return pl.pallas_call(
        functools.partial(_fox_kernel, key_lo=key_lo, q_off=q_off, causal_blocks=causal_blocks,
                          emit_caches=emit_caches),
        grid=(bsz, pl.cdiv(lq, qb)),
        in_specs=[pl.BlockSpec((None, qb, w), lambda b, j: (b, j, q_col)),
                  pl.BlockSpec((None, lk, w), lambda b, j: (b, 0, k_col)),
                  pl.BlockSpec((None, lk, w), lambda b, j: (b, 0, v_col)),
                  pl.BlockSpec((None, qb, w), lambda b, j: (b, j, g_col)),
                  pl.BlockSpec((None, lk, LANES), lambda b, j: (b, 0, 0)),
                  pl.BlockSpec((None, SUBLANES, cr.shape[2]), lambda b, j: (b, 0, 0))],
        out_specs=out_specs,
        out_shape=out_shape,
        scratch_shapes=[pltpu.VMEM((key_rows, w), BF16),
                        pltpu.VMEM((w + N_HEADS * ONES_ROWS, key_rows), BF16),
                        pltpu.VMEM((N_HEADS, key_rows, LANES), F32),
                        pltpu.VMEM((N_HEADS * key_rows // KEY_STEP, KEY_STEP, qb), F32),
                        pltpu.VMEM((w + N_HEADS * ONES_ROWS, qb), F32)],
        compiler_params=_cparams("parallel", "arbitrary"),
        name="fox",
    )(q_arr, k_arr, v_arr, g_arr, cc, cr)


def _dsa_kernel(cq_ref, cg_ref, ciq_ref, mq_ref, ckv_ref, mk_ref, y_ref, *rest,
                k_top, key_lo, key_hi, chunk_causal, q_lo, q_hi, emit_caches):
    if emit_caches:
        kc_ref, vc_ref, key_scr, kvb_scr, vt_scr, mkb_scr, s_scr, acc_scr = rest
    else:
        key_scr, kvb_scr, vt_scr, mkb_scr, s_scr, acc_scr = rest
    qb, w = cq_ref.shape
    hd = w // N_HEADS
    hpd = _heads_per_dot(qb)
    nkb_total = ckv_ref.shape[0] // ROW_BLOCK
    j = pl.program_id(1)
    nkb = jnp.minimum((j + 1) * (qb // ROW_BLOCK), nkb_total) if chunk_causal else nkb_total
    idx_scale = (H_IDX * D_IDX) ** -0.5

    @pl.when(j == 0)
    def _prepare_batch_row():
        def blk(i, carry):
            rows = pl.ds(pl.multiple_of(i * ROW_BLOCK, ROW_BLOCK), ROW_BLOCK)
            kv = ckv_ref[rows, :]
            kvb_scr[rows, :] = kv.astype(BF16)
            vt_scr[0:hd, rows] = kv.T[hd:2 * hd, :].astype(BF16)
            mkb_scr[rows, :] = mk_ref[rows, :].astype(BF16)
            return carry
        lax.fori_loop(0, nkb_total, blk, 0)
        if emit_caches:
            kv_new = ckv_ref[key_lo:, :]
            kc_ref[...] = kv_new[:, 0:hd]
            vc_ref[...] = kv_new[:, hd:2 * hd]
        pad = vt_scr.shape[1] - nkb_total * ROW_BLOCK
        if pad:
            vt_scr[0:hd, nkb_total * ROW_BLOCK:] = jnp.zeros((hd, pad), BF16)
        vt_scr[hd:, :] = jnp.ones((ONES_ROWS, vt_scr.shape[1]), BF16)

    def pad_rows(x):
        return jnp.concatenate([x, jnp.zeros((LANES - x.shape[0], qb), F32)], axis=0)

    iq_t = ciq_ref[...].T
    iw_t = mq_ref[...].T[IW0:IW0 + H_IDX, :]
    q_t = (cq_ref[...] * (hd ** -0.5 * LOG2E)).T
    def side_by_side(x_t, rows, n_heads):
        return [jnp.concatenate([pad_rows(x_t[(d * hpd + i) * rows:(d * hpd + i + 1) * rows, :])
                                 for i in range(hpd)], axis=1).astype(BF16)
                for d in range(n_heads // hpd)]

    iq_rhs = side_by_side(iq_t, D_IDX, H_IDX)
    q_rhs = side_by_side(q_t, hd, N_HEADS)

    qrow = j * qb + lax.broadcasted_iota(I32, (1, qb), 1)
    if chunk_causal:
        hi = (lax.shift_right_logical(qrow, 6) + 1) * CHUNK
    else:
        hi = jnp.full((1, qb), key_hi, I32)
    sub = lax.broadcasted_iota(I32, (KEY_STEP, 1), 0)
    n_steps = lax.shift_right_logical(nkb + (STEP_BLOCKS - 1), _log2(STEP_BLOCKS))

    def score_step(i, carry):
        k0 = pl.multiple_of(i * KEY_STEP, KEY_STEP)
        mk = mkb_scr[pl.ds(k0, KEY_STEP), :]
        acc = jnp.zeros((KEY_STEP, qb), F32)
        for d in range(H_IDX // hpd):
            sc2 = jnp.dot(mk, iq_rhs[d], preferred_element_type=F32)
            for i2 in range(hpd):
                h = d * hpd + i2
                acc = acc + jnp.maximum(sc2[:, i2 * qb:(i2 + 1) * qb], 0.0) * iw_t[h:h + 1, :]
        score = acc * idx_scale + 0.0
        kidx = k0 + sub
        adm = (kidx >= key_lo) & (kidx < hi)
        u = pltpu.bitcast(score, I32)
        key = u ^ (lax.shift_right_arithmetic(u, 31) & np.int32(0x7FFFFFFF))
        key_scr[pl.ds(k0, KEY_STEP), :] = jnp.where(adm, key, INT_MIN)
        return carry

    lax.fori_loop(0, n_steps, score_step, 0)

    def count(pred):
        groups = qb // LANES

        def cb(i, c8s):
            k0 = pl.multiple_of(i * KEY_STEP, KEY_STEP)
            out = []
            for lg in range(groups):
                lanes = lambda x, lg=lg: x[:, lg * LANES:(lg + 1) * LANES]
                kk = key_scr[pl.ds(k0, KEY_STEP), lg * LANES:(lg + 1) * LANES]
                ind = jnp.where(pred(kk, k0, lanes), 1, 0).astype(I32)
                out.append(c8s[lg] + jnp.sum(ind.reshape(KEY_STEP // SUBLANES, SUBLANES, LANES), axis=0))
            return tuple(out)
        c8s = lax.fori_loop(0, n_steps, cb,
                            tuple(jnp.zeros((SUBLANES, LANES), I32) for _ in range(groups)))
        return jnp.concatenate([jnp.sum(c, axis=0, keepdims=True) for c in c8s], axis=1)

    def bisect(n_bits, count_ge, need):
        def bit_body(i, carry):
            ans, cnt_ans = carry
            cand = ans | lax.shift_left(np.int32(1), jnp.int32(n_bits - 1) - i)
            cnt = count_ge(cand)
            ok = cnt >= need
            return jnp.where(ok, cand, ans), jnp.where(ok, cnt, cnt_ans)
        return lax.fori_loop(0, n_bits, bit_body,
                             (jnp.zeros((1, qb), I32), jnp.full((1, qb), np.int32(2 ** 30), I32)))

    ans, cnt_thr = bisect(32, lambda cand: count(
        lambda kk, k0, lanes: kk >= lanes(cand ^ INT_MIN)), k_top)
    thr = ans ^ INT_MIN
    tie = (cnt_thr > k_top) & (thr != INT_MIN) & (qrow >= q_lo) & (qrow < q_hi)

    @pl.when(jnp.max(jnp.where(tie, 1, 0)) > 0)
    def _break_ties():
        n_rev_bits = _log2(pl.next_power_of_2(key_scr.shape[0]))
        rev_base = np.int32(2 ** n_rev_bits - 1)
        need = k_top - count(lambda kk, k0, lanes: kk > lanes(thr))
        ans2, _ = bisect(n_rev_bits, lambda cand: count(
            lambda kk, k0, lanes: (kk == lanes(thr)) & ((rev_base - (k0 + sub)) >= lanes(cand))), need)

        def demote(i, carry):
            k0 = pl.multiple_of(i * KEY_STEP, KEY_STEP)
            kk = key_scr[pl.ds(k0, KEY_STEP), :]
            lose = tie & (kk == thr) & ((rev_base - (k0 + sub)) < ans2)
            key_scr[pl.ds(k0, KEY_STEP), :] = jnp.where(lose, INT_MIN, kk)
            return carry

        lax.fori_loop(0, n_steps, demote, 0)

    thr_sel = jnp.maximum(thr, INT_MIN + 1)

    def logits(i):
        krows = pl.ds(pl.multiple_of(i * KEY_STEP, KEY_STEP), KEY_STEP)
        sel = key_scr[krows, :] >= thr_sel
        kv = kvb_scr[krows, :]
        tiles = []
        for d in range(N_HEADS // hpd):
            s2 = jnp.dot(kv, q_rhs[d], preferred_element_type=F32)
            tiles += [jnp.where(sel, s2[:, i2 * qb:(i2 + 1) * qb], NEG) for i2 in range(hpd)]
        return tiles

    def value_t(h, i):
        return vt_scr[:, pl.ds(pl.multiple_of(i * KEY_STEP, KEY_STEP), KEY_STEP)]

    no_shift = [jnp.zeros((1, qb), F32)] * N_HEADS
    o_t = _attend_two_pass(n_steps, logits, value_t, no_shift, s_scr, acc_scr, qb, hd)
    y_ref[...] = (o_t.T * _silu(cg_ref[...])).astype(y_ref.dtype)


def _dsa(p3, cols, ckv_arr, ckv_col, mk_arr, mk_col, k_top, key_lo, key_hi, chunk_causal, q_lo, q_hi,
         emit_caches):
    bsz, lq, _ = p3.shape
    lk = ckv_arr.shape[1]
    w = GROUP_W
    hd = w // N_HEADS
    qb = _query_block(lq)
    cq_col, cg_col, ciq_col, mq_col = cols
    key_rows = pl.cdiv(lk, KEY_STEP) * KEY_STEP
    out_specs = [pl.BlockSpec((None, qb, w), lambda b, j: (b, j, 0))]
    out_shape = [jax.ShapeDtypeStruct((bsz, lq, w), Y_DTYPE)]
    if emit_caches:
        out_specs += [pl.BlockSpec((None, lk - key_lo, hd), lambda b, j: (b, 0, 0))] * 2
        out_shape += [jax.ShapeDtypeStruct((bsz, lk - key_lo, hd), F32)] * 2
    return pl.pallas_call(
        functools.partial(_dsa_kernel, k_top=k_top, key_lo=key_lo, key_hi=key_hi,
                          chunk_causal=chunk_causal, q_lo=q_lo, q_hi=q_hi, emit_caches=emit_caches),
        grid=(bsz, pl.cdiv(lq, qb)),
        in_specs=[pl.BlockSpec((None, qb, w), lambda b, j: (b, j, cq_col)),
                  pl.BlockSpec((None, qb, w), lambda b, j: (b, j, cg_col)),
                  pl.BlockSpec((None, qb, w), lambda b, j: (b, j, ciq_col)),
                  pl.BlockSpec((None, qb, LANES), lambda b, j: (b, j, mq_col)),
                  pl.BlockSpec((None, lk, LANES), lambda b, j: (b, 0, ckv_col)),
                  pl.BlockSpec((None, lk, LANES), lambda b, j: (b, 0, mk_col))],
        out_specs=out_specs,
        out_shape=out_shape,
        scratch_shapes=[pltpu.VMEM((key_rows, qb), I32),
                        pltpu.VMEM((key_rows, LANES), BF16),
                        pltpu.VMEM((w // N_HEADS + ONES_ROWS, key_rows), BF16),
                        pltpu.VMEM((key_rows, LANES), BF16),
                        pltpu.VMEM((N_HEADS * key_rows // KEY_STEP, KEY_STEP, qb), F32),
                        pltpu.VMEM((w + N_HEADS * ONES_ROWS, qb), F32)],
        compiler_params=_cparams("parallel", "arbitrary"),
        name="dsa",
    )(p3, p3, p3, p3, ckv_arr, mk_arr)


def _merge_kernel(ya_ref, yb_ref, yc_ref, yd_ref, x_ref, w_ref, g_ref, o_ref, *, period, valid_lo, valid_hi):
    tm = x_ref.shape[0]
    gw = ya_ref.shape[1]
    acc = jnp.zeros(o_ref.shape, F32)
    for i, y_ref in enumerate((ya_ref, yb_ref, yc_ref, yd_ref)):
        acc = acc + jnp.dot(y_ref[...].astype(BF16), w_ref[i * gw:(i + 1) * gw, :],
                            preferred_element_type=F32)
    ms = jnp.mean(acc * acc, axis=-1, keepdims=True)
    out = x_ref[...] + (acc * lax.rsqrt(ms + EPS)) * g_ref[...]
    r0 = pl.program_id(0) * tm
    local = (r0 - (r0 // period) * period) + lax.broadcasted_iota(I32, (tm, 1), 0)
    local = jnp.where(local >= period, local - period, local)
    valid = (local >= valid_lo) & (local < valid_hi)
    o_ref[...] = jnp.where(valid, out, 0.0)


def _merge(ys, x2d, w_bf16, g, tm, period, valid_lo, valid_hi):
    rows, d = x2d.shape
    assert tm <= period
    gw = ys[0].shape[1]
    yspec = pl.BlockSpec((tm, gw), lambda i: (i, 0))
    return pl.pallas_call(
        functools.partial(_merge_kernel, period=period, valid_lo=valid_lo, valid_hi=valid_hi),
        grid=(rows // tm,),
        in_specs=[yspec, yspec, yspec, yspec,
                  pl.BlockSpec((tm, d), lambda i: (i, 0)),
                  pl.BlockSpec(w_bf16.shape, lambda i: (0, 0)),
                  pl.BlockSpec((1, d), lambda i: (0, 0))],
        out_specs=pl.BlockSpec((tm, d), lambda i: (i, 0)),
        out_shape=jax.ShapeDtypeStruct((rows, d), F32),
        compiler_params=_cparams("parallel"),
        name="merge",
    )(*ys, x2d, w_bf16, g)


def _row_tile(rows, at_most=512):
    return next(t for t in (512, 256, ROW_BLOCK) if rows % t == 0 and t <= at_most)


def _column_layout(gw):
    sizes = [gw] * 4 + [gw, gw, gw, FOX_HEADS, gw] + [gw, gw // 4, gw // 4, gw, H_IDX * D_IDX, D_IDX, H_IDX] + [gw] * 4
    names = ["aq", "af", "ai", "ag", "bq", "bk", "bv", "bf", "bg",
             "cq", "ck", "cv", "cg", "ciq", "cik", "ciw", "dq", "dk", "dv", "dg"]
    start = dict(zip(names, np.cumsum([0] + sizes[:-1])))
    size = dict(zip(names, sizes))
    src = -np.ones((16 * gw,), np.int64)
    def put(dst, name, off=0):
        src[dst + off:dst + off + size[name]] = np.arange(start[name], start[name] + size[name])
    for i, n in enumerate(["aq", "af", "ai", "ag", "bq", "bk", "bv", "bg", "cq", "cg", "ciq"]):
        put(i * gw, n)
    ckv0 = 11 * gw
    put(ckv0, "ck")
    put(ckv0 + gw // 4, "cv")
    misc0 = ckv0 + LANES
    put(misc0, "cik", IK0)
    put(misc0, "ciw", IW0)
    put(misc0, "bf", BF0)
    for i, n in enumerate(["dq", "dk", "dv", "dg"]):
        put(12 * gw + i * gw, n)
    return src


def _relayout_w_in(w_in_l, src):
    cols = jnp.take(w_in_l, jnp.asarray(np.maximum(src, 0)), axis=1)
    return jnp.where(jnp.asarray(src >= 0)[None, :], cols, 0.0).astype(BF16)


def _ret_tables(pos, chunk, gw):
    hd = gw // N_HEADS
    half = hd // 2
    inv = ROPE_BASE ** (-jnp.arange(half, dtype=F32) / half)
    ang = pos.astype(F32)[:, None] * inv[None, :]
    cos_h = jnp.concatenate([jnp.cos(ang), jnp.cos(ang)], axis=-1)
    sin_h = jnp.concatenate([-jnp.sin(ang), jnp.sin(ang)], axis=-1)
    cos = jnp.tile(cos_h, (1, N_HEADS))
    sin = jnp.tile(sin_h, (1, N_HEADS))
    lg = jnp.log(1.0 - 2.0 ** (-5.0 - jnp.arange(N_HEADS, dtype=F32)))
    lg_l = jnp.repeat(lg, hd)[None, :]
    t = jnp.arange(chunk, dtype=F32)[:, None]
    eb = jnp.exp((t + 1.0) * lg_l)
    ke2s = jnp.exp((chunk - 1.0 - t) * lg_l)
    dend = jnp.exp(chunk * lg_l)
    dt = jnp.arange(chunk, dtype=F32)[:, None] - jnp.arange(chunk, dtype=F32)[None, :]
    gam = jnp.concatenate([jnp.where(dt >= 0, jnp.exp(dt * lg[h]), 0.0) for h in range(N_HEADS)], axis=0)
    return cos, sin, eb, ke2s, dend, gam


def _state_to_bd(state):
    bsz, h, k, v = state.shape
    eye = jnp.eye(h, dtype=state.dtype)
    st = jnp.einsum('bhkv,hg->bhvgk', state, eye)
    return st.reshape(bsz, h * v, h * k)


def _bd_to_state(st, h):
    bsz, hv, hk = st.shape
    st5 = st.reshape(bsz, h, hv // h, h, hk // h)
    diag = jnp.stack([st5[:, i, :, i, :] for i in range(h)], axis=1)
    return jnp.swapaxes(diag, 2, 3)


def kernel(x_prompt, x_sample, state_hgrn, cache_fox_k, cache_fox_v, cache_fox_logf, cache_dsa_k,
           cache_dsa_v, cache_dsa_idx_k, state_ret, meta_tokens, w_in, w_out, fox_bias, hgrn_lb,
           norm_pre, norm_post):
    bsz, seq, d = x_prompt.shape
    dbsz, t_new, _ = x_sample.shape
    depth = w_in.shape[0]
    past = cache_fox_k.shape[2]
    gw = d // N_GROUPS
    hd = gw // N_HEADS
    assert gw == 2 * LANES and seq % ROW_BLOCK == 0 and past % ROW_BLOCK == 0 and t_new <= ROW_BLOCK
    assert t_new % SUBLANES == 0

    pad_front = ROW_BLOCK - N_META
    lp = ROW_BLOCK + seq
    ls = ROW_BLOCK
    lks = past + ROW_BLOCK
    n_chunks_p = lp // CHUNK
    k_top_p = min(TOP_K_MAX, seq // 4)
    k_top_s = min(TOP_K_MAX, (past + t_new) // 4)

    src = _column_layout(gw)
    n_cols = src.shape[0]
    col = {"a": 0, "bq": 4, "bk": 5, "bv": 6, "bg": 7, "cq": 8, "cg": 9, "ciq": 10, "d": 3}
    ckv_col = (11 * gw) // LANES
    misc_col = ckv_col + 1

    sm = jax.nn.softmax(hgrn_lb.astype(F32), axis=0)
    lbs = jnp.cumsum(sm, axis=0) - sm[0:1]

    xp = jnp.concatenate([jnp.zeros((bsz, pad_front, d), F32),
                          jnp.broadcast_to(meta_tokens.astype(F32)[None], (bsz, N_META, d)),
                          x_prompt], axis=1)
    xs = jnp.concatenate([x_sample, jnp.zeros((dbsz, ls - t_new, d), F32)], axis=1)

    tab_p = _ret_tables(jnp.arange(lp) - ROW_BLOCK, CHUNK, gw)
    tab_s = _ret_tables(past + jnp.arange(ls), t_new, gw)
    zero_state_p = jnp.zeros((bsz, gw, gw), F32)

    outs_p = {k: [] for k in ("hgrn", "fk", "fv", "fl", "ck", "cv", "ci", "ret")}
    outs_s = {k: [] for k in ("hgrn", "fk", "fv", "fl", "ck", "cv", "ci", "ret")}

    for l in range(depth):
        w_l = _relayout_w_in(w_in[l], src)
        w_o = w_out[l].astype(BF16)
        g_pre = norm_pre[l][None, :]
        g_post = norm_post[l][None, :]
        lb = lbs[l][None, :]
        bias = jnp.zeros((1, LANES), F32).at[0, BF0:BF0 + FOX_HEADS].set(fox_bias[l].astype(F32))

        p = _project(xp.reshape(bsz * lp, d), g_pre, w_l, _row_tile(bsz * lp)).reshape(bsz, lp, n_cols)
        ya, st_a = _hgrn(p, 0, lb, zero_state_p, CHUNK, n_chunks_p)
        yd, st_d = _ret(p, 3, tab_p, zero_state_p, CHUNK, n_chunks_p)
        lf, cc, cr = _foxprep(p, misc_col, bias, 0)
        yb, fox_k_rows, fox_v_rows = _fox(p, col["bq"], p, col["bk"], p, col["bv"], p, col["bg"], cc, cr,
                                          pad_front, 0, True, True)
        yc, dsa_k_rows, dsa_v_rows = _dsa(p, (col["cq"], col["cg"], col["ciq"], misc_col), p, ckv_col,
                                          p, misc_col, k_top_p, pad_front, 0, True, pad_front, lp, True)
        flat = lambda a: a.reshape(bsz * lp, gw)
        xp = _merge([flat(ya), flat(yb), flat(yc), flat(yd)], xp.reshape(bsz * lp, d), w_o, g_post,
                    _row_tile(bsz * lp, lp), lp, pad_front, lp).reshape(bsz, lp, d)
        pv = p[:, pad_front:, :]
        outs_p["hgrn"].append(_bd_to_state(st_a, N_HEADS))
        outs_p["ret"].append(_bd_to_state(st_d, N_HEADS))
        outs_p["fk"].append(fox_k_rows.reshape(bsz, -1, N_HEADS, hd))
        outs_p["fv"].append(fox_v_rows.reshape(bsz, -1, N_HEADS, hd))
        outs_p["fl"].append(lf[:, pad_front:, BF0:BF0 + FOX_HEADS])
        outs_p["ck"].append(dsa_k_rows)
        outs_p["cv"].append(dsa_v_rows)
        outs_p["ci"].append(pv[:, :, misc_col * LANES + IK0:misc_col * LANES + IK0 + D_IDX])

        ps = _project(xs.reshape(dbsz * ls, d), g_pre, w_l, _row_tile(dbsz * ls)).reshape(dbsz, ls, n_cols)
        ya, st_a = _hgrn(ps, 0, lb, _state_to_bd(state_hgrn[l].astype(F32)), t_new, 1)
        yd, st_d = _ret(ps, 3, tab_s, _state_to_bd(state_ret[l].astype(F32)), t_new, 1)
        z = jnp.concatenate(
            [jnp.pad(cache_fox_logf[l].astype(F32), ((0, 0), (0, 0), (BF0, LANES - BF0 - FOX_HEADS))),
             ps[:, :, misc_col * LANES:(misc_col + 1) * LANES]], axis=1)
        lf, cc, cr = _foxprep(z, 0, bias, past)
        k_all = jnp.concatenate([cache_fox_k[l].reshape(dbsz, past, gw), ps[:, :, 5 * gw:6 * gw]], axis=1)
        v_all = jnp.concatenate([cache_fox_v[l].reshape(dbsz, past, gw), ps[:, :, 6 * gw:7 * gw]], axis=1)
        (yb,) = _fox(ps, col["bq"], k_all, 0, v_all, 0, ps, col["bg"], cc, cr, 0, past, False, False)
        ckv_all = jnp.concatenate(
            [jnp.concatenate([cache_dsa_k[l], cache_dsa_v[l]], axis=-1).astype(F32),
             ps[:, :, 11 * gw:11 * gw + LANES]], axis=1)
        mk_all = jnp.concatenate(
            [jnp.pad(cache_dsa_idx_k[l].astype(F32), ((0, 0), (0, 0), (IK0, LANES - IK0 - D_IDX))),
             ps[:, :, misc_col * LANES:(misc_col + 1) * LANES]], axis=1)
        (yc,) = _dsa(ps, (col["cq"], col["cg"], col["ciq"], misc_col), ckv_all, 0, mk_all, 0,
                     k_top_s, 0, past + t_new, False, 0, t_new, False)
        flat = lambda a: a.reshape(dbsz * ls, gw)
        xs = _merge([flat(ya), flat(yb), flat(yc), flat(yd)], xs.reshape(dbsz * ls, d), w_o, g_post,
                    ls, ls, 0, t_new).reshape(dbsz, ls, d)
        pn = ps[:, :t_new, :]
        outs_s["hgrn"].append(_bd_to_state(st_a, N_HEADS))
        outs_s["ret"].append(_bd_to_state(st_d, N_HEADS))
        outs_s["fk"].append(pn[:, :, 5 * gw:6 * gw].reshape(dbsz, -1, N_HEADS, hd))
        outs_s["fv"].append(pn[:, :, 6 * gw:7 * gw].reshape(dbsz, -1, N_HEADS, hd))
        outs_s["fl"].append(lf[:, past:past + t_new, BF0:BF0 + FOX_HEADS])
        outs_s["ck"].append(pn[:, :, 11 * gw:11 * gw + hd])
        outs_s["cv"].append(pn[:, :, 11 * gw + hd:11 * gw + 2 * hd])
        outs_s["ci"].append(pn[:, :, misc_col * LANES + IK0:misc_col * LANES + IK0 + D_IDX])

    dt = x_prompt.dtype
    st = lambda xs_list: jnp.stack(xs_list, axis=0).astype(dt)
    order = ("hgrn", "fk", "fv", "fl", "ck", "cv", "ci", "ret")
    return ((xp[:, ROW_BLOCK:, :].astype(dt), xs[:, :t_new, :].astype(dt))
            + tuple(st(outs_p[k]) for k in order) + tuple(st(outs_s[k]) for k in order))
```

```python
import functools

import numpy as np
import jax
import jax.numpy as jnp
from jax import lax
from jax.experimental import pallas as pl
from jax.experimental.pallas import tpu as pltpu

F32 = jnp.float32
BF16 = jnp.bfloat16
Y_DTYPE = BF16
I32 = jnp.int32
LOG2E = float(np.log2(np.e))
ONES_ROWS = 16

N_META = 16
CHUNK = 64
N_GROUPS = 4
N_HEADS = 4
H_IDX = 8
D_IDX = 32
TOP_K_MAX = 256
ROPE_BASE = 10000.0
EPS = 1e-6
FOX_HEADS = 4

LANES = 128
SUBLANES = 8
ROW_BLOCK = 128
GROUP_W = 2 * LANES
VMEM_LIMIT_BYTES = 56 * 1024 * 1024

IK0 = 0
IW0 = 32
BF0 = 64

NEG = -1e30
M_FLOOR = -1e20
INT_MIN = np.int32(-2 ** 31)
STEP_BLOCKS = 4
KEY_STEP = STEP_BLOCKS * ROW_BLOCK
GLA_BATCH_ROWS = 4


def _cparams(*sem):
    return pltpu.CompilerParams(dimension_semantics=sem, vmem_limit_bytes=VMEM_LIMIT_BYTES)


def _split3(x):
    h = x.astype(BF16)
    r = x - h.astype(F32)
    m = r.astype(BF16)
    lo = (r - m.astype(F32)).astype(BF16)
    return h, m, lo


def _dot_exact_lhs(a_bf16, x):
    d = lambda y: jnp.dot(a_bf16, y, preferred_element_type=F32)
    h, m, lo = _split3(x)
    return d(h) + d(m) + d(lo)


def _dot_nt(a, b):
    return lax.dot_general(a, b, (((1,), (1,)), ((), ())), preferred_element_type=F32)


def _dot_tn(a, b):
    return lax.dot_general(a, b, (((0,), (0,)), ((), ())), preferred_element_type=F32)


def _log2(n):
    assert n > 0 and n & (n - 1) == 0, n
    return n.bit_length() - 1


def _head_masks(width, n_heads):
    lane = lax.broadcasted_iota(I32, (1, width), 1)
    sh = _log2(width // n_heads)
    return [(lax.shift_right_logical(lane, sh) == h).astype(F32) for h in range(n_heads)]


def _block_diag(width, n_heads, value):
    r = lax.broadcasted_iota(I32, (width, width), 0)
    c = lax.broadcasted_iota(I32, (width, width), 1)
    sh = _log2(width // n_heads)
    same = lax.shift_right_logical(r, sh) == lax.shift_right_logical(c, sh)
    return jnp.where(same, value, 0.0).astype(F32)


def _silu(x):
    return x * jax.nn.sigmoid(x)


def _proj_kernel(x_ref, g_ref, w_ref, o_ref, *, col_chunk):
    x = x_ref[...]
    ms = jnp.mean(x * x, axis=-1, keepdims=True)
    xn = ((x * lax.rsqrt(ms + EPS)) * g_ref[...]).astype(BF16)
    for c in range(o_ref.shape[1] // col_chunk):
        cols = slice(c * col_chunk, (c + 1) * col_chunk)
        o_ref[:, cols] = jnp.dot(xn, w_ref[:, cols], preferred_element_type=F32)


def _project(x2d, g, w_bf16, tm):
    rows, d = x2d.shape
    n = w_bf16.shape[1]
    return pl.pallas_call(
        functools.partial(_proj_kernel, col_chunk=1024),
        grid=(rows // tm,),
        in_specs=[pl.BlockSpec((tm, d), lambda i: (i, 0)),
                  pl.BlockSpec((1, d), lambda i: (0, 0)),
                  pl.BlockSpec((d, n), lambda i: (0, 0))],
        out_specs=pl.BlockSpec((tm, n), lambda i: (i, 0)),
        out_shape=jax.ShapeDtypeStruct((rows, n), F32),
        compiler_params=_cparams("parallel"),
        name="proj",
    )(x2d, g, w_bf16)


def _head_scores(qa, ka, hm):
    q_stack = jnp.concatenate([qa * hm[h] for h in range(N_HEADS)], axis=0).astype(BF16)
    return _dot_nt(q_stack, ka.astype(BF16))


def _hier_constants(chunk):
    halves = [chunk >> (i + 1) for i in range(_log2(chunk))]
    r = lax.broadcasted_iota(I32, (chunk, chunk), 0)
    c = lax.broadcasted_iota(I32, (chunk, chunk), 1)
    ts = lax.broadcasted_iota(I32, (N_HEADS * chunk, chunk), 0) & (chunk - 1)
    ss = lax.broadcasted_iota(I32, (N_HEADS * chunk, chunk), 1)
    sels, masks = [], []
    for h in halves:
        sh = _log2(h)
        grp = lambda x: lax.shift_right_logical(x, sh + 1)
        if h < SUBLANES:
            sels.append((c == lax.shift_left(grp(r), sh + 1) + (h - 1)).astype(BF16))
        upper_t = (lax.shift_right_logical(ts, sh) & 1).astype(F32)
        lower_s = 1.0 - (lax.shift_right_logical(ss, sh) & 1).astype(F32)
        masks.append(jnp.where(grp(ts) == grp(ss), upper_t * lower_s, 0.0))
    return halves, jnp.concatenate(sels, axis=0), masks


def _decayed_scores(q, k, b, hier, hm):
    halves, sel_small, masks = hier
    chunk, w = q.shape
    small_refs = _dot_exact_lhs(sel_small, b)
    att, n_small = None, 0
    for h, mask in zip(halves, masks):
        if h < SUBLANES:
            ref = small_refs[n_small * chunk:(n_small + 1) * chunk, :]
            n_small += 1
        else:
            ref = jnp.concatenate([jnp.broadcast_to(b[g0 + h - 1:g0 + h, :], (2 * h, w))
                                   for g0 in range(0, chunk, 2 * h)], axis=0)
        qa = q * jnp.exp(jnp.minimum(b - ref, 0.0))
        ka = k * jnp.exp(jnp.minimum(ref - b, 0.0))
        term = _head_scores(qa, ka, hm) * mask
        att = term if att is None else att + term
    return att


def _gla_chunk(att, qe, ke2, v, decay_end, st, hm, bd):
    c = qe.shape[0]
    o_stack = jnp.dot(att.astype(BF16), v.astype(BF16), preferred_element_type=F32)
    o_intra = o_stack[0:c] * hm[0]
    for h in range(1, N_HEADS):
        o_intra = o_intra + o_stack[h * c:(h + 1) * c] * hm[h]
    o_inter = _dot_nt(qe.astype(BF16), st.astype(BF16))
    st_new = st * decay_end + _dot_tn(v.astype(BF16), ke2.astype(BF16)) * bd
    return o_inter + o_intra, st_new


def _head_rms_gate(o, gate, bd_mean_bf16):
    h, m, _ = _split3(o * o)
    ms = (jnp.dot(h, bd_mean_bf16, preferred_element_type=F32)
          + jnp.dot(m, bd_mean_bf16, preferred_element_type=F32))
    return (o * lax.rsqrt(ms + EPS)) * _silu(gate)


def _gla_prologue(st0_ref, y_ref, st_ref, chunk, n_chunks):
    @pl.when(pl.program_id(1) == 0)
    def _():
        st_ref[...] = st0_ref[...]
    tail = n_chunks * chunk
    if tail < y_ref.shape[1]:
        y_ref[:, tail:, :] = jnp.zeros((y_ref.shape[0], y_ref.shape[1] - tail, y_ref.shape[2]),
                                       y_ref.dtype)


def _hgrn_kernel(a_ref, lb_ref, st0_ref, y_ref, st_ref, *, chunk, n_chunks):
    n_b, _, w = y_ref.shape
    hm = _head_masks(w, N_HEADS)
    bd = _block_diag(w, N_HEADS, 1.0)
    bd_mean = _block_diag(w, N_HEADS, 1.0 / (w // N_HEADS)).astype(BF16)
    r = lax.broadcasted_iota(I32, (chunk, chunk), 0)
    s = lax.broadcasted_iota(I32, (chunk, chunk), 1)
    tri = (r >= s).astype(BF16)
    hier = _hier_constants(chunk)
    bd_ones = bd.astype(BF16)
    lb = lb_ref[...]
    _gla_prologue(st0_ref, y_ref, st_ref, chunk, n_chunks)

    def body(c, carry):
        rows = pl.ds(pl.multiple_of(c * chunk, chunk), chunk)
        for g in range(n_b):
            q = a_ref[g, rows, 0:w]
            f = lb + (1.0 - lb) * jax.nn.sigmoid(a_ref[g, rows, w:2 * w])
            k = 1.0 - f
            v = a_ref[g, rows, 2 * w:3 * w]
            gate = a_ref[g, rows, 3 * w:4 * w]
            b = _dot_exact_lhs(tri, jnp.log(f))
            b_end = b[chunk - 1:chunk, :]
            att = _decayed_scores(q, k, b, hier, hm)
            o, st_new = _gla_chunk(att, q * jnp.exp(b), k * jnp.exp(b_end - b), v, jnp.exp(b_end),
                                   st_ref[g], hm, bd)
            o = o + jnp.dot((q * k).astype(BF16), bd_ones, preferred_element_type=F32) * v
            st_ref[g] = st_new
            y_ref[g, rows, :] = _head_rms_gate(o, gate, bd_mean).astype(y_ref.dtype)
        return carry

    lax.fori_loop(0, n_chunks, body, 0)


def _ret_kernel(d_ref, cos_ref, sin_ref, eb_ref, ke2s_ref, dend_ref, gam_ref, st0_ref, y_ref, st_ref,
                *, chunk, n_chunks):
    n_b, _, w = y_ref.shape
    hd = w // N_HEADS
    hm = _head_masks(w, N_HEADS)
    bd = _block_diag(w, N_HEADS, 1.0)
    bd_mean = _block_diag(w, N_HEADS, 1.0 / hd).astype(BF16)
    lane = lax.broadcasted_iota(I32, (1, w), 1)
    first_half = (lane & (hd - 1)) < (hd // 2)
    eb = eb_ref[...]
    ke2s = ke2s_ref[...]
    dend = dend_ref[...]
    gam = gam_ref[...]
    _gla_prologue(st0_ref, y_ref, st_ref, chunk, n_chunks)

    def rope(x, cos, sin_signed):
        swapped = jnp.where(first_half, pltpu.roll(x, w - hd // 2, 1), pltpu.roll(x, hd // 2, 1))
        return x * cos + swapped * sin_signed

    def body(c, carry):
        rows = pl.ds(pl.multiple_of(c * chunk, chunk), chunk)
        cos = cos_ref[rows, :]
        sin = sin_ref[rows, :]
        for g in range(n_b):
            q = rope(d_ref[g, rows, 0:w], cos, sin)
            k = rope(d_ref[g, rows, w:2 * w], cos, sin) * (hd ** -0.5)
            v = d_ref[g, rows, 2 * w:3 * w]
            gate = d_ref[g, rows, 3 * w:4 * w]
            att = _head_scores(q, k, hm) * gam
            o, st_new = _gla_chunk(att, q * eb, k * ke2s, v, dend, st_ref[g], hm, bd)
            st_ref[g] = st_new
            y_ref[g, rows, :] = _head_rms_gate(o, gate, bd_mean).astype(y_ref.dtype)
        return carry

    lax.fori_loop(0, n_chunks, body, 0)


def _gla_grid(bsz, lp, chunk, n_chunks):
    n_b = next(n for n in (GLA_BATCH_ROWS, 2, 1) if bsz % n == 0)
    halves = n_chunks * chunk == lp and n_chunks % 2 == 0 and (lp // 2) % SUBLANES == 0
    n_seq = 2 if halves else 1
    return n_b, n_seq, n_chunks // n_seq


def _hgrn(p3, col_block, lb, st0, chunk, n_chunks):
    bsz, lp, _ = p3.shape
    w = lb.shape[1]
    n_b, n_seq, n_chunks_blk = _gla_grid(bsz, lp, chunk, n_chunks)
    rows = lp // n_seq
    return pl.pallas_call(
        functools.partial(_hgrn_kernel, chunk=chunk, n_chunks=n_chunks_blk),
        grid=(bsz // n_b, n_seq),
        in_specs=[pl.BlockSpec((n_b, rows, 4 * w), lambda b, s: (b, s, col_block)),
                  pl.BlockSpec((1, w), lambda b, s: (0, 0)),
                  pl.BlockSpec((n_b, w, w), lambda b, s: (b, 0, 0))],
        out_specs=[pl.BlockSpec((n_b, rows, w), lambda b, s: (b, s, 0)),
                   pl.BlockSpec((n_b, w, w), lambda b, s: (b, 0, 0))],
        out_shape=[jax.ShapeDtypeStruct((bsz, lp, w), Y_DTYPE),
                   jax.ShapeDtypeStruct((bsz, w, w), F32)],
        compiler_params=_cparams("parallel", "arbitrary"),
        name="hgrn",
    )(p3, lb, st0)


def _ret(p3, col_block, tables, st0, chunk, n_chunks):
    bsz, lp, _ = p3.shape
    w = st0.shape[1]
    n_b, n_seq, n_chunks_blk = _gla_grid(bsz, lp, chunk, n_chunks)
    rows = lp // n_seq
    cos, sin, eb, ke2s, dend, gam = tables
    full = lambda a: pl.BlockSpec(a.shape, lambda b, s: (0,) * a.ndim)
    per_seq = pl.BlockSpec((rows, w), lambda b, s: (s, 0))
    return pl.pallas_call(
        functools.partial(_ret_kernel, chunk=chunk, n_chunks=n_chunks_blk),
        grid=(bsz // n_b, n_seq),
        in_specs=[pl.BlockSpec((n_b, rows, 4 * w), lambda b, s: (b, s, col_block)),
                  per_seq, per_seq, full(eb), full(ke2s), full(dend), full(gam),
                  pl.BlockSpec((n_b, w, w), lambda b, s: (b, 0, 0))],
        out_specs=[pl.BlockSpec((n_b, rows, w), lambda b, s: (b, s, 0)),
                   pl.BlockSpec((n_b, w, w), lambda b, s: (b, 0, 0))],
        out_shape=[jax.ShapeDtypeStruct((bsz, lp, w), Y_DTYPE),
                   jax.ShapeDtypeStruct((bsz, w, w), F32)],
        compiler_params=_cparams("parallel", "arbitrary"),
        name="ret",
    )(p3, cos, sin, eb, ke2s, dend, gam, st0)


def _log_sigmoid(x):
    return -(jnp.maximum(-x, 0.0) + jnp.log(1.0 + jnp.exp(-jnp.abs(x))))


def _foxprep_kernel(z_ref, bias_ref, lf_ref, cc_ref, cr_ref, *, n_pass):
    n_b = z_ref.shape[0]
    nblk = z_ref.shape[1] // ROW_BLOCK
    r = lax.broadcasted_iota(I32, (ROW_BLOCK, ROW_BLOCK), 0)
    s = lax.broadcasted_iota(I32, (ROW_BLOCK, ROW_BLOCK), 1)
    tri = (r >= s).astype(BF16)
    bias = bias_ref[...]

    def body(i, carries):
        rows = pl.ds(pl.multiple_of(i * ROW_BLOCK, ROW_BLOCK), ROW_BLOCK)
        rowi = i * ROW_BLOCK + lax.broadcasted_iota(I32, (ROW_BLOCK, 1), 0)
        out = []
        for g in range(n_b):
            z = z_ref[g, rows, :]
            lf = jnp.where(rowi < n_pass, z, _log_sigmoid(z + bias))
            lf_ref[g, rows, :] = lf
            cs = _dot_exact_lhs(tri, lf) + carries[g]
            cc_ref[g, rows, :] = cs
            cr_ref[g, :, rows] = cs.T[BF0:BF0 + SUBLANES, :]
            out.append(cs[ROW_BLOCK - 1:ROW_BLOCK, :])
        return tuple(out)

    lax.fori_loop(0, nblk, body, tuple(jnp.zeros((1, LANES), F32) for _ in range(n_b)))
    pad = cr_ref.shape[2] - nblk * ROW_BLOCK
    if pad:
        cr_ref[:, :, nblk * ROW_BLOCK:] = jnp.zeros((n_b, SUBLANES, pad), F32)


def _foxprep(z3, z_col, bias, n_pass):
    bsz, lk, _ = z3.shape
    cr_cols = pl.cdiv(lk, KEY_STEP) * KEY_STEP
    n_b = next(n for n in (GLA_BATCH_ROWS, 2, 1) if bsz % n == 0)
    return pl.pallas_call(
        functools.partial(_foxprep_kernel, n_pass=n_pass),
        grid=(bsz // n_b,),
        in_specs=[pl.BlockSpec((n_b, lk, LANES), lambda b: (b, 0, z_col)),
                  pl.BlockSpec((1, LANES), lambda b: (0, 0))],
        out_specs=[pl.BlockSpec((n_b, lk, LANES), lambda b: (b, 0, 0)),
                   pl.BlockSpec((n_b, lk, LANES), lambda b: (b, 0, 0)),
                   pl.BlockSpec((n_b, SUBLANES, cr_cols), lambda b: (b, 0, 0))],
        out_shape=[jax.ShapeDtypeStruct((bsz, lk, LANES), F32),
                   jax.ShapeDtypeStruct((bsz, lk, LANES), F32),
                   jax.ShapeDtypeStruct((bsz, SUBLANES, cr_cols), F32)],
        compiler_params=_cparams("parallel"),
        name="foxprep",
    )(z3, bias)


MXU_COLS = 2 * LANES


def _query_block(lq):
    return MXU_COLS if lq > ROW_BLOCK else ROW_BLOCK


def _heads_per_dot(qb):
    assert MXU_COLS % qb == 0
    return MXU_COLS // qb


def _head_weights(x_t, rows_per_head, n_heads):
    hpd = _heads_per_dot(x_t.shape[1])
    rowh = lax.shift_right_logical(lax.broadcasted_iota(I32, (x_t.shape[0], 1), 0),
                                   _log2(rows_per_head))
    only = lambda h: jnp.where(rowh == h, x_t, 0.0)
    return [jnp.concatenate([only(d * hpd + i) for i in range(hpd)], axis=1).astype(BF16)
            for d in range(n_heads // hpd)]


def _attend_two_pass(n_steps, logits_fn, value_t_fn, shifts, s_scr, acc_scr, qb, hd):
    fold = lambda x, op: op(x.reshape(KEY_STEP // SUBLANES, SUBLANES, qb), axis=0)
    hv = hd + ONES_ROWS

    def max_step(i, ms):
        tiles = logits_fn(i)
        for h in range(N_HEADS):
            s_scr[i * N_HEADS + h] = tiles[h]
        return tuple(jnp.maximum(ms[h], fold(tiles[h], jnp.max)) for h in range(N_HEADS))

    ms = lax.fori_loop(0, n_steps, max_step,
                       tuple(jnp.full((SUBLANES, qb), M_FLOOR, F32) for _ in range(N_HEADS)))
    m_logit = [jnp.max(ms[h], axis=0, keepdims=True) + shifts[h] for h in range(N_HEADS)]
    acc_scr[...] = jnp.zeros(acc_scr.shape, F32)

    def sum_step(i, carry):
        for h in range(N_HEADS):
            p = jnp.exp2(s_scr[i * N_HEADS + h] - (m_logit[h] - shifts[h]))
            acc_scr[h * hv:(h + 1) * hv, :] += jnp.dot(value_t_fn(h, i), p.astype(BF16),
                                                       preferred_element_type=F32)
        return carry

    lax.fori_loop(0, n_steps, sum_step, 0)
    outs = []
    for h in range(N_HEADS):
        l = acc_scr[h * hv + hd:h * hv + hd + 1, :]
        outs.append(acc_scr[h * hv:h * hv + hd, :] / jnp.where(l > 0.0, l, 1.0))
    return jnp.concatenate(outs, axis=0)


def _fox_kernel(q_ref, k_ref, v_ref, g_ref, cc_ref, cr_ref, y_ref, *rest, key_lo, q_off, causal_blocks,
                emit_caches):
    if emit_caches:
        kc_ref, vc_ref, kbf_scr, vt_scr, ck_scr, s_scr, acc_scr = rest
    else:
        kbf_scr, vt_scr, ck_scr, s_scr, acc_scr = rest
    qb, w = q_ref.shape
    hd = w // N_HEADS
    hv = hd + ONES_ROWS
    nkb_total = k_ref.shape[0] // ROW_BLOCK
    hpd = _heads_per_dot(qb)
    j = pl.program_id(1)
    nkb = jnp.minimum((j + 1) * (qb // ROW_BLOCK), nkb_total) if causal_blocks else nkb_total

    @pl.when(j == 0)
    def _prepare_batch_row():
        def blk(i, carry):
            rows = pl.ds(pl.multiple_of(i * ROW_BLOCK, ROW_BLOCK), ROW_BLOCK)
            kbf_scr[rows, :] = k_ref[rows, :].astype(BF16)
            v_t = v_ref[rows, :].T.astype(BF16)
            cs = cc_ref[rows, :] * LOG2E
            for h in range(N_HEADS):
                vt_scr[h * hv:h * hv + hd, rows] = v_t[h * hd:(h + 1) * hd, :]
                ck_scr[h, rows, :] = jnp.broadcast_to(cs[:, BF0 + h:BF0 + h + 1], (ROW_BLOCK, LANES))
            return carry
        lax.fori_loop(0, nkb_total, blk, 0)
        if emit_caches:
            kc_ref[...] = k_ref[key_lo:, :]
            vc_ref[...] = v_ref[key_lo:, :]
        pad = vt_scr.shape[1] - nkb_total * ROW_BLOCK
        if pad:
            kbf_scr[nkb_total * ROW_BLOCK:, :] = jnp.zeros((pad, w), BF16)
            ck_scr[:, nkb_total * ROW_BLOCK:, :] = jnp.zeros((N_HEADS, pad, LANES), F32)
        for h in range(N_HEADS):
            if pad:
                vt_scr[h * hv:h * hv + hd, nkb_total * ROW_BLOCK:] = jnp.zeros((hd, pad), BF16)
            vt_scr[h * hv + hd:(h + 1) * hv, :] = jnp.ones((ONES_ROWS, vt_scr.shape[1]), BF16)

    wq = _head_weights((q_ref[...] * (hd ** -0.5 * LOG2E)).T, hd, N_HEADS)
    qcol = pl.ds(pl.multiple_of(j * qb + q_off, ROW_BLOCK), qb)
    cq = [cr_ref[h:h + 1, qcol] * LOG2E for h in range(N_HEADS)]
    qrow = j * qb + q_off + lax.broadcasted_iota(I32, (1, qb), 1)
    sub = lax.broadcasted_iota(I32, (KEY_STEP, 1), 0)

    def logits(i):
        k0 = pl.multiple_of(i * KEY_STEP, KEY_STEP)
        krows = pl.ds(k0, KEY_STEP)
        kblk = kbf_scr[krows, :]
        kidx = k0 + sub
        ok = (kidx >= key_lo) & (kidx <= qrow)
        tiles = []
        for d in range(N_HEADS // hpd):
            s2 = jnp.dot(kblk, wq[d], preferred_element_type=F32)
            for i2 in range(hpd):
                h = d * hpd + i2
                ck = jnp.concatenate([ck_scr[h, krows, :]] * (qb // LANES), axis=1)
                tiles.append(jnp.where(ok, s2[:, i2 * qb:(i2 + 1) * qb] - ck, NEG))
        return tiles

    def value_t(h, i):
        return vt_scr[h * hv:(h + 1) * hv, pl.ds(pl.multiple_of(i * KEY_STEP, KEY_STEP), KEY_STEP)]

    n_steps = lax.shift_right_logical(nkb + (STEP_BLOCKS - 1), _log2(STEP_BLOCKS))
    o_t = _attend_two_pass(n_steps, logits, value_t, cq, s_scr, acc_scr, qb, hd)
    y_ref[...] = (o_t.T * _silu(g_ref[...])).astype(y_ref.dtype)


def _fox(q_arr, q_col, k_arr, k_col, v_arr, v_col, g_arr, g_col, cc, cr, key_lo, q_off, causal_blocks,
         emit_caches):
    bsz, lq, _ = q_arr.shape
    lk = k_arr.shape[1]
    w = GROUP_W
    qb = _query_block(lq)
    key_rows = pl.cdiv(lk, KEY_STEP) * KEY_STEP
    assert cr.shape[2] >= q_off + pl.cdiv(lq, qb) * qb
    out_specs = [pl.BlockSpec((None, qb, w), lambda b, j: (b, j, 0))]
    out_shape = [jax.ShapeDtypeStruct((bsz, lq, w), Y_DTYPE)]
    if emit_caches:
        out_specs += [pl.BlockSpec((None, lk - key_lo, w), lambda b, j: (b, 0, 0))] * 2
        out_shape += [jax.ShapeDtypeStruct((bsz, lk - key_lo, w), F32)] * 2
    return pl.pallas_call(
        functools.partial(_fox_kernel, key_lo=key_lo, q_off=q_off, causal_blocks=causal_blocks,
                          emit_caches=emit_caches),
        grid=(bsz, pl.cdiv(lq, qb)),
        in_specs=[pl.BlockSpec((None, qb, w), lambda b, j: (b, j, q_col)),
                  pl.BlockSpec((None, lk, w), lambda b, j: (b, 0, k_col)),
                  pl.BlockSpec((None, lk, w), lambda b, j: (b, 0, v_col)),
                  pl.BlockSpec((None, qb, w), lambda b, j: (b, j, g_col)),
                  pl.BlockSpec((None, lk, LANES), lambda b, j: (b, 0, 0)),
                  pl.BlockSpec((None, SUBLANES, cr.shape[2]), lambda b, j: (b, 0, 0))],
        out_specs=out_specs,
        out_shape=out_shape,
        scratch_shapes=[pltpu.VMEM((key_rows, w), BF16),
                        pltpu.VMEM((w + N_HEADS * ONES_ROWS, key_rows), BF16),
                        pltpu.VMEM((N_HEADS, key_rows, LANES), F32),
                        pltpu.VMEM((N_HEADS * key_rows // KEY_STEP, KEY_STEP, qb), F32),
                        pltpu.VMEM((w + N_HEADS * ONES_ROWS, qb), F32)],
        compiler_params=_cparams("parallel", "arbitrary"),
        name="fox",
    )(q_arr, k_arr, v_arr, g_arr, cc, cr)


def _dsa_kernel(cq_ref, cg_ref, ciq_ref, mq_ref, ckv_ref, mk_ref, y_ref, *rest,
                k_top, key_lo, key_hi, chunk_causal, q_lo, q_hi, emit_caches):
    if emit_caches:
        kc_ref, vc_ref, key_scr, kvb_scr, vt_scr, mkb_scr, s_scr, acc_scr = rest
    else:
        key_scr, kvb_scr, vt_scr, mkb_scr, s_scr, acc_scr = rest
    qb, w = cq_ref.shape
    hd = w // N_HEADS
    hpd = _heads_per_dot(qb)
    nkb_total = ckv_ref.shape[0] // ROW_BLOCK
    j = pl.program_id(1)
    nkb = jnp.minimum((j + 1) * (qb // ROW_BLOCK), nkb_total) if chunk_causal else nkb_total
    idx_scale = (H_IDX * D_IDX) ** -0.5

    @pl.when(j == 0)
    def _prepare_batch_row():
        def blk(i, carry):
            rows = pl.ds(pl.multiple_of(i * ROW_BLOCK, ROW_BLOCK), ROW_BLOCK)
            kv = ckv_ref[rows, :]
            kvb_scr[rows, :] = kv.astype(BF16)
            vt_scr[0:hd, rows] = kv.T[hd:2 * hd, :].astype(BF16)
            mkb_scr[rows, :] = mk_ref[rows, :].astype(BF16)
            return carry
        lax.fori_loop(0, nkb_total, blk, 0)
        if emit_caches:
            kv_new = ckv_ref[key_lo:, :]
            kc_ref[...] = kv_new[:, 0:hd]
            vc_ref[...] = kv_new[:, hd:2 * hd]
        pad = vt_scr.shape[1] - nkb_total * ROW_BLOCK
        if pad:
            kvb_scr[nkb_total * ROW_BLOCK:, :] = jnp.zeros((pad, LANES), BF16)
            mkb_scr[nkb_total * ROW_BLOCK:, :] = jnp.zeros((pad, LANES), BF16)
            vt_scr[0:hd, nkb_total * ROW_BLOCK:] = jnp.zeros((hd, pad), BF16)
        vt_scr[hd:, :] = jnp.ones((ONES_ROWS, vt_scr.shape[1]), BF16)

    def pad_rows(x):
        return jnp.concatenate([x, jnp.zeros((LANES - x.shape[0], qb), F32)], axis=0)

    iq_t = ciq_ref[...].T
    iw_t = mq_ref[...].T[IW0:IW0 + H_IDX, :]
    q_t = (cq_ref[...] * (hd ** -0.5 * LOG2E)).T
    def side_by_side(x_t, rows, n_heads):
        return [jnp.concatenate([pad_rows(x_t[(d * hpd + i) * rows:(d * hpd + i + 1) * rows, :])
                                 for i in range(hpd)], axis=1).astype(BF16)
                for d in range(n_heads // hpd)]

    iq_rhs = side_by_side(iq_t, D_IDX, H_IDX)
    q_rhs = side_by_side(q_t, hd, N_HEADS)

    qrow = j * qb + lax.broadcasted_iota(I32, (1, qb), 1)
    if chunk_causal:
        hi = (lax.shift_right_logical(qrow, 6) + 1) * CHUNK
    else:
        hi = jnp.full((1, qb), key_hi, I32)
    sub = lax.broadcasted_iota(I32, (KEY_STEP, 1), 0)
    n_steps = lax.shift_right_logical(nkb + (STEP_BLOCKS - 1), _log2(STEP_BLOCKS))

    def score_step(i, carry):
        k0 = pl.multiple_of(i * KEY_STEP, KEY_STEP)
        mk = mkb_scr[pl.ds(k0, KEY_STEP), :]
        acc = jnp.zeros((KEY_STEP, qb), F32)
        for d in range(H_IDX // hpd):
            sc2 = jnp.dot(mk, iq_rhs[d], preferred_element_type=F32)
            for i2 in range(hpd):
                h = d * hpd + i2
                acc = acc + jnp.maximum(sc2[:, i2 * qb:(i2 + 1) * qb], 0.0) * iw_t[h:h + 1, :]
        score = acc * idx_scale + 0.0
        kidx = k0 + sub
        adm = (kidx >= key_lo) & (kidx < hi)
        u = pltpu.bitcast(score, I32)
        key = u ^ (lax.shift_right_arithmetic(u, 31) & np.int32(0x7FFFFFFF))
        key_scr[pl.ds(k0, KEY_STEP), :] = jnp.where(adm, key, INT_MIN)
        return carry

    lax.fori_loop(0, n_steps, score_step, 0)

    def count(pred):
        groups = qb // LANES

        def cb(i, c8s):
            k0 = pl.multiple_of(i * KEY_STEP, KEY_STEP)
            out = []
            for lg in range(groups):
                lanes = lambda x, lg=lg: x[:, lg * LANES:(lg + 1) * LANES]
                kk = key_scr[pl.ds(k0, KEY_STEP), lg * LANES:(lg + 1) * LANES]
                ind = jnp.where(pred(kk, k0, lanes), 1, 0).astype(I32)
                out.append(c8s[lg] + jnp.sum(ind.reshape(KEY_STEP // SUBLANES, SUBLANES, LANES), axis=0))
            return tuple(out)
        c8s = lax.fori_loop(0, n_steps, cb,
                            tuple(jnp.zeros((SUBLANES, LANES), I32) for _ in range(groups)))
        return jnp.concatenate([jnp.sum(c, axis=0, keepdims=True) for c in c8s], axis=1)

    def bisect(n_bits, count_ge, need):
        def bit_body(i, carry):
            ans, cnt_ans = carry
            cand = ans | lax.shift_left(np.int32(1), jnp.int32(n_bits - 1) - i)
            cnt = count_ge(cand)
            ok = cnt >= need
            return jnp.where(ok, cand, ans), jnp.where(ok, cnt, cnt_ans)
        return lax.fori_loop(0, n_bits, bit_body,
                             (jnp.zeros((1, qb), I32), jnp.full((1, qb), np.int32(2 ** 30), I32)))

    ans, cnt_thr = bisect(32, lambda cand: count(
        lambda kk, k0, lanes: kk >= lanes(cand ^ INT_MIN)), k_top)
    thr = ans ^ INT_MIN
    tie = (cnt_thr > k_top) & (thr != INT_MIN) & (qrow >= q_lo) & (qrow < q_hi)

    @pl.when(jnp.max(jnp.where(tie, 1, 0)) > 0)
    def _break_ties():
        n_rev_bits = _log2(pl.next_power_of_2(key_scr.shape[0]))
        rev_base = np.int32(2 ** n_rev_bits - 1)
        need = k_top - count(lambda kk, k0, lanes: kk > lanes(thr))
        ans2, _ = bisect(n_rev_bits, lambda cand: count(
            lambda kk, k0, lanes: (kk == lanes(thr)) & ((rev_base - (k0 + sub)) >= lanes(cand))), need)

        def demote(i, carry):
            k0 = pl.multiple_of(i * KEY_STEP, KEY_STEP)
            kk = key_scr[pl.ds(k0, KEY_STEP), :]
            lose = tie & (kk == thr) & ((rev_base - (k0 + sub)) < ans2)
            key_scr[pl.ds(k0, KEY_STEP), :] = jnp.where(lose, INT_MIN, kk)
            return carry

        lax.fori_loop(0, n_steps, demote, 0)

    thr_sel = jnp.maximum(thr, INT_MIN + 1)

    def logits(i):
        krows = pl.ds(pl.multiple_of(i * KEY_STEP, KEY_STEP), KEY_STEP)
        sel = key_scr[krows, :] >= thr_sel
        kv = kvb_scr[krows, :]
        tiles = []
        for d in range(N_HEADS // hpd):
            s2 = jnp.dot(kv, q_rhs[d], preferred_element_type=F32)
            tiles += [jnp.where(sel, s2[:, i2 * qb:(i2 + 1) * qb], NEG) for i2 in range(hpd)]
        return tiles

    def value_t(h, i):
        return vt_scr[:, pl.ds(pl.multiple_of(i * KEY_STEP, KEY_STEP), KEY_STEP)]

    no_shift = [jnp.zeros((1, qb), F32)] * N_HEADS
    o_t = _attend_two_pass(n_steps, logits, value_t, no_shift, s_scr, acc_scr, qb, hd)
    y_ref[...] = (o_t.T * _silu(cg_ref[...])).astype(y_ref.dtype)


def _dsa(p3, cols, ckv_arr, ckv_col, mk_arr, mk_col, k_top, key_lo, key_hi, chunk_causal, q_lo, q_hi,
         emit_caches):
    bsz, lq, _ = p3.shape
    lk = ckv_arr.shape[1]
    w = GROUP_W
    hd = w // N_HEADS
    qb = _query_block(lq)
    cq_col, cg_col, ciq_col, mq_col = cols
    key_rows = pl.cdiv(lk, KEY_STEP) * KEY_STEP
    out_specs = [pl.BlockSpec((None, qb, w), lambda b, j: (b, j, 0))]
    out_shape = [jax.ShapeDtypeStruct((bsz, lq, w), Y_DTYPE)]
    if emit_caches:
        out_specs += [pl.BlockSpec((None, lk - key_lo, hd), lambda b, j: (b, 0, 0))] * 2
        out_shape += [jax.ShapeDtypeStruct((bsz, lk - key_lo, hd), F32)] * 2
    return pl.pallas_call(
        functools.partial(_dsa_kernel, k_top=k_top, key_lo=key_lo, key_hi=key_hi,
                          chunk_causal=chunk_causal, q_lo=q_lo, q_hi=q_hi, emit_caches=emit_caches),
        grid=(bsz, pl.cdiv(lq, qb)),
        in_specs=[pl.BlockSpec((None, qb, w), lambda b, j: (b, j, cq_col)),
                  pl.BlockSpec((None, qb, w), lambda b, j: (b, j, cg_col)),
                  pl.BlockSpec((None, qb, w), lambda b, j: (b, j, ciq_col)),
                  pl.BlockSpec((None, qb, LANES), lambda b, j: (b, j, mq_col)),
                  pl.BlockSpec((None, lk, LANES), lambda b, j: (b, 0, ckv_col)),
                  pl.BlockSpec((None, lk, LANES), lambda b, j: (b, 0, mk_col))],
        out_specs=out_specs,
        out_shape=out_shape,
        scratch_shapes=[pltpu.VMEM((key_rows, qb), I32),
                        pltpu.VMEM((key_rows, LANES), BF16),
                        pltpu.VMEM((w // N_HEADS + ONES_ROWS, key_rows), BF16),
                        pltpu.VMEM((key_rows, LANES), BF16),
                        pltpu.VMEM((N_HEADS * key_rows // KEY_STEP, KEY_STEP, qb), F32),
                        pltpu.VMEM((w + N_HEADS * ONES_ROWS, qb), F32)],
        compiler_params=_cparams("parallel", "arbitrary"),
        name="dsa",
    )(p3, p3, p3, p3, ckv_arr, mk_arr)


def _merge_kernel(ya_ref, yb_ref, yc_ref, yd_ref, x_ref, w_ref, g_ref, o_ref, *, period, valid_lo, valid_hi):
    tm = x_ref.shape[0]
    gw = ya_ref.shape[1]
    acc = jnp.zeros(o_ref.shape, F32)
    for i, y_ref in enumerate((ya_ref, yb_ref, yc_ref, yd_ref)):
        acc = acc + jnp.dot(y_ref[...].astype(BF16), w_ref[i * gw:(i + 1) * gw, :],
                            preferred_element_type=F32)
    ms = jnp.mean(acc * acc, axis=-1, keepdims=True)
    out = x_ref[...] + (acc * lax.rsqrt(ms + EPS)) * g_ref[...]
    r0 = pl.program_id(0) * tm
    local = (r0 - (r0 // period) * period) + lax.broadcasted_iota(I32, (tm, 1), 0)
    local = jnp.where(local >= period, local - period, local)
    valid = (local >= valid_lo) & (local < valid_hi)
    o_ref[...] = jnp.where(valid, out, 0.0)


def _merge(ys, x2d, w_bf16, g, tm, period, valid_lo, valid_hi):
    rows, d = x2d.shape
    assert tm <= period
    gw = ys[0].shape[1]
    yspec = pl.BlockSpec((tm, gw), lambda i: (i, 0))
    return pl.pallas_call(
        functools.partial(_merge_kernel, period=period, valid_lo=valid_lo, valid_hi=valid_hi),
        grid=(rows // tm,),
        in_specs=[yspec, yspec, yspec, yspec,
                  pl.BlockSpec((tm, d), lambda i: (i, 0)),
                  pl.BlockSpec(w_bf16.shape, lambda i: (0, 0)),
                  pl.BlockSpec((1, d), lambda i: (0, 0))],
        out_specs=pl.BlockSpec((tm, d), lambda i: (i, 0)),
        out_shape=jax.ShapeDtypeStruct((rows, d), F32),
        compiler_params=_cparams("parallel"),
        name="merge",
    )(*ys, x2d, w_bf16, g)


def _row_tile(rows, at_most=512):
    return next(t for t in (512, 256, ROW_BLOCK) if rows % t == 0 and t <= at_most)


def _column_layout(gw):
    sizes = [gw] * 4 + [gw, gw, gw, FOX_HEADS, gw] + [gw, gw // 4, gw // 4, gw, H_IDX * D_IDX, D_IDX, H_IDX] + [gw] * 4
    names = ["aq", "af", "ai", "ag", "bq", "bk", "bv", "bf", "bg",
             "cq", "ck", "cv", "cg", "ciq", "cik", "ciw", "dq", "dk", "dv", "dg"]
    start = dict(zip(names, np.cumsum([0] + sizes[:-1])))
    size = dict(zip(names, sizes))
    src = -np.ones((16 * gw,), np.int64)
    def put(dst, name, off=0):
        src[dst + off:dst + off + size[name]] = np.arange(start[name], start[name] + size[name])
    for i, n in enumerate(["aq", "af", "ai", "ag", "bq", "bk", "bv", "bg", "cq", "cg", "ciq"]):
        put(i * gw, n)
    ckv0 = 11 * gw
    put(ckv0, "ck")
    put(ckv0 + gw // 4, "cv")
    misc0 = ckv0 + LANES
    put(misc0, "cik", IK0)
    put(misc0, "ciw", IW0)
    put(misc0, "bf", BF0)
    for i, n in enumerate(["dq", "dk", "dv", "dg"]):
        put(12 * gw + i * gw, n)
    return src


def _relayout_w_in(w_in_l, src):
    cols = jnp.take(w_in_l, jnp.asarray(np.maximum(src, 0)), axis=1)
    return jnp.where(jnp.asarray(src >= 0)[None, :], cols, 0.0).astype(BF16)


def _ret_tables(pos, chunk, gw):
    hd = gw // N_HEADS
    half = hd // 2
    inv = ROPE_BASE ** (-jnp.arange(half, dtype=F32) / half)
    ang = pos.astype(F32)[:, None] * inv[None, :]
    cos_h = jnp.concatenate([jnp.cos(ang), jnp.cos(ang)], axis=-1)
    sin_h = jnp.concatenate([-jnp.sin(ang), jnp.sin(ang)], axis=-1)
    cos = jnp.tile(cos_h, (1, N_HEADS))
    sin = jnp.tile(sin_h, (1, N_HEADS))
    lg = jnp.log(1.0 - 2.0 ** (-5.0 - jnp.arange(N_HEADS, dtype=F32)))
    lg_l = jnp.repeat(lg, hd)[None, :]
    t = jnp.arange(chunk, dtype=F32)[:, None]
    eb = jnp.exp((t + 1.0) * lg_l)
    ke2s = jnp.exp((chunk - 1.0 - t) * lg_l)
    dend = jnp.exp(chunk * lg_l)
    dt = jnp.arange(chunk, dtype=F32)[:, None] - jnp.arange(chunk, dtype=F32)[None, :]
    gam = jnp.concatenate([jnp.where(dt >= 0, jnp.exp(dt * lg[h]), 0.0) for h in range(N_HEADS)], axis=0)
    return cos, sin, eb, ke2s, dend, gam


def _state_to_bd(state):
    bsz, h, k, v = state.shape
    eye = jnp.eye(h, dtype=state.dtype)
    st = jnp.einsum('bhkv,hg->bhvgk', state, eye)
    return st.reshape(bsz, h * v, h * k)


def _bd_to_state(st, h):
    bsz, hv, hk = st.shape
    st5 = st.reshape(bsz, h, hv // h, h, hk // h)
    diag = jnp.stack([st5[:, i, :, i, :] for i in range(h)], axis=1)
    return jnp.swapaxes(diag, 2, 3)


def kernel(x_prompt, x_sample, state_hgrn, cache_fox_k, cache_fox_v, cache_fox_logf, cache_dsa_k,
           cache_dsa_v, cache_dsa_idx_k, state_ret, meta_tokens, w_in, w_out, fox_bias, hgrn_lb,
           norm_pre, norm_post):
    bsz, seq, d = x_prompt.shape
    dbsz, t_new, _ = x_sample.shape
    depth = w_in.shape[0]
    past = cache_fox_k.shape[2]
    gw = d // N_GROUPS
    hd = gw // N_HEADS
    assert gw == 2 * LANES and seq % ROW_BLOCK == 0 and past % ROW_BLOCK == 0 and t_new <= ROW_BLOCK
    assert t_new % SUBLANES == 0

    pad_front = ROW_BLOCK - N_META
    lp = ROW_BLOCK + seq
    ls = ROW_BLOCK
    lks = past + ROW_BLOCK
    n_chunks_p = lp // CHUNK
    k_top_p = min(TOP_K_MAX, seq // 4)
    k_top_s = min(TOP_K_MAX, (past + t_new) // 4)

    src = _column_layout(gw)
    n_cols = src.shape[0]
    col = {"a": 0, "bq": 4, "bk": 5, "bv": 6, "bg": 7, "cq": 8, "cg": 9, "ciq": 10, "d": 3}
    ckv_col = (11 * gw) // LANES
    misc_col = ckv_col + 1

    sm = jax.nn.softmax(hgrn_lb.astype(F32), axis=0)
    lbs = jnp.cumsum(sm, axis=0) - sm[0:1]

    xp = jnp.concatenate([jnp.zeros((bsz, pad_front, d), F32),
                          jnp.broadcast_to(meta_tokens.astype(F32)[None], (bsz, N_META, d)),
                          x_prompt], axis=1)
    xs = jnp.concatenate([x_sample, jnp.zeros((dbsz, ls - t_new, d), F32)], axis=1)

    tab_p = _ret_tables(jnp.arange(lp) - ROW_BLOCK, CHUNK, gw)
    tab_s = _ret_tables(past + jnp.arange(ls), t_new, gw)
    zero_state_p = jnp.zeros((bsz, gw, gw), F32)

    outs_p = {k: [] for k in ("hgrn", "fk", "fv", "fl", "ck", "cv", "ci", "ret")}
    outs_s = {k: [] for k in ("hgrn", "fk", "fv", "fl", "ck", "cv", "ci", "ret")}

    for l in range(depth):
        w_l = _relayout_w_in(w_in[l], src)
        w_o = w_out[l].astype(BF16)
        g_pre = norm_pre[l][None, :]
        g_post = norm_post[l][None, :]
        lb = lbs[l][None, :]
        bias = jnp.zeros((1, LANES), F32).at[0, BF0:BF0 + FOX_HEADS].set(fox_bias[l].astype(F32))

        p = _project(xp.reshape(bsz * lp, d), g_pre, w_l, _row_tile(bsz * lp)).reshape(bsz, lp, n_cols)
        ya, st_a = _hgrn(p, 0, lb, zero_state_p, CHUNK, n_chunks_p)
        yd, st_d = _ret(p, 3, tab_p, zero_state_p, CHUNK, n_chunks_p)
        lf, cc, cr = _foxprep(p, misc_col, bias, 0)
        yb, fox_k_rows, fox_v_rows = _fox(p, col["bq"], p, col["bk"], p, col["bv"], p, col["bg"], cc, cr,
                                          pad_front, 0, True, True)
        yc, dsa_k_rows, dsa_v_rows = _dsa(p, (col["cq"], col["cg"], col["ciq"], misc_col), p, ckv_col,
                                          p, misc_col, k_top_p, pad_front, 0, True, pad_front, lp, True)
        flat = lambda a: a.reshape(bsz * lp, gw)
        xp = _merge([flat(ya), flat(yb), flat(yc), flat(yd)], xp.reshape(bsz * lp, d), w_o, g_post,
                    _row_tile(bsz * lp, lp), lp, pad_front, lp).reshape(bsz, lp, d)
        pv = p[:, pad_front:, :]
        outs_p["hgrn"].append(_bd_to_state(st_a, N_HEADS))
        outs_p["ret"].append(_bd_to_state(st_d, N_HEADS))
        outs_p["fk"].append(fox_k_rows.reshape(bsz, -1, N_HEADS, hd))
        outs_p["fv"].append(fox_v_rows.reshape(bsz, -1, N_HEADS, hd))
        outs_p["fl"].append(lf[:, pad_front:, BF0:BF0 + FOX_HEADS])
        outs_p["ck"].append(dsa_k_rows)
        outs_p["cv"].append(dsa_v_rows)
        outs_p["ci"].append(pv[:, :, misc_col * LANES + IK0:misc_col * LANES + IK0 + D_IDX])

        ps = _project(xs.reshape(dbsz * ls, d), g_pre, w_l, _row_tile(dbsz * ls)).reshape(dbsz, ls, n_cols)
        ya, st_a = _hgrn(ps, 0, lb, _state_to_bd(state_hgrn[l].astype(F32)), t_new, 1)
        yd, st_d = _ret(ps, 3, tab_s, _state_to_bd(state_ret[l].astype(F32)), t_new, 1)
        z = jnp.concatenate(
            [jnp.pad(cache_fox_logf[l].astype(F32), ((0, 0), (0, 0), (BF0, LANES - BF0 - FOX_HEADS))),
             ps[:, :, misc_col * LANES:(misc_col + 1) * LANES]], axis=1)
        lf, cc, cr = _foxprep(z, 0, bias, past)
        k_all = jnp.concatenate([cache_fox_k[l].reshape(dbsz, past, gw), ps[:, :, 5 * gw:6 * gw]], axis=1)
        v_all = jnp.concatenate([cache_fox_v[l].reshape(dbsz, past, gw), ps[:, :, 6 * gw:7 * gw]], axis=1)
        (yb,) = _fox(ps, col["bq"], k_all, 0, v_all, 0, ps, col["bg"], cc, cr, 0, past, False, False)
        ckv_all = jnp.concatenate(
            [jnp.concatenate([cache_dsa_k[l], cache_dsa_v[l]], axis=-1).astype(F32),
             ps[:, :, 11 * gw:11 * gw + LANES]], axis=1)
        mk_all = jnp.concatenate(
            [jnp.pad(cache_dsa_idx_k[l].astype(F32), ((0, 0), (0, 0), (IK0, LANES - IK0 - D_IDX))),
             ps[:, :, misc_col * LANES:(misc_col + 1) * LANES]], axis=1)
        (yc,) = _dsa(ps, (col["cq"], col["cg"], col["ciq"], misc_col), ckv_all, 0, mk_all, 0,
                     k_top_s, 0, past + t_new, False, 0, t_new, False)
        flat = lambda a: a.reshape(dbsz * ls, gw)
        xs = _merge([flat(ya), flat(yb), flat(yc), flat(yd)], xs.reshape(dbsz * ls, d), w_o, g_post,
                    ls, ls, 0, t_new).reshape(dbsz, ls, d)
        pn = ps[:, :t_new, :]
        outs_s["hgrn"].append(_bd_to_state(st_a, N_HEADS))
        outs_s["ret"].append(_bd_to_state(st_d, N_HEADS))
        outs_s["fk"].append(pn[:, :, 5 * gw:6 * gw].reshape(dbsz, -1, N_HEADS, hd))
        outs_s["fv"].append(pn[:, :, 6 * gw:7 * gw].reshape(dbsz, -1, N_HEADS, hd))
        outs_s["fl"].append(lf[:, past:past + t_new, BF0:BF0 + FOX_HEADS])
        outs_s["ck"].append(pn[:, :, 11 * gw:11 * gw + hd])
        outs_s["cv"].append(pn[:, :, 11 * gw + hd:11 * gw + 2 * hd])
        outs_s["ci"].append(pn[:, :, misc_col * LANES + IK0:misc_col * LANES + IK0 + D_IDX])

    dt = x_prompt.dtype
    st = lambda xs_list: jnp.stack(xs_list, axis=0).astype(dt)
    order = ("hgrn", "fk", "fv", "fl", "ck", "cv", "ci", "ret")
    return ((xp[:, ROW_BLOCK:, :].astype(dt), xs[:, :t_new, :].astype(dt))
            + tuple(st(outs_p[k]) for k in order) + tuple(st(outs_s[k]) for k in order))
```

```python
import functools

import numpy as np
import jax
import jax.numpy as jnp
from jax import lax
from jax.experimental import pallas as pl
from jax.experimental.pallas import tpu as pltpu

F32 = jnp.float32
BF16 = jnp.bfloat16
Y_DTYPE = BF16
I32 = jnp.int32
LOG2E = float(np.log2(np.e))
ONES_ROWS = 16

N_META = 16
CHUNK = 64
N_GROUPS = 4
N_HEADS = 4
H_IDX = 8
D_IDX = 32
TOP_K_MAX = 256
ROPE_BASE = 10000.0
EPS = 1e-6
FOX_HEADS = 4

LANES = 128
SUBLANES = 8
ROW_BLOCK = 128
GROUP_W = 2 * LANES
VMEM_LIMIT_BYTES = 56 * 1024 * 1024

IK0 = 0
IW0 = 32
BF0 = 64

NEG = -1e30
M_FLOOR = -1e20
INT_MIN = np.int32(-2 ** 31)
STEP_BLOCKS = 4
KEY_STEP = STEP_BLOCKS * ROW_BLOCK
GLA_BATCH_ROWS = 4


def _cparams(*sem):
    return pltpu.CompilerParams(dimension_semantics=sem, vmem_limit_bytes=VMEM_LIMIT_BYTES)


def _split3(x):
    h = x.astype(BF16)
    r = x - h.astype(F32)
    m = r.astype(BF16)
    lo = (r - m.astype(F32)).astype(BF16)
    return h, m, lo


def _dot_exact_lhs(a_bf16, x):
    d = lambda y: jnp.dot(a_bf16, y, preferred_element_type=F32)
    h, m, lo = _split3(x)
    return d(h) + d(m) + d(lo)


def _dot_nt(a, b):
    return lax.dot_general(a, b, (((1,), (1,)), ((), ())), preferred_element_type=F32)


def _dot_tn(a, b):
    return lax.dot_general(a, b, (((0,), (0,)), ((), ())), preferred_element_type=F32)


def _log2(n):
    assert n > 0 and n & (n - 1) == 0, n
    return n.bit_length() - 1


def _head_masks(width, n_heads):
    lane = lax.broadcasted_iota(I32, (1, width), 1)
    sh = _log2(width // n_heads)
    return [(lax.shift_right_logical(lane, sh) == h).astype(F32) for h in range(n_heads)]


def _block_diag(width, n_heads, value):
    r = lax.broadcasted_iota(I32, (width, width), 0)
    c = lax.broadcasted_iota(I32, (width, width), 1)
    sh = _log2(width // n_heads)
    same = lax.shift_right_logical(r, sh) == lax.shift_right_logical(c, sh)
    return jnp.where(same, value, 0.0).astype(F32)


def _silu(x):
    return x * jax.nn.sigmoid(x)


def _proj_kernel(x_ref, g_ref, w_ref, o_ref, *, col_chunk):
    x = x_ref[...]
    ms = jnp.mean(x * x, axis=-1, keepdims=True)
    xn = ((x * lax.rsqrt(ms + EPS)) * g_ref[...]).astype(BF16)
    for c in range(o_ref.shape[1] // col_chunk):
        cols = slice(c * col_chunk, (c + 1) * col_chunk)
        o_ref[:, cols] = jnp.dot(xn, w_ref[:, cols], preferred_element_type=F32)


def _project(x2d, g, w_bf16, tm):
    rows, d = x2d.shape
    n = w_bf16.shape[1]
    return pl.pallas_call(
        functools.partial(_proj_kernel, col_chunk=1024),
        grid=(rows // tm,),
        in_specs=[pl.BlockSpec((tm, d), lambda i: (i, 0)),
                  pl.BlockSpec((1, d), lambda i: (0, 0)),
                  pl.BlockSpec((d, n), lambda i: (0, 0))],
        out_specs=pl.BlockSpec((tm, n), lambda i: (i, 0)),
        out_shape=jax.ShapeDtypeStruct((rows, n), F32),
        compiler_params=_cparams("parallel"),
        name="proj",
    )(x2d, g, w_bf16)


def _head_scores(qa, ka, hm):
    q_stack = jnp.concatenate([qa * hm[h] for h in range(N_HEADS)], axis=0).astype(BF16)
    return _dot_nt(q_stack, ka.astype(BF16))


def _hier_constants(chunk):
    halves = [chunk >> (i + 1) for i in range(_log2(chunk))]
    r = lax.broadcasted_iota(I32, (chunk, chunk), 0)
    c = lax.broadcasted_iota(I32, (chunk, chunk), 1)
    ts = lax.broadcasted_iota(I32, (N_HEADS * chunk, chunk), 0) & (chunk - 1)
    ss = lax.broadcasted_iota(I32, (N_HEADS * chunk, chunk), 1)
    sels, masks = [], []
    for h in halves:
        sh = _log2(h)
        grp = lambda x: lax.shift_right_logical(x, sh + 1)
        if h < SUBLANES:
            sels.append((c == lax.shift_left(grp(r), sh + 1) + (h - 1)).astype(BF16))
        upper_t = (lax.shift_right_logical(ts, sh) & 1).astype(F32)
        lower_s = 1.0 - (lax.shift_right_logical(ss, sh) & 1).astype(F32)
        masks.append(jnp.where(grp(ts) == grp(ss), upper_t * lower_s, 0.0))
    return halves, jnp.concatenate(sels, axis=0), masks


def _decayed_scores(q, k, b, hier, hm):
    halves, sel_small, masks = hier
    chunk, w = q.shape
    small_refs = _dot_exact_lhs(sel_small, b)
    att, n_small = None, 0
    for h, mask in zip(halves, masks):
        if h < SUBLANES:
            ref = small_refs[n_small * chunk:(n_small + 1) * chunk, :]
            n_small += 1
        else:
            ref = jnp.concatenate([jnp.broadcast_to(b[g0 + h - 1:g0 + h, :], (2 * h, w))
                                   for g0 in range(0, chunk, 2 * h)], axis=0)
        qa = q * jnp.exp(jnp.minimum(b - ref, 0.0))
        ka = k * jnp.exp(jnp.minimum(ref - b, 0.0))
        term = _head_scores(qa, ka, hm) * mask
        att = term if att is None else att + term
    return att


def _gla_chunk(att, qe, ke2, v, decay_end, st, hm, bd):
    c = qe.shape[0]
    o_stack = jnp.dot(att.astype(BF16), v.astype(BF16), preferred_element_type=F32)
    o_intra = o_stack[0:c] * hm[0]
    for h in range(1, N_HEADS):
        o_intra = o_intra + o_stack[h * c:(h + 1) * c] * hm[h]
    o_inter = _dot_nt(qe.astype(BF16), st.astype(BF16))
    st_new = st * decay_end + _dot_tn(v.astype(BF16), ke2.astype(BF16)) * bd
    return o_inter + o_intra, st_new


def _head_rms_gate(o, gate, bd_mean_bf16):
    h, m, _ = _split3(o * o)
    ms = (jnp.dot(h, bd_mean_bf16, preferred_element_type=F32)
          + jnp.dot(m, bd_mean_bf16, preferred_element_type=F32))
    return (o * lax.rsqrt(ms + EPS)) * _silu(gate)


def _gla_prologue(st0_ref, y_ref, st_ref, chunk, n_chunks):
    @pl.when(pl.program_id(1) == 0)
    def _():
        st_ref[...] = st0_ref[...]
    tail = n_chunks * chunk
    if tail < y_ref.shape[1]:
        y_ref[:, tail:, :] = jnp.zeros((y_ref.shape[0], y_ref.shape[1] - tail, y_ref.shape[2]),
                                       y_ref.dtype)


def _hgrn_kernel(a_ref, lb_ref, st0_ref, y_ref, st_ref, *, chunk, n_chunks):
    n_b, _, w = y_ref.shape
    hm = _head_masks(w, N_HEADS)
    bd = _block_diag(w, N_HEADS, 1.0)
    bd_mean = _block_diag(w, N_HEADS, 1.0 / (w // N_HEADS)).astype(BF16)
    r = lax.broadcasted_iota(I32, (chunk, chunk), 0)
    s = lax.broadcasted_iota(I32, (chunk, chunk), 1)
    tri = (r >= s).astype(BF16)
    hier = _hier_constants(chunk)
    bd_ones = bd.astype(BF16)
    lb = lb_ref[...]
    _gla_prologue(st0_ref, y_ref, st_ref, chunk, n_chunks)

    def body(c, carry):
        rows = pl.ds(pl.multiple_of(c * chunk, chunk), chunk)
        for g in range(n_b):
            q = a_ref[g, rows, 0:w]
            f = lb + (1.0 - lb) * jax.nn.sigmoid(a_ref[g, rows, w:2 * w])
            k = 1.0 - f
            v = a_ref[g, rows, 2 * w:3 * w]
            gate = a_ref[g, rows, 3 * w:4 * w]
            b = _dot_exact_lhs(tri, jnp.log(f))
            b_end = b[chunk - 1:chunk, :]
            att = _decayed_scores(q, k, b, hier, hm)
            o, st_new = _gla_chunk(att, q * jnp.exp(b), k * jnp.exp(b_end - b), v, jnp.exp(b_end),
                                   st_ref[g], hm, bd)
            o = o + jnp.dot((q * k).astype(BF16), bd_ones, preferred_element_type=F32) * v
            st_ref[g] = st_new
            y_ref[g, rows, :] = _head_rms_gate(o, gate, bd_mean).astype(y_ref.dtype)
        return carry

    lax.fori_loop(0, n_chunks, body, 0)


def _ret_kernel(d_ref, cos_ref, sin_ref, eb_ref, ke2s_ref, dend_ref, gam_ref, st0_ref, y_ref, st_ref,
                *, chunk, n_chunks):
    n_b, _, w = y_ref.shape
    hd = w // N_HEADS
    hm = _head_masks(w, N_HEADS)
    bd = _block_diag(w, N_HEADS, 1.0)
    bd_mean = _block_diag(w, N_HEADS, 1.0 / hd).astype(BF16)
    lane = lax.broadcasted_iota(I32, (1, w), 1)
    first_half = (lane & (hd - 1)) < (hd // 2)
    eb = eb_ref[...]
    ke2s = ke2s_ref[...]
    dend = dend_ref[...]
    gam = gam_ref[...]
    _gla_prologue(st0_ref, y_ref, st_ref, chunk, n_chunks)

    def rope(x, cos, sin_signed):
        swapped = jnp.where(first_half, pltpu.roll(x, w - hd // 2, 1), pltpu.roll(x, hd // 2, 1))
        return x * cos + swapped * sin_signed

    def body(c, carry):
        rows = pl.ds(pl.multiple_of(c * chunk, chunk), chunk)
        cos = cos_ref[rows, :]
        sin = sin_ref[rows, :]
        for g in range(n_b):
            q = rope(d_ref[g, rows, 0:w], cos, sin)
            k = rope(d_ref[g, rows, w:2 * w], cos, sin) * (hd ** -0.5)
            v = d_ref[g, rows, 2 * w:3 * w]
            gate = d_ref[g, rows, 3 * w:4 * w]
            att = _head_scores(q, k, hm) * gam
            o, st_new = _gla_chunk(att, q * eb, k * ke2s, v, dend, st_ref[g], hm, bd)
            st_ref[g] = st_new
            y_ref[g, rows, :] = _head_rms_gate(o, gate, bd_mean).astype(y_ref.dtype)
        return carry

    lax.fori_loop(0, n_chunks, body, 0)


def _gla_grid(bsz, lp, chunk, n_chunks):
    n_b = next(n for n in (GLA_BATCH_ROWS, 2, 1) if bsz % n == 0)
    halves = n_chunks * chunk == lp and n_chunks % 2 == 0 and (lp // 2) % SUBLANES == 0
    n_seq = 2 if halves else 1
    return n_b, n_seq, n_chunks // n_seq


def _hgrn(p3, col_block, lb, st0, chunk, n_chunks):
    bsz, lp, _ = p3.shape
    w = lb.shape[1]
    n_b, n_seq, n_chunks_blk = _gla_grid(bsz, lp, chunk, n_chunks)
    rows = lp // n_seq
    return pl.pallas_call(
        functools.partial(_hgrn_kernel, chunk=chunk, n_chunks=n_chunks_blk),
        grid=(bsz // n_b, n_seq),
        in_specs=[pl.BlockSpec((n_b, rows, 4 * w), lambda b, s: (b, s, col_block)),
                  pl.BlockSpec((1, w), lambda b, s: (0, 0)),
                  pl.BlockSpec((n_b, w, w), lambda b, s: (b, 0, 0))],
        out_specs=[pl.BlockSpec((n_b, rows, w), lambda b, s: (b, s, 0)),
                   pl.BlockSpec((n_b, w, w), lambda b, s: (b, 0, 0))],
        out_shape=[jax.ShapeDtypeStruct((bsz, lp, w), Y_DTYPE),
                   jax.ShapeDtypeStruct((bsz, w, w), F32)],
        compiler_params=_cparams("parallel", "arbitrary"),
        name="hgrn",
    )(p3, lb, st0)


def _ret(p3, col_block, tables, st0, chunk, n_chunks):
    bsz, lp, _ = p3.shape
    w = st0.shape[1]
    n_b, n_seq, n_chunks_blk = _gla_grid(bsz, lp, chunk, n_chunks)
    rows = lp // n_seq
    cos, sin, eb, ke2s, dend, gam = tables
    full = lambda a: pl.BlockSpec(a.shape, lambda b, s: (0,) * a.ndim)
    per_seq = pl.BlockSpec((rows, w), lambda b, s: (s, 0))
    return pl.pallas_call(
        functools.partial(_ret_kernel, chunk=chunk, n_chunks=n_chunks_blk),
        grid=(bsz // n_b, n_seq),
        in_specs=[pl.BlockSpec((n_b, rows, 4 * w), lambda b, s: (b, s, col_block)),
                  per_seq, per_seq, full(eb), full(ke2s), full(dend), full(gam),
                  pl.BlockSpec((n_b, w, w), lambda b, s: (b, 0, 0))],
        out_specs=[pl.BlockSpec((n_b, rows, w), lambda b, s: (b, s, 0)),
                   pl.BlockSpec((n_b, w, w), lambda b, s: (b, 0, 0))],
        out_shape=[jax.ShapeDtypeStruct((bsz, lp, w), Y_DTYPE),
                   jax.ShapeDtypeStruct((bsz, w, w), F32)],
        compiler_params=_cparams("parallel", "arbitrary"),
        name="ret",
    )(p3, cos, sin, eb, ke2s, dend, gam, st0)


def _log_sigmoid(x):
    return -(jnp.maximum(-x, 0.0) + jnp.log(1.0 + jnp.exp(-jnp.abs(x))))


def _foxprep_kernel(z_ref, bias_ref, lf_ref, cc_ref, cr_ref, *, n_pass):
    n_b = z_ref.shape[0]
    nblk = z_ref.shape[1] // ROW_BLOCK
    r = lax.broadcasted_iota(I32, (ROW_BLOCK, ROW_BLOCK), 0)
    s = lax.broadcasted_iota(I32, (ROW_BLOCK, ROW_BLOCK), 1)
    tri = (r >= s).astype(BF16)
    bias = bias_ref[...]

    def body(i, carries):
        rows = pl.ds(pl.multiple_of(i * ROW_BLOCK, ROW_BLOCK), ROW_BLOCK)
        rowi = i * ROW_BLOCK + lax.broadcasted_iota(I32, (ROW_BLOCK, 1), 0)
        out = []
        for g in range(n_b):
            z = z_ref[g, rows, :]
            lf = jnp.where(rowi < n_pass, z, _log_sigmoid(z + bias))
            lf_ref[g, rows, :] = lf
            cs = _dot_exact_lhs(tri, lf) + carries[g]
            cc_ref[g, rows, :] = cs
            cr_ref[g, :, rows] = cs.T[BF0:BF0 + SUBLANES, :]
            out.append(cs[ROW_BLOCK - 1:ROW_BLOCK, :])
        return tuple(out)

    lax.fori_loop(0, nblk, body, tuple(jnp.zeros((1, LANES), F32) for _ in range(n_b)))
    pad = cr_ref.shape[2] - nblk * ROW_BLOCK
    if pad:
        cr_ref[:, :, nblk * ROW_BLOCK:] = jnp.zeros((n_b, SUBLANES, pad), F32)


def _foxprep(z3, z_col, bias, n_pass):
    bsz, lk, _ = z3.shape
    cr_cols = pl.cdiv(lk, KEY_STEP) * KEY_STEP
    n_b = next(n for n in (GLA_BATCH_ROWS, 2, 1) if bsz % n == 0)
    return pl.pallas_call(
        functools.partial(_foxprep_kernel, n_pass=n_pass),
        grid=(bsz // n_b,),
        in_specs=[pl.BlockSpec((n_b, lk, LANES), lambda b: (b, 0, z_col)),
                  pl.BlockSpec((1, LANES), lambda b: (0, 0))],
        out_specs=[pl.BlockSpec((n_b, lk, LANES), lambda b: (b, 0, 0)),
                   pl.BlockSpec((n_b, lk, LANES), lambda b: (b, 0, 0)),
                   pl.BlockSpec((n_b, SUBLANES, cr_cols), lambda b: (b, 0, 0))],
        out_shape=[jax.ShapeDtypeStruct((bsz, lk, LANES), F32),
                   jax.ShapeDtypeStruct((bsz, lk, LANES), F32),
                   jax.ShapeDtypeStruct((bsz, SUBLANES, cr_cols), F32)],
        compiler_params=_cparams("parallel"),
        name="foxprep",
    )(z3, bias)


MXU_COLS = 2 * LANES


def _query_block(lq):
    return MXU_COLS if lq > ROW_BLOCK else ROW_BLOCK


def _heads_per_dot(qb):
    assert MXU_COLS % qb == 0
    return MXU_COLS // qb


def _head_weights(x_t, rows_per_head, n_heads):
    hpd = _heads_per_dot(x_t.shape[1])
    rowh = lax.shift_right_logical(lax.broadcasted_iota(I32, (x_t.shape[0], 1), 0),
                                   _log2(rows_per_head))
    only = lambda h: jnp.where(rowh == h, x_t, 0.0)
    return [jnp.concatenate([only(d * hpd + i) for i in range(hpd)], axis=1).astype(BF16)
            for d in range(n_heads // hpd)]


def _attend_two_pass(n_steps, logits_fn, value_t_fn, shifts, s_scr, acc_scr, qb, hd):
    fold = lambda x, op: op(x.reshape(KEY_STEP // SUBLANES, SUBLANES, qb), axis=0)
    hv = hd + ONES_ROWS

    def max_step(i, ms):
        tiles = logits_fn(i)
        for h in range(N_HEADS):
            s_scr[i * N_HEADS + h] = tiles[h]
        return tuple(jnp.maximum(ms[h], fold(tiles[h], jnp.max)) for h in range(N_HEADS))

    ms = lax.fori_loop(0, n_steps, max_step,
                       tuple(jnp.full((SUBLANES, qb), M_FLOOR, F32) for _ in range(N_HEADS)))
    m_logit = [jnp.max(ms[h], axis=0, keepdims=True) + shifts[h] for h in range(N_HEADS)]
    acc_scr[...] = jnp.zeros(acc_scr.shape, F32)

    def sum_step(i, carry):
        for h in range(N_HEADS):
            p = jnp.exp2(s_scr[i * N_HEADS + h] - (m_logit[h] - shifts[h]))
            acc_scr[h * hv:(h + 1) * hv, :] += jnp.dot(value_t_fn(h, i), p.astype(BF16),
                                                       preferred_element_type=F32)
        return carry

    lax.fori_loop(0, n_steps, sum_step, 0)
    outs = []
    for h in range(N_HEADS):
        l = acc_scr[h * hv + hd:h * hv + hd + 1, :]
        outs.append(acc_scr[h * hv:h * hv + hd, :] / jnp.where(l > 0.0, l, 1.0))
    return jnp.concatenate(outs, axis=0)


def _fox_kernel(q_ref, k_ref, v_ref, g_ref, cc_ref, cr_ref, y_ref, *rest, key_lo, q_off, causal_blocks,
                emit_caches):
    if emit_caches:
        kc_ref, vc_ref, kbf_scr, vt_scr, ck_scr, s_scr, acc_scr = rest
    else:
        kbf_scr, vt_scr, ck_scr, s_scr, acc_scr = rest
    qb, w = q_ref.shape
    hd = w // N_HEADS
    hv = hd + ONES_ROWS
    nkb_total = k_ref.shape[0] // ROW_BLOCK
    hpd = _heads_per_dot(qb)
    j = pl.program_id(1)
    nkb = jnp.minimum((j + 1) * (qb // ROW_BLOCK), nkb_total) if causal_blocks else nkb_total

    @pl.when(j == 0)
    def _prepare_batch_row():
        def blk(i, carry):
            rows = pl.ds(pl.multiple_of(i * ROW_BLOCK, ROW_BLOCK), ROW_BLOCK)
            kbf_scr[rows, :] = k_ref[rows, :].astype(BF16)
            v_t = v_ref[rows, :].T.astype(BF16)
            cs = cc_ref[rows, :] * LOG2E
            for h in range(N_HEADS):
                vt_scr[h * hv:h * hv + hd, rows] = v_t[h * hd:(h + 1) * hd, :]
                ck_scr[h, rows, :] = jnp.broadcast_to(cs[:, BF0 + h:BF0 + h + 1], (ROW_BLOCK, LANES))
            return carry
        lax.fori_loop(0, nkb_total, blk, 0)
        if emit_caches:
            kc_ref[...] = k_ref[key_lo:, :]
            vc_ref[...] = v_ref[key_lo:, :]
        pad = vt_scr.shape[1] - nkb_total * ROW_BLOCK
        if pad:
            kbf_scr[nkb_total * ROW_BLOCK:, :] = jnp.zeros((pad, w), BF16)
            ck_scr[:, nkb_total * ROW_BLOCK:, :] = jnp.zeros((N_HEADS, pad, LANES), F32)
        for h in range(N_HEADS):
            if pad:
                vt_scr[h * hv:h * hv + hd, nkb_total * ROW_BLOCK:] = jnp.zeros((hd, pad), BF16)
            vt_scr[h * hv + hd:(h + 1) * hv, :] = jnp.ones((ONES_ROWS, vt_scr.shape[1]), BF16)

    wq = _head_weights((q_ref[...] * (hd ** -0.5 * LOG2E)).T, hd, N_HEADS)
    qcol = pl.ds(pl.multiple_of(j * qb + q_off, ROW_BLOCK), qb)
    cq = [cr_ref[h:h + 1, qcol] * LOG2E for h in range(N_HEADS)]
    qrow = j * qb + q_off + lax.broadcasted_iota(I32, (1, qb), 1)
    sub = lax.broadcasted_iota(I32, (KEY_STEP, 1), 0)

    def logits(i):
        k0 = pl.multiple_of(i * KEY_STEP, KEY_STEP)
        krows = pl.ds(k0, KEY_STEP)
        kblk = kbf_scr[krows, :]
        kidx = k0 + sub
        ok = (kidx >= key_lo) & (kidx <= qrow)
        tiles = []
        for d in range(N_HEADS // hpd):
            s2 = jnp.dot(kblk, wq[d], preferred_element_type=F32)
            for i2 in range(hpd):
                h = d * hpd + i2
                ck = jnp.concatenate([ck_scr[h, krows, :]] * (qb // LANES), axis=1)
                tiles.append(jnp.where(ok, s2[:, i2 * qb:(i2 + 1) * qb] - ck, NEG))
        return tiles

    def value_t(h, i):
        return vt_scr[h * hv:(h + 1) * hv, pl.ds(pl.multiple_of(i * KEY_STEP, KEY_STEP), KEY_STEP)]

    n_steps = lax.shift_right_logical(nkb + (STEP_BLOCKS - 1), _log2(STEP_BLOCKS))
    o_t = _attend_two_pass(n_steps, logits, value_t, cq, s_scr, acc_scr, qb, hd)
    y_ref[...] = (o_t.T * _silu(g_ref[...])).astype(y_ref.dtype)


def _fox(q_arr, q_col, k_arr, k_col, v_arr, v_col, g_arr, g_col, cc, cr, key_lo, q_off, causal_blocks,
         emit_caches):
    bsz, lq, _ = q_arr.shape
    lk = k_arr.shape[1]
    w = GROUP_W
    qb = _query_block(lq)
    key_rows = pl.cdiv(lk, KEY_STEP) * KEY_STEP
    assert cr.shape[2] >= q_off + pl.cdiv(lq, qb) * qb
    out_specs = [pl.BlockSpec((None, qb, w), lambda b, j: (b, j, 0))]
    out_shape = [jax.ShapeDtypeStruct((bsz, lq, w), Y_DTYPE)]
    if emit_caches:
        out_specs += [pl.BlockSpec((None, lk - key_lo, w), lambda b, j: (b, 0, 0))] * 2
        out_shape += [jax.ShapeDtypeStruct((bsz, lk - key_lo, w), F32)] * 2
    return pl.pallas_call(
        functools.partial(_fox_kernel, key_lo=key_lo, q_off=q_off, causal_blocks=causal_blocks,
                          emit_caches=emit_caches),
        grid=(bsz, pl.cdiv(lq, qb)),
        in_specs=[pl.BlockSpec((None, qb, w), lambda b, j: (b, j, q_col)),
                  pl.BlockSpec((None, lk, w), lambda b, j: (b, 0, k_col)),
                  pl.BlockSpec((None, lk, w), lambda b, j: (b, 0, v_col)),
                  pl.BlockSpec((None, qb, w), lambda b, j: (b, j, g_col)),
                  pl.BlockSpec((None, lk, LANES), lambda b, j: (b, 0, 0)),
                  pl.BlockSpec((None, SUBLANES, cr.shape[2]), lambda b, j: (b, 0, 0))],
        out_specs=out_specs,
        out_shape=out_shape,
        scratch_shapes=[pltpu.VMEM((key_rows, w), BF16),
                        pltpu.VMEM((w + N_HEADS * ONES_ROWS, key_rows), BF16),
                        pltpu.VMEM((N_HEADS, key_rows, LANES), F32),
                        pltpu.VMEM((N_HEADS * key_rows // KEY_STEP, KEY_STEP, qb), F32),
                        pltpu.VMEM((w + N_HEADS * ONES_ROWS, qb), F32)],
        compiler_params=_cparams("parallel", "arbitrary"),
        name="fox",
    )(q_arr, k_arr, v_arr, g_arr, cc, cr)


def _dsa_kernel(cq_ref, cg_ref, ciq_ref, mq_ref, ckv_ref, mk_ref, y_ref, *rest,
                k_top, key_lo, key_hi, chunk_causal, q_lo, q_hi, emit_caches):
    if emit_caches:
        kc_ref, vc_ref, key_scr, kvb_scr, vt_scr, mkb_scr, s_scr, acc_scr = rest
    else:
        key_scr, kvb_scr, vt_scr, mkb_scr, s_scr, acc_scr = rest
    qb, w = cq_ref.shape
    hd = w // N_HEADS
    hpd = _heads_per_dot(qb)
    nkb_total = ckv_ref.shape[0] // ROW_BLOCK
    j = pl.program_id(1)
    nkb = jnp.minimum((j + 1) * (qb // ROW_BLOCK), nkb_total) if chunk_causal else nkb_total
    idx_scale = (H_IDX * D_IDX) ** -0.5

    @pl.when(j == 0)
    def _prepare_batch_row():
        def blk(i, carry):
            rows = pl.ds(pl.multiple_of(i * ROW_BLOCK, ROW_BLOCK), ROW_BLOCK)
            kv = ckv_ref[rows, :]
            kvb_scr[rows, :] = kv.astype(BF16)
            vt_scr[0:hd, rows] = kv.T[hd:2 * hd, :].astype(BF16)
            mkb_scr[rows, :] = mk_ref[rows, :].astype(BF16)
            return carry
        lax.fori_loop(0, nkb_total, blk, 0)
        if emit_caches:
            kv_new = ckv_ref[key_lo:, :]
            kc_ref[...] = kv_new[:, 0:hd]
            vc_ref[...] = kv_new[:, hd:2 * hd]
        pad = vt_scr.shape[1] - nkb_total * ROW_BLOCK
        if pad:
            kvb_scr[nkb_total * ROW_BLOCK:, :] = jnp.zeros((pad, LANES), BF16)
            mkb_scr[nkb_total * ROW_BLOCK:, :] = jnp.zeros((pad, LANES), BF16)
            vt_scr[0:hd, nkb_total * ROW_BLOCK:] = jnp.zeros((hd, pad), BF16)
        vt_scr[hd:, :] = jnp.ones((ONES_ROWS, vt_scr.shape[1]), BF16)

    def pad_rows(x):
        return jnp.concatenate([x, jnp.zeros((LANES - x.shape[0], qb), F32)], axis=0)

    iq_t = ciq_ref[...].T
    iw_t = mq_ref[...].T[IW0:IW0 + H_IDX, :]
    q_t = (cq_ref[...] * (hd ** -0.5 * LOG2E)).T
    def side_by_side(x_t, rows, n_heads):
        return [jnp.concatenate([pad_rows(x_t[(d * hpd + i) * rows:(d * hpd + i + 1) * rows, :])
                                 for i in range(hpd)], axis=1).astype(BF16)
                for d in range(n_heads // hpd)]

    iq_rhs = side_by_side(iq_t, D_IDX, H_IDX)
    q_rhs = side_by_side(q_t, hd, N_HEADS)

    qrow = j * qb + lax.broadcasted_iota(I32, (1, qb), 1)
    if chunk_causal:
        hi = (lax.shift_right_logical(qrow, 6) + 1) * CHUNK
    else:
        hi = jnp.full((1, qb), key_hi, I32)
    sub = lax.broadcasted_iota(I32, (KEY_STEP, 1), 0)
    n_steps = lax.shift_right_logical(nkb + (STEP_BLOCKS - 1), _log2(STEP_BLOCKS))

    def score_step(i, carry):
        k0 = pl.multiple_of(i * KEY_STEP, KEY_STEP)
        mk = mkb_scr[pl.ds(k0, KEY_STEP), :]
        acc = jnp.zeros((KEY_STEP, qb), F32)
        for d in range(H_IDX // hpd):
            sc2 = jnp.dot(mk, iq_rhs[d], preferred_element_type=F32)
            for i2 in range(hpd):
                h = d * hpd + i2
                acc = acc + jnp.maximum(sc2[:, i2 * qb:(i2 + 1) * qb], 0.0) * iw_t[h:h + 1, :]
        score = acc * idx_scale + 0.0
        kidx = k0 + sub
        adm = (kidx >= key_lo) & (kidx < hi)
        u = pltpu.bitcast(score, I32)
        key = u ^ (lax.shift_right_arithmetic(u, 31) & np.int32(0x7FFFFFFF))
        key_scr[pl.ds(k0, KEY_STEP), :] = jnp.where(adm, key, INT_MIN)
        return carry

    lax.fori_loop(0, n_steps, score_step, 0)

    def count(pred):
        groups = qb // LANES

        def cb(i, c8s):
            k0 = pl.multiple_of(i * KEY_STEP, KEY_STEP)
            out = []
            for lg in range(groups):
                lanes = lambda x, lg=lg: x[:, lg * LANES:(lg + 1) * LANES]
                kk = key_scr[pl.ds(k0, KEY_STEP), lg * LANES:(lg + 1) * LANES]
                ind = jnp.where(pred(kk, k0, lanes), 1, 0).astype(I32)
                out.append(c8s[lg] + jnp.sum(ind.reshape(KEY_STEP // SUBLANES, SUBLANES, LANES), axis=0))
            return tuple(out)
        c8s = lax.fori_loop(0, n_steps, cb,
                            tuple(jnp.zeros((SUBLANES, LANES), I32) for _ in range(groups)))
        return jnp.concatenate([jnp.sum(c, axis=0, keepdims=True) for c in c8s], axis=1)

    def bisect(n_bits, count_ge, need):
        def bit_body(i, carry):
            ans, cnt_ans = carry
            cand = ans | lax.shift_left(np.int32(1), jnp.int32(n_bits - 1) - i)
            cnt = count_ge(cand)
            ok = cnt >= need
            return jnp.where(ok, cand, ans), jnp.where(ok, cnt, cnt_ans)
        return lax.fori_loop(0, n_bits, bit_body,
                             (jnp.zeros((1, qb), I32), jnp.full((1, qb), np.int32(2 ** 30), I32)))

    ans, cnt_thr = bisect(32, lambda cand: count(
        lambda kk, k0, lanes: kk >= lanes(cand ^ INT_MIN)), k_top)
    thr = ans ^ INT_MIN
    tie_flag = jnp.where((qrow >= q_lo) & (qrow < q_hi),
                         jnp.where((cnt_thr > k_top) & (thr != INT_MIN), 1, 0), 0)
    tie = tie_flag > 0

    @pl.when(jnp.max(tie_flag) > 0)
    def _break_ties():
        n_rev_bits = _log2(pl.next_power_of_2(key_scr.shape[0]))
        rev_base = np.int32(2 ** n_rev_bits - 1)
        need = k_top - count(lambda kk, k0, lanes: kk > lanes(thr))
        ans2, _ = bisect(n_rev_bits, lambda cand: count(
            lambda kk, k0, lanes: (kk == lanes(thr)) & ((rev_base - (k0 + sub)) >= lanes(cand))), need)

        def demote(i, carry):
            k0 = pl.multiple_of(i * KEY_STEP, KEY_STEP)
            kk = key_scr[pl.ds(k0, KEY_STEP), :]
            lose = tie & (kk == thr) & ((rev_base - (k0 + sub)) < ans2)
            key_scr[pl.ds(k0, KEY_STEP), :] = jnp.where(lose, INT_MIN, kk)
            return carry

        lax.fori_loop(0, n_steps, demote, 0)

    thr_sel = jnp.maximum(thr, INT_MIN + 1)

    def logits(i):
        krows = pl.ds(pl.multiple_of(i * KEY_STEP, KEY_STEP), KEY_STEP)
        sel = key_scr[krows, :] >= thr_sel
        kv = kvb_scr[krows, :]
        tiles = []
        for d in range(N_HEADS // hpd):
            s2 = jnp.dot(kv, q_rhs[d], preferred_element_type=F32)
            tiles += [jnp.where(sel, s2[:, i2 * qb:(i2 + 1) * qb], NEG) for i2 in range(hpd)]
        return tiles

    def value_t(h, i):
        return vt_scr[:, pl.ds(pl.multiple_of(i * KEY_STEP, KEY_STEP), KEY_STEP)]

    no_shift = [jnp.zeros((1, qb), F32)] * N_HEADS
    o_t = _attend_two_pass(n_steps, logits, value_t, no_shift, s_scr, acc_scr, qb, hd)
    y_ref[...] = (o_t.T * _silu(cg_ref[...])).astype(y_ref.dtype)


def _dsa(p3, cols, ckv_arr, ckv_col, mk_arr, mk_col, k_top, key_lo, key_hi, chunk_causal, q_lo, q_hi,
         emit_caches):
    bsz, lq, _ = p3.shape
    lk = ckv_arr.shape[1]
    w = GROUP_W
    hd = w // N_HEADS
    qb = _query_block(lq)
    cq_col, cg_col, ciq_col, mq_col = cols
    key_rows = pl.cdiv(lk, KEY_STEP) * KEY_STEP
    out_specs = [pl.BlockSpec((None, qb, w), lambda b, j: (b, j, 0))]
    out_shape = [jax.ShapeDtypeStruct((bsz, lq, w), Y_DTYPE)]
    if emit_caches:
        out_specs += [pl.BlockSpec((None, lk - key_lo, hd), lambda b, j: (b, 0, 0))] * 2
        out_shape += [jax.ShapeDtypeStruct((bsz, lk - key_lo, hd), F32)] * 2
    return pl.pallas_call(
        functools.partial(_dsa_kernel, k_top=k_top, key_lo=key_lo, key_hi=key_hi,
                          chunk_causal=chunk_causal, q_lo=q_lo, q_hi=q_hi, emit_caches=emit_caches),
        grid=(bsz, pl.cdiv(lq, qb)),
        in_specs=[pl.BlockSpec((None, qb, w), lambda b, j: (b, j, cq_col)),
                  pl.BlockSpec((None, qb, w), lambda b, j: (b, j, cg_col)),
                  pl.BlockSpec((None, qb, w), lambda b, j: (b, j, ciq_col)),
                  pl.BlockSpec((None, qb, LANES), lambda b, j: (b, j, mq_col)),
                  pl.BlockSpec((None, lk, LANES), lambda b, j: (b, 0, ckv_col)),
                  pl.BlockSpec((None, lk, LANES), lambda b, j: (b, 0, mk_col))],
        out_specs=out_specs,
        out_shape=out_shape,
        scratch_shapes=[pltpu.VMEM((key_rows, qb), I32),
                        pltpu.VMEM((key_rows, LANES), BF16),
                        pltpu.VMEM((w // N_HEADS + ONES_ROWS, key_rows), BF16),
                        pltpu.VMEM((key_rows, LANES), BF16),
                        pltpu.VMEM((N_HEADS * key_rows // KEY_STEP, KEY_STEP, qb), F32),
                        pltpu.VMEM((w + N_HEADS * ONES_ROWS, qb), F32)],
        compiler_params=_cparams("parallel", "arbitrary"),
        name="dsa",
    )(p3, p3, p3, p3, ckv_arr, mk_arr)


def _merge_kernel(ya_ref, yb_ref, yc_ref, yd_ref, x_ref, w_ref, g_ref, o_ref, *, period, valid_lo, valid_hi):
    tm = x_ref.shape[0]
    gw = ya_ref.shape[1]
    acc = jnp.zeros(o_ref.shape, F32)
    for i, y_ref in enumerate((ya_ref, yb_ref, yc_ref, yd_ref)):
        acc = acc + jnp.dot(y_ref[...].astype(BF16), w_ref[i * gw:(i + 1) * gw, :],
                            preferred_element_type=F32)
    ms = jnp.mean(acc * acc, axis=-1, keepdims=True)
    out = x_ref[...] + (acc * lax.rsqrt(ms + EPS)) * g_ref[...]
    r0 = pl.program_id(0) * tm
    local = (r0 - (r0 // period) * period) + lax.broadcasted_iota(I32, (tm, 1), 0)
    local = jnp.where(local >= period, local - period, local)
    valid = (local >= valid_lo) & (local < valid_hi)
    o_ref[...] = jnp.where(valid, out, 0.0)


def _merge(ys, x2d, w_bf16, g, tm, period, valid_lo, valid_hi):
    rows, d = x2d.shape
    assert tm <= period
    gw = ys[0].shape[1]
    yspec = pl.BlockSpec((tm, gw), lambda i: (i, 0))
    return pl.pallas_call(
        functools.partial(_merge_kernel, period=period, valid_lo=valid_lo, valid_hi=valid_hi),
        grid=(rows // tm,),
        in_specs=[yspec, yspec, yspec, yspec,
                  pl.BlockSpec((tm, d), lambda i: (i, 0)),
                  pl.BlockSpec(w_bf16.shape, lambda i: (0, 0)),
                  pl.BlockSpec((1, d), lambda i: (0, 0))],
        out_specs=pl.BlockSpec((tm, d), lambda i: (i, 0)),
        out_shape=jax.ShapeDtypeStruct((rows, d), F32),
        compiler_params=_cparams("parallel"),
        name="merge",
    )(*ys, x2d, w_bf16, g)


def _row_tile(rows, at_most=512):
    return next(t for t in (512, 256, ROW_BLOCK) if rows % t == 0 and t <= at_most)


def _column_layout(gw):
    sizes = [gw] * 4 + [gw, gw, gw, FOX_HEADS, gw] + [gw, gw // 4, gw // 4, gw, H_IDX * D_IDX, D_IDX, H_IDX] + [gw] * 4
    names = ["aq", "af", "ai", "ag", "bq", "bk", "bv", "bf", "bg",
             "cq", "ck", "cv", "cg", "ciq", "cik", "ciw", "dq", "dk", "dv", "dg"]
    start = dict(zip(names, np.cumsum([0] + sizes[:-1])))
    size = dict(zip(names, sizes))
    src = -np.ones((16 * gw,), np.int64)
    def put(dst, name, off=0):
        src[dst + off:dst + off + size[name]] = np.arange(start[name], start[name] + size[name])
    for i, n in enumerate(["aq", "af", "ai", "ag", "bq", "bk", "bv", "bg", "cq", "cg", "ciq"]):
        put(i * gw, n)
    ckv0 = 11 * gw
    put(ckv0, "ck")
    put(ckv0 + gw // 4, "cv")
    misc0 = ckv0 + LANES
    put(misc0, "cik", IK0)
    put(misc0, "ciw", IW0)
    put(misc0, "bf", BF0)
    for i, n in enumerate(["dq", "dk", "dv", "dg"]):
        put(12 * gw + i * gw, n)
    return src


def _relayout_w_in(w_in_l, src):
    cols = jnp.take(w_in_l, jnp.asarray(np.maximum(src, 0)), axis=1)
    return jnp.where(jnp.asarray(src >= 0)[None, :], cols, 0.0).astype(BF16)


def _ret_tables(pos, chunk, gw):
    hd = gw // N_HEADS
    half = hd // 2
    inv = ROPE_BASE ** (-jnp.arange(half, dtype=F32) / half)
    ang = pos.astype(F32)[:, None] * inv[None, :]
    cos_h = jnp.concatenate([jnp.cos(ang), jnp.cos(ang)], axis=-1)
    sin_h = jnp.concatenate([-jnp.sin(ang), jnp.sin(ang)], axis=-1)
    cos = jnp.tile(cos_h, (1, N_HEADS))
    sin = jnp.tile(sin_h, (1, N_HEADS))
    lg = jnp.log(1.0 - 2.0 ** (-5.0 - jnp.arange(N_HEADS, dtype=F32)))
    lg_l = jnp.repeat(lg, hd)[None, :]
    t = jnp.arange(chunk, dtype=F32)[:, None]
    eb = jnp.exp((t + 1.0) * lg_l)
    ke2s = jnp.exp((chunk - 1.0 - t) * lg_l)
    dend = jnp.exp(chunk * lg_l)
    dt = jnp.arange(chunk, dtype=F32)[:, None] - jnp.arange(chunk, dtype=F32)[None, :]
    gam = jnp.concatenate([jnp.where(dt >= 0, jnp.exp(dt * lg[h]), 0.0) for h in range(N_HEADS)], axis=0)
    return cos, sin, eb, ke2s, dend, gam


def _state_to_bd(state):
    bsz, h, k, v = state.shape
    eye = jnp.eye(h, dtype=state.dtype)
    st = jnp.einsum('bhkv,hg->bhvgk', state, eye)
    return st.reshape(bsz, h * v, h * k)


def _bd_to_state(st, h):
    bsz, hv, hk = st.shape
    st5 = st.reshape(bsz, h, hv // h, h, hk // h)
    diag = jnp.stack([st5[:, i, :, i, :] for i in range(h)], axis=1)
    return jnp.swapaxes(diag, 2, 3)


def kernel(x_prompt, x_sample, state_hgrn, cache_fox_k, cache_fox_v, cache_fox_logf, cache_dsa_k,
           cache_dsa_v, cache_dsa_idx_k, state_ret, meta_tokens, w_in, w_out, fox_bias, hgrn_lb,
           norm_pre, norm_post):
    bsz, seq, d = x_prompt.shape
    dbsz, t_new, _ = x_sample.shape
    depth = w_in.shape[0]
    past = cache_fox_k.shape[2]
    gw = d // N_GROUPS
    hd = gw // N_HEADS
    assert gw == 2 * LANES and seq % ROW_BLOCK == 0 and past % ROW_BLOCK == 0 and t_new <= ROW_BLOCK
    assert t_new % SUBLANES == 0

    pad_front = ROW_BLOCK - N_META
    lp = ROW_BLOCK + seq
    ls = ROW_BLOCK
    lks = past + ROW_BLOCK
    n_chunks_p = lp // CHUNK
    k_top_p = min(TOP_K_MAX, seq // 4)
    k_top_s = min(TOP_K_MAX, (past + t_new) // 4)

    src = _column_layout(gw)
    n_cols = src.shape[0]
    col = {"a": 0, "bq": 4, "bk": 5, "bv": 6, "bg": 7, "cq": 8, "cg": 9, "ciq": 10, "d": 3}
    ckv_col = (11 * gw) // LANES
    misc_col = ckv_col + 1

    sm = jax.nn.softmax(hgrn_lb.astype(F32), axis=0)
    lbs = jnp.cumsum(sm, axis=0) - sm[0:1]

    xp = jnp.concatenate([jnp.zeros((bsz, pad_front, d), F32),
                          jnp.broadcast_to(meta_tokens.astype(F32)[None], (bsz, N_META, d)),
                          x_prompt], axis=1)
    xs = jnp.concatenate([x_sample, jnp.zeros((dbsz, ls - t_new, d), F32)], axis=1)

    tab_p = _ret_tables(jnp.arange(lp) - ROW_BLOCK, CHUNK, gw)
    tab_s = _ret_tables(past + jnp.arange(ls), t_new, gw)
    zero_state_p = jnp.zeros((bsz, gw, gw), F32)

    outs_p = {k: [] for k in ("hgrn", "fk", "fv", "fl", "ck", "cv", "ci", "ret")}
    outs_s = {k: [] for k in ("hgrn", "fk", "fv", "fl", "ck", "cv", "ci", "ret")}

    for l in range(depth):
        w_l = _relayout_w_in(w_in[l], src)
        w_o = w_out[l].astype(BF16)
        g_pre = norm_pre[l][None, :]
        g_post = norm_post[l][None, :]
        lb = lbs[l][None, :]
        bias = jnp.zeros((1, LANES), F32).at[0, BF0:BF0 + FOX_HEADS].set(fox_bias[l].astype(F32))

        p = _project(xp.reshape(bsz * lp, d), g_pre, w_l, _row_tile(bsz * lp)).reshape(bsz, lp, n_cols)
        ya, st_a = _hgrn(p, 0, lb, zero_state_p, CHUNK, n_chunks_p)
        yd, st_d = _ret(p, 3, tab_p, zero_state_p, CHUNK, n_chunks_p)
        lf, cc, cr = _foxprep(p, misc_col, bias, 0)
        yb, fox_k_rows, fox_v_rows = _fox(p, col["bq"], p, col["bk"], p, col["bv"], p, col["bg"], cc, cr,
                                          pad_front, 0, True, True)
        yc, dsa_k_rows, dsa_v_rows = _dsa(p, (col["cq"], col["cg"], col["ciq"], misc_col), p, ckv_col,
                                          p, misc_col, k_top_p, pad_front, 0, True, pad_front, lp, True)
        flat = lambda a: a.reshape(bsz * lp, gw)
        xp = _merge([flat(ya), flat(yb), flat(yc), flat(yd)], xp.reshape(bsz * lp, d), w_o, g_post,
                    _row_tile(bsz * lp, lp), lp, pad_front, lp).reshape(bsz, lp, d)
        pv = p[:, pad_front:, :]
        outs_p["hgrn"].append(_bd_to_state(st_a, N_HEADS))
        outs_p["ret"].append(_bd_to_state(st_d, N_HEADS))
        outs_p["fk"].append(fox_k_rows.reshape(bsz, -1, N_HEADS, hd))
        outs_p["fv"].append(fox_v_rows.reshape(bsz, -1, N_HEADS, hd))
        outs_p["fl"].append(lf[:, pad_front:, BF0:BF0 + FOX_HEADS])
        outs_p["ck"].append(dsa_k_rows)
        outs_p["cv"].append(dsa_v_rows)
        outs_p["ci"].append(pv[:, :, misc_col * LANES + IK0:misc_col * LANES + IK0 + D_IDX])

        ps = _project(xs.reshape(dbsz * ls, d), g_pre, w_l, _row_tile(dbsz * ls)).reshape(dbsz, ls, n_cols)
        ya, st_a = _hgrn(ps, 0, lb, _state_to_bd(state_hgrn[l].astype(F32)), t_new, 1)
        yd, st_d = _ret(ps, 3, tab_s, _state_to_bd(state_ret[l].astype(F32)), t_new, 1)
        z = jnp.concatenate(
            [jnp.pad(cache_fox_logf[l].astype(F32), ((0, 0), (0, 0), (BF0, LANES - BF0 - FOX_HEADS))),
             ps[:, :, misc_col * LANES:(misc_col + 1) * LANES]], axis=1)
        lf, cc, cr = _foxprep(z, 0, bias, past)
        k_all = jnp.concatenate([cache_fox_k[l].reshape(dbsz, past, gw), ps[:, :, 5 * gw:6 * gw]], axis=1)
        v_all = jnp.concatenate([cache_fox_v[l].reshape(dbsz, past, gw), ps[:, :, 6 * gw:7 * gw]], axis=1)
        (yb,) = _fox(ps, col["bq"], k_all, 0, v_all, 0, ps, col["bg"], cc, cr, 0, past, False, False)
        ckv_all = jnp.concatenate(
            [jnp.concatenate([cache_dsa_k[l], cache_dsa_v[l]], axis=-1).astype(F32),
             ps[:, :, 11 * gw:11 * gw + LANES]], axis=1)
        mk_all = jnp.concatenate(
            [jnp.pad(cache_dsa_idx_k[l].astype(F32), ((0, 0), (0, 0), (IK0, LANES - IK0 - D_IDX))),
             ps[:, :, misc_col * LANES:(misc_col + 1) * LANES]], axis=1)
        (yc,) = _dsa(ps, (col["cq"], col["cg"], col["ciq"], misc_col), ckv_all, 0, mk_all, 0,
                     k_top_s, 0, past + t_new, False, 0, t_new, False)
        flat = lambda a: a.reshape(dbsz * ls, gw)
        xs = _merge([flat(ya), flat(yb), flat(yc), flat(yd)], xs.reshape(dbsz * ls, d), w_o, g_post,
                    ls, ls, 0, t_new).reshape(dbsz, ls, d)
        pn = ps[:, :t_new, :]
        outs_s["hgrn"].append(_bd_to_state(st_a, N_HEADS))
        outs_s["ret"].append(_bd_to_state(st_d, N_HEADS))
        outs_s["fk"].append(pn[:, :, 5 * gw:6 * gw].reshape(dbsz, -1, N_HEADS, hd))
        outs_s["fv"].append(pn[:, :, 6 * gw:7 * gw].reshape(dbsz, -1, N_HEADS, hd))
        outs_s["fl"].append(lf[:, past:past + t_new, BF0:BF0 + FOX_HEADS])
        outs_s["ck"].append(pn[:, :, 11 * gw:11 * gw + hd])
        outs_s["cv"].append(pn[:, :, 11 * gw + hd:11 * gw + 2 * hd])
        outs_s["ci"].append(pn[:, :, misc_col * LANES + IK0:misc_col * LANES + IK0 + D_IDX])

    dt = x_prompt.dtype
    st = lambda xs_list: jnp.stack(xs_list, axis=0).astype(dt)
    order = ("hgrn", "fk", "fv", "fl", "ck", "cv", "ci", "ret")
    return ((xp[:, ROW_BLOCK:, :].astype(dt), xs[:, :t_new, :].astype(dt))
            + tuple(st(outs_p[k]) for k in order) + tuple(st(outs_s[k]) for k in order))
```

```python
import functools

import numpy as np
import jax
import jax.numpy as jnp
from jax import lax
from jax.experimental import pallas as pl
from jax.experimental.pallas import tpu as pltpu

F32 = jnp.float32
BF16 = jnp.bfloat16
Y_DTYPE = BF16
I32 = jnp.int32
LOG2E = float(np.log2(np.e))
ONES_ROWS = 16

N_META = 16
CHUNK = 64
N_GROUPS = 4
N_HEADS = 4
H_IDX = 8
D_IDX = 32
TOP_K_MAX = 256
ROPE_BASE = 10000.0
EPS = 1e-6
FOX_HEADS = 4

LANES = 128
SUBLANES = 8
ROW_BLOCK = 128
GROUP_W = 2 * LANES
VMEM_LIMIT_BYTES = 56 * 1024 * 1024

IK0 = 0
IW0 = 32
BF0 = 64

NEG = -1e30
M_FLOOR = -1e20
INT_MIN = np.int32(-2 ** 31)
STEP_BLOCKS = 4
KEY_STEP = STEP_BLOCKS * ROW_BLOCK
GLA_BATCH_ROWS = 4


def _cparams(*sem):
    return pltpu.CompilerParams(dimension_semantics=sem, vmem_limit_bytes=VMEM_LIMIT_BYTES)


def _split3(x):
    h = x.astype(BF16)
    r = x - h.astype(F32)
    m = r.astype(BF16)
    lo = (r - m.astype(F32)).astype(BF16)
    return h, m, lo


def _dot_exact_lhs(a_bf16, x):
    d = lambda y: jnp.dot(a_bf16, y, preferred_element_type=F32)
    h, m, lo = _split3(x)
    return d(h) + d(m) + d(lo)


def _dot_nt(a, b):
    return lax.dot_general(a, b, (((1,), (1,)), ((), ())), preferred_element_type=F32)


def _dot_tn(a, b):
    return lax.dot_general(a, b, (((0,), (0,)), ((), ())), preferred_element_type=F32)


def _log2(n):
    assert n > 0 and n & (n - 1) == 0, n
    return n.bit_length() - 1


def _head_masks(width, n_heads):
    lane = lax.broadcasted_iota(I32, (1, width), 1)
    sh = _log2(width // n_heads)
    return [(lax.shift_right_logical(lane, sh) == h).astype(F32) for h in range(n_heads)]


def _block_diag(width, n_heads, value):
    r = lax.broadcasted_iota(I32, (width, width), 0)
    c = lax.broadcasted_iota(I32, (width, width), 1)
    sh = _log2(width // n_heads)
    same = lax.shift_right_logical(r, sh) == lax.shift_right_logical(c, sh)
    return jnp.where(same, value, 0.0).astype(F32)


def _silu(x):
    return x * jax.nn.sigmoid(x)


def _proj_kernel(x_ref, g_ref, w_ref, o_ref, *, col_chunk):
    x = x_ref[...]
    ms = jnp.mean(x * x, axis=-1, keepdims=True)
    xn = ((x * lax.rsqrt(ms + EPS)) * g_ref[...]).astype(BF16)
    for c in range(o_ref.shape[1] // col_chunk):
        cols = slice(c * col_chunk, (c + 1) * col_chunk)
        o_ref[:, cols] = jnp.dot(xn, w_ref[:, cols], preferred_element_type=F32)


def _project(x2d, g, w_bf16, tm):
    rows, d = x2d.shape
    n = w_bf16.shape[1]
    return pl.pallas_call(
        functools.partial(_proj_kernel, col_chunk=1024),
        grid=(rows // tm,),
        in_specs=[pl.BlockSpec((tm, d), lambda i: (i, 0)),
                  pl.BlockSpec((1, d), lambda i: (0, 0)),
                  pl.BlockSpec((d, n), lambda i: (0, 0))],
        out_specs=pl.BlockSpec((tm, n), lambda i: (i, 0)),
        out_shape=jax.ShapeDtypeStruct((rows, n), F32),
        compiler_params=_cparams("parallel"),
        name="proj",
    )(x2d, g, w_bf16)


def _head_scores(qa, ka, hm):
    q_stack = jnp.concatenate([qa * hm[h] for h in range(N_HEADS)], axis=0).astype(BF16)
    return _dot_nt(q_stack, ka.astype(BF16))


def _hier_constants(chunk):
    halves = [chunk >> (i + 1) for i in range(_log2(chunk))]
    r = lax.broadcasted_iota(I32, (chunk, chunk), 0)
    c = lax.broadcasted_iota(I32, (chunk, chunk), 1)
    ts = lax.broadcasted_iota(I32, (N_HEADS * chunk, chunk), 0) & (chunk - 1)
    ss = lax.broadcasted_iota(I32, (N_HEADS * chunk, chunk), 1)
    sels, masks = [], []
    for h in halves:
        sh = _log2(h)
        grp = lambda x: lax.shift_right_logical(x, sh + 1)
        if h < SUBLANES:
            sels.append((c == lax.shift_left(grp(r), sh + 1) + (h - 1)).astype(BF16))
        upper_t = (lax.shift_right_logical(ts, sh) & 1).astype(F32)
        lower_s = 1.0 - (lax.shift_right_logical(ss, sh) & 1).astype(F32)
        masks.append(jnp.where(grp(ts) == grp(ss), upper_t * lower_s, 0.0))
    return halves, jnp.concatenate(sels, axis=0), masks


def _decayed_scores(q, k, b, hier, hm):
    halves, sel_small, masks = hier
    chunk, w = q.shape
    small_refs = _dot_exact_lhs(sel_small, b)
    att, n_small = None, 0
    for h, mask in zip(halves, masks):
        if h < SUBLANES:
            ref = small_refs[n_small * chunk:(n_small + 1) * chunk, :]
            n_small += 1
        else:
            ref = jnp.concatenate([jnp.broadcast_to(b[g0 + h - 1:g0 + h, :], (2 * h, w))
                                   for g0 in range(0, chunk, 2 * h)], axis=0)
        qa = q * jnp.exp(jnp.minimum(b - ref, 0.0))
        ka = k * jnp.exp(jnp.minimum(ref - b, 0.0))
        term = _head_scores(qa, ka, hm) * mask
        att = term if att is None else att + term
    return att


def _gla_chunk(att, qe, ke2, v, decay_end, st, hm, bd):
    c = qe.shape[0]
    o_stack = jnp.dot(att.astype(BF16), v.astype(BF16), preferred_element_type=F32)
    o_intra = o_stack[0:c] * hm[0]
    for h in range(1, N_HEADS):
        o_intra = o_intra + o_stack[h * c:(h + 1) * c] * hm[h]
    o_inter = _dot_nt(qe.astype(BF16), st.astype(BF16))
    st_new = st * decay_end + _dot_tn(v.astype(BF16), ke2.astype(BF16)) * bd
    return o_inter + o_intra, st_new


def _head_rms_gate(o, gate, bd_mean_bf16):
    h, m, _ = _split3(o * o)
    ms = (jnp.dot(h, bd_mean_bf16, preferred_element_type=F32)
          + jnp.dot(m, bd_mean_bf16, preferred_element_type=F32))
    return (o * lax.rsqrt(ms + EPS)) * _silu(gate)


def _gla_prologue(st0_ref, y_ref, st_ref, chunk, n_chunks):
    @pl.when(pl.program_id(1) == 0)
    def _():
        st_ref[...] = st0_ref[...]
    tail = n_chunks * chunk
    if tail < y_ref.shape[1]:
        y_ref[:, tail:, :] = jnp.zeros((y_ref.shape[0], y_ref.shape[1] - tail, y_ref.shape[2]),
                                       y_ref.dtype)


def _hgrn_kernel(a_ref, lb_ref, st0_ref, y_ref, st_ref, *, chunk, n_chunks):
    n_b, _, w = y_ref.shape
    hm = _head_masks(w, N_HEADS)
    bd = _block_diag(w, N_HEADS, 1.0)
    bd_mean = _block_diag(w, N_HEADS, 1.0 / (w // N_HEADS)).astype(BF16)
    r = lax.broadcasted_iota(I32, (chunk, chunk), 0)
    s = lax.broadcasted_iota(I32, (chunk, chunk), 1)
    tri = (r >= s).astype(BF16)
    hier = _hier_constants(chunk)
    bd_ones = bd.astype(BF16)
    lb = lb_ref[...]
    _gla_prologue(st0_ref, y_ref, st_ref, chunk, n_chunks)

    def body(c, carry):
        rows = pl.ds(pl.multiple_of(c * chunk, chunk), chunk)
        for g in range(n_b):
            q = a_ref[g, rows, 0:w]
            f = lb + (1.0 - lb) * jax.nn.sigmoid(a_ref[g, rows, w:2 * w])
            k = 1.0 - f
            v = a_ref[g, rows, 2 * w:3 * w]
            gate = a_ref[g, rows, 3 * w:4 * w]
            b = _dot_exact_lhs(tri, jnp.log(f))
            b_end = b[chunk - 1:chunk, :]
            att = _decayed_scores(q, k, b, hier, hm)
            o, st_new = _gla_chunk(att, q * jnp.exp(b), k * jnp.exp(b_end - b), v, jnp.exp(b_end),
                                   st_ref[g], hm, bd)
            o = o + jnp.dot((q * k).astype(BF16), bd_ones, preferred_element_type=F32) * v
            st_ref[g] = st_new
            y_ref[g, rows, :] = _head_rms_gate(o, gate, bd_mean).astype(y_ref.dtype)
        return carry

    lax.fori_loop(0, n_chunks, body, 0)


def _ret_kernel(d_ref, cos_ref, sin_ref, eb_ref, ke2s_ref, dend_ref, gam_ref, st0_ref, y_ref, st_ref,
                *, chunk, n_chunks):
    n_b, _, w = y_ref.shape
    hd = w // N_HEADS
    hm = _head_masks(w, N_HEADS)
    bd = _block_diag(w, N_HEADS, 1.0)
    bd_mean = _block_diag(w, N_HEADS, 1.0 / hd).astype(BF16)
    lane = lax.broadcasted_iota(I32, (1, w), 1)
    first_half = (lane & (hd - 1)) < (hd // 2)
    eb = eb_ref[...]
    ke2s = ke2s_ref[...]
    dend = dend_ref[...]
    gam = gam_ref[...]
    _gla_prologue(st0_ref, y_ref, st_ref, chunk, n_chunks)

    def rope(x, cos, sin_signed):
        swapped = jnp.where(first_half, pltpu.roll(x, w - hd // 2, 1), pltpu.roll(x, hd // 2, 1))
        return x * cos + swapped * sin_signed

    def body(c, carry):
        rows = pl.ds(pl.multiple_of(c * chunk, chunk), chunk)
        cos = cos_ref[rows, :]
        sin = sin_ref[rows, :]
        for g in range(n_b):
            q = rope(d_ref[g, rows, 0:w], cos, sin)
            k = rope(d_ref[g, rows, w:2 * w], cos, sin) * (hd ** -0.5)
            v = d_ref[g, rows, 2 * w:3 * w]
            gate = d_ref[g, rows, 3 * w:4 * w]
            att = _head_scores(q, k, hm) * gam
            o, st_new = _gla_chunk(att, q * eb, k * ke2s, v, dend, st_ref[g], hm, bd)
            st_ref[g] = st_new
            y_ref[g, rows, :] = _head_rms_gate(o, gate, bd_mean).astype(y_ref.dtype)
        return carry

    lax.fori_loop(0, n_chunks, body, 0)


def _gla_grid(bsz, lp, chunk, n_chunks):
    n_b = next(n for n in (GLA_BATCH_ROWS, 2, 1) if bsz % n == 0)
    halves = n_chunks * chunk == lp and n_chunks % 2 == 0 and (lp // 2) % SUBLANES == 0
    n_seq = 2 if halves else 1
    return n_b, n_seq, n_chunks // n_seq


def _hgrn(p3, col_block, lb, st0, chunk, n_chunks):
    bsz, lp, _ = p3.shape
    w = lb.shape[1]
    n_b, n_seq, n_chunks_blk = _gla_grid(bsz, lp, chunk, n_chunks)
    rows = lp // n_seq
    return pl.pallas_call(
        functools.partial(_hgrn_kernel, chunk=chunk, n_chunks=n_chunks_blk),
        grid=(bsz // n_b, n_seq),
        in_specs=[pl.BlockSpec((n_b, rows, 4 * w), lambda b, s: (b, s, col_block)),
                  pl.BlockSpec((1, w), lambda b, s: (0, 0)),
                  pl.BlockSpec((n_b, w, w), lambda b, s: (b, 0, 0))],
        out_specs=[pl.BlockSpec((n_b, rows, w), lambda b, s: (b, s, 0)),
                   pl.BlockSpec((n_b, w, w), lambda b, s: (b, 0, 0))],
        out_shape=[jax.ShapeDtypeStruct((bsz, lp, w), Y_DTYPE),
                   jax.ShapeDtypeStruct((bsz, w, w), F32)],
        compiler_params=_cparams("parallel", "arbitrary"),
        name="hgrn",
    )(p3, lb, st0)


def _ret(p3, col_block, tables, st0, chunk, n_chunks):
    bsz, lp, _ = p3.shape
    w = st0.shape[1]
    n_b, n_seq, n_chunks_blk = _gla_grid(bsz, lp, chunk, n_chunks)
    rows = lp // n_seq
    cos, sin, eb, ke2s, dend, gam = tables
    full = lambda a: pl.BlockSpec(a.shape, lambda b, s: (0,) * a.ndim)
    per_seq = pl.BlockSpec((rows, w), lambda b, s: (s, 0))
    return pl.pallas_call(
        functools.partial(_ret_kernel, chunk=chunk, n_chunks=n_chunks_blk),
        grid=(bsz // n_b, n_seq),
        in_specs=[pl.BlockSpec((n_b, rows, 4 * w), lambda b, s: (b, s, col_block)),
                  per_seq, per_seq, full(eb), full(ke2s), full(dend), full(gam),
                  pl.BlockSpec((n_b, w, w), lambda b, s: (b, 0, 0))],
        out_specs=[pl.BlockSpec((n_b, rows, w), lambda b, s: (b, s, 0)),
                   pl.BlockSpec((n_b, w, w), lambda b, s: (b, 0, 0))],
        out_shape=[jax.ShapeDtypeStruct((bsz, lp, w), Y_DTYPE),
                   jax.ShapeDtypeStruct((bsz, w, w), F32)],
        compiler_params=_cparams("parallel", "arbitrary"),
        name="ret",
    )(p3, cos, sin, eb, ke2s, dend, gam, st0)


def _log_sigmoid(x):
    return -(jnp.maximum(-x, 0.0) + jnp.log(1.0 + jnp.exp(-jnp.abs(x))))


def _foxprep_kernel(z_ref, bias_ref, lf_ref, cc_ref, cr_ref, *, n_pass):
    n_b = z_ref.shape[0]
    nblk = z_ref.shape[1] // ROW_BLOCK
    r = lax.broadcasted_iota(I32, (ROW_BLOCK, ROW_BLOCK), 0)
    s = lax.broadcasted_iota(I32, (ROW_BLOCK, ROW_BLOCK), 1)
    tri = (r >= s).astype(BF16)
    bias = bias_ref[...]

    def body(i, carries):
        rows = pl.ds(pl.multiple_of(i * ROW_BLOCK, ROW_BLOCK), ROW_BLOCK)
        rowi = i * ROW_BLOCK + lax.broadcasted_iota(I32, (ROW_BLOCK, 1), 0)
        out = []
        for g in range(n_b):
            z = z_ref[g, rows, :]
            lf = jnp.where(rowi < n_pass, z, _log_sigmoid(z + bias))
            lf_ref[g, rows, :] = lf
            cs = _dot_exact_lhs(tri, lf) + carries[g]
            cc_ref[g, rows, :] = cs
            cr_ref[g, :, rows] = cs.T[BF0:BF0 + SUBLANES, :]
            out.append(cs[ROW_BLOCK - 1:ROW_BLOCK, :])
        return tuple(out)

    lax.fori_loop(0, nblk, body, tuple(jnp.zeros((1, LANES), F32) for _ in range(n_b)))
    pad = cr_ref.shape[2] - nblk * ROW_BLOCK
    if pad:
        cr_ref[:, :, nblk * ROW_BLOCK:] = jnp.zeros((n_b, SUBLANES, pad), F32)


def _foxprep(z3, z_col, bias, n_pass):
    bsz, lk, _ = z3.shape
    cr_cols = pl.cdiv(lk, KEY_STEP) * KEY_STEP
    n_b = next(n for n in (GLA_BATCH_ROWS, 2, 1) if bsz % n == 0)
    return pl.pallas_call(
        functools.partial(_foxprep_kernel, n_pass=n_pass),
        grid=(bsz // n_b,),
        in_specs=[pl.BlockSpec((n_b, lk, LANES), lambda b: (b, 0, z_col)),
                  pl.BlockSpec((1, LANES), lambda b: (0, 0))],
        out_specs=[pl.BlockSpec((n_b, lk, LANES), lambda b: (b, 0, 0)),
                   pl.BlockSpec((n_b, lk, LANES), lambda b: (b, 0, 0)),
                   pl.BlockSpec((n_b, SUBLANES, cr_cols), lambda b: (b, 0, 0))],
        out_shape=[jax.ShapeDtypeStruct((bsz, lk, LANES), F32),
                   jax.ShapeDtypeStruct((bsz, lk, LANES), F32),
                   jax.ShapeDtypeStruct((bsz, SUBLANES, cr_cols), F32)],
        compiler_params=_cparams("parallel"),
        name="foxprep",
    )(z3, bias)


MXU_COLS = 2 * LANES


def _query_block(lq):
    return MXU_COLS if lq > ROW_BLOCK else ROW_BLOCK


def _heads_per_dot(qb):
    assert MXU_COLS % qb == 0
    return MXU_COLS // qb


def _head_weights(x_t, rows_per_head, n_heads):
    hpd = _heads_per_dot(x_t.shape[1])
    rowh = lax.shift_right_logical(lax.broadcasted_iota(I32, (x_t.shape[0], 1), 0),
                                   _log2(rows_per_head))
    only = lambda h: jnp.where(rowh == h, x_t, 0.0)
    return [jnp.concatenate([only(d * hpd + i) for i in range(hpd)], axis=1).astype(BF16)
            for d in range(n_heads // hpd)]


def _attend_two_pass(n_steps, logits_fn, value_t_fn, shifts, s_scr, acc_scr, qb, hd):
    fold = lambda x, op: op(x.reshape(KEY_STEP // SUBLANES, SUBLANES, qb), axis=0)
    hv = hd + ONES_ROWS

    def max_step(i, ms):
        tiles = logits_fn(i)
        for h in range(N_HEADS):
            s_scr[i * N_HEADS + h] = tiles[h]
        return tuple(jnp.maximum(ms[h], fold(tiles[h], jnp.max)) for h in range(N_HEADS))

    ms = lax.fori_loop(0, n_steps, max_step,
                       tuple(jnp.full((SUBLANES, qb), M_FLOOR, F32) for _ in range(N_HEADS)))
    m_logit = [jnp.max(ms[h], axis=0, keepdims=True) + shifts[h] for h in range(N_HEADS)]
    acc_scr[...] = jnp.zeros(acc_scr.shape, F32)

    def sum_step(i, carry):
        for h in range(N_HEADS):
            p = jnp.exp2(s_scr[i * N_HEADS + h] - (m_logit[h] - shifts[h]))
            acc_scr[h * hv:(h + 1) * hv, :] += jnp.dot(value_t_fn(h, i), p.astype(BF16),
                                                       preferred_element_type=F32)
        return carry

    lax.fori_loop(0, n_steps, sum_step, 0)
    outs = []
    for h in range(N_HEADS):
        l = acc_scr[h * hv + hd:h * hv + hd + 1, :]
        outs.append(acc_scr[h * hv:h * hv + hd, :] / jnp.where(l > 0.0, l, 1.0))
    return jnp.concatenate(outs, axis=0)


def _fox_kernel(q_ref, k_ref, v_ref, g_ref, cc_ref, cr_ref, y_ref, *rest, key_lo, q_off, causal_blocks,
                emit_caches):
    if emit_caches:
        kc_ref, vc_ref, kbf_scr, vt_scr, ck_scr, s_scr, acc_scr = rest
    else:
        kbf_scr, vt_scr, ck_scr, s_scr, acc_scr = rest
    qb, w = q_ref.shape
    hd = w // N_HEADS
    hv = hd + ONES_ROWS
    nkb_total = k_ref.shape[0] // ROW_BLOCK
    hpd = _heads_per_dot(qb)
    j = pl.program_id(1)
    nkb = jnp.minimum((j + 1) * (qb // ROW_BLOCK), nkb_total) if causal_blocks else nkb_total

    @pl.when(j == 0)
    def _prepare_batch_row():
        def blk(i, carry):
            rows = pl.ds(pl.multiple_of(i * ROW_BLOCK, ROW_BLOCK), ROW_BLOCK)
            kbf_scr[rows, :] = k_ref[rows, :].astype(BF16)
            v_t = v_ref[rows, :].T.astype(BF16)
            cs = cc_ref[rows, :] * LOG2E
            for h in range(N_HEADS):
                vt_scr[h * hv:h * hv + hd, rows] = v_t[h * hd:(h + 1) * hd, :]
                ck_scr[h, rows, :] = jnp.broadcast_to(cs[:, BF0 + h:BF0 + h + 1], (ROW_BLOCK, LANES))
            return carry
        lax.fori_loop(0, nkb_total, blk, 0)
        if emit_caches:
            kc_ref[...] = k_ref[key_lo:, :]
            vc_ref[...] = v_ref[key_lo:, :]
        pad = vt_scr.shape[1] - nkb_total * ROW_BLOCK
        if pad:
            kbf_scr[nkb_total * ROW_BLOCK:, :] = jnp.zeros((pad, w), BF16)
            ck_scr[:, nkb_total * ROW_BLOCK:, :] = jnp.zeros((N_HEADS, pad, LANES), F32)
        for h in range(N_HEADS):
            if pad:
                vt_scr[h * hv:h * hv + hd, nkb_total * ROW_BLOCK:] = jnp.zeros((hd, pad), BF16)
            vt_scr[h * hv + hd:(h + 1) * hv, :] = jnp.ones((ONES_ROWS, vt_scr.shape[1]), BF16)

    wq = _head_weights((q_ref[...] * (hd ** -0.5 * LOG2E)).T, hd, N_HEADS)
    qcol = pl.ds(pl.multiple_of(j * qb + q_off, ROW_BLOCK), qb)
    cq = [cr_ref[h:h + 1, qcol] * LOG2E for h in range(N_HEADS)]
    qrow = j * qb + q_off + lax.broadcasted_iota(I32, (1, qb), 1)
    sub = lax.broadcasted_iota(I32, (KEY_STEP, 1), 0)

    def logits(i):
        k0 = pl.multiple_of(i * KEY_STEP, KEY_STEP)
        krows = pl.ds(k0, KEY_STEP)
        kblk = kbf_scr[krows, :]
        kidx = k0 + sub
        ok = (kidx >= key_lo) & (kidx <= qrow)
        tiles = []
        for d in range(N_HEADS // hpd):
            s2 = jnp.dot(kblk, wq[d], preferred_element_type=F32)
            for i2 in range(hpd):
                h = d * hpd + i2
                ck = jnp.concatenate([ck_scr[h, krows, :]] * (qb // LANES), axis=1)
                tiles.append(jnp.where(ok, s2[:, i2 * qb:(i2 + 1) * qb] - ck, NEG))
        return tiles

    def value_t(h, i):
        return vt_scr[h * hv:(h + 1) * hv, pl.ds(pl.multiple_of(i * KEY_STEP, KEY_STEP), KEY_STEP)]

    n_steps = lax.shift_right_logical(nkb + (STEP_BLOCKS - 1), _log2(STEP_BLOCKS))
    o_t = _attend_two_pass(n_steps, logits, value_t, cq, s_scr, acc_scr, qb, hd)
    y_ref[...] = (o_t.T * _silu(g_ref[...])).astype(y_ref.dtype)


def _fox(q_arr, q_col, k_arr, k_col, v_arr, v_col, g_arr, g_col, cc, cr, key_lo, q_off, causal_blocks,
         emit_caches):
    bsz, lq, _ = q_arr.shape
    lk = k_arr.shape[1]
    w = GROUP_W
    qb = _query_block(lq)
    key_rows = pl.cdiv(lk, KEY_STEP) * KEY_STEP
    assert cr.shape[2] >= q_off + pl.cdiv(lq, qb) * qb
    out_specs = [pl.BlockSpec((None, qb, w), lambda b, j: (b, j, 0))]
    out_shape = [jax.ShapeDtypeStruct((bsz, lq, w), Y_DTYPE)]
    if emit_caches:
        out_specs += [pl.BlockSpec((None, lk - key_lo, w), lambda b, j: (b, 0, 0))] * 2
        out_shape += [jax.ShapeDtypeStruct((bsz, lk - key_lo, w), F32)] * 2
    return pl.pallas_call(
        functools.partial(_fox_kernel, key_lo=key_lo, q_off=q_off, causal_blocks=causal_blocks,
                          emit_caches=emit_caches),
        grid=(bsz, pl.cdiv(lq, qb)),
        in_specs=[pl.BlockSpec((None, qb, w), lambda b, j: (b, j, q_col)),
                  pl.BlockSpec((None, lk, w), lambda b, j: (b, 0, k_col)),
                  pl.BlockSpec((None, lk, w), lambda b, j: (b, 0, v_col)),
                  pl.BlockSpec((None, qb, w), lambda b, j: (b, j, g_col)),
                  pl.BlockSpec((None, lk, LANES), lambda b, j: (b, 0, 0)),
                  pl.BlockSpec((None, SUBLANES, cr.shape[2]), lambda b, j: (b, 0, 0))],
        out_specs=out_specs,
        out_shape=out_shape,
        scratch_shapes=[pltpu.VMEM((key_rows, w), BF16),
                        pltpu.VMEM((w + N_HEADS * ONES_ROWS, key_rows), BF16),
                        pltpu.VMEM((N_HEADS, key_rows, LANES), F32),
                        pltpu.VMEM((N_HEADS * key_rows // KEY_STEP, KEY_STEP, qb), F32),
                        pltpu.VMEM((w + N_HEADS * ONES_ROWS, qb), F32)],
        compiler_params=_cparams("parallel", "arbitrary"),
        name="fox",
    )(q_arr, k_arr, v_arr, g_arr, cc, cr)


def _dsa_kernel(cq_ref, cg_ref, ciq_ref, mq_ref, ckv_ref, mk_ref, y_ref, *rest,
                k_top, key_lo, key_hi, chunk_causal, q_lo, q_hi, emit_caches):
    if emit_caches:
        kc_ref, vc_ref, key_scr, kvb_scr, vt_scr, mkb_scr, s_scr, acc_scr = rest
    else:
        key_scr, kvb_scr, vt_scr, mkb_scr, s_scr, acc_scr = rest
    qb, w = cq_ref.shape
    hd = w // N_HEADS
    hpd = _heads_per_dot(qb)
    nkb_total = ckv_ref.shape[0] // ROW_BLOCK
    j = pl.program_id(1)
    nkb = jnp.minimum((j + 1) * (qb // ROW_BLOCK), nkb_total) if chunk_causal else nkb_total
    idx_scale = (H_IDX * D_IDX) ** -0.5

    @pl.when(j == 0)
    def _prepare_batch_row():
        def blk(i, carry):
            rows = pl.ds(pl.multiple_of(i * ROW_BLOCK, ROW_BLOCK), ROW_BLOCK)
            kv = ckv_ref[rows, :]
            kvb_scr[rows, :] = kv.astype(BF16)
            vt_scr[0:hd, rows] = kv.T[hd:2 * hd, :].astype(BF16)
            mkb_scr[rows, :] = mk_ref[rows, :].astype(BF16)
            return carry
        lax.fori_loop(0, nkb_total, blk, 0)
        if emit_caches:
            kv_new = ckv_ref[key_lo:, :]
            kc_ref[...] = kv_new[:, 0:hd]
            vc_ref[...] = kv_new[:, hd:2 * hd]
        pad = vt_scr.shape[1] - nkb_total * ROW_BLOCK
        if pad:
            kvb_scr[nkb_total * ROW_BLOCK:, :] = jnp.zeros((pad, LANES), BF16)
            mkb_scr[nkb_total * ROW_BLOCK:, :] = jnp.zeros((pad, LANES), BF16)
            vt_scr[0:hd, nkb_total * ROW_BLOCK:] = jnp.zeros((hd, pad), BF16)
        vt_scr[hd:, :] = jnp.ones((ONES_ROWS, vt_scr.shape[1]), BF16)

    def pad_rows(x):
        return jnp.concatenate([x, jnp.zeros((LANES - x.shape[0], qb), F32)], axis=0)

    iq_t = ciq_ref[...].T
    iw_t = mq_ref[...].T[IW0:IW0 + H_IDX, :]
    q_t = (cq_ref[...] * (hd ** -0.5 * LOG2E)).T
    def side_by_side(x_t, rows, n_heads):
        return [jnp.concatenate([pad_rows(x_t[(d * hpd + i) * rows:(d * hpd + i + 1) * rows, :])
                                 for i in range(hpd)], axis=1).astype(BF16)
                for d in range(n_heads // hpd)]

    iq_rhs = side_by_side(iq_t, D_IDX, H_IDX)
    q_rhs = side_by_side(q_t, hd, N_HEADS)

    qrow = j * qb + lax.broadcasted_iota(I32, (1, qb), 1)
    if chunk_causal:
        hi = (lax.shift_right_logical(qrow, 6) + 1) * CHUNK
    else:
        hi = jnp.full((1, qb), key_hi, I32)
    sub = lax.broadcasted_iota(I32, (KEY_STEP, 1), 0)
    n_steps = lax.shift_right_logical(nkb + (STEP_BLOCKS - 1), _log2(STEP_BLOCKS))

    def score_step(i, carry):
        k0 = pl.multiple_of(i * KEY_STEP, KEY_STEP)
        mk = mkb_scr[pl.ds(k0, KEY_STEP), :]
        acc = jnp.zeros((KEY_STEP, qb), F32)
        for d in range(H_IDX // hpd):
            sc2 = jnp.dot(mk, iq_rhs[d], preferred_element_type=F32)
            for i2 in range(hpd):
                h = d * hpd + i2
                acc = acc + jnp.maximum(sc2[:, i2 * qb:(i2 + 1) * qb], 0.0) * iw_t[h:h + 1, :]
        score = acc * idx_scale + 0.0
        kidx = k0 + sub
        adm = (kidx >= key_lo) & (kidx < hi)
        u = pltpu.bitcast(score, I32)
        key = u ^ (lax.shift_right_arithmetic(u, 31) & np.int32(0x7FFFFFFF))
        key_scr[pl.ds(k0, KEY_STEP), :] = jnp.where(adm, key, INT_MIN)
        return carry

    lax.fori_loop(0, n_steps, score_step, 0)

    def count(pred):
        groups = qb // LANES

        def cb(i, c8s):
            k0 = pl.multiple_of(i * KEY_STEP, KEY_STEP)
            out = []
            for lg in range(groups):
                lanes = lambda x, lg=lg: x[:, lg * LANES:(lg + 1) * LANES]
                kk = key_scr[pl.ds(k0, KEY_STEP), lg * LANES:(lg + 1) * LANES]
                ind = jnp.where(pred(kk, k0, lanes), 1, 0).astype(I32)
                out.append(c8s[lg] + jnp.sum(ind.reshape(KEY_STEP // SUBLANES, SUBLANES, LANES), axis=0))
            return tuple(out)
        c8s = lax.fori_loop(0, n_steps, cb,
                            tuple(jnp.zeros((SUBLANES, LANES), I32) for _ in range(groups)))
        return jnp.concatenate([jnp.sum(c, axis=0, keepdims=True) for c in c8s], axis=1)

    def bisect(n_bits, count_ge, need):
        def bit_body(i, carry):
            ans, cnt_ans = carry
            cand = ans | lax.shift_left(np.int32(1), jnp.int32(n_bits - 1) - i)
            cnt = count_ge(cand)
            ok = cnt >= need
            return jnp.where(ok, cand, ans), jnp.where(ok, cnt, cnt_ans)
        return lax.fori_loop(0, n_bits, bit_body,
                             (jnp.zeros((1, qb), I32), jnp.full((1, qb), np.int32(2 ** 30), I32)))

    ans, cnt_thr = bisect(32, lambda cand: count(
        lambda kk, k0, lanes: kk >= lanes(cand ^ INT_MIN)), k_top)
    thr = ans ^ INT_MIN
    tie_flag = jnp.where((qrow >= q_lo) & (qrow < q_hi),
                         jnp.where((cnt_thr > k_top) & (thr != INT_MIN), 1, 0), 0)
    tie = tie_flag > 0

    @pl.when(jnp.max(tie_flag) > 0)
    def _break_ties():
        n_rev_bits = _log2(pl.next_power_of_2(key_scr.shape[0]))
        rev_base = np.int32(2 ** n_rev_bits - 1)
        need = k_top - count(lambda kk, k0, lanes: kk > lanes(thr))
        ans2, _ = bisect(n_rev_bits, lambda cand: count(
            lambda kk, k0, lanes: (kk == lanes(thr)) & ((rev_base - (k0 + sub)) >= lanes(cand))), need)

        def demote(i, carry):
            k0 = pl.multiple_of(i * KEY_STEP, KEY_STEP)
            kk = key_scr[pl.ds(k0, KEY_STEP), :]
            lose = tie & (kk == thr) & ((rev_base - (k0 + sub)) < ans2)
            key_scr[pl.ds(k0, KEY_STEP), :] = jnp.where(lose, INT_MIN, kk)
            return carry

        lax.fori_loop(0, n_steps, demote, 0)

    thr_sel = jnp.maximum(thr, INT_MIN + 1)

    def logits(i):
        krows = pl.ds(pl.multiple_of(i * KEY_STEP, KEY_STEP), KEY_STEP)
        sel = key_scr[krows, :] >= thr_sel
        kv = kvb_scr[krows, :]
        tiles = []
        for d in range(N_HEADS // hpd):
            s2 = jnp.dot(kv, q_rhs[d], preferred_element_type=F32)
            tiles += [jnp.where(sel, s2[:, i2 * qb:(i2 + 1) * qb], NEG) for i2 in range(hpd)]
        return tiles

    def value_t(h, i):
        return vt_scr[:, pl.ds(pl.multiple_of(i * KEY_STEP, KEY_STEP), KEY_STEP)]

    no_shift = [jnp.zeros((1, qb), F32)] * N_HEADS
    o_t = _attend_two_pass(n_steps, logits, value_t, no_shift, s_scr, acc_scr, qb, hd)
    y_ref[...] = (o_t.T * _silu(cg_ref[...])).astype(y_ref.dtype)


def _dsa(p3, cols, ckv_arr, ckv_col, mk_arr, mk_col, k_top, key_lo, key_hi, chunk_causal, q_lo, q_hi,
         emit_caches):
    bsz, lq, _ = p3.shape
    lk = ckv_arr.shape[1]
    w = GROUP_W
    hd = w // N_HEADS
    qb = _query_block(lq)
    cq_col, cg_col, ciq_col, mq_col = cols
    key_rows = pl.cdiv(lk, KEY_STEP) * KEY_STEP
    out_specs = [pl.BlockSpec((None, qb, w), lambda b, j: (b, j, 0))]
    out_shape = [jax.ShapeDtypeStruct((bsz, lq, w), Y_DTYPE)]
    if emit_caches:
        out_specs += [pl.BlockSpec((None, lk - key_lo, hd), lambda b, j: (b, 0, 0))] * 2
        out_shape += [jax.ShapeDtypeStruct((bsz, lk - key_lo, hd), F32)] * 2
    return pl.pallas_call(
        functools.partial(_dsa_kernel, k_top=k_top, key_lo=key_lo, key_hi=key_hi,
                          chunk_causal=chunk_causal, q_lo=q_lo, q_hi=q_hi, emit_caches=emit_caches),
        grid=(bsz, pl.cdiv(lq, qb)),
        in_specs=[pl.BlockSpec((None, qb, w), lambda b, j: (b, j, cq_col)),
                  pl.BlockSpec((None, qb, w), lambda b, j: (b, j, cg_col)),
                  pl.BlockSpec((None, qb, w), lambda b, j: (b, j, ciq_col)),
                  pl.BlockSpec((None, qb, LANES), lambda b, j: (b, j, mq_col)),
                  pl.BlockSpec((None, lk, LANES), lambda b, j: (b, 0, ckv_col)),
                  pl.BlockSpec((None, lk, LANES), lambda b, j: (b, 0, mk_col))],
        out_specs=out_specs,
        out_shape=out_shape,
        scratch_shapes=[pltpu.VMEM((key_rows, qb), I32),
                        pltpu.VMEM((key_rows, LANES), BF16),
                        pltpu.VMEM((w // N_HEADS + ONES_ROWS, key_rows), BF16),
                        pltpu.VMEM((key_rows, LANES), BF16),
                        pltpu.VMEM((N_HEADS * key_rows // KEY_STEP, KEY_STEP, qb), F32),
                        pltpu.VMEM((w + N_HEADS * ONES_ROWS, qb), F32)],
        compiler_params=_cparams("parallel", "arbitrary"),
        name="dsa",
    )(p3, p3, p3, p3, ckv_arr, mk_arr)


def _merge_kernel(ya_ref, yb_ref, yc_ref, yd_ref, x_ref, w_ref, g_ref, o_ref, *, period, valid_lo, valid_hi):
    tm = x_ref.shape[0]
    gw = ya_ref.shape[1]
    acc = jnp.zeros(o_ref.shape, F32)
    for i, y_ref in enumerate((ya_ref, yb_ref, yc_ref, yd_ref)):
        acc = acc + jnp.dot(y_ref[...].astype(BF16), w_ref[i * gw:(i + 1) * gw, :],
                            preferred_element_type=F32)
    ms = jnp.mean(acc * acc, axis=-1, keepdims=True)
    out = x_ref[...] + (acc * lax.rsqrt(ms + EPS)) * g_ref[...]
    r0 = pl.program_id(0) * tm
    local = (r0 - (r0 // period) * period) + lax.broadcasted_iota(I32, (tm, 1), 0)
    local = jnp.where(local >= period, local - period, local)
    valid = (local >= valid_lo) & (local < valid_hi)
    o_ref[...] = jnp.where(valid, out, 0.0)


def _merge(ys, x2d, w_bf16, g, tm, period, valid_lo, valid_hi):
    rows, d = x2d.shape
    assert tm <= period
    gw = ys[0].shape[1]
    yspec = pl.BlockSpec((tm, gw), lambda i: (i, 0))
    return pl.pallas_call(
        functools.partial(_merge_kernel, period=period, valid_lo=valid_lo, valid_hi=valid_hi),
        grid=(rows // tm,),
        in_specs=[yspec, yspec, yspec, yspec,
                  pl.BlockSpec((tm, d), lambda i: (i, 0)),
                  pl.BlockSpec(w_bf16.shape, lambda i: (0, 0)),
                  pl.BlockSpec((1, d), lambda i: (0, 0))],
        out_specs=pl.BlockSpec((tm, d), lambda i: (i, 0)),
        out_shape=jax.ShapeDtypeStruct((rows, d), F32),
        compiler_params=_cparams("parallel"),
        name="merge",
    )(*ys, x2d, w_bf16, g)


def _row_tile(rows, at_most=512):
    return next(t for t in (1024, 512, 256, ROW_BLOCK) if rows % t == 0 and t <= at_most)


def _column_layout(gw):
    sizes = [gw] * 4 + [gw, gw, gw, FOX_HEADS, gw] + [gw, gw // 4, gw // 4, gw, H_IDX * D_IDX, D_IDX, H_IDX] + [gw] * 4
    names = ["aq", "af", "ai", "ag", "bq", "bk", "bv", "bf", "bg",
             "cq", "ck", "cv", "cg", "ciq", "cik", "ciw", "dq", "dk", "dv", "dg"]
    start = dict(zip(names, np.cumsum([0] + sizes[:-1])))
    size = dict(zip(names, sizes))
    src = -np.ones((16 * gw,), np.int64)
    def put(dst, name, off=0):
        src[dst + off:dst + off + size[name]] = np.arange(start[name], start[name] + size[name])
    for i, n in enumerate(["aq", "af", "ai", "ag", "bq", "bk", "bv", "bg", "cq", "cg", "ciq"]):
        put(i * gw, n)
    ckv0 = 11 * gw
    put(ckv0, "ck")
    put(ckv0 + gw // 4, "cv")
    misc0 = ckv0 + LANES
    put(misc0, "cik", IK0)
    put(misc0, "ciw", IW0)
    put(misc0, "bf", BF0)
    for i, n in enumerate(["dq", "dk", "dv", "dg"]):
        put(12 * gw + i * gw, n)
    return src


def _relayout_w_in(w_in_l, src):
    cols = jnp.take(w_in_l, jnp.asarray(np.maximum(src, 0)), axis=1)
    return jnp.where(jnp.asarray(src >= 0)[None, :], cols, 0.0).astype(BF16)


def _ret_tables(pos, chunk, gw):
    hd = gw // N_HEADS
    half = hd // 2
    inv = ROPE_BASE ** (-jnp.arange(half, dtype=F32) / half)
    ang = pos.astype(F32)[:, None] * inv[None, :]
    cos_h = jnp.concatenate([jnp.cos(ang), jnp.cos(ang)], axis=-1)
    sin_h = jnp.concatenate([-jnp.sin(ang), jnp.sin(ang)], axis=-1)
    cos = jnp.tile(cos_h, (1, N_HEADS))
    sin = jnp.tile(sin_h, (1, N_HEADS))
    lg = jnp.log(1.0 - 2.0 ** (-5.0 - jnp.arange(N_HEADS, dtype=F32)))
    lg_l = jnp.repeat(lg, hd)[None, :]
    t = jnp.arange(chunk, dtype=F32)[:, None]
    eb = jnp.exp((t + 1.0) * lg_l)
    ke2s = jnp.exp((chunk - 1.0 - t) * lg_l)
    dend = jnp.exp(chunk * lg_l)
    dt = jnp.arange(chunk, dtype=F32)[:, None] - jnp.arange(chunk, dtype=F32)[None, :]
    gam = jnp.concatenate([jnp.where(dt >= 0, jnp.exp(dt * lg[h]), 0.0) for h in range(N_HEADS)], axis=0)
    return cos, sin, eb, ke2s, dend, gam


def _state_to_bd(state):
    bsz, h, k, v = state.shape
    eye = jnp.eye(h, dtype=state.dtype)
    st = jnp.einsum('bhkv,hg->bhvgk', state, eye)
    return st.reshape(bsz, h * v, h * k)


def _bd_to_state(st, h):
    bsz, hv, hk = st.shape
    st5 = st.reshape(bsz, h, hv // h, h, hk // h)
    diag = jnp.stack([st5[:, i, :, i, :] for i in range(h)], axis=1)
    return jnp.swapaxes(diag, 2, 3)


def kernel(x_prompt, x_sample, state_hgrn, cache_fox_k, cache_fox_v, cache_fox_logf, cache_dsa_k,
           cache_dsa_v, cache_dsa_idx_k, state_ret, meta_tokens, w_in, w_out, fox_bias, hgrn_lb,
           norm_pre, norm_post):
    bsz, seq, d = x_prompt.shape
    dbsz, t_new, _ = x_sample.shape
    depth = w_in.shape[0]
    past = cache_fox_k.shape[2]
    gw = d // N_GROUPS
    hd = gw // N_HEADS
    assert gw == 2 * LANES and seq % ROW_BLOCK == 0 and past % ROW_BLOCK == 0 and t_new <= ROW_BLOCK
    assert t_new % SUBLANES == 0

    pad_front = ROW_BLOCK - N_META
    lp = ROW_BLOCK + seq
    ls = ROW_BLOCK
    lks = past + ROW_BLOCK
    n_chunks_p = lp // CHUNK
    k_top_p = min(TOP_K_MAX, seq // 4)
    k_top_s = min(TOP_K_MAX, (past + t_new) // 4)

    src = _column_layout(gw)
    n_cols = src.shape[0]
    col = {"a": 0, "bq": 4, "bk": 5, "bv": 6, "bg": 7, "cq": 8, "cg": 9, "ciq": 10, "d": 3}
    ckv_col = (11 * gw) // LANES
    misc_col = ckv_col + 1

    sm = jax.nn.softmax(hgrn_lb.astype(F32), axis=0)
    lbs = jnp.cumsum(sm, axis=0) - sm[0:1]

    xp = jnp.concatenate([jnp.zeros((bsz, pad_front, d), F32),
                          jnp.broadcast_to(meta_tokens.astype(F32)[None], (bsz, N_META, d)),
                          x_prompt], axis=1)
    xs = jnp.concatenate([x_sample, jnp.zeros((dbsz, ls - t_new, d), F32)], axis=1)

    tab_p = _ret_tables(jnp.arange(lp) - ROW_BLOCK, CHUNK, gw)
    tab_s = _ret_tables(past + jnp.arange(ls), t_new, gw)
    zero_state_p = jnp.zeros((bsz, gw, gw), F32)

    outs_p = {k: [] for k in ("hgrn", "fk", "fv", "fl", "ck", "cv", "ci", "ret")}
    outs_s = {k: [] for k in ("hgrn", "fk", "fv", "fl", "ck", "cv", "ci", "ret")}

    for l in range(depth):
        w_l = _relayout_w_in(w_in[l], src)
        w_o = w_out[l].astype(BF16)
        g_pre = norm_pre[l][None, :]
        g_post = norm_post[l][None, :]
        lb = lbs[l][None, :]
        bias = jnp.zeros((1, LANES), F32).at[0, BF0:BF0 + FOX_HEADS].set(fox_bias[l].astype(F32))

        p = _project(xp.reshape(bsz * lp, d), g_pre, w_l, _row_tile(bsz * lp)).reshape(bsz, lp, n_cols)
        ya, st_a = _hgrn(p, 0, lb, zero_state_p, CHUNK, n_chunks_p)
        yd, st_d = _ret(p, 3, tab_p, zero_state_p, CHUNK, n_chunks_p)
        lf, cc, cr = _foxprep(p, misc_col, bias, 0)
        yb, fox_k_rows, fox_v_rows = _fox(p, col["bq"], p, col["bk"], p, col["bv"], p, col["bg"], cc, cr,
                                          pad_front, 0, True, True)
        yc, dsa_k_rows, dsa_v_rows = _dsa(p, (col["cq"], col["cg"], col["ciq"], misc_col), p, ckv_col,
                                          p, misc_col, k_top_p, pad_front, 0, True, pad_front, lp, True)
        flat = lambda a: a.reshape(bsz * lp, gw)
        xp = _merge([flat(ya), flat(yb), flat(yc), flat(yd)], xp.reshape(bsz * lp, d), w_o, g_post,
                    _row_tile(bsz * lp, min(lp, 1024)), lp, pad_front, lp).reshape(bsz, lp, d)
        pv = p[:, pad_front:, :]
        outs_p["hgrn"].append(_bd_to_state(st_a, N_HEADS))
        outs_p["ret"].append(_bd_to_state(st_d, N_HEADS))
        outs_p["fk"].append(fox_k_rows.reshape(bsz, -1, N_HEADS, hd))
        outs_p["fv"].append(fox_v_rows.reshape(bsz, -1, N_HEADS, hd))
        outs_p["fl"].append(lf[:, pad_front:, BF0:BF0 + FOX_HEADS])
        outs_p["ck"].append(dsa_k_rows)
        outs_p["cv"].append(dsa_v_rows)
        outs_p["ci"].append(pv[:, :, misc_col * LANES + IK0:misc_col * LANES + IK0 + D_IDX])

        ps = _project(xs.reshape(dbsz * ls, d), g_pre, w_l, _row_tile(dbsz * ls)).reshape(dbsz, ls, n_cols)
        ya, st_a = _hgrn(ps, 0, lb, _state_to_bd(state_hgrn[l].astype(F32)), t_new, 1)
        yd, st_d = _ret(ps, 3, tab_s, _state_to_bd(state_ret[l].astype(F32)), t_new, 1)
        z = jnp.concatenate(
            [jnp.pad(cache_fox_logf[l].astype(F32), ((0, 0), (0, 0), (BF0, LANES - BF0 - FOX_HEADS))),
             ps[:, :, misc_col * LANES:(misc_col + 1) * LANES]], axis=1)
        lf, cc, cr = _foxprep(z, 0, bias, past)
        k_all = jnp.concatenate([cache_fox_k[l].reshape(dbsz, past, gw), ps[:, :, 5 * gw:6 * gw]], axis=1)
        v_all = jnp.concatenate([cache_fox_v[l].reshape(dbsz, past, gw), ps[:, :, 6 * gw:7 * gw]], axis=1)
        (yb,) = _fox(ps, col["bq"], k_all, 0, v_all, 0, ps, col["bg"], cc, cr, 0, past, False, False)
        ckv_all = jnp.concatenate(
            [jnp.concatenate([cache_dsa_k[l], cache_dsa_v[l]], axis=-1).astype(F32),
             ps[:, :, 11 * gw:11 * gw + LANES]], axis=1)
        mk_all = jnp.concatenate(
            [jnp.pad(cache_dsa_idx_k[l].astype(F32), ((0, 0), (0, 0), (IK0, LANES - IK0 - D_IDX))),
             ps[:, :, misc_col * LANES:(misc_col + 1) * LANES]], axis=1)
        (yc,) = _dsa(ps, (col["cq"], col["cg"], col["ciq"], misc_col), ckv_all, 0, mk_all, 0,
                     k_top_s, 0, past + t_new, False, 0, t_new, False)
        flat = lambda a: a.reshape(dbsz * ls, gw)
        xs = _merge([flat(ya), flat(yb), flat(yc), flat(yd)], xs.reshape(dbsz * ls, d), w_o, g_post,
                    ls, ls, 0, t_new).reshape(dbsz, ls, d)
        pn = ps[:, :t_new, :]
        outs_s["hgrn"].append(_bd_to_state(st_a, N_HEADS))
        outs_s["ret"].append(_bd_to_state(st_d, N_HEADS))
        outs_s["fk"].append(pn[:, :, 5 * gw:6 * gw].reshape(dbsz, -1, N_HEADS, hd))
        outs_s["fv"].append(pn[:, :, 6 * gw:7 * gw].reshape(dbsz, -1, N_HEADS, hd))
        outs_s["fl"].append(lf[:, past:past + t_new, BF0:BF0 + FOX_HEADS])
        outs_s["ck"].append(pn[:, :, 11 * gw:11 * gw + hd])
        outs_s["cv"].append(pn[:, :, 11 * gw + hd:11 * gw + 2 * hd])
        outs_s["ci"].append(pn[:, :, misc_col * LANES + IK0:misc_col * LANES + IK0 + D_IDX])

    dt = x_prompt.dtype
    st = lambda xs_list: jnp.stack(xs_list, axis=0).astype(dt)
    order = ("hgrn", "fk", "fv", "fl", "ck", "cv", "ci", "ret")
    return ((xp[:, ROW_BLOCK:, :].astype(dt), xs[:, :t_new, :].astype(dt))
            + tuple(st(outs_p[k]) for k in order) + tuple(st(outs_s[k]) for k in order))
```

```python
import functools

import numpy as np
import jax
import jax.numpy as jnp
from jax import lax
from jax.experimental import pallas as pl
from jax.experimental.pallas import tpu as pltpu

F32 = jnp.float32
BF16 = jnp.bfloat16
Y_DTYPE = BF16
I32 = jnp.int32
LOG2E = float(np.log2(np.e))
ONES_ROWS = 16

N_META = 16
CHUNK = 64
N_GROUPS = 4
N_HEADS = 4
H_IDX = 8
D_IDX = 32
TOP_K_MAX = 256
ROPE_BASE = 10000.0
EPS = 1e-6
FOX_HEADS = 4

LANES = 128
SUBLANES = 8
ROW_BLOCK = 128
GROUP_W = 2 * LANES
VMEM_LIMIT_BYTES = 56 * 1024 * 1024

IK0 = 0
IW0 = 32
BF0 = 64

NEG = -1e30
M_FLOOR = -1e20
INT_MIN = np.int32(-2 ** 31)
STEP_BLOCKS = 4
KEY_STEP = STEP_BLOCKS * ROW_BLOCK
GLA_BATCH_ROWS = 4


def _cparams(*sem):
    return pltpu.CompilerParams(dimension_semantics=sem, vmem_limit_bytes=VMEM_LIMIT_BYTES)


def _split3(x):
    h = x.astype(BF16)
    r = x - h.astype(F32)
    m = r.astype(BF16)
    lo = (r - m.astype(F32)).astype(BF16)
    return h, m, lo


def _dot_exact_lhs(a_bf16, x):
    d = lambda y: jnp.dot(a_bf16, y, preferred_element_type=F32)
    h, m, lo = _split3(x)
    return d(h) + d(m) + d(lo)


def _dot_nt(a, b):
    return lax.dot_general(a, b, (((1,), (1,)), ((), ())), preferred_element_type=F32)


def _dot_tn(a, b):
    return lax.dot_general(a, b, (((0,), (0,)), ((), ())), preferred_element_type=F32)


def _log2(n):
    assert n > 0 and n & (n - 1) == 0, n
    return n.bit_length() - 1


def _head_masks(width, n_heads):
    lane = lax.broadcasted_iota(I32, (1, width), 1)
    sh = _log2(width // n_heads)
    return [(lax.shift_right_logical(lane, sh) == h).astype(F32) for h in range(n_heads)]


def _block_diag(width, n_heads, value):
    r = lax.broadcasted_iota(I32, (width, width), 0)
    c = lax.broadcasted_iota(I32, (width, width), 1)
    sh = _log2(width // n_heads)
    same = lax.shift_right_logical(r, sh) == lax.shift_right_logical(c, sh)
    return jnp.where(same, value, 0.0).astype(F32)


def _silu(x):
    return x * jax.nn.sigmoid(x)


def _proj_kernel(x_ref, g_ref, w_ref, o_ref, *, col_chunk):
    x = x_ref[...]
    ms = jnp.mean(x * x, axis=-1, keepdims=True)
    xn = ((x * lax.rsqrt(ms + EPS)) * g_ref[...]).astype(BF16)
    for c in range(o_ref.shape[1] // col_chunk):
        cols = slice(c * col_chunk, (c + 1) * col_chunk)
        o_ref[:, cols] = jnp.dot(xn, w_ref[:, cols], preferred_element_type=F32)


def _project(x2d, g, w_bf16, tm):
    rows, d = x2d.shape
    n = w_bf16.shape[1]
    return pl.pallas_call(
        functools.partial(_proj_kernel, col_chunk=1024),
        grid=(rows // tm,),
        in_specs=[pl.BlockSpec((tm, d), lambda i: (i, 0)),
                  pl.BlockSpec((1, d), lambda i: (0, 0)),
                  pl.BlockSpec((d, n), lambda i: (0, 0))],
        out_specs=pl.BlockSpec((tm, n), lambda i: (i, 0)),
        out_shape=jax.ShapeDtypeStruct((rows, n), F32),
        compiler_params=_cparams("parallel"),
        name="proj",
    )(x2d, g, w_bf16)


def _head_scores(qa, ka, hm):
    q_stack = jnp.concatenate([qa * hm[h] for h in range(N_HEADS)], axis=0).astype(BF16)
    return _dot_nt(q_stack, ka.astype(BF16))


def _hier_constants(chunk):
    halves = [chunk >> (i + 1) for i in range(_log2(chunk))]
    r = lax.broadcasted_iota(I32, (chunk, chunk), 0)
    c = lax.broadcasted_iota(I32, (chunk, chunk), 1)
    ts = lax.broadcasted_iota(I32, (N_HEADS * chunk, chunk), 0) & (chunk - 1)
    ss = lax.broadcasted_iota(I32, (N_HEADS * chunk, chunk), 1)
    sels, masks = [], []
    for h in halves:
        sh = _log2(h)
        grp = lambda x: lax.shift_right_logical(x, sh + 1)
        if h < SUBLANES:
            sels.append((c == lax.shift_left(grp(r), sh + 1) + (h - 1)).astype(BF16))
        upper_t = (lax.shift_right_logical(ts, sh) & 1).astype(F32)
        lower_s = 1.0 - (lax.shift_right_logical(ss, sh) & 1).astype(F32)
        masks.append(jnp.where(grp(ts) == grp(ss), upper_t * lower_s, 0.0))
    return halves, jnp.concatenate(sels, axis=0), masks


def _decayed_scores(q, k, b, hier, hm):
    halves, sel_small, masks = hier
    chunk, w = q.shape
    small_refs = _dot_exact_lhs(sel_small, b)
    att, n_small = None, 0
    for h, mask in zip(halves, masks):
        if h < SUBLANES:
            ref = small_refs[n_small * chunk:(n_small + 1) * chunk, :]
            n_small += 1
        else:
            ref = jnp.concatenate([jnp.broadcast_to(b[g0 + h - 1:g0 + h, :], (2 * h, w))
                                   for g0 in range(0, chunk, 2 * h)], axis=0)
        qa = q * jnp.exp(jnp.minimum(b - ref, 0.0))
        ka = k * jnp.exp(jnp.minimum(ref - b, 0.0))
        term = _head_scores(qa, ka, hm) * mask
        att = term if att is None else att + term
    return att


def _gla_chunk(att, qe, ke2, v, decay_end, st, hm, bd):
    c = qe.shape[0]
    o_stack = jnp.dot(att.astype(BF16), v.astype(BF16), preferred_element_type=F32)
    o_intra = o_stack[0:c] * hm[0]
    for h in range(1, N_HEADS):
        o_intra = o_intra + o_stack[h * c:(h + 1) * c] * hm[h]
    o_inter = _dot_nt(qe.astype(BF16), st.astype(BF16))
    st_new = st * decay_end + _dot_tn(v.astype(BF16), ke2.astype(BF16)) * bd
    return o_inter + o_intra, st_new


def _head_rms_gate(o, gate, bd_mean_bf16):
    h, m, _ = _split3(o * o)
    ms = (jnp.dot(h, bd_mean_bf16, preferred_element_type=F32)
          + jnp.dot(m, bd_mean_bf16, preferred_element_type=F32))
    return (o * lax.rsqrt(ms + EPS)) * _silu(gate)


def _gla_prologue(st0_ref, y_ref, st_ref, chunk, n_chunks):
    @pl.when(pl.program_id(1) == 0)
    def _():
        st_ref[...] = st0_ref[...]
    tail = n_chunks * chunk
    if tail < y_ref.shape[1]:
        y_ref[:, tail:, :] = jnp.zeros((y_ref.shape[0], y_ref.shape[1] - tail, y_ref.shape[2]),
                                       y_ref.dtype)


def _hgrn_kernel(a_ref, lb_ref, st0_ref, y_ref, st_ref, *, chunk, n_chunks):
    n_b, _, w = y_ref.shape
    hm = _head_masks(w, N_HEADS)
    bd = _block_diag(w, N_HEADS, 1.0)
    bd_mean = _block_diag(w, N_HEADS, 1.0 / (w // N_HEADS)).astype(BF16)
    r = lax.broadcasted_iota(I32, (chunk, chunk), 0)
    s = lax.broadcasted_iota(I32, (chunk, chunk), 1)
    tri = (r >= s).astype(BF16)
    hier = _hier_constants(chunk)
    bd_ones = bd.astype(BF16)
    lb = lb_ref[...]
    _gla_prologue(st0_ref, y_ref, st_ref, chunk, n_chunks)

    def body(c, carry):
        rows = pl.ds(pl.multiple_of(c * chunk, chunk), chunk)
        for g in range(n_b):
            q = a_ref[g, rows, 0:w]
            f = lb + (1.0 - lb) * jax.nn.sigmoid(a_ref[g, rows, w:2 * w])
            k = 1.0 - f
            v = a_ref[g, rows, 2 * w:3 * w]
            gate = a_ref[g, rows, 3 * w:4 * w]
            b = _dot_exact_lhs(tri, jnp.log(f))
            b_end = b[chunk - 1:chunk, :]
            att = _decayed_scores(q, k, b, hier, hm)
            o, st_new = _gla_chunk(att, q * jnp.exp(b), k * jnp.exp(b_end - b), v, jnp.exp(b_end),
                                   st_ref[g], hm, bd)
            o = o + jnp.dot((q * k).astype(BF16), bd_ones, preferred_element_type=F32) * v
            st_ref[g] = st_new
            y_ref[g, rows, :] = _head_rms_gate(o, gate, bd_mean).astype(y_ref.dtype)
        return carry

    lax.fori_loop(0, n_chunks, body, 0)


def _ret_kernel(d_ref, cos_ref, sin_ref, eb_ref, ke2s_ref, dend_ref, gam_ref, st0_ref, y_ref, st_ref,
                *, chunk, n_chunks):
    n_b, _, w = y_ref.shape
    hd = w // N_HEADS
    hm = _head_masks(w, N_HEADS)
    bd = _block_diag(w, N_HEADS, 1.0)
    bd_mean = _block_diag(w, N_HEADS, 1.0 / hd).astype(BF16)
    lane = lax.broadcasted_iota(I32, (1, w), 1)
    first_half = (lane & (hd - 1)) < (hd // 2)
    eb = eb_ref[...]
    ke2s = ke2s_ref[...]
    dend = dend_ref[...]
    gam = gam_ref[...]
    _gla_prologue(st0_ref, y_ref, st_ref, chunk, n_chunks)

    def rope(x, cos, sin_signed):
        swapped = jnp.where(first_half, pltpu.roll(x, w - hd // 2, 1), pltpu.roll(x, hd // 2, 1))
        return x * cos + swapped * sin_signed

    def body(c, carry):
        rows = pl.ds(pl.multiple_of(c * chunk, chunk), chunk)
        cos = cos_ref[rows, :]
        sin = sin_ref[rows, :]
        for g in range(n_b):
            q = rope(d_ref[g, rows, 0:w], cos, sin)
            k = rope(d_ref[g, rows, w:2 * w], cos, sin) * (hd ** -0.5)
            v = d_ref[g, rows, 2 * w:3 * w]
            gate = d_ref[g, rows, 3 * w:4 * w]
            att = _head_scores(q, k, hm) * gam
            o, st_new = _gla_chunk(att, q * eb, k * ke2s, v, dend, st_ref[g], hm, bd)
            st_ref[g] = st_new
            y_ref[g, rows, :] = _head_rms_gate(o, gate, bd_mean).astype(y_ref.dtype)
        return carry

    lax.fori_loop(0, n_chunks, body, 0)


def _gla_grid(bsz, lp, chunk, n_chunks):
    n_b = next(n for n in (GLA_BATCH_ROWS, 2, 1) if bsz % n == 0)
    halves = n_chunks * chunk == lp and n_chunks % 2 == 0 and (lp // 2) % SUBLANES == 0
    n_seq = 2 if halves else 1
    return n_b, n_seq, n_chunks // n_seq


def _hgrn(p3, col_block, lb, st0, chunk, n_chunks):
    bsz, lp, _ = p3.shape
    w = lb.shape[1]
    n_b, n_seq, n_chunks_blk = _gla_grid(bsz, lp, chunk, n_chunks)
    rows = lp // n_seq
    return pl.pallas_call(
        functools.partial(_hgrn_kernel, chunk=chunk, n_chunks=n_chunks_blk),
        grid=(bsz // n_b, n_seq),
        in_specs=[pl.BlockSpec((n_b, rows, 4 * w), lambda b, s: (b, s, col_block)),
                  pl.BlockSpec((1, w), lambda b, s: (0, 0)),
                  pl.BlockSpec((n_b, w, w), lambda b, s: (b, 0, 0))],
        out_specs=[pl.BlockSpec((n_b, rows, w), lambda b, s: (b, s, 0)),
                   pl.BlockSpec((n_b, w, w), lambda b, s: (b, 0, 0))],
        out_shape=[jax.ShapeDtypeStruct((bsz, lp, w), Y_DTYPE),
                   jax.ShapeDtypeStruct((bsz, w, w), F32)],
        compiler_params=_cparams("parallel", "arbitrary"),
        name="hgrn",
    )(p3, lb, st0)


def _ret(p3, col_block, tables, st0, chunk, n_chunks):
    bsz, lp, _ = p3.shape
    w = st0.shape[1]
    n_b, n_seq, n_chunks_blk = _gla_grid(bsz, lp, chunk, n_chunks)
    rows = lp // n_seq
    cos, sin, eb, ke2s, dend, gam = tables
    full = lambda a: pl.BlockSpec(a.shape, lambda b, s: (0,) * a.ndim)
    per_seq = pl.BlockSpec((rows, w), lambda b, s: (s, 0))
    return pl.pallas_call(
        functools.partial(_ret_kernel, chunk=chunk, n_chunks=n_chunks_blk),
        grid=(bsz // n_b, n_seq),
        in_specs=[pl.BlockSpec((n_b, rows, 4 * w), lambda b, s: (b, s, col_block)),
                  per_seq, per_seq, full(eb), full(ke2s), full(dend), full(gam),
                  pl.BlockSpec((n_b, w, w), lambda b, s: (b, 0, 0))],
        out_specs=[pl.BlockSpec((n_b, rows, w), lambda b, s: (b, s, 0)),
                   pl.BlockSpec((n_b, w, w), lambda b, s: (b, 0, 0))],
        out_shape=[jax.ShapeDtypeStruct((bsz, lp, w), Y_DTYPE),
                   jax.ShapeDtypeStruct((bsz, w, w), F32)],
        compiler_params=_cparams("parallel", "arbitrary"),
        name="ret",
    )(p3, cos, sin, eb, ke2s, dend, gam, st0)


def _log_sigmoid(x):
    return -(jnp.maximum(-x, 0.0) + jnp.log(1.0 + jnp.exp(-jnp.abs(x))))


def _foxprep_kernel(z_ref, bias_ref, lf_ref, cc_ref, cr_ref, *, n_pass):
    n_b = z_ref.shape[0]
    nblk = z_ref.shape[1] // ROW_BLOCK
    r = lax.broadcasted_iota(I32, (ROW_BLOCK, ROW_BLOCK), 0)
    s = lax.broadcasted_iota(I32, (ROW_BLOCK, ROW_BLOCK), 1)
    tri = (r >= s).astype(BF16)
    bias = bias_ref[...]

    def body(i, carries):
        rows = pl.ds(pl.multiple_of(i * ROW_BLOCK, ROW_BLOCK), ROW_BLOCK)
        rowi = i * ROW_BLOCK + lax.broadcasted_iota(I32, (ROW_BLOCK, 1), 0)
        out = []
        for g in range(n_b):
            z = z_ref[g, rows, :]
            lf = jnp.where(rowi < n_pass, z, _log_sigmoid(z + bias))
            lf_ref[g, rows, :] = lf
            cs = _dot_exact_lhs(tri, lf) + carries[g]
            cc_ref[g, rows, :] = cs
            cr_ref[g, :, rows] = cs.T[BF0:BF0 + SUBLANES, :]
            out.append(cs[ROW_BLOCK - 1:ROW_BLOCK, :])
        return tuple(out)

    lax.fori_loop(0, nblk, body, tuple(jnp.zeros((1, LANES), F32) for _ in range(n_b)))
    pad = cr_ref.shape[2] - nblk * ROW_BLOCK
    if pad:
        cr_ref[:, :, nblk * ROW_BLOCK:] = jnp.zeros((n_b, SUBLANES, pad), F32)


def _foxprep(z3, z_col, bias, n_pass):
    bsz, lk, _ = z3.shape
    cr_cols = pl.cdiv(lk, KEY_STEP) * KEY_STEP
    n_b = next(n for n in (GLA_BATCH_ROWS, 2, 1) if bsz % n == 0)
    return pl.pallas_call(
        functools.partial(_foxprep_kernel, n_pass=n_pass),
        grid=(bsz // n_b,),
        in_specs=[pl.BlockSpec((n_b, lk, LANES), lambda b: (b, 0, z_col)),
                  pl.BlockSpec((1, LANES), lambda b: (0, 0))],
        out_specs=[pl.BlockSpec((n_b, lk, LANES), lambda b: (b, 0, 0)),
                   pl.BlockSpec((n_b, lk, LANES), lambda b: (b, 0, 0)),
                   pl.BlockSpec((n_b, SUBLANES, cr_cols), lambda b: (b, 0, 0))],
        out_shape=[jax.ShapeDtypeStruct((bsz, lk, LANES), F32),
                   jax.ShapeDtypeStruct((bsz, lk, LANES), F32),
                   jax.ShapeDtypeStruct((bsz, SUBLANES, cr_cols), F32)],
        compiler_params=_cparams("parallel"),
        name="foxprep",
    )(z3, bias)


MXU_COLS = 2 * LANES


def _query_block(lq):
    return MXU_COLS if lq > ROW_BLOCK else ROW_BLOCK


def _heads_per_dot(qb):
    assert MXU_COLS % qb == 0
    return MXU_COLS // qb


def _head_weights(x_t, rows_per_head, n_heads):
    hpd = _heads_per_dot(x_t.shape[1])
    rowh = lax.shift_right_logical(lax.broadcasted_iota(I32, (x_t.shape[0], 1), 0),
                                   _log2(rows_per_head))
    only = lambda h: jnp.where(rowh == h, x_t, 0.0)
    return [jnp.concatenate([only(d * hpd + i) for i in range(hpd)], axis=1).astype(BF16)
            for d in range(n_heads // hpd)]


def _attend_two_pass(n_steps, logits_fn, value_t_fn, shifts, s_scr, acc_scr, qb, hd):
    fold = lambda x, op: op(x.reshape(KEY_STEP // SUBLANES, SUBLANES, qb), axis=0)
    hv = hd + ONES_ROWS

    def max_step(i, ms):
        tiles = logits_fn(i)
        for h in range(N_HEADS):
            s_scr[i * N_HEADS + h] = tiles[h]
        return tuple(jnp.maximum(ms[h], fold(tiles[h], jnp.max)) for h in range(N_HEADS))

    ms = lax.fori_loop(0, n_steps, max_step,
                       tuple(jnp.full((SUBLANES, qb), M_FLOOR, F32) for _ in range(N_HEADS)))
    m_logit = [jnp.max(ms[h], axis=0, keepdims=True) + shifts[h] for h in range(N_HEADS)]
    acc_scr[...] = jnp.zeros(acc_scr.shape, F32)

    def sum_step(i, carry):
        for h in range(N_HEADS):
            p = jnp.exp2(s_scr[i * N_HEADS + h] - (m_logit[h] - shifts[h]))
            acc_scr[h * hv:(h + 1) * hv, :] += jnp.dot(value_t_fn(h, i), p.astype(BF16),
                                                       preferred_element_type=F32)
        return carry

    lax.fori_loop(0, n_steps, sum_step, 0)
    outs = []
    for h in range(N_HEADS):
        l = acc_scr[h * hv + hd:h * hv + hd + 1, :]
        outs.append(acc_scr[h * hv:h * hv + hd, :] / jnp.where(l > 0.0, l, 1.0))
    return jnp.concatenate(outs, axis=0)


def _fox_kernel(q_ref, k_ref, v_ref, g_ref, cc_ref, cr_ref, y_ref, *rest, key_lo, q_off, causal_blocks,
                emit_caches):
    if emit_caches:
        kc_ref, vc_ref, kbf_scr, vt_scr, ck_scr, s_scr, acc_scr = rest
    else:
        kbf_scr, vt_scr, ck_scr, s_scr, acc_scr = rest
    qb, w = q_ref.shape
    hd = w // N_HEADS
    hv = hd + ONES_ROWS
    nkb_total = k_ref.shape[0] // ROW_BLOCK
    hpd = _heads_per_dot(qb)
    j = pl.program_id(1)
    nkb = jnp.minimum((j + 1) * (qb // ROW_BLOCK), nkb_total) if causal_blocks else nkb_total

    @pl.when(j == 0)
    def _prepare_batch_row():
        def blk(i, carry):
            rows = pl.ds(pl.multiple_of(i * ROW_BLOCK, ROW_BLOCK), ROW_BLOCK)
            kbf_scr[rows, :] = k_ref[rows, :].astype(BF16)
            v_t = v_ref[rows, :].T.astype(BF16)
            cs = cc_ref[rows, :] * LOG2E
            for h in range(N_HEADS):
                vt_scr[h * hv:h * hv + hd, rows] = v_t[h * hd:(h + 1) * hd, :]
                ck_scr[h, rows, :] = jnp.broadcast_to(cs[:, BF0 + h:BF0 + h + 1], (ROW_BLOCK, LANES))
            return carry
        lax.fori_loop(0, nkb_total, blk, 0)
        if emit_caches:
            kc_ref[...] = k_ref[key_lo:, :]
            vc_ref[...] = v_ref[key_lo:, :]
        pad = vt_scr.shape[1] - nkb_total * ROW_BLOCK
        if pad:
            kbf_scr[nkb_total * ROW_BLOCK:, :] = jnp.zeros((pad, w), BF16)
            ck_scr[:, nkb_total * ROW_BLOCK:, :] = jnp.zeros((N_HEADS, pad, LANES), F32)
        for h in range(N_HEADS):
            if pad:
                vt_scr[h * hv:h * hv + hd, nkb_total * ROW_BLOCK:] = jnp.zeros((hd, pad), BF16)
            vt_scr[h * hv + hd:(h + 1) * hv, :] = jnp.ones((ONES_ROWS, vt_scr.shape[1]), BF16)

    wq = _head_weights((q_ref[...] * (hd ** -0.5 * LOG2E)).T, hd, N_HEADS)
    qcol = pl.ds(pl.multiple_of(j * qb + q_off, ROW_BLOCK), qb)
    cq = [cr_ref[h:h + 1, qcol] * LOG2E for h in range(N_HEADS)]
    qrow = j * qb + q_off + lax.broadcasted_iota(I32, (1, qb), 1)
    sub = lax.broadcasted_iota(I32, (KEY_STEP, 1), 0)

    def logits(i):
        k0 = pl.multiple_of(i * KEY_STEP, KEY_STEP)
        krows = pl.ds(k0, KEY_STEP)
        kblk = kbf_scr[krows, :]
        kidx = k0 + sub
        ok = (kidx >= key_lo) & (kidx <= qrow)
        tiles = []
        for d in range(N_HEADS // hpd):
            s2 = jnp.dot(kblk, wq[d], preferred_element_type=F32)
            for i2 in range(hpd):
                h = d * hpd + i2
                ck = jnp.concatenate([ck_scr[h, krows, :]] * (qb // LANES), axis=1)
                tiles.append(jnp.where(ok, s2[:, i2 * qb:(i2 + 1) * qb] - ck, NEG))
        return tiles

    def value_t(h, i):
        return vt_scr[h * hv:(h + 1) * hv, pl.ds(pl.multiple_of(i * KEY_STEP, KEY_STEP), KEY_STEP)]

    n_steps = lax.shift_right_logical(nkb + (STEP_BLOCKS - 1), _log2(STEP_BLOCKS))
    o_t = _attend_two_pass(n_steps, logits, value_t, cq, s_scr, acc_scr, qb, hd)
    y_ref[...] = (o_t.T * _silu(g_ref[...])).astype(y_ref.dtype)


def _fox(q_arr, q_col, k_arr, k_col, v_arr, v_col, g_arr, g_col, cc, cr, key_lo, q_off, causal_blocks,
         emit_caches):
    bsz, lq, _ = q_arr.shape
    lk = k_arr.shape[1]
    w = GROUP_W
    qb = _query_block(lq)
    key_rows = pl.cdiv(lk, KEY_STEP) * KEY_STEP
    assert cr.shape[2] >= q_off + pl.cdiv(lq, qb) * qb
    out_specs = [pl.BlockSpec((None, qb, w), lambda b, j: (b, j, 0))]
    out_shape = [jax.ShapeDtypeStruct((bsz, lq, w), Y_DTYPE)]
    if emit_caches:
        out_specs += [pl.BlockSpec((None, lk - key_lo, w), lambda b, j: (b, 0, 0))] * 2
        out_shape += [jax.ShapeDtypeStruct((bsz, lk - key_lo, w), F32)] * 2
    return pl.pallas_call(
        functools.partial(_fox_kernel, key_lo=key_lo, q_off=q_off, causal_blocks=causal_blocks,
                          emit_caches=emit_caches),
        grid=(bsz, pl.cdiv(lq, qb)),
        in_specs=[pl.BlockSpec((None, qb, w), lambda b, j: (b, j, q_col)),
                  pl.BlockSpec((None, lk, w), lambda b, j: (b, 0, k_col)),
                  pl.BlockSpec((None, lk, w), lambda b, j: (b, 0, v_col)),
                  pl.BlockSpec((None, qb, w), lambda b, j: (b, j, g_col)),
                  pl.BlockSpec((None, lk, LANES), lambda b, j: (b, 0, 0)),
                  pl.BlockSpec((None, SUBLANES, cr.shape[2]), lambda b, j: (b, 0, 0))],
        out_specs=out_specs,
        out_shape=out_shape,
        scratch_shapes=[pltpu.VMEM((key_rows, w), BF16),
                        pltpu.VMEM((w + N_HEADS * ONES_ROWS, key_rows), BF16),
                        pltpu.VMEM((N_HEADS, key_rows, LANES), F32),
                        pltpu.VMEM((N_HEADS * key_rows // KEY_STEP, KEY_STEP, qb), F32),
                        pltpu.VMEM((w + N_HEADS * ONES_ROWS, qb), F32)],
        compiler_params=_cparams("parallel", "arbitrary"),
        name="fox",
    )(q_arr, k_arr, v_arr, g_arr, cc, cr)


def _dsa_kernel(cq_ref, cg_ref, ciq_ref, mq_ref, ckv_ref, mk_ref, y_ref, *rest,
                k_top, key_lo, key_hi, chunk_causal, q_lo, q_hi, emit_caches):
    if emit_caches:
        kc_ref, vc_ref, key_scr, kvb_scr, vt_scr, mkb_scr, s_scr, acc_scr = rest
    else:
        key_scr, kvb_scr, vt_scr, mkb_scr, s_scr, acc_scr = rest
    qb, w = cq_ref.shape
    hd = w // N_HEADS
    hpd = _heads_per_dot(qb)
    nkb_total = ckv_ref.shape[0] // ROW_BLOCK
    j = pl.program_id(1)
    nkb = jnp.minimum((j + 1) * (qb // ROW_BLOCK), nkb_total) if chunk_causal else nkb_total
    idx_scale = (H_IDX * D_IDX) ** -0.5

    @pl.when(j == 0)
    def _prepare_batch_row():
        def blk(i, carry):
            rows = pl.ds(pl.multiple_of(i * ROW_BLOCK, ROW_BLOCK), ROW_BLOCK)
            kv = ckv_ref[rows, :]
            kvb_scr[rows, :] = kv.astype(BF16)
            vt_scr[0:hd, rows] = kv.T[hd:2 * hd, :].astype(BF16)
            mkb_scr[rows, :] = mk_ref[rows, :].astype(BF16)
            return carry
        lax.fori_loop(0, nkb_total, blk, 0)
        if emit_caches:
            kv_new = ckv_ref[key_lo:, :]
            kc_ref[...] = kv_new[:, 0:hd]
            vc_ref[...] = kv_new[:, hd:2 * hd]
        pad = vt_scr.shape[1] - nkb_total * ROW_BLOCK
        if pad:
            kvb_scr[nkb_total * ROW_BLOCK:, :] = jnp.zeros((pad, LANES), BF16)
            mkb_scr[nkb_total * ROW_BLOCK:, :] = jnp.zeros((pad, LANES), BF16)
            vt_scr[0:hd, nkb_total * ROW_BLOCK:] = jnp.zeros((hd, pad), BF16)
        vt_scr[hd:, :] = jnp.ones((ONES_ROWS, vt_scr.shape[1]), BF16)

    def pad_rows(x):
        return jnp.concatenate([x, jnp.zeros((LANES - x.shape[0], qb), F32)], axis=0)

    iq_t = ciq_ref[...].T
    iw_t = mq_ref[...].T[IW0:IW0 + H_IDX, :]
    q_t = (cq_ref[...] * (hd ** -0.5 * LOG2E)).T
    def side_by_side(x_t, rows, n_heads):
        return [jnp.concatenate([pad_rows(x_t[(d * hpd + i) * rows:(d * hpd + i + 1) * rows, :])
                                 for i in range(hpd)], axis=1).astype(BF16)
                for d in range(n_heads // hpd)]

    iq_rhs = side_by_side(iq_t, D_IDX, H_IDX)
    q_rhs = side_by_side(q_t, hd, N_HEADS)

    qrow = j * qb + lax.broadcasted_iota(I32, (1, qb), 1)
    if chunk_causal:
        hi = (lax.shift_right_logical(qrow, 6) + 1) * CHUNK
    else:
        hi = jnp.full((1, qb), key_hi, I32)
    sub = lax.broadcasted_iota(I32, (KEY_STEP, 1), 0)
    n_steps = lax.shift_right_logical(nkb + (STEP_BLOCKS - 1), _log2(STEP_BLOCKS))

    def score_step(i, carry):
        k0 = pl.multiple_of(i * KEY_STEP, KEY_STEP)
        mk = mkb_scr[pl.ds(k0, KEY_STEP), :]
        acc = jnp.zeros((KEY_STEP, qb), F32)
        for d in range(H_IDX // hpd):
            sc2 = jnp.dot(mk, iq_rhs[d], preferred_element_type=F32)
            for i2 in range(hpd):
                h = d * hpd + i2
                acc = acc + jnp.maximum(sc2[:, i2 * qb:(i2 + 1) * qb], 0.0) * iw_t[h:h + 1, :]
        score = acc * idx_scale + 0.0
        kidx = k0 + sub
        adm = (kidx >= key_lo) & (kidx < hi)
        u = pltpu.bitcast(score, I32)
        key = u ^ (lax.shift_right_arithmetic(u, 31) & np.int32(0x7FFFFFFF))
        key_scr[pl.ds(k0, KEY_STEP), :] = jnp.where(adm, key, INT_MIN)
        return carry

    lax.fori_loop(0, n_steps, score_step, 0)

    def count(pred):
        groups = qb // LANES

        def cb(i, c8s):
            k0 = pl.multiple_of(i * KEY_STEP, KEY_STEP)
            out = []
            for lg in range(groups):
                lanes = lambda x, lg=lg: x[:, lg * LANES:(lg + 1) * LANES]
                kk = key_scr[pl.ds(k0, KEY_STEP), lg * LANES:(lg + 1) * LANES]
                ind = jnp.where(pred(kk, k0, lanes), 1, 0).astype(I32)
                out.append(c8s[lg] + jnp.sum(ind.reshape(KEY_STEP // SUBLANES, SUBLANES, LANES), axis=0))
            return tuple(out)
        c8s = lax.fori_loop(0, n_steps, cb,
                            tuple(jnp.zeros((SUBLANES, LANES), I32) for _ in range(groups)))
        return jnp.concatenate([jnp.sum(c, axis=0, keepdims=True) for c in c8s], axis=1)

    def bisect(n_bits, count_ge, need, run=True):
        def bit_body(i, carry):
            ans, cnt_ans = carry
            cand = ans | lax.shift_left(np.int32(1), jnp.int32(n_bits - 1) - i)
            cnt = count_ge(cand)
            ok = cnt >= need
            return jnp.where(ok, cand, ans), jnp.where(ok, cnt, cnt_ans)
        return lax.fori_loop(0, jnp.where(run, n_bits, 0), bit_body,
                             (jnp.zeros((1, qb), I32), jnp.full((1, qb), np.int32(2 ** 30), I32)))

    can_overflow = nkb * ROW_BLOCK - key_lo > k_top
    ans, cnt_thr = bisect(32, lambda cand: count(
        lambda kk, k0, lanes: kk >= lanes(cand ^ INT_MIN)), k_top, can_overflow)
    thr = ans ^ INT_MIN
    tie_flag = jnp.where((qrow >= q_lo) & (qrow < q_hi),
                         jnp.where((cnt_thr > k_top) & (thr != INT_MIN), 1, 0), 0)
    tie = tie_flag > 0

    @pl.when(jnp.max(tie_flag) > 0)
    def _break_ties():
        n_rev_bits = _log2(pl.next_power_of_2(key_scr.shape[0]))
        rev_base = np.int32(2 ** n_rev_bits - 1)
        need = k_top - count(lambda kk, k0, lanes: kk > lanes(thr))
        ans2, _ = bisect(n_rev_bits, lambda cand: count(
            lambda kk, k0, lanes: (kk == lanes(thr)) & ((rev_base - (k0 + sub)) >= lanes(cand))), need)

        def demote(i, carry):
            k0 = pl.multiple_of(i * KEY_STEP, KEY_STEP)
            kk = key_scr[pl.ds(k0, KEY_STEP), :]
            lose = tie & (kk == thr) & ((rev_base - (k0 + sub)) < ans2)
            key_scr[pl.ds(k0, KEY_STEP), :] = jnp.where(lose, INT_MIN, kk)
            return carry

        lax.fori_loop(0, n_steps, demote, 0)

    thr_sel = jnp.maximum(thr, INT_MIN + 1)

    def logits(i):
        krows = pl.ds(pl.multiple_of(i * KEY_STEP, KEY_STEP), KEY_STEP)
        sel = key_scr[krows, :] >= thr_sel
        kv = kvb_scr[krows, :]
        tiles = []
        for d in range(N_HEADS // hpd):
            s2 = jnp.dot(kv, q_rhs[d], preferred_element_type=F32)
            tiles += [jnp.where(sel, s2[:, i2 * qb:(i2 + 1) * qb], NEG) for i2 in range(hpd)]
        return tiles

    def value_t(h, i):
        return vt_scr[:, pl.ds(pl.multiple_of(i * KEY_STEP, KEY_STEP), KEY_STEP)]

    no_shift = [jnp.zeros((1, qb), F32)] * N_HEADS
    o_t = _attend_two_pass(n_steps, logits, value_t, no_shift, s_scr, acc_scr, qb, hd)
    y_ref[...] = (o_t.T * _silu(cg_ref[...])).astype(y_ref.dtype)


def _dsa(p3, cols, ckv_arr, ckv_col, mk_arr, mk_col, k_top, key_lo, key_hi, chunk_causal, q_lo, q_hi,
         emit_caches):
    bsz, lq, _ = p3.shape
    lk = ckv_arr.shape[1]
    w = GROUP_W
    hd = w // N_HEADS
    qb = _query_block(lq)
    cq_col, cg_col, ciq_col, mq_col = cols
    key_rows = pl.cdiv(lk, KEY_STEP) * KEY_STEP
    out_specs = [pl.BlockSpec((None, qb, w), lambda b, j: (b, j, 0))]
    out_shape = [jax.ShapeDtypeStruct((bsz, lq, w), Y_DTYPE)]
    if emit_caches:
        out_specs += [pl.BlockSpec((None, lk - key_lo, hd), lambda b, j: (b, 0, 0))] * 2
        out_shape += [jax.ShapeDtypeStruct((bsz, lk - key_lo, hd), F32)] * 2
    return pl.pallas_call(
        functools.partial(_dsa_kernel, k_top=k_top, key_lo=key_lo, key_hi=key_hi,
                          chunk_causal=chunk_causal, q_lo=q_lo, q_hi=q_hi, emit_caches=emit_caches),
        grid=(bsz, pl.cdiv(lq, qb)),
        in_specs=[pl.BlockSpec((None, qb, w), lambda b, j: (b, j, cq_col)),
                  pl.BlockSpec((None, qb, w), lambda b, j: (b, j, cg_col)),
                  pl.BlockSpec((None, qb, w), lambda b, j: (b, j, ciq_col)),
                  pl.BlockSpec((None, qb, LANES), lambda b, j: (b, j, mq_col)),
                  pl.BlockSpec((None, lk, LANES), lambda b, j: (b, 0, ckv_col)),
                  pl.BlockSpec((None, lk, LANES), lambda b, j: (b, 0, mk_col))],
        out_specs=out_specs,
        out_shape=out_shape,
        scratch_shapes=[pltpu.VMEM((key_rows, qb), I32),
                        pltpu.VMEM((key_rows, LANES), BF16),
                        pltpu.VMEM((w // N_HEADS + ONES_ROWS, key_rows), BF16),
                        pltpu.VMEM((key_rows, LANES), BF16),
                        pltpu.VMEM((N_HEADS * key_rows // KEY_STEP, KEY_STEP, qb), F32),
                        pltpu.VMEM((w + N_HEADS * ONES_ROWS, qb), F32)],
        compiler_params=_cparams("parallel", "arbitrary"),
        name="dsa",
    )(p3, p3, p3, p3, ckv_arr, mk_arr)


def _merge_kernel(ya_ref, yb_ref, yc_ref, yd_ref, x_ref, w_ref, g_ref, o_ref, *, period, valid_lo, valid_hi):
    tm = x_ref.shape[0]
    gw = ya_ref.shape[1]
    acc = jnp.zeros(o_ref.shape, F32)
    for i, y_ref in enumerate((ya_ref, yb_ref, yc_ref, yd_ref)):
        acc = acc + jnp.dot(y_ref[...].astype(BF16), w_ref[i * gw:(i + 1) * gw, :],
                            preferred_element_type=F32)
    ms = jnp.mean(acc * acc, axis=-1, keepdims=True)
    out = x_ref[...] + (acc * lax.rsqrt(ms + EPS)) * g_ref[...]
    r0 = pl.program_id(0) * tm
    local = (r0 - (r0 // period) * period) + lax.broadcasted_iota(I32, (tm, 1), 0)
    local = jnp.where(local >= period, local - period, local)
    valid = (local >= valid_lo) & (local < valid_hi)
    o_ref[...] = jnp.where(valid, out, 0.0)


def _merge(ys, x2d, w_bf16, g, tm, period, valid_lo, valid_hi):
    rows, d = x2d.shape
    assert tm <= period
    gw = ys[0].shape[1]
    yspec = pl.BlockSpec((tm, gw), lambda i: (i, 0))
    return pl.pallas_call(
        functools.partial(_merge_kernel, period=period, valid_lo=valid_lo, valid_hi=valid_hi),
        grid=(rows // tm,),
        in_specs=[yspec, yspec, yspec, yspec,
                  pl.BlockSpec((tm, d), lambda i: (i, 0)),
                  pl.BlockSpec(w_bf16.shape, lambda i: (0, 0)),
                  pl.BlockSpec((1, d), lambda i: (0, 0))],
        out_specs=pl.BlockSpec((tm, d), lambda i: (i, 0)),
        out_shape=jax.ShapeDtypeStruct((rows, d), F32),
        compiler_params=_cparams("parallel"),
        name="merge",
    )(*ys, x2d, w_bf16, g)


def _row_tile(rows, at_most=512):
    return next(t for t in (1024, 512, 256, ROW_BLOCK) if rows % t == 0 and t <= at_most)


def _column_layout(gw):
    sizes = [gw] * 4 + [gw, gw, gw, FOX_HEADS, gw] + [gw, gw // 4, gw // 4, gw, H_IDX * D_IDX, D_IDX, H_IDX] + [gw] * 4
    names = ["aq", "af", "ai", "ag", "bq", "bk", "bv", "bf", "bg",
             "cq", "ck", "cv", "cg", "ciq", "cik", "ciw", "dq", "dk", "dv", "dg"]
    start = dict(zip(names, np.cumsum([0] + sizes[:-1])))
    size = dict(zip(names, sizes))
    src = -np.ones((16 * gw,), np.int64)
    def put(dst, name, off=0):
        src[dst + off:dst + off + size[name]] = np.arange(start[name], start[name] + size[name])
    for i, n in enumerate(["aq", "af", "ai", "ag", "bq", "bk", "bv", "bg", "cq", "cg", "ciq"]):
        put(i * gw, n)
    ckv0 = 11 * gw
    put(ckv0, "ck")
    put(ckv0 + gw // 4, "cv")
    misc0 = ckv0 + LANES
    put(misc0, "cik", IK0)
    put(misc0, "ciw", IW0)
    put(misc0, "bf", BF0)
    for i, n in enumerate(["dq", "dk", "dv", "dg"]):
        put(12 * gw + i * gw, n)
    return src


def _relayout_w_in(w_in_l, src):
    cols = jnp.take(w_in_l, jnp.asarray(np.maximum(src, 0)), axis=1)
    return jnp.where(jnp.asarray(src >= 0)[None, :], cols, 0.0).astype(BF16)


def _ret_tables(pos, chunk, gw):
    hd = gw // N_HEADS
    half = hd // 2
    inv = ROPE_BASE ** (-jnp.arange(half, dtype=F32) / half)
    ang = pos.astype(F32)[:, None] * inv[None, :]
    cos_h = jnp.concatenate([jnp.cos(ang), jnp.cos(ang)], axis=-1)
    sin_h = jnp.concatenate([-jnp.sin(ang), jnp.sin(ang)], axis=-1)
    cos = jnp.tile(cos_h, (1, N_HEADS))
    sin = jnp.tile(sin_h, (1, N_HEADS))
    lg = jnp.log(1.0 - 2.0 ** (-5.0 - jnp.arange(N_HEADS, dtype=F32)))
    lg_l = jnp.repeat(lg, hd)[None, :]
    t = jnp.arange(chunk, dtype=F32)[:, None]
    eb = jnp.exp((t + 1.0) * lg_l)
    ke2s = jnp.exp((chunk - 1.0 - t) * lg_l)
    dend = jnp.exp(chunk * lg_l)
    dt = jnp.arange(chunk, dtype=F32)[:, None] - jnp.arange(chunk, dtype=F32)[None, :]
    gam = jnp.concatenate([jnp.where(dt >= 0, jnp.exp(dt * lg[h]), 0.0) for h in range(N_HEADS)], axis=0)
    return cos, sin, eb, ke2s, dend, gam


def _state_to_bd(state):
    bsz, h, k, v = state.shape
    eye = jnp.eye(h, dtype=state.dtype)
    st = jnp.einsum('bhkv,hg->bhvgk', state, eye)
    return st.reshape(bsz, h * v, h * k)


def _bd_to_state(st, h):
    bsz, hv, hk = st.shape
    st5 = st.reshape(bsz, h, hv // h, h, hk // h)
    diag = jnp.stack([st5[:, i, :, i, :] for i in range(h)], axis=1)
    return jnp.swapaxes(diag, 2, 3)


def kernel(x_prompt, x_sample, state_hgrn, cache_fox_k, cache_fox_v, cache_fox_logf, cache_dsa_k,
           cache_dsa_v, cache_dsa_idx_k, state_ret, meta_tokens, w_in, w_out, fox_bias, hgrn_lb,
           norm_pre, norm_post):
    bsz, seq, d = x_prompt.shape
    dbsz, t_new, _ = x_sample.shape
    depth = w_in.shape[0]
    past = cache_fox_k.shape[2]
    gw = d // N_GROUPS
    hd = gw // N_HEADS
    assert gw == 2 * LANES and seq % ROW_BLOCK == 0 and past % ROW_BLOCK == 0 and t_new <= ROW_BLOCK
    assert t_new % SUBLANES == 0

    pad_front = ROW_BLOCK - N_META
    lp = ROW_BLOCK + seq
    ls = ROW_BLOCK
    lks = past + ROW_BLOCK
    n_chunks_p = lp // CHUNK
    k_top_p = min(TOP_K_MAX, seq // 4)
    k_top_s = min(TOP_K_MAX, (past + t_new) // 4)

    src = _column_layout(gw)
    n_cols = src.shape[0]
    col = {"a": 0, "bq": 4, "bk": 5, "bv": 6, "bg": 7, "cq": 8, "cg": 9, "ciq": 10, "d": 3}
    ckv_col = (11 * gw) // LANES
    misc_col = ckv_col + 1

    sm = jax.nn.softmax(hgrn_lb.astype(F32), axis=0)
    lbs = jnp.cumsum(sm, axis=0) - sm[0:1]

    xp = jnp.concatenate([jnp.zeros((bsz, pad_front, d), F32),
                          jnp.broadcast_to(meta_tokens.astype(F32)[None], (bsz, N_META, d)),
                          x_prompt], axis=1)
    xs = jnp.concatenate([x_sample, jnp.zeros((dbsz, ls - t_new, d), F32)], axis=1)

    tab_p = _ret_tables(jnp.arange(lp) - ROW_BLOCK, CHUNK, gw)
    tab_s = _ret_tables(past + jnp.arange(ls), t_new, gw)
    zero_state_p = jnp.zeros((bsz, gw, gw), F32)

    outs_p = {k: [] for k in ("hgrn", "fk", "fv", "fl", "ck", "cv", "ci", "ret")}
    outs_s = {k: [] for k in ("hgrn", "fk", "fv", "fl", "ck", "cv", "ci", "ret")}

    for l in range(depth):
        w_l = _relayout_w_in(w_in[l], src)
        w_o = w_out[l].astype(BF16)
        g_pre = norm_pre[l][None, :]
        g_post = norm_post[l][None, :]
        lb = lbs[l][None, :]
        bias = jnp.zeros((1, LANES), F32).at[0, BF0:BF0 + FOX_HEADS].set(fox_bias[l].astype(F32))

        p = _project(xp.reshape(bsz * lp, d), g_pre, w_l, _row_tile(bsz * lp)).reshape(bsz, lp, n_cols)
        ya, st_a = _hgrn(p, 0, lb, zero_state_p, CHUNK, n_chunks_p)
        yd, st_d = _ret(p, 3, tab_p, zero_state_p, CHUNK, n_chunks_p)
        lf, cc, cr = _foxprep(p, misc_col, bias, 0)
        yb, fox_k_rows, fox_v_rows = _fox(p, col["bq"], p, col["bk"], p, col["bv"], p, col["bg"], cc, cr,
                                          pad_front, 0, True, True)
        yc, dsa_k_rows, dsa_v_rows = _dsa(p, (col["cq"], col["cg"], col["ciq"], misc_col), p, ckv_col,
                                          p, misc_col, k_top_p, pad_front, 0, True, pad_front, lp, True)
        flat = lambda a: a.reshape(bsz * lp, gw)
        xp = _merge([flat(ya), flat(yb), flat(yc), flat(yd)], xp.reshape(bsz * lp, d), w_o, g_post,
                    _row_tile(bsz * lp, min(lp, 1024)), lp, pad_front, lp).reshape(bsz, lp, d)
        pv = p[:, pad_front:, :]
        outs_p["hgrn"].append(_bd_to_state(st_a, N_HEADS))
        outs_p["ret"].append(_bd_to_state(st_d, N_HEADS))
        outs_p["fk"].append(fox_k_rows.reshape(bsz, -1, N_HEADS, hd))
        outs_p["fv"].append(fox_v_rows.reshape(bsz, -1, N_HEADS, hd))
        outs_p["fl"].append(lf[:, pad_front:, BF0:BF0 + FOX_HEADS])
        outs_p["ck"].append(dsa_k_rows)
        outs_p["cv"].append(dsa_v_rows)
        outs_p["ci"].append(pv[:, :, misc_col * LANES + IK0:misc_col * LANES + IK0 + D_IDX])

        ps = _project(xs.reshape(dbsz * ls, d), g_pre, w_l, _row_tile(dbsz * ls)).reshape(dbsz, ls, n_cols)
        ya, st_a = _hgrn(ps, 0, lb, _state_to_bd(state_hgrn[l].astype(F32)), t_new, 1)
        yd, st_d = _ret(ps, 3, tab_s, _state_to_bd(state_ret[l].astype(F32)), t_new, 1)
        z = jnp.concatenate(
            [jnp.pad(cache_fox_logf[l].astype(F32), ((0, 0), (0, 0), (BF0, LANES - BF0 - FOX_HEADS))),
             ps[:, :, misc_col * LANES:(misc_col + 1) * LANES]], axis=1)
        lf, cc, cr = _foxprep(z, 0, bias, past)
        k_all = jnp.concatenate([cache_fox_k[l].reshape(dbsz, past, gw), ps[:, :, 5 * gw:6 * gw]], axis=1)
        v_all = jnp.concatenate([cache_fox_v[l].reshape(dbsz, past, gw), ps[:, :, 6 * gw:7 * gw]], axis=1)
        (yb,) = _fox(ps, col["bq"], k_all, 0, v_all, 0, ps, col["bg"], cc, cr, 0, past, False, False)
        ckv_all = jnp.concatenate(
            [jnp.concatenate([cache_dsa_k[l], cache_dsa_v[l]], axis=-1).astype(F32),
             ps[:, :, 11 * gw:11 * gw + LANES]], axis=1)
        mk_all = jnp.concatenate(
            [jnp.pad(cache_dsa_idx_k[l].astype(F32), ((0, 0), (0, 0), (IK0, LANES - IK0 - D_IDX))),
             ps[:, :, misc_col * LANES:(misc_col + 1) * LANES]], axis=1)
        (yc,) = _dsa(ps, (col["cq"], col["cg"], col["ciq"], misc_col), ckv_all, 0, mk_all, 0,
                     k_top_s, 0, past + t_new, False, 0, t_new, False)
        flat = lambda a: a.reshape(dbsz * ls, gw)
        xs = _merge([flat(ya), flat(yb), flat(yc), flat(yd)], xs.reshape(dbsz * ls, d), w_o, g_post,
                    ls, ls, 0, t_new).reshape(dbsz, ls, d)
        pn = ps[:, :t_new, :]
        outs_s["hgrn"].append(_bd_to_state(st_a, N_HEADS))
        outs_s["ret"].append(_bd_to_state(st_d, N_HEADS))
        outs_s["fk"].append(pn[:, :, 5 * gw:6 * gw].reshape(dbsz, -1, N_HEADS, hd))
        outs_s["fv"].append(pn[:, :, 6 * gw:7 * gw].reshape(dbsz, -1, N_HEADS, hd))
        outs_s["fl"].append(lf[:, past:past + t_new, BF0:BF0 + FOX_HEADS])
        outs_s["ck"].append(pn[:, :, 11 * gw:11 * gw + hd])
        outs_s["cv"].append(pn[:, :, 11 * gw + hd:11 * gw + 2 * hd])
        outs_s["ci"].append(pn[:, :, misc_col * LANES + IK0:misc_col * LANES + IK0 + D_IDX])

    dt = x_prompt.dtype
    st = lambda xs_list: jnp.stack(xs_list, axis=0).astype(dt)
    order = ("hgrn", "fk", "fv", "fl", "ck", "cv", "ci", "ret")
    return ((xp[:, ROW_BLOCK:, :].astype(dt), xs[:, :t_new, :].astype(dt))
            + tuple(st(outs_p[k]) for k in order) + tuple(st(outs_s[k]) for k in order))
```
